```python
import math
import jax, jax.numpy as jnp
from jax import lax
import numpy as np

D_MODEL = 2048
BATCH = 2
SEQ = 8192
DEPTH = 2

HEAD_DIM = 128
A_HEADS = 4
MOBA_BLOCK = 256
MOBA_TOPK = 3
MOBA_Q_CHUNK = 64
B_HEADS = 4
FOX_Q_BLOCK = 128
CONV_CH = 512
CONV_WIDTH = 31
D_Q_HEADS = 8
D_KV_HEADS = 2
D_HEAD_DIM = 64
WINDOW = 128
N_BUCKETS = 32
MAX_DISTANCE = 1024
N_BIAS_HEADS = A_HEADS + D_Q_HEADS
N_BRANCHES = 4
D_FF = -(-8 * D_MODEL // (3 * 256)) * 256
RMS_EPS = 1e-6
LN_EPS = 1e-5

A_W = A_HEADS * HEAD_DIM
B_W = B_HEADS * HEAD_DIM
D_QW = D_Q_HEADS * D_HEAD_DIM
D_KVW = D_KV_HEADS * D_HEAD_DIM
IN_SPLITS = (A_W, A_W, A_W, B_W, B_W, B_W, B_HEADS, 2 * CONV_CH, D_QW, D_KVW, D_KVW, N_BRANCHES * D_MODEL)
IN_OFFSETS = tuple(int(v) for v in np.cumsum(IN_SPLITS)[:-1])
IN_W = int(sum(IN_SPLITS))

kernel_name = 'hybrid_gated_moba_fox_conformer_swa'


def rms_norm(x, g):
    xf = x.astype(jnp.float32)
    y = xf * lax.rsqrt(jnp.mean(xf * xf, axis=-1, keepdims=True) + RMS_EPS)
    return (y * g.astype(jnp.float32)).astype(x.dtype)


def layer_norm(x, g, b):
    xf = x.astype(jnp.float32)
    mu = jnp.mean(xf, axis=-1, keepdims=True)
    var = jnp.mean(jnp.square(xf - mu), axis=-1, keepdims=True)
    y = (xf - mu) * lax.rsqrt(var + LN_EPS)
    return (y * g.astype(jnp.float32) + b.astype(jnp.float32)).astype(x.dtype)


def t5_bucket(dist):
    max_exact = N_BUCKETS // 2
    d = jnp.maximum(dist, 0)
    log_ratio = jnp.log(jnp.maximum(d, 1).astype(jnp.float32) / max_exact) / math.log(MAX_DISTANCE / max_exact)
    large = max_exact + (log_ratio * (N_BUCKETS - max_exact)).astype(jnp.int32)
    large = jnp.minimum(large, N_BUCKETS - 1)
    return jnp.where(d < max_exact, d, large)


def moba_attention(q, k, v, rel_bias):
    bsz, seq, n_heads, dh = q.shape
    n_blk = -(-seq // MOBA_BLOCK)
    seq_p = n_blk * MOBA_BLOCK
    n_sel = min(MOBA_TOPK, n_blk)
    pad = ((0, 0), (0, seq_p - seq), (0, 0), (0, 0))
    qh = jnp.pad(q, pad).transpose(0, 2, 1, 3)
    k_blocks = jnp.pad(k, pad).transpose(0, 2, 1, 3).reshape(bsz, n_heads, n_blk, MOBA_BLOCK, dh)
    v_blocks = jnp.pad(v, pad).transpose(0, 2, 1, 3).reshape(bsz, n_heads, n_blk, MOBA_BLOCK, dh)
    k_mean = jnp.mean(k_blocks.astype(jnp.float32), axis=3)
    bias_tab = rel_bias.T
    head_ix = jnp.arange(n_heads)[None, :, None, None, None]
    blk_ids = jnp.arange(n_blk)
    offs = jnp.arange(MOBA_BLOCK)
    gather_blocks = jax.vmap(jax.vmap(lambda blocks, idx: blocks[idx]))
    scale = dh ** -0.5

    def one_chunk(ci):
        q0 = ci * MOBA_Q_CHUNK
        qc = lax.dynamic_slice_in_dim(qh, q0, MOBA_Q_CHUNK, axis=2)
        t = q0 + jnp.arange(MOBA_Q_CHUNK)
        own = t // MOBA_BLOCK
        score = jnp.einsum('bhqd,bhnd->bhqn', qc.astype(jnp.float32), k_mean)
        score = jnp.where(blk_ids[None, :] < own[:, None], score, -jnp.inf)
        _, top_i = lax.top_k(score, n_sel)
        own_b = jnp.broadcast_to(own[:, None], top_i.shape[:-1] + (1,))
        blk_idx = jnp.concatenate([top_i, own_b], axis=-1)
        blk_ok = jnp.concatenate([top_i < own[:, None], jnp.ones_like(own_b, dtype=bool)], axis=-1)
        ks = gather_blocks(k_blocks, blk_idx)
        vs = gather_blocks(v_blocks, blk_idx)
        dist = t[:, None, None] - (blk_idx[..., None] * MOBA_BLOCK + offs)
        valid = blk_ok[..., None] & (dist >= 0)
        logits = jnp.einsum('bhqd,bhqnkd->bhqnk', qc, ks, preferred_element_type=jnp.float32) * scale
        logits = logits + bias_tab[head_ix, t5_bucket(dist)].astype(jnp.float32)
        logits = jnp.where(valid, logits, -jnp.inf).reshape(bsz, n_heads, MOBA_Q_CHUNK, -1)
        p = jax.nn.softmax(logits, axis=-1).reshape(valid.shape).astype(vs.dtype)
        return jnp.einsum('bhqnk,bhqnkd->bhqd', p, vs)

    out = lax.map(one_chunk, jnp.arange(seq_p // MOBA_Q_CHUNK))
    out = out.transpose(1, 0, 3, 2, 4).reshape(bsz, seq_p, n_heads * dh)
    return out[:, :seq]


def forgetting_attention(q, k, v, log_f):
    bsz, seq, n_heads, dh = q.shape
    qh = q.transpose(0, 2, 1, 3)
    kh = k.transpose(0, 2, 1, 3)
    vh = v.transpose(0, 2, 1, 3)
    cum = jnp.cumsum(log_f, axis=1).transpose(0, 2, 1)
    kpos = jnp.arange(seq)
    scale = dh ** -0.5

    def one_block(bi):
        q0 = bi * FOX_Q_BLOCK
        qb = lax.dynamic_slice_in_dim(qh, q0, FOX_Q_BLOCK, axis=2)
        cq = lax.dynamic_slice_in_dim(cum, q0, FOX_Q_BLOCK, axis=2)
        t = q0 + jnp.arange(FOX_Q_BLOCK)
        logits = (jnp.einsum('bhqd,bhkd->bhqk', qb, kh, preferred_element_type=jnp.float32) * scale
                  + (cq[..., None] - cum[:, :, None, :]))
        logits = jnp.where(kpos[None, :] <= t[:, None], logits, -jnp.inf)
        p = jax.nn.softmax(logits, axis=-1).astype(vh.dtype)
        return jnp.einsum('bhqk,bhkd->bhqd', p, vh)

    out = lax.map(one_block, jnp.arange(seq // FOX_Q_BLOCK))
    return out.transpose(1, 0, 3, 2, 4).reshape(bsz, seq, n_heads * dh)


def conformer_conv(u, conv_w, conv_b, ln_g, ln_b):
    a, gte = jnp.split(u, 2, axis=-1)
    h = a * jax.nn.sigmoid(gte)
    h = lax.conv_general_dilated(h, conv_w[:, None, :], window_strides=(1,), padding=[(CONV_WIDTH - 1, 0)],
                                 dimension_numbers=('NWC', 'WIO', 'NWC'), feature_group_count=CONV_CH) + conv_b
    return jax.nn.silu(layer_norm(h, ln_g, ln_b))


def sliding_window_attention(q, k, v, sinks, rel_bias):
    bsz, seq, n_q, dh = q.shape
    n_kv = k.shape[2]
    grp = n_q // n_kv
    nb = seq // WINDOW
    qb = q.reshape(bsz, nb, WINDOW, n_kv, grp, dh)

    def band(t):
        prev = jnp.pad(t, ((0, 0), (WINDOW, 0), (0, 0), (0, 0)))[:, :seq]
        return jnp.concatenate([prev.reshape(bsz, nb, WINDOW, n_kv, dh), t.reshape(bsz, nb, WINDOW, n_kv, dh)], axis=2)

    kb, vb = band(k), band(v)
    qi = jnp.arange(WINDOW)[:, None]
    kj = jnp.arange(2 * WINDOW)[None, :]
    dist = qi + WINDOW - kj
    in_win = (dist >= 0) & (dist < WINDOW)
    not_pad = (jnp.arange(nb)[:, None, None] > 0) | (kj >= WINDOW)[None]
    valid = in_win[None] & not_pad
    bias = rel_bias[t5_bucket(dist)].transpose(2, 0, 1).reshape(n_kv, grp, 1, WINDOW, 2 * WINDOW)
    logits = jnp.einsum('bnqhgd,bnkhd->bhgnqk', qb, kb, preferred_element_type=jnp.float32) * (dh ** -0.5)
    logits = jnp.where(valid, logits + bias.astype(jnp.float32), -jnp.inf)
    sink = jnp.broadcast_to(sinks.reshape(n_kv, grp, 1, 1, 1).astype(jnp.float32), logits.shape[:-1] + (1,))
    p = jax.nn.softmax(jnp.concatenate([logits, sink], axis=-1), axis=-1)[..., :-1].astype(vb.dtype)
    out = jnp.einsum('bhgnqk,bnkhd->bnqhgd', p, vb)
    return out.reshape(bsz, seq, n_q * dh)


def hybrid_mixer(h, rel_bias, w_in, fox_b, conv_w, conv_b, conv_g, conv_beta, sinks,
                 w_br_a, w_br_b, w_br_c, w_br_d, w_out):
    bsz, seq, _ = h.shape
    proj = h @ w_in
    aq, ak, av, bq, bk, bv, bf, cu, dq, dk, dv, gl = jnp.split(proj, IN_OFFSETS, axis=-1)

    def heads(t, n):
        return t.reshape(bsz, seq, n, -1)

    ya = moba_attention(heads(aq, A_HEADS), heads(ak, A_HEADS), heads(av, A_HEADS), rel_bias[:, :A_HEADS])
    log_f = jax.nn.log_sigmoid((bf + fox_b).astype(jnp.float32))
    yb = forgetting_attention(heads(bq, B_HEADS), heads(bk, B_HEADS), heads(bv, B_HEADS), log_f)
    yc = conformer_conv(cu, conv_w, conv_b, conv_g, conv_beta)
    yd = sliding_window_attention(heads(dq, D_Q_HEADS), heads(dk, D_KV_HEADS), heads(dv, D_KV_HEADS),
                                  sinks, rel_bias[:, A_HEADS:])
    gates = jax.nn.sigmoid(gl.reshape(bsz, seq, N_BRANCHES, D_MODEL))
    y = (gates[:, :, 0] * (ya @ w_br_a) + gates[:, :, 1] * (yb @ w_br_b)
         + gates[:, :, 2] * (yc @ w_br_c) + gates[:, :, 3] * (yd @ w_br_d))
    return y @ w_out


def swiglu(h, w_gate, w_up, w_down):
    return (jax.nn.silu(h @ w_gate) * (h @ w_up)) @ w_down


def _normal(key, shape, scale):
    return jax.random.normal(key, shape, jnp.float32) * scale


def setup_inputs(seed: int = 0) -> dict:
    key = jax.random.key(seed)
    ks = jax.random.split(key, 24)
    d = D_MODEL
    return {
        'x': _normal(ks[0], (BATCH, SEQ, d), 1.0),
        'c': _normal(ks[1], (BATCH, d), 1.0),
        'rel_bias': _normal(ks[2], (N_BUCKETS, N_BIAS_HEADS), 0.5),
        'w_mod': _normal(ks[3], (DEPTH, d, 6 * d), 0.5 * d ** -0.5),
        'b_mod': _normal(ks[4], (DEPTH, 6 * d), 0.02),
        'mix_norm_pre': 1.0 + _normal(ks[5], (DEPTH, d), 0.05),
        'mix_norm_post': 1.0 + _normal(ks[6], (DEPTH, d), 0.05),
        'w_in': _normal(ks[7], (DEPTH, d, IN_W), d ** -0.5),
        'fox_bias': jax.random.uniform(ks[8], (DEPTH, B_HEADS), jnp.float32, minval=1.0, maxval=4.0),
        'conv_w': _normal(ks[9], (DEPTH, CONV_WIDTH, CONV_CH), CONV_WIDTH ** -0.5),
        'conv_b': _normal(ks[10], (DEPTH, CONV_CH), 0.02),
        'conv_ln_g': 1.0 + _normal(ks[11], (DEPTH, CONV_CH), 0.05),
        'conv_ln_b': _normal(ks[12], (DEPTH, CONV_CH), 0.02),
        'sinks': _normal(ks[13], (DEPTH, D_Q_HEADS), 1.0),
        'w_branch_a': _normal(ks[14], (DEPTH, A_W, d), A_W ** -0.5),
        'w_branch_b': _normal(ks[15], (DEPTH, B_W, d), B_W ** -0.5),
        'w_branch_c': _normal(ks[16], (DEPTH, CONV_CH, d), CONV_CH ** -0.5),
        'w_branch_d': _normal(ks[17], (DEPTH, D_QW, d), D_QW ** -0.5),
        'w_out': _normal(ks[18], (DEPTH, d, d), d ** -0.5),
        'ffn_norm_pre': 1.0 + _normal(ks[19], (DEPTH, d), 0.05),
        'ffn_norm_post': 1.0 + _normal(ks[20], (DEPTH, d), 0.05),
        'w_ffn_gate': _normal(ks[21], (DEPTH, d, D_FF), d ** -0.5),
        'w_ffn_up': _normal(ks[22], (DEPTH, d, D_FF), d ** -0.5),
        'w_ffn_down': _normal(ks[23], (DEPTH, D_FF, d), D_FF ** -0.5),
    }


def reference(x, c, rel_bias, w_mod, b_mod, mix_norm_pre, mix_norm_post, w_in, fox_bias,
              conv_w, conv_b, conv_ln_g, conv_ln_b, sinks, w_branch_a, w_branch_b, w_branch_c,
              w_branch_d, w_out, ffn_norm_pre, ffn_norm_post, w_ffn_gate, w_ffn_up, w_ffn_down):
    c_act = jax.nn.silu(c)
    for l in range(DEPTH):
        mod = c_act @ w_mod[l] + b_mod[l]
        sh_m, sc_m, gt_m, sh_f, sc_f, gt_f = jnp.split(mod[:, None, :], 6, axis=-1)
        h = rms_norm(x, mix_norm_pre[l]) * (1.0 + sc_m) + sh_m
        y = hybrid_mixer(h, rel_bias, w_in[l], fox_bias[l], conv_w[l], conv_b[l], conv_ln_g[l], conv_ln_b[l],
                         sinks[l], w_branch_a[l], w_branch_b[l], w_branch_c[l], w_branch_d[l], w_out[l])
        x = x + gt_m * rms_norm(y, mix_norm_post[l])
        h = rms_norm(x, ffn_norm_pre[l]) * (1.0 + sc_f) + sh_f
        y = swiglu(h, w_ffn_gate[l], w_ffn_up[l], w_ffn_down[l])
        x = x + gt_f * rms_norm(y, ffn_norm_post[l])
    return x
```

```python
import functools
import math

import jax
import jax.numpy as jnp
import numpy as np
from jax import lax
from jax.experimental import pallas as pl
from jax.experimental.pallas import tpu as pltpu

F32 = jnp.float32
BF16 = jnp.bfloat16
HIGHEST = lax.Precision.HIGHEST

HEAD_DIM = 128
A_HEADS = 4
MOBA_BLOCK = 256
MOBA_TOPK = 3
B_HEADS = 4
CONV_CH = 512
CONV_WIDTH = 31
D_Q_HEADS = 8
D_KV_HEADS = 2
D_HEAD_DIM = 64
WINDOW = 128
N_BUCKETS = 32
MAX_DISTANCE = 1024
N_BRANCHES = 4
RMS_EPS = 1e-6
LN_EPS = 1e-5

A_W = A_HEADS * HEAD_DIM
B_W = B_HEADS * HEAD_DIM
D_QW = D_Q_HEADS * D_HEAD_DIM
D_KVW = D_KV_HEADS * D_HEAD_DIM

LANES = 128
VMEM_LIMIT = 56 * 1024 * 1024

NEG = -1e30

COL_AQ, COL_AK, COL_AV = 0, A_W, 2 * A_W
COL_BQ, COL_BK, COL_BV = 3 * A_W, 3 * A_W + B_W, 3 * A_W + 2 * B_W
COL_CU = 3 * A_W + 3 * B_W
COL_DQ = COL_CU + 2 * CONV_CH
COL_DK = COL_DQ + D_QW
COL_DV = COL_DK + D_KVW
MAIN_W = COL_DV + D_KVW
MAIN_TN = 1024
MAIN_W_PAD = -(-MAIN_W // MAIN_TN) * MAIN_TN
MOBA_FAR = -(-(MAX_DISTANCE + MOBA_BLOCK - 1) // MOBA_BLOCK)


def _params(sem, vmem=VMEM_LIMIT):
    return pltpu.CompilerParams(dimension_semantics=sem, vmem_limit_bytes=vmem)


def _t5_bucket(dist):
    max_exact = N_BUCKETS // 2
    d = jnp.maximum(dist, 0)
    log_ratio = jnp.log(jnp.maximum(d, 1).astype(jnp.float32) / max_exact) / math.log(MAX_DISTANCE / max_exact)
    large = max_exact + (log_ratio * (N_BUCKETS - max_exact)).astype(jnp.int32)
    large = jnp.minimum(large, N_BUCKETS - 1)
    return jnp.where(d < max_exact, d, large)


def _rms(y):
    return y * lax.rsqrt(jnp.mean(y * y, axis=-1, keepdims=True) + RMS_EPS)


def _modulated_norm(x, g, sc, sh):
    return (_rms(x) * g) * (1.0 + sc) + sh


MOD_TN = 512
MOD_KC = 256


def _mod_kernel(ct_ref, w_ref, b_ref, o_ref):
    d, nb = ct_ref.shape
    tn = w_ref.shape[1]
    ct = ct_ref[...]
    ca = ct * jax.nn.sigmoid(ct)
    rows = []
    for r in range(nb):
        acc = jnp.zeros((1, tn), F32)
        for kc in range(d // MOD_KC):
            sl = slice(kc * MOD_KC, (kc + 1) * MOD_KC)
            acc = acc + jnp.sum(w_ref[sl, :] * ca[sl, r:r + 1], axis=0, keepdims=True)
        rows.append(acc)
    o_ref[...] = jnp.concatenate(rows, axis=0) + b_ref[...]


def _modulation(c, w_mod, b_mod):
    depth, d, n = w_mod.shape
    bsz = c.shape[0]
    return pl.pallas_call(
        _mod_kernel,
        grid=(depth, n // MOD_TN),
        in_specs=[
            pl.BlockSpec((d, bsz), lambda l, j: (0, 0)),
            pl.BlockSpec((None, d, MOD_TN), lambda l, j: (l, 0, j)),
            pl.BlockSpec((None, 1, MOD_TN), lambda l, j: (l, 0, j)),
        ],
        out_specs=pl.BlockSpec((None, bsz, MOD_TN), lambda l, j: (l, 0, j)),
        out_shape=jax.ShapeDtypeStruct((depth, bsz, n), F32),
        compiler_params=_params(("parallel", "parallel")),
        name="mod",
    )(c.T, w_mod, b_mod.reshape(depth, 1, n))


def _norm_proj_kernel(x_ref, g_ref, sc_ref, sh_ref, w_ref, *rest, act, with_side):
    if with_side:
        ws_ref, o_ref, os_ref, h_scr = rest
    else:
        o_ref, h_scr = rest

    @pl.when(pl.program_id(2) == 0)
    def _():
        h = _modulated_norm(x_ref[...], g_ref[...], sc_ref[...], sh_ref[...]).astype(BF16)
        h_scr[...] = h
        if with_side:
            os_ref[...] = jnp.dot(h, ws_ref[...], preferred_element_type=F32)

    acc = jnp.dot(h_scr[...], w_ref[...], preferred_element_type=F32)
    if act == "sigmoid":
        acc = jax.nn.sigmoid(acc)
    o_ref[...] = acc.astype(o_ref.dtype)


def _norm_proj(x, g, sc, sh, w, w_side=None, *, act, tm, tn):
    bsz, seq, d = x.shape
    n = w.shape[1]
    with_side = w_side is not None
    in_specs = [
        pl.BlockSpec((None, tm, d), lambda b, i, j: (b, i, 0)),
        pl.BlockSpec((1, d), lambda b, i, j: (0, 0)),
        pl.BlockSpec((None, 1, d), lambda b, i, j: (b, 0, 0)),
        pl.BlockSpec((None, 1, d), lambda b, i, j: (b, 0, 0)),
        pl.BlockSpec((d, tn), lambda b, i, j: (0, j)),
    ]
    out_specs = [pl.BlockSpec((None, tm, tn), lambda b, i, j: (b, i, j))]
    out_shape = [jax.ShapeDtypeStruct((bsz, seq, n), BF16)]
    args = [x, g.reshape(1, d), sc, sh, w]
    if with_side:
        ns = w_side.shape[1]
        in_specs.append(pl.BlockSpec((d, ns), lambda b, i, j: (0, 0)))
        out_specs.append(pl.BlockSpec((None, tm, ns), lambda b, i, j: (b, i, 0)))
        out_shape.append(jax.ShapeDtypeStruct((bsz, seq, ns), F32))
        args.append(w_side)
    outs = pl.pallas_call(
        functools.partial(_norm_proj_kernel, act=act, with_side=with_side),
        grid=(bsz, seq // tm, n // tn),
        in_specs=in_specs,
        out_specs=out_specs,
        out_shape=out_shape,
        scratch_shapes=[pltpu.VMEM((tm, d), BF16)],
        compiler_params=_params(("parallel", "parallel", "arbitrary")),
        name="norm_proj_" + act,
    )(*args)
    return outs if with_side else outs[0]


def _fox_gate_kernel(x_ref, fb_ref, cum_ref, rep_ref, carry_scr):
    t = x_ref.shape[0]

    @pl.when(pl.program_id(1) == 0)
    def _():
        carry_scr[...] = jnp.zeros_like(carry_scr)

    lf = jax.nn.log_sigmoid(x_ref[...] + fb_ref[...])
    tri = (lax.broadcasted_iota(jnp.int32, (t, t), 0) >= lax.broadcasted_iota(jnp.int32, (t, t), 1)).astype(F32)
    c = jnp.dot(tri, lf, precision=HIGHEST, preferred_element_type=F32) + carry_scr[...]
    carry_scr[...] = c[t - 1:t, :]
    cum_ref[...] = c
    for h in range(B_HEADS):
        rep_ref[h] = jnp.broadcast_to(c[:, h:h + 1], (t, LANES))


def _fox_gate(raw, fox_b, *, t):
    bsz, seq, _ = raw.shape
    fb = jnp.zeros((1, LANES), F32).at[0, :B_HEADS].set(fox_b)
    return pl.pallas_call(
        _fox_gate_kernel,
        grid=(bsz, seq // t),
        in_specs=[
            pl.BlockSpec((None, t, LANES), lambda b, i: (b, i, 0)),
            pl.BlockSpec((1, LANES), lambda b, i: (0, 0)),
        ],
        out_specs=[
            pl.BlockSpec((None, t, LANES), lambda b, i: (b, i, 0)),
            pl.BlockSpec((None, B_HEADS, t, LANES), lambda b, i: (b, 0, i, 0)),
        ],
        out_shape=[
            jax.ShapeDtypeStruct((bsz, seq, LANES), F32),
            jax.ShapeDtypeStruct((bsz, B_HEADS, seq, LANES), F32),
        ],
        scratch_shapes=[pltpu.VMEM((1, LANES), F32)],
        compiler_params=_params(("parallel", "arbitrary")),
        name="fox_gate",
    )(raw, fb)


def _softmax_step(z, vb, carry):
    m, l, acc = carry
    m_new = jnp.maximum(m, jnp.max(z, axis=-1, keepdims=True))
    alpha = jnp.exp(m - m_new)
    p = jnp.exp(z - m_new)
    l = alpha * l + jnp.sum(p, axis=-1, keepdims=True)
    acc = alpha * acc + jnp.dot(p.astype(BF16), vb, preferred_element_type=F32)
    return m_new, l, acc


def _softmax_init(tq, dh):
    return (jnp.full((tq, 1), NEG, F32), jnp.zeros((tq, 1), F32), jnp.zeros((tq, dh), F32))


_NT = (((1,), (1,)), ((), ()))


def _moba_kernel(q_ref, k_ref, v_ref, tab_ref, o_ref, kmean_scr):
    blk = MOBA_BLOCK
    seq, dh = k_ref.shape
    nblk = seq // blk
    own = pl.program_id(2)

    @pl.when(own == 0)
    def _():
        r = lax.broadcasted_iota(jnp.int32, (nblk, seq), 0)
        c = lax.broadcasted_iota(jnp.int32, (nblk, seq), 1)
        avg = jnp.where((c >= r * blk) & (c < (r + 1) * blk), 1.0 / blk, 0.0).astype(BF16)
        kmean_scr[...] = jnp.dot(avg, k_ref[...], preferred_element_type=F32)

    q = q_ref[...]
    sc = lax.dot_general(kmean_scr[...], q.astype(F32), _NT, precision=HIGHEST, preferred_element_type=F32)
    row = lax.broadcasted_iota(jnp.int32, (nblk, blk), 0)
    past = row < own
    sc = jnp.where(past, sc, -jnp.inf)
    rank = jnp.zeros((nblk, blk), jnp.int32)
    for m in range(nblk):
        sm = sc[m:m + 1, :]
        beats = jnp.where(sm > sc, 1, jnp.where(sm == sc, jnp.where(row > m, 1, 0), 0))
        rank = rank + beats
    chosen = jnp.where(past, jnp.where(rank < MOBA_TOPK, 1, 0), jnp.where(row == own, 1, 0))
    pen = jnp.where(chosen > 0, 0.0, NEG).astype(F32)
    pen = jnp.concatenate([pen, jnp.zeros((LANES - nblk, blk), F32)], axis=0)
    q_aug = jnp.concatenate([q, pen.T.astype(BF16)], axis=1)

    lane = lax.broadcasted_iota(jnp.int32, (blk, LANES), 1)

    def body(n, carry):
        start = pl.multiple_of(n * blk, blk)
        kb = k_ref[pl.ds(start, blk), :]
        vb = v_ref[pl.ds(start, blk), :]
        onehot = jnp.where(lane == n, 1.0, 0.0).astype(BF16)
        k_aug = jnp.concatenate([kb, onehot], axis=1)
        s = lax.dot_general(q_aug, k_aug, _NT, preferred_element_type=F32)
        z = s + tab_ref[jnp.minimum(own - n, MOBA_FAR)]
        return _softmax_step(z, vb, carry)

    _, l, acc = lax.fori_loop(0, own + 1, body, _softmax_init(blk, dh))
    o_ref[...] = (acc / l).astype(o_ref.dtype)


def _moba(proj, tab):
    bsz, seq, _ = proj.shape
    blk = MOBA_BLOCK
    nblk = seq // blk
    assert seq % blk == 0 and nblk <= LANES and nblk % 8 == 0
    qc, kc, vc = COL_AQ // HEAD_DIM, COL_AK // HEAD_DIM, COL_AV // HEAD_DIM
    return pl.pallas_call(
        _moba_kernel,
        grid=(bsz, A_HEADS, nblk),
        in_specs=[
            pl.BlockSpec((None, blk, HEAD_DIM), lambda b, h, i: (b, i, qc + h)),
            pl.BlockSpec((None, seq, HEAD_DIM), lambda b, h, i: (b, 0, kc + h)),
            pl.BlockSpec((None, seq, HEAD_DIM), lambda b, h, i: (b, 0, vc + h)),
            pl.BlockSpec((None, MOBA_FAR + 1, blk, blk), lambda b, h, i: (h, 0, 0, 0)),
        ],
        out_specs=pl.BlockSpec((None, blk, HEAD_DIM), lambda b, h, i: (b, i, h)),
        out_shape=jax.ShapeDtypeStruct((bsz, seq, A_W), BF16),
        scratch_shapes=[pltpu.VMEM((nblk, HEAD_DIM), F32)],
        compiler_params=_params(("parallel", "parallel", "arbitrary")),
        name="moba",
    )(proj, proj, proj, tab)


def _moba_tables(rel_bias_a):
    blk = MOBA_BLOCK
    i = jnp.arange(blk)[:, None]
    j = jnp.arange(blk)[None, :]
    tabs = []
    for delta in range(MOBA_FAR):
        dist = delta * blk + i - j
        t = rel_bias_a[_t5_bucket(dist)].transpose(2, 0, 1).astype(F32)
        tabs.append(jnp.where((dist >= 0)[None], t, NEG))
    far = rel_bias_a[_t5_bucket(jnp.full((blk, blk), MAX_DISTANCE, jnp.int32))].transpose(2, 0, 1).astype(F32)
    tabs.append(far)
    return jnp.stack(tabs, axis=1)


FOX_T = 256


def _fox_kernel(q_ref, k_ref, v_ref, cq_ref, ck_ref, o_ref):
    tq = FOX_T
    dh = k_ref.shape[1]
    i = pl.program_id(2)
    q = q_ref[...]
    cq = cq_ref[...]
    cq = jnp.concatenate([cq] * (tq // LANES), axis=1)

    def logits(n):
        start = pl.multiple_of(n * tq, tq)
        kb = k_ref[pl.ds(start, tq), :]
        vb = v_ref[pl.ds(start, tq), :]
        s = lax.dot_general(q, kb, _NT, preferred_element_type=F32)
        return s + (cq - ck_ref[:, pl.ds(start, tq)]), vb

    def body(n, carry):
        z, vb = logits(n)
        return _softmax_step(z, vb, carry)

    carry = lax.fori_loop(0, i, body, _softmax_init(tq, dh))
    z, vb = logits(i)
    causal = lax.broadcasted_iota(jnp.int32, (tq, tq), 0) >= lax.broadcasted_iota(jnp.int32, (tq, tq), 1)
    _, l, acc = _softmax_step(jnp.where(causal, z, NEG), vb, carry)
    o_ref[...] = (acc / l).astype(o_ref.dtype)


def _fox(proj, cum_rep, cum_row):
    bsz, seq, _ = proj.shape
    tq = FOX_T
    qc, kc, vc = COL_BQ // HEAD_DIM, COL_BK // HEAD_DIM, COL_BV // HEAD_DIM
    return pl.pallas_call(
        _fox_kernel,
        grid=(bsz, B_HEADS, seq // tq),
        in_specs=[
            pl.BlockSpec((None, tq, HEAD_DIM), lambda b, h, i: (b, i, qc + h)),
            pl.BlockSpec((None, seq, HEAD_DIM), lambda b, h, i: (b, 0, kc + h)),
            pl.BlockSpec((None, seq, HEAD_DIM), lambda b, h, i: (b, 0, vc + h)),
            pl.BlockSpec((None, None, tq, LANES), lambda b, h, i: (b, h, i, 0)),
            pl.BlockSpec((None, None, 1, seq), lambda b, h, i: (b, h, 0, 0)),
        ],
        out_specs=pl.BlockSpec((None, tq, HEAD_DIM), lambda b, h, i: (b, i, h)),
        out_shape=jax.ShapeDtypeStruct((bsz, seq, B_W), BF16),
        compiler_params=_params(("parallel", "parallel", "arbitrary")),
        name="fox",
    )(proj, proj, proj, cum_rep, cum_row)


def _swa_kernel(sink_ref, q_ref, kp_ref, kc_ref, vp_ref, vc_ref, tab_ref, o_ref):
    w = WINDOW
    dh = D_HEAD_DIM
    grp = D_Q_HEADS // D_KV_HEADS
    i = pl.program_id(1)
    q = q_ref[...]
    kcat = jnp.concatenate([kp_ref[...], kc_ref[...]], axis=0)
    vcat = jnp.concatenate([vp_ref[...], vc_ref[...]], axis=0)
    col = lax.broadcasted_iota(jnp.int32, (w, 2 * w), 1)
    keep = col >= jnp.where(i > 0, 0, w)
    outs = []
    for h in range(D_Q_HEADS):
        g = h // grp
        s = lax.dot_general(q[:, h * dh:(h + 1) * dh], kcat[:, g * dh:(g + 1) * dh], _NT,
                            preferred_element_type=F32)
        s = jnp.where(keep, s + tab_ref[h], NEG)
        sink = sink_ref[h]
        m = jnp.maximum(jnp.max(s, axis=-1, keepdims=True), sink)
        p = jnp.exp(s - m)
        den = jnp.sum(p, axis=-1, keepdims=True) + jnp.exp(sink - m)
        pv = jnp.dot(p.astype(BF16), vcat[:, g * dh:(g + 1) * dh], preferred_element_type=F32)
        outs.append(pv / den)
    o_ref[...] = jnp.concatenate(outs, axis=1).astype(o_ref.dtype)


def _swa(proj, sinks, tab):
    bsz, seq, _ = proj.shape
    w = WINDOW
    qc, kc, vc = COL_DQ // D_QW, COL_DK // D_KVW, COL_DV // D_KVW
    prev = lambda i: jnp.maximum(i - 1, 0)
    return pl.pallas_call(
        _swa_kernel,
        grid=(bsz, seq // w),
        in_specs=[
            pl.BlockSpec(memory_space=pltpu.SMEM),
            pl.BlockSpec((None, w, D_QW), lambda b, i: (b, i, qc)),
            pl.BlockSpec((None, w, D_KVW), lambda b, i: (b, prev(i), kc)),
            pl.BlockSpec((None, w, D_KVW), lambda b, i: (b, i, kc)),
            pl.BlockSpec((None, w, D_KVW), lambda b, i: (b, prev(i), vc)),
            pl.BlockSpec((None, w, D_KVW), lambda b, i: (b, i, vc)),
            pl.BlockSpec((D_Q_HEADS, w, 2 * w), lambda b, i: (0, 0, 0)),
        ],
        out_specs=pl.BlockSpec((None, w, D_QW), lambda b, i: (b, i, 0)),
        out_shape=jax.ShapeDtypeStruct((bsz, seq, D_QW), BF16),
        compiler_params=_params(("parallel", "parallel")),
        name="swa",
    )(sinks, proj, proj, proj, proj, proj, tab)


def _swa_table(rel_bias_d):
    qi = jnp.arange(WINDOW)[:, None]
    kj = jnp.arange(2 * WINDOW)[None, :]
    dist = qi + WINDOW - kj
    in_win = (dist >= 0) & (dist < WINDOW)
    t = rel_bias_d[_t5_bucket(dist)].transpose(2, 0, 1).astype(F32)
    return jnp.where(in_win[None], t, NEG)


CONV_T = 512
CONV_HALO = 32
CONV_ROWS = 64


def _conv_kernel(a_ref, g_ref, ha_ref, hg_ref, w_ref, b_ref, lg_ref, lb_ref, o_ref, buf):
    t = a_ref.shape[0]
    halo = CONV_HALO
    hp = ha_ref[...].astype(F32) * jax.nn.sigmoid(hg_ref[...].astype(F32))
    buf[0:halo, :] = jnp.where(pl.program_id(1) > 0, hp, 0.0)
    buf[halo:halo + t, :] = a_ref[...].astype(F32) * jax.nn.sigmoid(g_ref[...].astype(F32))
    off = halo - (CONV_WIDTH - 1)
    for r in range(t // CONV_ROWS):
        base = r * CONV_ROWS + off
        acc = jnp.broadcast_to(b_ref[...], (CONV_ROWS, CONV_CH))
        for k in range(CONV_WIDTH):
            acc = acc + w_ref[k:k + 1, :] * buf[base + k:base + k + CONV_ROWS, :]
        mu = jnp.mean(acc, axis=-1, keepdims=True)
        d = acc - mu
        var = jnp.mean(d * d, axis=-1, keepdims=True)
        y = d * lax.rsqrt(var + LN_EPS) * lg_ref[...] + lb_ref[...]
        o_ref[r * CONV_ROWS:(r + 1) * CONV_ROWS, :] = (y * jax.nn.sigmoid(y)).astype(o_ref.dtype)


def _conv(proj, conv_w, conv_b, ln_g, ln_b):
    bsz, seq, _ = proj.shape
    t = min(CONV_T, seq)
    ca, cg = COL_CU // CONV_CH, COL_CU // CONV_CH + 1
    hb = t // CONV_HALO
    prev = lambda i: jnp.maximum(i * hb - 1, 0)
    row = lambda v: v.reshape(1, CONV_CH)
    return pl.pallas_call(
        _conv_kernel,
        grid=(bsz, seq // t),
        in_specs=[
            pl.BlockSpec((None, t, CONV_CH), lambda b, i: (b, i, ca)),
            pl.BlockSpec((None, t, CONV_CH), lambda b, i: (b, i, cg)),
            pl.BlockSpec((None, CONV_HALO, CONV_CH), lambda b, i: (b, prev(i), ca)),
            pl.BlockSpec((None, CONV_HALO, CONV_CH), lambda b, i: (b, prev(i), cg)),
            pl.BlockSpec((CONV_WIDTH, CONV_CH), lambda b, i: (0, 0)),
            pl.BlockSpec((1, CONV_CH), lambda b, i: (0, 0)),
            pl.BlockSpec((1, CONV_CH), lambda b, i: (0, 0)),
            pl.BlockSpec((1, CONV_CH), lambda b, i: (0, 0)),
        ],
        out_specs=pl.BlockSpec((None, t, CONV_CH), lambda b, i: (b, i, 0)),
        out_shape=jax.ShapeDtypeStruct((bsz, seq, CONV_CH), BF16),
        scratch_shapes=[pltpu.VMEM((CONV_HALO + t, CONV_CH), F32)],
        compiler_params=_params(("parallel", "parallel")),
        name="conv",
    )(proj, proj, proj, proj, conv_w, row(conv_b), row(ln_g), row(ln_b))


def _merge_kernel(ya_ref, yb_ref, yc_ref, yd_ref, gates_ref, x_ref, gt_ref, gp_ref,
                  wa_ref, wb_ref, wc_ref, wd_ref, wo_ref, o_ref):
    d = x_ref.shape[1]
    y = None
    for i, (br, w) in enumerate(((ya_ref, wa_ref), (yb_ref, wb_ref), (yc_ref, wc_ref), (yd_ref, wd_ref))):
        t = jnp.dot(br[...], w[...], preferred_element_type=F32) * gates_ref[:, i * d:(i + 1) * d].astype(F32)
        y = t if y is None else y + t
    z = jnp.dot(y.astype(BF16), wo_ref[...], preferred_element_type=F32)
    o_ref[...] = x_ref[...] + gt_ref[...] * (_rms(z) * gp_ref[...])


def _merge(ya, yb, yc, yd, gates, x, gt, g_post, wa, wb, wc, wd, wo, *, tm):
    bsz, seq, d = x.shape
    tok = lambda width: pl.BlockSpec((None, tm, width), lambda b, i: (b, i, 0))
    full = lambda a: pl.BlockSpec(a.shape, lambda b, i: (0,) * a.ndim)
    return pl.pallas_call(
        _merge_kernel,
        grid=(bsz, seq // tm),
        in_specs=[tok(A_W), tok(B_W), tok(CONV_CH), tok(D_QW), tok(N_BRANCHES * d), tok(d),
                  pl.BlockSpec((None, 1, d), lambda b, i: (b, 0, 0)),
                  pl.BlockSpec((1, d), lambda b, i: (0, 0)),
                  full(wa), full(wb), full(wc), full(wd), full(wo)],
        out_specs=tok(d),
        out_shape=jax.ShapeDtypeStruct((bsz, seq, d), F32),
        compiler_params=_params(("parallel", "parallel")),
        name="merge",
    )(ya, yb, yc, yd, gates, x, gt, g_post.reshape(1, d), wa, wb, wc, wd, wo)


def _ffn_kernel(x_ref, g_ref, sc_ref, sh_ref, gt_ref, gp_ref, wg_ref, wu_ref, wd_ref, o_ref, h_scr, acc_scr):
    f = pl.program_id(2)

    @pl.when(f == 0)
    def _():
        h_scr[...] = _modulated_norm(x_ref[...], g_ref[...], sc_ref[...], sh_ref[...]).astype(BF16)
        acc_scr[...] = jnp.zeros_like(acc_scr)

    h = h_scr[...]
    g = jnp.dot(h, wg_ref[...], preferred_element_type=F32)
    u = jnp.dot(h, wu_ref[...], preferred_element_type=F32)
    a = ((g * jax.nn.sigmoid(g)) * u).astype(BF16)
    acc_scr[...] += jnp.dot(a, wd_ref[...], preferred_element_type=F32)

    @pl.when(f == pl.num_programs(2) - 1)
    def _():
        o_ref[...] = x_ref[...] + gt_ref[...] * (_rms(acc_scr[...]) * gp_ref[...])


def _ffn(x, g_pre, sc, sh, gt, g_post, wg, wu, wd, *, tm, tf):
    bsz, seq, d = x.shape
    dff = wg.shape[1]
    vec = lambda: pl.BlockSpec((None, 1, d), lambda b, i, f: (b, 0, 0))
    par = lambda: pl.BlockSpec((1, d), lambda b, i, f: (0, 0))
    return pl.pallas_call(
        _ffn_kernel,
        grid=(bsz, seq // tm, dff // tf),
        in_specs=[
            pl.BlockSpec((None, tm, d), lambda b, i, f: (b, i, 0)),
            par(), vec(), vec(), vec(), par(),
            pl.BlockSpec((d, tf), lambda b, i, f: (0, f)),
            pl.BlockSpec((d, tf), lambda b, i, f: (0, f)),
            pl.BlockSpec((tf, d), lambda b, i, f: (f, 0)),
        ],
        out_specs=pl.BlockSpec((None, tm, d), lambda b, i, f: (b, i, 0)),
        out_shape=jax.ShapeDtypeStruct((bsz, seq, d), F32),
        scratch_shapes=[pltpu.VMEM((tm, d), BF16), pltpu.VMEM((tm, d), F32)],
        compiler_params=_params(("parallel", "parallel", "arbitrary")),
        name="ffn",
    )(x, g_pre.reshape(1, d), sc, sh, gt, g_post.reshape(1, d), wg, wu, wd)


def _pack_in_proj(w):
    d = w.shape[0]
    splits = (A_W, A_W, A_W, B_W, B_W, B_W, B_HEADS, 2 * CONV_CH, D_QW, D_KVW, D_KVW, N_BRANCHES * d)
    offs = np.cumsum(splits)[:-1].tolist()
    aq, ak, av, bq, bk, bv, bf, cu, dq, dk, dv, gl = jnp.split(w, offs, axis=1)
    main = jnp.concatenate(
        [aq * HEAD_DIM ** -0.5, ak, av, bq * HEAD_DIM ** -0.5, bk, bv, cu, dq * D_HEAD_DIM ** -0.5, dk, dv,
         jnp.zeros((d, MAIN_W_PAD - MAIN_W), w.dtype)], axis=1).astype(BF16)
    side = jnp.concatenate([bf, jnp.zeros((d, LANES - B_HEADS), w.dtype)], axis=1).astype(BF16)
    return main, side, gl.astype(BF16)


def kernel(x, c, rel_bias, w_mod, b_mod, mix_norm_pre, mix_norm_post, w_in, fox_bias, conv_w, conv_b, conv_ln_g, conv_ln_b, sinks, w_branch_a, w_branch_b, w_branch_c, w_branch_d, w_out, ffn_norm_pre, ffn_norm_post, w_ffn_gate, w_ffn_up, w_ffn_down):
    depth = w_mod.shape[0]
    bsz, seq, d = x.shape
    tm = min(1024, seq)
    tm_small = min(256, seq)
    tm_ffn = min(512, seq)

    mod = _modulation(c, w_mod, b_mod)
    tab_a = _moba_tables(rel_bias[:, :A_HEADS])
    tab_d = _swa_table(rel_bias[:, A_HEADS:])

    for l in range(depth):
        sh_m, sc_m, gt_m, sh_f, sc_f, gt_f = [mod[l, :, None, i * d:(i + 1) * d] for i in range(6)]
        w_main, w_side, w_gl = _pack_in_proj(w_in[l])

        proj, fox_raw = _norm_proj(x, mix_norm_pre[l], sc_m, sh_m, w_main, w_side, act="none", tm=tm, tn=MAIN_TN)
        gates = _norm_proj(x, mix_norm_pre[l], sc_m, sh_m, w_gl, act="sigmoid", tm=tm, tn=1024)

        cum, cum_rep = _fox_gate(fox_raw, fox_bias[l], t=min(512, seq))
        cum_row = cum[:, :, :B_HEADS].transpose(0, 2, 1)[:, :, None, :]

        ya = _moba(proj, tab_a)
        yb = _fox(proj, cum_rep, cum_row)
        yc = _conv(proj, conv_w[l], conv_b[l], conv_ln_g[l], conv_ln_b[l])
        yd = _swa(proj, sinks[l], tab_d)

        x = _merge(ya, yb, yc, yd, gates, x, gt_m, mix_norm_post[l],
                   w_branch_a[l].astype(BF16), w_branch_b[l].astype(BF16), w_branch_c[l].astype(BF16),
                   w_branch_d[l].astype(BF16), w_out[l].astype(BF16), tm=tm_small)
        x = _ffn(x, ffn_norm_pre[l], sc_f, sh_f, gt_f, ffn_norm_post[l],
                 w_ffn_gate[l].astype(BF16), w_ffn_up[l].astype(BF16), w_ffn_down[l].astype(BF16),
                 tm=tm_ffn, tf=512)
    return x
```

```python
import functools
import math

import jax
import jax.numpy as jnp
import numpy as np
from jax import lax
from jax.experimental import pallas as pl
from jax.experimental.pallas import tpu as pltpu

F32 = jnp.float32
BF16 = jnp.bfloat16
HIGHEST = lax.Precision.HIGHEST

HEAD_DIM = 128
A_HEADS = 4
MOBA_BLOCK = 256
MOBA_TOPK = 3
B_HEADS = 4
CONV_CH = 512
CONV_WIDTH = 31
D_Q_HEADS = 8
D_KV_HEADS = 2
D_HEAD_DIM = 64
WINDOW = 128
N_BUCKETS = 32
MAX_DISTANCE = 1024
N_BRANCHES = 4
RMS_EPS = 1e-6
LN_EPS = 1e-5

A_W = A_HEADS * HEAD_DIM
B_W = B_HEADS * HEAD_DIM
D_QW = D_Q_HEADS * D_HEAD_DIM
D_KVW = D_KV_HEADS * D_HEAD_DIM

LANES = 128
SUBLANES = 8
VMEM_LIMIT = 56 * 1024 * 1024

NEG = -1e30
LOG2E = math.log2(math.e)
ATT_G = 4
ATT_T = MOBA_BLOCK

COL_AQ, COL_AK, COL_AV = 0, A_W, 2 * A_W
COL_BQ, COL_BK, COL_BV = 3 * A_W, 3 * A_W + B_W, 3 * A_W + 2 * B_W
COL_CU = 3 * A_W + 3 * B_W
COL_DQ = COL_CU + 2 * CONV_CH
COL_DK = COL_DQ + D_QW
COL_DV = COL_DK + D_KVW
MAIN_W = COL_DV + D_KVW
MAIN_TN = 1024
MAIN_W_PAD = -(-MAIN_W // MAIN_TN) * MAIN_TN
MOBA_FAR = -(-(MAX_DISTANCE + MOBA_BLOCK - 1) // MOBA_BLOCK)
VOID_COL = LANES - 1


def _params(sem, vmem=VMEM_LIMIT):
    return pltpu.CompilerParams(dimension_semantics=sem, vmem_limit_bytes=vmem)


def _t5_bucket(dist):
    max_exact = N_BUCKETS // 2
    d = jnp.maximum(dist, 0)
    log_ratio = jnp.log(jnp.maximum(d, 1).astype(jnp.float32) / max_exact) / math.log(MAX_DISTANCE / max_exact)
    large = max_exact + (log_ratio * (N_BUCKETS - max_exact)).astype(jnp.int32)
    large = jnp.minimum(large, N_BUCKETS - 1)
    return jnp.where(d < max_exact, d, large)


def _rms(y):
    return y * lax.rsqrt(jnp.mean(y * y, axis=-1, keepdims=True) + RMS_EPS)


def _modulated_norm(x, g, sc, sh):
    return (_rms(x) * g) * (1.0 + sc) + sh


MOD_TN = 512
MOD_KC = 256


def _mod_kernel(ct_ref, w_ref, b_ref, o_ref):
    d, nb = ct_ref.shape
    tn = w_ref.shape[1]
    ct = ct_ref[...]
    ca = ct * jax.nn.sigmoid(ct)
    rows = []
    for r in range(nb):
        acc = jnp.zeros((1, tn), F32)
        for kc in range(d // MOD_KC):
            sl = slice(kc * MOD_KC, (kc + 1) * MOD_KC)
            acc = acc + jnp.sum(w_ref[sl, :] * ca[sl, r:r + 1], axis=0, keepdims=True)
        rows.append(acc)
    o_ref[...] = jnp.concatenate(rows, axis=0) + b_ref[...]


def _modulation(c, w_mod, b_mod):
    depth, d, n = w_mod.shape
    bsz = c.shape[0]
    return pl.pallas_call(
        _mod_kernel,
        grid=(depth, n // MOD_TN),
        in_specs=[
            pl.BlockSpec((d, bsz), lambda l, j: (0, 0)),
            pl.BlockSpec((None, d, MOD_TN), lambda l, j: (l, 0, j)),
            pl.BlockSpec((None, 1, MOD_TN), lambda l, j: (l, 0, j)),
        ],
        out_specs=pl.BlockSpec((None, bsz, MOD_TN), lambda l, j: (l, 0, j)),
        out_shape=jax.ShapeDtypeStruct((depth, bsz, n), F32),
        compiler_params=_params(("parallel", "parallel")),
        name="mod",
    )(c.T, w_mod, b_mod.reshape(depth, 1, n))


def _bias_table_kernel(rb_ref, bucket_ref, o_ref, *, scale):
    h = pl.program_id(0)
    b = bucket_ref[...]
    acc = jnp.where(b < 0, NEG, 0.0).astype(F32)
    for u in range(N_BUCKETS):
        acc = jnp.where(b == u, rb_ref[h, u] * scale, acc)
    o_ref[...] = acc


def _bias_tables(rel_bias_heads, bucket, *, scale):
    nh = rel_bias_heads.shape[1]
    nt, r, c = bucket.shape
    return pl.pallas_call(
        functools.partial(_bias_table_kernel, scale=scale),
        grid=(nh, nt),
        in_specs=[
            pl.BlockSpec(memory_space=pltpu.SMEM),
            pl.BlockSpec((None, r, c), lambda h, t: (t, 0, 0)),
        ],
        out_specs=pl.BlockSpec((None, None, r, c), lambda h, t: (h, t, 0, 0)),
        out_shape=jax.ShapeDtypeStruct((nh, nt, r, c), F32),
        compiler_params=_params(("parallel", "parallel")),
        name="bias_table",
    )(rel_bias_heads.T, bucket)


def _moba_buckets():
    blk = MOBA_BLOCK
    i = jnp.arange(blk)[:, None]
    j = jnp.arange(blk)[None, :]
    tabs = []
    for delta in range(MOBA_FAR):
        dist = delta * blk + i - j
        tabs.append(jnp.where(dist >= 0, _t5_bucket(dist), -1))
    tabs.append(_t5_bucket(jnp.full((blk, blk), MAX_DISTANCE, jnp.int32)))
    return jnp.stack(tabs).astype(jnp.int32)


def _swa_buckets():
    qi = jnp.arange(WINDOW)[:, None]
    kj = jnp.arange(2 * WINDOW)[None, :]
    dist = qi + WINDOW - kj
    in_win = (dist >= 0) & (dist < WINDOW)
    return jnp.where(in_win, _t5_bucket(dist), -1).astype(jnp.int32)[None]


def _norm_proj_kernel(x_ref, g_ref, sc_ref, sh_ref, w_ref, *rest, act, with_side):
    if with_side:
        ws_ref, o_ref, os_ref, h_scr = rest
    else:
        o_ref, h_scr = rest

    @pl.when(pl.program_id(2) == 0)
    def _():
        h = _modulated_norm(x_ref[...], g_ref[...], sc_ref[...], sh_ref[...]).astype(BF16)
        h_scr[...] = h
        if with_side:
            os_ref[...] = jnp.dot(h, ws_ref[...], preferred_element_type=F32)

    acc = jnp.dot(h_scr[...], w_ref[...], preferred_element_type=F32)
    if act == "sigmoid":
        acc = jax.nn.sigmoid(acc)
    o_ref[...] = acc.astype(o_ref.dtype)


def _norm_proj(x, g, sc, sh, w, w_side=None, *, act, tm, tn):
    bsz, seq, d = x.shape
    n = w.shape[1]
    with_side = w_side is not None
    in_specs = [
        pl.BlockSpec((None, tm, d), lambda b, i, j: (b, i, 0)),
        pl.BlockSpec((1, d), lambda b, i, j: (0, 0)),
        pl.BlockSpec((None, 1, d), lambda b, i, j: (b, 0, 0)),
        pl.BlockSpec((None, 1, d), lambda b, i, j: (b, 0, 0)),
        pl.BlockSpec((d, tn), lambda b, i, j: (0, j)),
    ]
    out_specs = [pl.BlockSpec((None, tm, tn), lambda b, i, j: (b, i, j))]
    out_shape = [jax.ShapeDtypeStruct((bsz, seq, n), BF16)]
    args = [x, g.reshape(1, d), sc, sh, w]
    if with_side:
        ns = w_side.shape[1]
        in_specs.append(pl.BlockSpec((d, ns), lambda b, i, j: (0, 0)))
        out_specs.append(pl.BlockSpec((None, tm, ns), lambda b, i, j: (b, i, 0)))
        out_shape.append(jax.ShapeDtypeStruct((bsz, seq, ns), F32))
        args.append(w_side)
    outs = pl.pallas_call(
        functools.partial(_norm_proj_kernel, act=act, with_side=with_side),
        grid=(bsz, seq // tm, n // tn),
        in_specs=in_specs,
        out_specs=out_specs,
        out_shape=out_shape,
        scratch_shapes=[pltpu.VMEM((tm, d), BF16)],
        compiler_params=_params(("parallel", "parallel", "arbitrary")),
        name="norm_proj_" + act,
    )(*args)
    return outs if with_side else outs[0]


def _split3(c):
    hi = c.astype(BF16).astype(F32)
    mid = (c - hi).astype(BF16).astype(F32)
    lo = (c - hi - mid).astype(BF16).astype(F32)
    return hi, mid, lo


def _fox_gate_kernel(x_ref, fb_ref, qa_ref, ka_ref, carry_scr):
    t = x_ref.shape[0]

    @pl.when(pl.program_id(1) == 0)
    def _():
        carry_scr[...] = jnp.zeros_like(carry_scr)

    lf = jax.nn.log_sigmoid(x_ref[...] + fb_ref[...])
    tri = (lax.broadcasted_iota(jnp.int32, (t, t), 0) >= lax.broadcasted_iota(jnp.int32, (t, t), 1)).astype(F32)
    c = jnp.dot(tri, lf, precision=HIGHEST, preferred_element_type=F32) + carry_scr[...]
    carry_scr[...] = c[t - 1:t, :]
    lane = lax.broadcasted_iota(jnp.int32, (t, LANES), 1)
    for h in range(B_HEADS):
        hi, mid, lo = _split3(jnp.broadcast_to(c[:, h:h + 1], (t, LANES)) * LOG2E)
        qa = jnp.where(lane < 3, 1.0, jnp.where(lane == 3, hi, jnp.where(lane == 4, mid, jnp.where(lane == 5, lo, 0.0))))
        ka = jnp.where(lane == 0, -hi, jnp.where(lane == 1, -mid, jnp.where(lane == 2, -lo, jnp.where(lane < 6, 1.0, 0.0))))
        qa_ref[h] = qa.astype(BF16)
        ka_ref[h] = ka.astype(BF16)


def _fox_gate(raw, fox_b, *, t):
    bsz, seq, _ = raw.shape
    fb = jnp.zeros((1, LANES), F32).at[0, :B_HEADS].set(fox_b)
    aug = pl.BlockSpec((None, B_HEADS, t, LANES), lambda b, i: (b, 0, i, 0))
    return pl.pallas_call(
        _fox_gate_kernel,
        grid=(bsz, seq // t),
        in_specs=[
            pl.BlockSpec((None, t, LANES), lambda b, i: (b, i, 0)),
            pl.BlockSpec((1, LANES), lambda b, i: (0, 0)),
        ],
        out_specs=[aug, aug],
        out_shape=[jax.ShapeDtypeStruct((bsz, B_HEADS, seq, LANES), BF16)] * 2,
        scratch_shapes=[pltpu.VMEM((1, LANES), F32)],
        compiler_params=_params(("parallel", "arbitrary")),
        name="fox_gate",
    )(raw, fb)


def _softmax_step(z, vb, carry):
    m, l, acc = carry
    m_new = jnp.maximum(m, jnp.max(z, axis=-1, keepdims=True))
    alpha = jnp.exp2(m - m_new)
    p = jnp.exp2(z - m_new)
    l = alpha * l + jnp.sum(p, axis=-1, keepdims=True)
    acc = alpha * acc + jnp.dot(p.astype(BF16), vb, preferred_element_type=F32)
    return m_new, l, acc


def _softmax_init(tq, dh):
    return (jnp.full((tq, 1), NEG, F32), jnp.zeros((tq, 1), F32), jnp.zeros((tq, dh), F32))


_NT = (((1,), (1,)), ((), ()))


def _dot_nt(a, b):
    return lax.dot_general(a, b, _NT, preferred_element_type=F32)


def _moba_kernel(cfar_ref, q_ref, k_ref, v_ref, tab_ref, o_ref, kmean_scr, sc_scr, qaug_scr):
    g, t = ATT_G, ATT_T
    seq, dh = k_ref.shape
    nblk = seq // t
    rows = g * t
    i = pl.program_id(2)
    base = i * g

    @pl.when(i == 0)
    def _():
        r = lax.broadcasted_iota(jnp.int32, (nblk, seq), 0)
        c = lax.broadcasted_iota(jnp.int32, (nblk, seq), 1)
        avg = jnp.where((c >= r * t) & (c < (r + 1) * t), 1.0 / t, 0.0).astype(BF16)
        kmean_scr[...] = jnp.dot(avg, k_ref[...], preferred_element_type=F32)

    q = q_ref[...]
    qaug_scr[:, :dh] = q
    sc = lax.dot_general(kmean_scr[...], q.astype(F32), _NT, precision=HIGHEST, preferred_element_type=F32)
    row = lax.broadcasted_iota(jnp.int32, (nblk, rows), 0)
    own = base + lax.broadcasted_iota(jnp.int32, (nblk, rows), 1) // t
    past = row < own
    sc = jnp.where(past, sc, -jnp.inf)
    sc_scr[...] = sc

    def rank_body(m, rank):
        sm = sc_scr[pl.ds(m, 1), :]
        return rank + jnp.where(sm > sc, 1.0, jnp.where(sm == sc, jnp.where(row > m, 1.0, 0.0), 0.0))

    rank = lax.fori_loop(0, base + g - 1, rank_body, jnp.zeros((nblk, rows), F32))
    pen = jnp.where(past, jnp.where(rank < MOBA_TOPK, 0.0, NEG), jnp.where(row == own, 0.0, NEG))
    void = jnp.where(lax.broadcasted_iota(jnp.int32, (SUBLANES, rows), 0) == SUBLANES - 1, NEG, 0.0)
    pen = jnp.concatenate([pen, jnp.zeros((LANES - SUBLANES - nblk, rows), F32), void], axis=0)
    qaug_scr[:, dh:] = pen.T.astype(BF16)

    lane = lax.broadcasted_iota(jnp.int32, (t, LANES), 1)

    def kv(n, col):
        start = pl.multiple_of(n * t, t)
        onehot = jnp.where(lane == col, 1.0, 0.0).astype(BF16)
        return jnp.concatenate([k_ref[pl.ds(start, t), :], onehot], axis=1), v_ref[pl.ds(start, t), :]

    def qj(j):
        return qaug_scr[j * t:(j + 1) * t, :]

    def far_body(n, carry):
        kb, vb = kv(n, n)
        return tuple(_softmax_step(_dot_nt(qj(j), kb), vb, carry[j]) for j in range(g))

    carry = list(lax.fori_loop(0, jnp.maximum(base - (MOBA_FAR - 1), 0), far_body,
                               tuple(_softmax_init(t, dh) for _ in range(g))))
    cfar = cfar_ref[pl.program_id(1)]
    for k in range(-(MOBA_FAR - 1), g):
        n = base + k
        if k < 0:
            kb, vb = kv(jnp.maximum(n, 0), jnp.where(n >= 0, n, VOID_COL))
        else:
            kb, vb = kv(n, n)
        for j in range(max(k, 0), g):
            delta = j - k
            z = _dot_nt(qj(j), kb)
            if delta < MOBA_FAR:
                if delta == MOBA_FAR - 1:
                    m, l, acc = carry[j]
                    carry[j] = (m + cfar, l, acc)
                z = z + tab_ref[delta]
            carry[j] = _softmax_step(z, vb, carry[j])
    for j in range(g):
        _, l, acc = carry[j]
        o_ref[j * t:(j + 1) * t, :] = (acc / l).astype(o_ref.dtype)


def _moba(proj, tab):
    bsz, seq, _ = proj.shape
    g, t = ATT_G, ATT_T
    nblk = seq // t
    assert seq % (g * t) == 0 and nblk <= LANES - SUBLANES and nblk % SUBLANES == 0
    qc, kc, vc = COL_AQ // HEAD_DIM, COL_AK // HEAD_DIM, COL_AV // HEAD_DIM
    cfar = tab[:, MOBA_FAR, 0, 0]
    return pl.pallas_call(
        _moba_kernel,
        grid=(bsz, A_HEADS, nblk // g),
        in_specs=[
            pl.BlockSpec(memory_space=pltpu.SMEM),
            pl.BlockSpec((None, g * t, HEAD_DIM), lambda b, h, i: (b, i, qc + h)),
            pl.BlockSpec((None, seq, HEAD_DIM), lambda b, h, i: (b, 0, kc + h)),
            pl.BlockSpec((None, seq, HEAD_DIM), lambda b, h, i: (b, 0, vc + h)),
            pl.BlockSpec((None, MOBA_FAR, t, t), lambda b, h, i: (h, 0, 0, 0)),
        ],
        out_specs=pl.BlockSpec((None, g * t, HEAD_DIM), lambda b, h, i: (b, i, h)),
        out_shape=jax.ShapeDtypeStruct((bsz, seq, A_W), BF16),
        scratch_shapes=[pltpu.VMEM((nblk, HEAD_DIM), F32), pltpu.VMEM((nblk, g * t), F32),
                        pltpu.VMEM((g * t, HEAD_DIM + LANES), BF16)],
        compiler_params=_params(("parallel", "parallel", "arbitrary")),
        name="moba",
    )(cfar, proj, proj, proj, tab)


def _fox_kernel(q_ref, qa_ref, k_ref, ka_ref, v_ref, o_ref, qaug_scr):
    g, t = ATT_G, ATT_T
    dh = q_ref.shape[1]
    base = pl.program_id(2) * g
    qaug_scr[:, :dh] = q_ref[...]
    qaug_scr[:, dh:] = qa_ref[...]

    def kv(n):
        start = pl.multiple_of(n * t, t)
        return (jnp.concatenate([k_ref[pl.ds(start, t), :], ka_ref[pl.ds(start, t), :]], axis=1),
                v_ref[pl.ds(start, t), :])

    def qj(j):
        return qaug_scr[j * t:(j + 1) * t, :]

    def body(n, carry):
        kb, vb = kv(n)
        return tuple(_softmax_step(_dot_nt(qj(j), kb), vb, carry[j]) for j in range(g))

    carry = list(lax.fori_loop(0, base, body, tuple(_softmax_init(t, dh) for _ in range(g))))
    causal = lax.broadcasted_iota(jnp.int32, (t, t), 0) >= lax.broadcasted_iota(jnp.int32, (t, t), 1)
    for k in range(g):
        kb, vb = kv(base + k)
        for j in range(k, g):
            z = _dot_nt(qj(j), kb)
            if j == k:
                z = jnp.where(causal, z, NEG)
            carry[j] = _softmax_step(z, vb, carry[j])
    for j in range(g):
        _, l, acc = carry[j]
        o_ref[j * t:(j + 1) * t, :] = (acc / l).astype(o_ref.dtype)


def _fox(proj, qa, ka):
    bsz, seq, _ = proj.shape
    g, t = ATT_G, ATT_T
    assert seq % (g * t) == 0
    qc, kc, vc = COL_BQ // HEAD_DIM, COL_BK // HEAD_DIM, COL_BV // HEAD_DIM
    return pl.pallas_call(
        _fox_kernel,
        grid=(bsz, B_HEADS, seq // (g * t)),
        in_specs=[
            pl.BlockSpec((None, g * t, HEAD_DIM), lambda b, h, i: (b, i, qc + h)),
            pl.BlockSpec((None, None, g * t, LANES), lambda b, h, i: (b, h, i, 0)),
            pl.BlockSpec((None, seq, HEAD_DIM), lambda b, h, i: (b, 0, kc + h)),
            pl.BlockSpec((None, None, seq, LANES), lambda b, h, i: (b, h, 0, 0)),
            pl.BlockSpec((None, seq, HEAD_DIM), lambda b, h, i: (b, 0, vc + h)),
        ],
        out_specs=pl.BlockSpec((None, g * t, HEAD_DIM), lambda b, h, i: (b, i, h)),
        out_shape=jax.ShapeDtypeStruct((bsz, seq, B_W), BF16),
        scratch_shapes=[pltpu.VMEM((g * t, HEAD_DIM + LANES), BF16)],
        compiler_params=_params(("parallel", "parallel", "arbitrary")),
        name="fox",
    )(proj, qa, proj, ka, proj)


def _swa_kernel(sink_ref, q_ref, kp_ref, kc_ref, vp_ref, vc_ref, tab_ref, o_ref):
    w = WINDOW
    dh = D_HEAD_DIM
    grp = D_Q_HEADS // D_KV_HEADS
    i = pl.program_id(1)
    q = q_ref[...]
    kcat = jnp.concatenate([kp_ref[...], kc_ref[...]], axis=0)
    vcat = jnp.concatenate([vp_ref[...], vc_ref[...]], axis=0)
    col = lax.broadcasted_iota(jnp.int32, (w, 2 * w), 1)
    keep = col >= jnp.where(i > 0, 0, w)
    outs = []
    for h in range(D_Q_HEADS):
        g = h // grp
        s = _dot_nt(q[:, h * dh:(h + 1) * dh], kcat[:, g * dh:(g + 1) * dh])
        s = jnp.where(keep, s + tab_ref[h], NEG)
        sink = sink_ref[h]
        m = jnp.maximum(jnp.max(s, axis=-1, keepdims=True), sink)
        p = jnp.exp(s - m)
        den = jnp.sum(p, axis=-1, keepdims=True) + jnp.exp(sink - m)
        pv = jnp.dot(p.astype(BF16), vcat[:, g * dh:(g + 1) * dh], preferred_element_type=F32)
        outs.append(pv / den)
    o_ref[...] = jnp.concatenate(outs, axis=1).astype(o_ref.dtype)


def _swa(proj, sinks, tab):
    bsz, seq, _ = proj.shape
    w = WINDOW
    qc, kc, vc = COL_DQ // D_QW, COL_DK // D_KVW, COL_DV // D_KVW
    prev = lambda i: jnp.maximum(i - 1, 0)
    return pl.pallas_call(
        _swa_kernel,
        grid=(bsz, seq // w),
        in_specs=[
            pl.BlockSpec(memory_space=pltpu.SMEM),
            pl.BlockSpec((None, w, D_QW), lambda b, i: (b, i, qc)),
            pl.BlockSpec((None, w, D_KVW), lambda b, i: (b, prev(i), kc)),
            pl.BlockSpec((None, w, D_KVW), lambda b, i: (b, i, kc)),
            pl.BlockSpec((None, w, D_KVW), lambda b, i: (b, prev(i), vc)),
            pl.BlockSpec((None, w, D_KVW), lambda b, i: (b, i, vc)),
            pl.BlockSpec((D_Q_HEADS, w, 2 * w), lambda b, i: (0, 0, 0)),
        ],
        out_specs=pl.BlockSpec((None, w, D_QW), lambda b, i: (b, i, 0)),
        out_shape=jax.ShapeDtypeStruct((bsz, seq, D_QW), BF16),
        compiler_params=_params(("parallel", "parallel")),
        name="swa",
    )(sinks, proj, proj, proj, proj, proj, tab)


CONV_T = 512
CONV_HALO = 32
CONV_ROWS = 64


def _conv_kernel(a_ref, g_ref, ha_ref, hg_ref, w_ref, b_ref, lg_ref, lb_ref, o_ref, buf):
    t = a_ref.shape[0]
    halo = CONV_HALO
    hp = ha_ref[...].astype(F32) * jax.nn.sigmoid(hg_ref[...].astype(F32))
    buf[0:halo, :] = jnp.where(pl.program_id(1) > 0, hp, 0.0)
    buf[halo:halo + t, :] = a_ref[...].astype(F32) * jax.nn.sigmoid(g_ref[...].astype(F32))
    off = halo - (CONV_WIDTH - 1)
    for r in range(t // CONV_ROWS):
        base = r * CONV_ROWS + off
        acc = jnp.broadcast_to(b_ref[...], (CONV_ROWS, CONV_CH))
        for k in range(CONV_WIDTH):
            acc = acc + w_ref[k:k + 1, :] * buf[base + k:base + k + CONV_ROWS, :]
        mu = jnp.mean(acc, axis=-1, keepdims=True)
        d = acc - mu
        var = jnp.mean(d * d, axis=-1, keepdims=True)
        y = d * lax.rsqrt(var + LN_EPS) * lg_ref[...] + lb_ref[...]
        o_ref[r * CONV_ROWS:(r + 1) * CONV_ROWS, :] = (y * jax.nn.sigmoid(y)).astype(o_ref.dtype)


def _conv(proj, conv_w, conv_b, ln_g, ln_b):
    bsz, seq, _ = proj.shape
    t = min(CONV_T, seq)
    ca, cg = COL_CU // CONV_CH, COL_CU // CONV_CH + 1
    hb = t // CONV_HALO
    prev = lambda i: jnp.maximum(i * hb - 1, 0)
    row = lambda v: v.reshape(1, CONV_CH)
    return pl.pallas_call(
        _conv_kernel,
        grid=(bsz, seq // t),
        in_specs=[
            pl.BlockSpec((None, t, CONV_CH), lambda b, i: (b, i, ca)),
            pl.BlockSpec((None, t, CONV_CH), lambda b, i: (b, i, cg)),
            pl.BlockSpec((None, CONV_HALO, CONV_CH), lambda b, i: (b, prev(i), ca)),
            pl.BlockSpec((None, CONV_HALO, CONV_CH), lambda b, i: (b, prev(i), cg)),
            pl.BlockSpec((CONV_WIDTH, CONV_CH), lambda b, i: (0, 0)),
            pl.BlockSpec((1, CONV_CH), lambda b, i: (0, 0)),
            pl.BlockSpec((1, CONV_CH), lambda b, i: (0, 0)),
            pl.BlockSpec((1, CONV_CH), lambda b, i: (0, 0)),
        ],
        out_specs=pl.BlockSpec((None, t, CONV_CH), lambda b, i: (b, i, 0)),
        out_shape=jax.ShapeDtypeStruct((bsz, seq, CONV_CH), BF16),
        scratch_shapes=[pltpu.VMEM((CONV_HALO + t, CONV_CH), F32)],
        compiler_params=_params(("parallel", "parallel")),
        name="conv",
    )(proj, proj, proj, proj, conv_w, row(conv_b), row(ln_g), row(ln_b))


def _merge_kernel(ya_ref, yb_ref, yc_ref, yd_ref, gates_ref, x_ref, gt_ref, gp_ref,
                  wa_ref, wb_ref, wc_ref, wd_ref, wo_ref, o_ref):
    d = x_ref.shape[1]
    y = None
    for i, (br, w) in enumerate(((ya_ref, wa_ref), (yb_ref, wb_ref), (yc_ref, wc_ref), (yd_ref, wd_ref))):
        t = jnp.dot(br[...], w[...], preferred_element_type=F32) * gates_ref[:, i * d:(i + 1) * d].astype(F32)
        y = t if y is None else y + t
    z = jnp.dot(y.astype(BF16), wo_ref[...], preferred_element_type=F32)
    o_ref[...] = x_ref[...] + gt_ref[...] * (_rms(z) * gp_ref[...])


def _merge(ya, yb, yc, yd, gates, x, gt, g_post, wa, wb, wc, wd, wo, *, tm):
    bsz, seq, d = x.shape
    tok = lambda width: pl.BlockSpec((None, tm, width), lambda b, i: (b, i, 0))
    full = lambda a: pl.BlockSpec(a.shape, lambda b, i: (0,) * a.ndim)
    return pl.pallas_call(
        _merge_kernel,
        grid=(bsz, seq // tm),
        in_specs=[tok(A_W), tok(B_W), tok(CONV_CH), tok(D_QW), tok(N_BRANCHES * d), tok(d),
                  pl.BlockSpec((None, 1, d), lambda b, i: (b, 0, 0)),
                  pl.BlockSpec((1, d), lambda b, i: (0, 0)),
                  full(wa), full(wb), full(wc), full(wd), full(wo)],
        out_specs=tok(d),
        out_shape=jax.ShapeDtypeStruct((bsz, seq, d), F32),
        compiler_params=_params(("parallel", "parallel")),
        name="merge",
    )(ya, yb, yc, yd, gates, x, gt, g_post.reshape(1, d), wa, wb, wc, wd, wo)


def _ffn_kernel(x_ref, g_ref, sc_ref, sh_ref, gt_ref, gp_ref, wg_ref, wu_ref, wd_ref, o_ref, h_scr, acc_scr):
    f = pl.program_id(2)

    @pl.when(f == 0)
    def _():
        h_scr[...] = _modulated_norm(x_ref[...], g_ref[...], sc_ref[...], sh_ref[...]).astype(BF16)
        acc_scr[...] = jnp.zeros_like(acc_scr)

    h = h_scr[...]
    g = jnp.dot(h, wg_ref[...], preferred_element_type=F32)
    u = jnp.dot(h, wu_ref[...], preferred_element_type=F32)
    a = ((g * jax.nn.sigmoid(g)) * u).astype(BF16)
    acc_scr[...] += jnp.dot(a, wd_ref[...], preferred_element_type=F32)

    @pl.when(f == pl.num_programs(2) - 1)
    def _():
        o_ref[...] = x_ref[...] + gt_ref[...] * (_rms(acc_scr[...]) * gp_ref[...])


def _ffn(x, g_pre, sc, sh, gt, g_post, wg, wu, wd, *, tm, tf):
    bsz, seq, d = x.shape
    dff = wg.shape[1]
    vec = lambda: pl.BlockSpec((None, 1, d), lambda b, i, f: (b, 0, 0))
    par = lambda: pl.BlockSpec((1, d), lambda b, i, f: (0, 0))
    return pl.pallas_call(
        _ffn_kernel,
        grid=(bsz, seq // tm, dff // tf),
        in_specs=[
            pl.BlockSpec((None, tm, d), lambda b, i, f: (b, i, 0)),
            par(), vec(), vec(), vec(), par(),
            pl.BlockSpec((d, tf), lambda b, i, f: (0, f)),
            pl.BlockSpec((d, tf), lambda b, i, f: (0, f)),
            pl.BlockSpec((tf, d), lambda b, i, f: (f, 0)),
        ],
        out_specs=pl.BlockSpec((None, tm, d), lambda b, i, f: (b, i, 0)),
        out_shape=jax.ShapeDtypeStruct((bsz, seq, d), F32),
        scratch_shapes=[pltpu.VMEM((tm, d), BF16), pltpu.VMEM((tm, d), F32)],
        compiler_params=_params(("parallel", "parallel", "arbitrary")),
        name="ffn",
    )(x, g_pre.reshape(1, d), sc, sh, gt, g_post.reshape(1, d), wg, wu, wd)


def _pack_in_proj(w):
    d = w.shape[0]
    splits = (A_W, A_W, A_W, B_W, B_W, B_W, B_HEADS, 2 * CONV_CH, D_QW, D_KVW, D_KVW, N_BRANCHES * d)
    offs = np.cumsum(splits)[:-1].tolist()
    aq, ak, av, bq, bk, bv, bf, cu, dq, dk, dv, gl = jnp.split(w, offs, axis=1)
    s2 = HEAD_DIM ** -0.5 * LOG2E
    main = jnp.concatenate(
        [aq * s2, ak, av, bq * s2, bk, bv, cu, dq * D_HEAD_DIM ** -0.5, dk, dv,
         jnp.zeros((d, MAIN_W_PAD - MAIN_W), w.dtype)], axis=1).astype(BF16)
    side = jnp.concatenate([bf, jnp.zeros((d, LANES - B_HEADS), w.dtype)], axis=1).astype(BF16)
    return main, side, gl.astype(BF16)


def kernel(x, c, rel_bias, w_mod, b_mod, mix_norm_pre, mix_norm_post, w_in, fox_bias, conv_w, conv_b, conv_ln_g, conv_ln_b, sinks, w_branch_a, w_branch_b, w_branch_c, w_branch_d, w_out, ffn_norm_pre, ffn_norm_post, w_ffn_gate, w_ffn_up, w_ffn_down):
    depth = w_mod.shape[0]
    bsz, seq, d = x.shape
    tm = min(1024, seq)
    tm_small = min(256, seq)
    tm_ffn = min(512, seq)

    mod = _modulation(c, w_mod, b_mod)
    tab_a = _bias_tables(rel_bias[:, :A_HEADS], _moba_buckets(), scale=LOG2E)
    tab_d = _bias_tables(rel_bias[:, A_HEADS:], _swa_buckets(), scale=1.0)[:, 0]

    for l in range(depth):
        sh_m, sc_m, gt_m, sh_f, sc_f, gt_f = [mod[l, :, None, i * d:(i + 1) * d] for i in range(6)]
        w_main, w_side, w_gl = _pack_in_proj(w_in[l])

        proj, fox_raw = _norm_proj(x, mix_norm_pre[l], sc_m, sh_m, w_main, w_side, act="none", tm=tm, tn=MAIN_TN)
        gates = _norm_proj(x, mix_norm_pre[l], sc_m, sh_m, w_gl, act="sigmoid", tm=tm, tn=1024)

        fox_qa, fox_ka = _fox_gate(fox_raw, fox_bias[l], t=min(512, seq))

        ya = _moba(proj, tab_a)
        yb = _fox(proj, fox_qa, fox_ka)
        yc = _conv(proj, conv_w[l], conv_b[l], conv_ln_g[l], conv_ln_b[l])
        yd = _swa(proj, sinks[l], tab_d)

        x = _merge(ya, yb, yc, yd, gates, x, gt_m, mix_norm_post[l],
                   w_branch_a[l].astype(BF16), w_branch_b[l].astype(BF16), w_branch_c[l].astype(BF16),
                   w_branch_d[l].astype(BF16), w_out[l].astype(BF16), tm=tm_small)
        x = _ffn(x, ffn_norm_pre[l], sc_f, sh_f, gt_f, ffn_norm_post[l],
                 w_ffn_gate[l].astype(BF16), w_ffn_up[l].astype(BF16), w_ffn_down[l].astype(BF16),
                 tm=tm_ffn, tf=512)
    return x
```

```python
import functools
import math

import jax
import jax.numpy as jnp
import numpy as np
from jax import lax
from jax.experimental import pallas as pl
from jax.experimental.pallas import tpu as pltpu

F32 = jnp.float32
BF16 = jnp.bfloat16
HIGHEST = lax.Precision.HIGHEST

HEAD_DIM = 128
A_HEADS = 4
MOBA_BLOCK = 256
MOBA_TOPK = 3
B_HEADS = 4
CONV_CH = 512
CONV_WIDTH = 31
D_Q_HEADS = 8
D_KV_HEADS = 2
D_HEAD_DIM = 64
WINDOW = 128
N_BUCKETS = 32
MAX_DISTANCE = 1024
N_BRANCHES = 4
RMS_EPS = 1e-6
LN_EPS = 1e-5

A_W = A_HEADS * HEAD_DIM
B_W = B_HEADS * HEAD_DIM
D_QW = D_Q_HEADS * D_HEAD_DIM
D_KVW = D_KV_HEADS * D_HEAD_DIM

LANES = 128
SUBLANES = 8
VMEM_LIMIT = 56 * 1024 * 1024

NEG = -1e30
LOG2E = math.log2(math.e)
ATT_G = 4
ATT_T = MOBA_BLOCK
ATT_U = 4

COL_AQ, COL_AK, COL_AV = 0, A_W, 2 * A_W
COL_BQ, COL_BK, COL_BV = 3 * A_W, 3 * A_W + B_W, 3 * A_W + 2 * B_W
COL_CU = 3 * A_W + 3 * B_W
COL_DQ = COL_CU + 2 * CONV_CH
COL_DK = COL_DQ + D_QW
COL_DV = COL_DK + D_KVW
MAIN_W = COL_DV + D_KVW
MAIN_TN = 1024
MAIN_W_PAD = -(-MAIN_W // MAIN_TN) * MAIN_TN
MOBA_FAR = -(-(MAX_DISTANCE + MOBA_BLOCK - 1) // MOBA_BLOCK)
VOID_COL = LANES - 1
assert ATT_G % ATT_U == 0 and (MOBA_FAR - 1) % ATT_U == 0


def _params(sem, vmem=VMEM_LIMIT):
    return pltpu.CompilerParams(dimension_semantics=sem, vmem_limit_bytes=vmem)


def _t5_bucket(dist):
    max_exact = N_BUCKETS // 2
    d = jnp.maximum(dist, 0)
    log_ratio = jnp.log(jnp.maximum(d, 1).astype(jnp.float32) / max_exact) / math.log(MAX_DISTANCE / max_exact)
    large = max_exact + (log_ratio * (N_BUCKETS - max_exact)).astype(jnp.int32)
    large = jnp.minimum(large, N_BUCKETS - 1)
    return jnp.where(d < max_exact, d, large)


def _rms(y):
    return y * lax.rsqrt(jnp.mean(y * y, axis=-1, keepdims=True) + RMS_EPS)


def _modulated_norm(x, g, sc, sh):
    return (_rms(x) * g) * (1.0 + sc) + sh


MOD_TN = 512
MOD_KC = 256


def _mod_kernel(ct_ref, w_ref, b_ref, o_ref):
    d, nb = ct_ref.shape
    tn = w_ref.shape[1]
    ct = ct_ref[...]
    ca = ct * jax.nn.sigmoid(ct)
    rows = []
    for r in range(nb):
        acc = jnp.zeros((1, tn), F32)
        for kc in range(d // MOD_KC):
            sl = slice(kc * MOD_KC, (kc + 1) * MOD_KC)
            acc = acc + jnp.sum(w_ref[sl, :] * ca[sl, r:r + 1], axis=0, keepdims=True)
        rows.append(acc)
    o_ref[...] = jnp.concatenate(rows, axis=0) + b_ref[...]


def _modulation(c, w_mod, b_mod):
    depth, d, n = w_mod.shape
    bsz = c.shape[0]
    return pl.pallas_call(
        _mod_kernel,
        grid=(depth, n // MOD_TN),
        in_specs=[
            pl.BlockSpec((d, bsz), lambda l, j: (0, 0)),
            pl.BlockSpec((None, d, MOD_TN), lambda l, j: (l, 0, j)),
            pl.BlockSpec((None, 1, MOD_TN), lambda l, j: (l, 0, j)),
        ],
        out_specs=pl.BlockSpec((None, bsz, MOD_TN), lambda l, j: (l, 0, j)),
        out_shape=jax.ShapeDtypeStruct((depth, bsz, n), F32),
        compiler_params=_params(("parallel", "parallel")),
        name="mod",
    )(c.T, w_mod, b_mod.reshape(depth, 1, n))


def _bias_table_kernel(rb_ref, bucket_ref, o_ref, *, scale):
    h = pl.program_id(0)
    b = bucket_ref[...]
    acc = jnp.where(b < 0, NEG, 0.0).astype(F32)
    for u in range(N_BUCKETS):
        acc = jnp.where(b == u, rb_ref[h, u] * scale, acc)
    o_ref[...] = acc


def _bias_tables(rel_bias_heads, bucket, *, scale):
    nh = rel_bias_heads.shape[1]
    nt, r, c = bucket.shape
    return pl.pallas_call(
        functools.partial(_bias_table_kernel, scale=scale),
        grid=(nh, nt),
        in_specs=[
            pl.BlockSpec(memory_space=pltpu.SMEM),
            pl.BlockSpec((None, r, c), lambda h, t: (t, 0, 0)),
        ],
        out_specs=pl.BlockSpec((None, None, r, c), lambda h, t: (h, t, 0, 0)),
        out_shape=jax.ShapeDtypeStruct((nh, nt, r, c), F32),
        compiler_params=_params(("parallel", "parallel")),
        name="bias_table",
    )(rel_bias_heads.T, bucket)


def _moba_buckets():
    blk = MOBA_BLOCK
    i = jnp.arange(blk)[:, None]
    j = jnp.arange(blk)[None, :]
    tabs = []
    for delta in range(MOBA_FAR):
        dist = delta * blk + i - j
        tabs.append(jnp.where(dist >= 0, _t5_bucket(dist), -1))
    tabs.append(_t5_bucket(jnp.full((blk, blk), MAX_DISTANCE, jnp.int32)))
    return jnp.stack(tabs).astype(jnp.int32)


def _swa_buckets():
    qi = jnp.arange(WINDOW)[:, None]
    kj = jnp.arange(2 * WINDOW)[None, :]
    dist = qi + WINDOW - kj
    in_win = (dist >= 0) & (dist < WINDOW)
    return jnp.where(in_win, _t5_bucket(dist), -1).astype(jnp.int32)[None]


def _norm_proj_kernel(x_ref, g_ref, sc_ref, sh_ref, w_ref, *rest, act, with_side):
    if with_side:
        ws_ref, o_ref, os_ref, h_scr = rest
    else:
        o_ref, h_scr = rest

    @pl.when(pl.program_id(2) == 0)
    def _():
        h = _modulated_norm(x_ref[...], g_ref[...], sc_ref[...], sh_ref[...]).astype(BF16)
        h_scr[...] = h
        if with_side:
            os_ref[...] = jnp.dot(h, ws_ref[...], preferred_element_type=F32)

    acc = jnp.dot(h_scr[...], w_ref[...], preferred_element_type=F32)
    if act == "sigmoid":
        acc = jax.nn.sigmoid(acc)
    o_ref[...] = acc.astype(o_ref.dtype)


def _norm_proj(x, g, sc, sh, w, w_side=None, *, act, tm, tn):
    bsz, seq, d = x.shape
    n = w.shape[1]
    with_side = w_side is not None
    in_specs = [
        pl.BlockSpec((None, tm, d), lambda b, i, j: (b, i, 0)),
        pl.BlockSpec((1, d), lambda b, i, j: (0, 0)),
        pl.BlockSpec((None, 1, d), lambda b, i, j: (b, 0, 0)),
        pl.BlockSpec((None, 1, d), lambda b, i, j: (b, 0, 0)),
        pl.BlockSpec((d, tn), lambda b, i, j: (0, j)),
    ]
    out_specs = [pl.BlockSpec((None, tm, tn), lambda b, i, j: (b, i, j))]
    out_shape = [jax.ShapeDtypeStruct((bsz, seq, n), BF16)]
    args = [x, g.reshape(1, d), sc, sh, w]
    if with_side:
        ns = w_side.shape[1]
        in_specs.append(pl.BlockSpec((d, ns), lambda b, i, j: (0, 0)))
        out_specs.append(pl.BlockSpec((None, tm, ns), lambda b, i, j: (b, i, 0)))
        out_shape.append(jax.ShapeDtypeStruct((bsz, seq, ns), F32))
        args.append(w_side)
    outs = pl.pallas_call(
        functools.partial(_norm_proj_kernel, act=act, with_side=with_side),
        grid=(bsz, seq // tm, n // tn),
        in_specs=in_specs,
        out_specs=out_specs,
        out_shape=out_shape,
        scratch_shapes=[pltpu.VMEM((tm, d), BF16)],
        compiler_params=_params(("parallel", "parallel", "arbitrary")),
        name="norm_proj_" + act,
    )(*args)
    return outs if with_side else outs[0]


def _split3(c):
    hi = c.astype(BF16).astype(F32)
    mid = (c - hi).astype(BF16).astype(F32)
    lo = (c - hi - mid).astype(BF16).astype(F32)
    return hi, mid, lo


def _fox_gate_kernel(x_ref, fb_ref, qa_ref, ka_ref, carry_scr):
    t = x_ref.shape[0]

    @pl.when(pl.program_id(1) == 0)
    def _():
        carry_scr[...] = jnp.zeros_like(carry_scr)

    lf = jax.nn.log_sigmoid(x_ref[...] + fb_ref[...])
    tri = (lax.broadcasted_iota(jnp.int32, (t, t), 0) >= lax.broadcasted_iota(jnp.int32, (t, t), 1)).astype(F32)
    c = jnp.dot(tri, lf, precision=HIGHEST, preferred_element_type=F32) + carry_scr[...]
    carry_scr[...] = c[t - 1:t, :]
    lane = lax.broadcasted_iota(jnp.int32, (t, LANES), 1)
    for h in range(B_HEADS):
        hi, mid, lo = _split3(jnp.broadcast_to(c[:, h:h + 1], (t, LANES)) * LOG2E)
        qa = jnp.where(lane < 3, 1.0, jnp.where(lane == 3, hi, jnp.where(lane == 4, mid, jnp.where(lane == 5, lo, 0.0))))
        ka = jnp.where(lane == 0, -hi, jnp.where(lane == 1, -mid, jnp.where(lane == 2, -lo, jnp.where(lane < 6, 1.0, 0.0))))
        qa_ref[h] = qa.astype(BF16)
        ka_ref[h] = ka.astype(BF16)


def _fox_gate(raw, fox_b, *, t):
    bsz, seq, _ = raw.shape
    fb = jnp.zeros((1, LANES), F32).at[0, :B_HEADS].set(fox_b)
    aug = pl.BlockSpec((None, B_HEADS, t, LANES), lambda b, i: (b, 0, i, 0))
    return pl.pallas_call(
        _fox_gate_kernel,
        grid=(bsz, seq // t),
        in_specs=[
            pl.BlockSpec((None, t, LANES), lambda b, i: (b, i, 0)),
            pl.BlockSpec((1, LANES), lambda b, i: (0, 0)),
        ],
        out_specs=[aug, aug],
        out_shape=[jax.ShapeDtypeStruct((bsz, B_HEADS, seq, LANES), BF16)] * 2,
        scratch_shapes=[pltpu.VMEM((1, LANES), F32)],
        compiler_params=_params(("parallel", "arbitrary")),
        name="fox_gate",
    )(raw, fb)


def _softmax_step(z, vb, carry):
    m, l, acc = carry
    m_new = jnp.maximum(m, jnp.max(z, axis=-1, keepdims=True))
    alpha = jnp.exp2(m - m_new)
    p = jnp.exp2(z - m_new)
    l = alpha * l + jnp.sum(p, axis=-1, keepdims=True)
    acc = alpha * acc + jnp.dot(p.astype(BF16), vb, preferred_element_type=F32)
    return m_new, l, acc


def _softmax_init(tq, dh):
    return (jnp.full((tq, 1), NEG, F32), jnp.zeros((tq, 1), F32), jnp.zeros((tq, dh), F32))


_NT = (((1,), (1,)), ((), ()))


def _dot_nt(a, b):
    return lax.dot_general(a, b, _NT, preferred_element_type=F32)


def _moba_kernel(cfar_ref, q_ref, k_ref, v_ref, tab_ref, o_ref, kmean_scr, sc_scr, qaug_scr):
    g, t = ATT_G, ATT_T
    seq, dh = k_ref.shape
    nblk = seq // t
    rows = g * t
    i = pl.program_id(2)
    base = i * g

    @pl.when(i == 0)
    def _():
        r = lax.broadcasted_iota(jnp.int32, (nblk, seq), 0)
        c = lax.broadcasted_iota(jnp.int32, (nblk, seq), 1)
        avg = jnp.where((c >= r * t) & (c < (r + 1) * t), 1.0 / t, 0.0).astype(BF16)
        kmean_scr[...] = jnp.dot(avg, k_ref[...], preferred_element_type=F32)

    q = q_ref[...]
    qaug_scr[:, :dh] = q
    sc = lax.dot_general(kmean_scr[...], q.astype(F32), _NT, precision=HIGHEST, preferred_element_type=F32)
    row = lax.broadcasted_iota(jnp.int32, (nblk, rows), 0)
    own = base + lax.broadcasted_iota(jnp.int32, (nblk, rows), 1) // t
    past = row < own
    sc = jnp.where(past, sc, -jnp.inf)
    sc_scr[...] = sc

    def rank_body(m, rank):
        sm = sc_scr[pl.ds(m, 1), :]
        return rank + jnp.where(sm > sc, 1.0, jnp.where(sm == sc, jnp.where(row > m, 1.0, 0.0), 0.0))

    rank = lax.fori_loop(0, base + g - 1, rank_body, jnp.zeros((nblk, rows), F32))
    pen = jnp.where(past, jnp.where(rank < MOBA_TOPK, 0.0, NEG), jnp.where(row == own, 0.0, NEG))
    void = jnp.where(lax.broadcasted_iota(jnp.int32, (SUBLANES, rows), 0) == SUBLANES - 1, NEG, 0.0)
    pen = jnp.concatenate([pen, jnp.zeros((LANES - SUBLANES - nblk, rows), F32), void], axis=0)
    qaug_scr[:, dh:] = pen.T.astype(BF16)

    lane = lax.broadcasted_iota(jnp.int32, (t, LANES), 1)

    def kv(n, col):
        start = pl.multiple_of(n * t, t)
        onehot = jnp.where(lane == col, 1.0, 0.0).astype(BF16)
        return jnp.concatenate([k_ref[pl.ds(start, t), :], onehot], axis=1), v_ref[pl.ds(start, t), :]

    def qj(j):
        return qaug_scr[j * t:(j + 1) * t, :]

    def far_body(grp, carry):
        carry = list(carry)
        for u in range(ATT_U):
            n = grp * ATT_U + u
            kb, vb = kv(n, n)
            for j in range(g):
                carry[j] = _softmax_step(_dot_nt(qj(j), kb), vb, carry[j])
        return tuple(carry)

    carry = list(lax.fori_loop(0, jnp.maximum(base - (MOBA_FAR - 1), 0) // ATT_U, far_body,
                               tuple(_softmax_init(t, dh) for _ in range(g))))
    cfar = cfar_ref[pl.program_id(1)]
    for k in range(-(MOBA_FAR - 1), g):
        n = base + k
        if k < 0:
            kb, vb = kv(jnp.maximum(n, 0), jnp.where(n >= 0, n, VOID_COL))
        else:
            kb, vb = kv(n, n)
        for j in range(max(k, 0), g):
            delta = j - k
            z = _dot_nt(qj(j), kb)
            if delta < MOBA_FAR:
                if delta == MOBA_FAR - 1:
                    m, l, acc = carry[j]
                    carry[j] = (m + cfar, l, acc)
                z = z + tab_ref[delta]
            carry[j] = _softmax_step(z, vb, carry[j])
    for j in range(g):
        _, l, acc = carry[j]
        o_ref[j * t:(j + 1) * t, :] = (acc / l).astype(o_ref.dtype)


def _moba(proj, tab):
    bsz, seq, _ = proj.shape
    g, t = ATT_G, ATT_T
    nblk = seq // t
    assert seq % (g * t) == 0 and nblk <= LANES - SUBLANES and nblk % SUBLANES == 0
    qc, kc, vc = COL_AQ // HEAD_DIM, COL_AK // HEAD_DIM, COL_AV // HEAD_DIM
    cfar = tab[:, MOBA_FAR, 0, 0]
    return pl.pallas_call(
        _moba_kernel,
        grid=(bsz, A_HEADS, nblk // g),
        in_specs=[
            pl.BlockSpec(memory_space=pltpu.SMEM),
            pl.BlockSpec((None, g * t, HEAD_DIM), lambda b, h, i: (b, i, qc + h)),
            pl.BlockSpec((None, seq, HEAD_DIM), lambda b, h, i: (b, 0, kc + h)),
            pl.BlockSpec((None, seq, HEAD_DIM), lambda b, h, i: (b, 0, vc + h)),
            pl.BlockSpec((None, MOBA_FAR, t, t), lambda b, h, i: (h, 0, 0, 0)),
        ],
        out_specs=pl.BlockSpec((None, g * t, HEAD_DIM), lambda b, h, i: (b, i, h)),
        out_shape=jax.ShapeDtypeStruct((bsz, seq, A_W), BF16),
        scratch_shapes=[pltpu.VMEM((nblk, HEAD_DIM), F32), pltpu.VMEM((nblk, g * t), F32),
                        pltpu.VMEM((g * t, HEAD_DIM + LANES), BF16)],
        compiler_params=_params(("parallel", "parallel", "arbitrary")),
        name="moba",
    )(cfar, proj, proj, proj, tab)


def _fox_kernel(q_ref, qa_ref, k_ref, ka_ref, v_ref, o_ref, qaug_scr):
    g, t = ATT_G, ATT_T
    dh = q_ref.shape[1]
    base = pl.program_id(2) * g
    qaug_scr[:, :dh] = q_ref[...]
    qaug_scr[:, dh:] = qa_ref[...]

    def kv(n):
        start = pl.multiple_of(n * t, t)
        return (jnp.concatenate([k_ref[pl.ds(start, t), :], ka_ref[pl.ds(start, t), :]], axis=1),
                v_ref[pl.ds(start, t), :])

    def qj(j):
        return qaug_scr[j * t:(j + 1) * t, :]

    def body(grp, carry):
        carry = list(carry)
        for u in range(ATT_U):
            kb, vb = kv(grp * ATT_U + u)
            for j in range(g):
                carry[j] = _softmax_step(_dot_nt(qj(j), kb), vb, carry[j])
        return tuple(carry)

    carry = list(lax.fori_loop(0, base // ATT_U, body, tuple(_softmax_init(t, dh) for _ in range(g))))
    causal = lax.broadcasted_iota(jnp.int32, (t, t), 0) >= lax.broadcasted_iota(jnp.int32, (t, t), 1)
    for k in range(g):
        kb, vb = kv(base + k)
        for j in range(k, g):
            z = _dot_nt(qj(j), kb)
            if j == k:
                z = jnp.where(causal, z, NEG)
            carry[j] = _softmax_step(z, vb, carry[j])
    for j in range(g):
        _, l, acc = carry[j]
        o_ref[j * t:(j + 1) * t, :] = (acc / l).astype(o_ref.dtype)


def _fox(proj, qa, ka):
    bsz, seq, _ = proj.shape
    g, t = ATT_G, ATT_T
    assert seq % (g * t) == 0
    qc, kc, vc = COL_BQ // HEAD_DIM, COL_BK // HEAD_DIM, COL_BV // HEAD_DIM
    return pl.pallas_call(
        _fox_kernel,
        grid=(bsz, B_HEADS, seq // (g * t)),
        in_specs=[
            pl.BlockSpec((None, g * t, HEAD_DIM), lambda b, h, i: (b, i, qc + h)),
            pl.BlockSpec((None, None, g * t, LANES), lambda b, h, i: (b, h, i, 0)),
            pl.BlockSpec((None, seq, HEAD_DIM), lambda b, h, i: (b, 0, kc + h)),
            pl.BlockSpec((None, None, seq, LANES), lambda b, h, i: (b, h, 0, 0)),
            pl.BlockSpec((None, seq, HEAD_DIM), lambda b, h, i: (b, 0, vc + h)),
        ],
        out_specs=pl.BlockSpec((None, g * t, HEAD_DIM), lambda b, h, i: (b, i, h)),
        out_shape=jax.ShapeDtypeStruct((bsz, seq, B_W), BF16),
        scratch_shapes=[pltpu.VMEM((g * t, HEAD_DIM + LANES), BF16)],
        compiler_params=_params(("parallel", "parallel", "arbitrary")),
        name="fox",
    )(proj, qa, proj, ka, proj)


def _swa_kernel(sink_ref, q_ref, kp_ref, kc_ref, vp_ref, vc_ref, tab_ref, o_ref):
    w = WINDOW
    dh = D_HEAD_DIM
    grp = D_Q_HEADS // D_KV_HEADS
    i = pl.program_id(1)
    q = q_ref[...]
    kcat = jnp.concatenate([kp_ref[...], kc_ref[...]], axis=0)
    vcat = jnp.concatenate([vp_ref[...], vc_ref[...]], axis=0)
    col = lax.broadcasted_iota(jnp.int32, (w, 2 * w), 1)
    keep = col >= jnp.where(i > 0, 0, w)
    outs = []
    for h in range(D_Q_HEADS):
        g = h // grp
        s = _dot_nt(q[:, h * dh:(h + 1) * dh], kcat[:, g * dh:(g + 1) * dh])
        s = jnp.where(keep, s + tab_ref[h], NEG)
        sink = sink_ref[h]
        m = jnp.maximum(jnp.max(s, axis=-1, keepdims=True), sink)
        p = jnp.exp(s - m)
        den = jnp.sum(p, axis=-1, keepdims=True) + jnp.exp(sink - m)
        pv = jnp.dot(p.astype(BF16), vcat[:, g * dh:(g + 1) * dh], preferred_element_type=F32)
        outs.append(pv / den)
    o_ref[...] = jnp.concatenate(outs, axis=1).astype(o_ref.dtype)


def _swa(proj, sinks, tab):
    bsz, seq, _ = proj.shape
    w = WINDOW
    qc, kc, vc = COL_DQ // D_QW, COL_DK // D_KVW, COL_DV // D_KVW
    prev = lambda i: jnp.maximum(i - 1, 0)
    return pl.pallas_call(
        _swa_kernel,
        grid=(bsz, seq // w),
        in_specs=[
            pl.BlockSpec(memory_space=pltpu.SMEM),
            pl.BlockSpec((None, w, D_QW), lambda b, i: (b, i, qc)),
            pl.BlockSpec((None, w, D_KVW), lambda b, i: (b, prev(i), kc)),
            pl.BlockSpec((None, w, D_KVW), lambda b, i: (b, i, kc)),
            pl.BlockSpec((None, w, D_KVW), lambda b, i: (b, prev(i), vc)),
            pl.BlockSpec((None, w, D_KVW), lambda b, i: (b, i, vc)),
            pl.BlockSpec((D_Q_HEADS, w, 2 * w), lambda b, i: (0, 0, 0)),
        ],
        out_specs=pl.BlockSpec((None, w, D_QW), lambda b, i: (b, i, 0)),
        out_shape=jax.ShapeDtypeStruct((bsz, seq, D_QW), BF16),
        compiler_params=_params(("parallel", "parallel")),
        name="swa",
    )(sinks, proj, proj, proj, proj, proj, tab)


CONV_T = 512
CONV_HALO = 32
CONV_ROWS = 64


def _conv_kernel(a_ref, g_ref, ha_ref, hg_ref, w_ref, b_ref, lg_ref, lb_ref, o_ref, buf):
    t = a_ref.shape[0]
    halo = CONV_HALO
    hp = ha_ref[...].astype(F32) * jax.nn.sigmoid(hg_ref[...].astype(F32))
    buf[0:halo, :] = jnp.where(pl.program_id(1) > 0, hp, 0.0)
    buf[halo:halo + t, :] = a_ref[...].astype(F32) * jax.nn.sigmoid(g_ref[...].astype(F32))
    off = halo - (CONV_WIDTH - 1)
    for r in range(t // CONV_ROWS):
        base = r * CONV_ROWS + off
        acc = jnp.broadcast_to(b_ref[...], (CONV_ROWS, CONV_CH))
        for k in range(CONV_WIDTH):
            acc = acc + w_ref[k:k + 1, :] * buf[base + k:base + k + CONV_ROWS, :]
        mu = jnp.mean(acc, axis=-1, keepdims=True)
        d = acc - mu
        var = jnp.mean(d * d, axis=-1, keepdims=True)
        y = d * lax.rsqrt(var + LN_EPS) * lg_ref[...] + lb_ref[...]
        o_ref[r * CONV_ROWS:(r + 1) * CONV_ROWS, :] = (y * jax.nn.sigmoid(y)).astype(o_ref.dtype)


def _conv(proj, conv_w, conv_b, ln_g, ln_b):
    bsz, seq, _ = proj.shape
    t = min(CONV_T, seq)
    ca, cg = COL_CU // CONV_CH, COL_CU // CONV_CH + 1
    hb = t // CONV_HALO
    prev = lambda i: jnp.maximum(i * hb - 1, 0)
    row = lambda v: v.reshape(1, CONV_CH)
    return pl.pallas_call(
        _conv_kernel,
        grid=(bsz, seq // t),
        in_specs=[
            pl.BlockSpec((None, t, CONV_CH), lambda b, i: (b, i, ca)),
            pl.BlockSpec((None, t, CONV_CH), lambda b, i: (b, i, cg)),
            pl.BlockSpec((None, CONV_HALO, CONV_CH), lambda b, i: (b, prev(i), ca)),
            pl.BlockSpec((None, CONV_HALO, CONV_CH), lambda b, i: (b, prev(i), cg)),
            pl.BlockSpec((CONV_WIDTH, CONV_CH), lambda b, i: (0, 0)),
            pl.BlockSpec((1, CONV_CH), lambda b, i: (0, 0)),
            pl.BlockSpec((1, CONV_CH), lambda b, i: (0, 0)),
            pl.BlockSpec((1, CONV_CH), lambda b, i: (0, 0)),
        ],
        out_specs=pl.BlockSpec((None, t, CONV_CH), lambda b, i: (b, i, 0)),
        out_shape=jax.ShapeDtypeStruct((bsz, seq, CONV_CH), BF16),
        scratch_shapes=[pltpu.VMEM((CONV_HALO + t, CONV_CH), F32)],
        compiler_params=_params(("parallel", "parallel")),
        name="conv",
    )(proj, proj, proj, proj, conv_w, row(conv_b), row(ln_g), row(ln_b))


def _merge_kernel(ya_ref, yb_ref, yc_ref, yd_ref, gates_ref, x_ref, gt_ref, gp_ref,
                  wa_ref, wb_ref, wc_ref, wd_ref, wo_ref, o_ref):
    d = x_ref.shape[1]
    y = None
    for i, (br, w) in enumerate(((ya_ref, wa_ref), (yb_ref, wb_ref), (yc_ref, wc_ref), (yd_ref, wd_ref))):
        t = jnp.dot(br[...], w[...], preferred_element_type=F32) * gates_ref[:, i * d:(i + 1) * d].astype(F32)
        y = t if y is None else y + t
    z = jnp.dot(y.astype(BF16), wo_ref[...], preferred_element_type=F32)
    o_ref[...] = x_ref[...] + gt_ref[...] * (_rms(z) * gp_ref[...])


def _merge(ya, yb, yc, yd, gates, x, gt, g_post, wa, wb, wc, wd, wo, *, tm):
    bsz, seq, d = x.shape
    tok = lambda width: pl.BlockSpec((None, tm, width), lambda b, i: (b, i, 0))
    full = lambda a: pl.BlockSpec(a.shape, lambda b, i: (0,) * a.ndim)
    return pl.pallas_call(
        _merge_kernel,
        grid=(bsz, seq // tm),
        in_specs=[tok(A_W), tok(B_W), tok(CONV_CH), tok(D_QW), tok(N_BRANCHES * d), tok(d),
                  pl.BlockSpec((None, 1, d), lambda b, i: (b, 0, 0)),
                  pl.BlockSpec((1, d), lambda b, i: (0, 0)),
                  full(wa), full(wb), full(wc), full(wd), full(wo)],
        out_specs=tok(d),
        out_shape=jax.ShapeDtypeStruct((bsz, seq, d), F32),
        compiler_params=_params(("parallel", "parallel")),
        name="merge",
    )(ya, yb, yc, yd, gates, x, gt, g_post.reshape(1, d), wa, wb, wc, wd, wo)


def _ffn_kernel(x_ref, g_ref, sc_ref, sh_ref, gt_ref, gp_ref, wg_ref, wu_ref, wd_ref, o_ref, h_scr, acc_scr):
    f = pl.program_id(2)

    @pl.when(f == 0)
    def _():
        h_scr[...] = _modulated_norm(x_ref[...], g_ref[...], sc_ref[...], sh_ref[...]).astype(BF16)
        acc_scr[...] = jnp.zeros_like(acc_scr)

    h = h_scr[...]
    g = jnp.dot(h, wg_ref[...], preferred_element_type=F32)
    u = jnp.dot(h, wu_ref[...], preferred_element_type=F32)
    a = ((g * jax.nn.sigmoid(g)) * u).astype(BF16)
    acc_scr[...] += jnp.dot(a, wd_ref[...], preferred_element_type=F32)

    @pl.when(f == pl.num_programs(2) - 1)
    def _():
        o_ref[...] = x_ref[...] + gt_ref[...] * (_rms(acc_scr[...]) * gp_ref[...])


def _ffn(x, g_pre, sc, sh, gt, g_post, wg, wu, wd, *, tm, tf):
    bsz, seq, d = x.shape
    dff = wg.shape[1]
    vec = lambda: pl.BlockSpec((None, 1, d), lambda b, i, f: (b, 0, 0))
    par = lambda: pl.BlockSpec((1, d), lambda b, i, f: (0, 0))
    return pl.pallas_call(
        _ffn_kernel,
        grid=(bsz, seq // tm, dff // tf),
        in_specs=[
            pl.BlockSpec((None, tm, d), lambda b, i, f: (b, i, 0)),
            par(), vec(), vec(), vec(), par(),
            pl.BlockSpec((d, tf), lambda b, i, f: (0, f)),
            pl.BlockSpec((d, tf), lambda b, i, f: (0, f)),
            pl.BlockSpec((tf, d), lambda b, i, f: (f, 0)),
        ],
        out_specs=pl.BlockSpec((None, tm, d), lambda b, i, f: (b, i, 0)),
        out_shape=jax.ShapeDtypeStruct((bsz, seq, d), F32),
        scratch_shapes=[pltpu.VMEM((tm, d), BF16), pltpu.VMEM((tm, d), F32)],
        compiler_params=_params(("parallel", "parallel", "arbitrary")),
        name="ffn",
    )(x, g_pre.reshape(1, d), sc, sh, gt, g_post.reshape(1, d), wg, wu, wd)


def _pack_in_proj(w):
    d = w.shape[0]
    splits = (A_W, A_W, A_W, B_W, B_W, B_W, B_HEADS, 2 * CONV_CH, D_QW, D_KVW, D_KVW, N_BRANCHES * d)
    offs = np.cumsum(splits)[:-1].tolist()
    aq, ak, av, bq, bk, bv, bf, cu, dq, dk, dv, gl = jnp.split(w, offs, axis=1)
    s2 = HEAD_DIM ** -0.5 * LOG2E
    main = jnp.concatenate(
        [aq * s2, ak, av, bq * s2, bk, bv, cu, dq * D_HEAD_DIM ** -0.5, dk, dv,
         jnp.zeros((d, MAIN_W_PAD - MAIN_W), w.dtype)], axis=1).astype(BF16)
    side = jnp.concatenate([bf, jnp.zeros((d, LANES - B_HEADS), w.dtype)], axis=1).astype(BF16)
    return main, side, gl.astype(BF16)


def kernel(x, c, rel_bias, w_mod, b_mod, mix_norm_pre, mix_norm_post, w_in, fox_bias, conv_w, conv_b, conv_ln_g, conv_ln_b, sinks, w_branch_a, w_branch_b, w_branch_c, w_branch_d, w_out, ffn_norm_pre, ffn_norm_post, w_ffn_gate, w_ffn_up, w_ffn_down):
    depth = w_mod.shape[0]
    bsz, seq, d = x.shape
    tm = min(1024, seq)
    tm_small = min(256, seq)
    tm_ffn = min(512, seq)

    mod = _modulation(c, w_mod, b_mod)
    tab_a = _bias_tables(rel_bias[:, :A_HEADS], _moba_buckets(), scale=LOG2E)
    tab_d = _bias_tables(rel_bias[:, A_HEADS:], _swa_buckets(), scale=1.0)[:, 0]

    for l in range(depth):
        sh_m, sc_m, gt_m, sh_f, sc_f, gt_f = [mod[l, :, None, i * d:(i + 1) * d] for i in range(6)]
        w_main, w_side, w_gl = _pack_in_proj(w_in[l])

        proj, fox_raw = _norm_proj(x, mix_norm_pre[l], sc_m, sh_m, w_main, w_side, act="none", tm=tm, tn=MAIN_TN)
        gates = _norm_proj(x, mix_norm_pre[l], sc_m, sh_m, w_gl, act="sigmoid", tm=tm, tn=1024)

        fox_qa, fox_ka = _fox_gate(fox_raw, fox_bias[l], t=min(512, seq))

        ya = _moba(proj, tab_a)
        yb = _fox(proj, fox_qa, fox_ka)
        yc = _conv(proj, conv_w[l], conv_b[l], conv_ln_g[l], conv_ln_b[l])
        yd = _swa(proj, sinks[l], tab_d)

        x = _merge(ya, yb, yc, yd, gates, x, gt_m, mix_norm_post[l],
                   w_branch_a[l].astype(BF16), w_branch_b[l].astype(BF16), w_branch_c[l].astype(BF16),
                   w_branch_d[l].astype(BF16), w_out[l].astype(BF16), tm=tm_small)
        x = _ffn(x, ffn_norm_pre[l], sc_f, sh_f, gt_f, ffn_norm_post[l],
                 w_ffn_gate[l].astype(BF16), w_ffn_up[l].astype(BF16), w_ffn_down[l].astype(BF16),
                 tm=tm_ffn, tf=512)
    return x
```

```python
import functools
import math

import jax
import jax.numpy as jnp
import numpy as np
from jax import lax
from jax.experimental import pallas as pl
from jax.experimental.pallas import tpu as pltpu

F32 = jnp.float32
BF16 = jnp.bfloat16
HIGHEST = lax.Precision.HIGHEST

HEAD_DIM = 128
A_HEADS = 4
MOBA_BLOCK = 256
MOBA_TOPK = 3
B_HEADS = 4
CONV_CH = 512
CONV_WIDTH = 31
D_Q_HEADS = 8
D_KV_HEADS = 2
D_HEAD_DIM = 64
WINDOW = 128
N_BUCKETS = 32
MAX_DISTANCE = 1024
N_BRANCHES = 4
RMS_EPS = 1e-6
LN_EPS = 1e-5

A_W = A_HEADS * HEAD_DIM
B_W = B_HEADS * HEAD_DIM
D_QW = D_Q_HEADS * D_HEAD_DIM
D_KVW = D_KV_HEADS * D_HEAD_DIM

LANES = 128
SUBLANES = 8
VMEM_LIMIT = 56 * 1024 * 1024

NEG = -1e30
LOG2E = math.log2(math.e)
ATT_G = 4
ATT_T = MOBA_BLOCK
ATT_U = 4

COL_AQ, COL_AK, COL_AV = 0, A_W, 2 * A_W
COL_BQ, COL_BK, COL_BV = 3 * A_W, 3 * A_W + B_W, 3 * A_W + 2 * B_W
COL_CU = 3 * A_W + 3 * B_W
COL_DQ = COL_CU + 2 * CONV_CH
COL_DK = COL_DQ + D_QW
COL_DV = COL_DK + D_KVW
MAIN_W = COL_DV + D_KVW
MAIN_TN = 1024
MAIN_W_PAD = -(-MAIN_W // MAIN_TN) * MAIN_TN
MOBA_FAR = -(-(MAX_DISTANCE + MOBA_BLOCK - 1) // MOBA_BLOCK)
VOID_COL = LANES - 1
assert ATT_G % ATT_U == 0 and (MOBA_FAR - 1) % ATT_U == 0


def _params(sem, vmem=VMEM_LIMIT):
    return pltpu.CompilerParams(dimension_semantics=sem, vmem_limit_bytes=vmem)


def _t5_bucket(dist):
    max_exact = N_BUCKETS // 2
    d = jnp.maximum(dist, 0)
    log_ratio = jnp.log(jnp.maximum(d, 1).astype(jnp.float32) / max_exact) / math.log(MAX_DISTANCE / max_exact)
    large = max_exact + (log_ratio * (N_BUCKETS - max_exact)).astype(jnp.int32)
    large = jnp.minimum(large, N_BUCKETS - 1)
    return jnp.where(d < max_exact, d, large)


def _rms(y):
    return y * lax.rsqrt(jnp.mean(y * y, axis=-1, keepdims=True) + RMS_EPS)


def _modulated_norm(x, g, sc, sh):
    return (_rms(x) * g) * (1.0 + sc) + sh


MOD_TN = 512
MOD_KC = 256


def _mod_kernel(ct_ref, w_ref, b_ref, o_ref):
    d, nb = ct_ref.shape
    tn = w_ref.shape[1]
    ct = ct_ref[...]
    ca = ct * jax.nn.sigmoid(ct)
    rows = []
    for r in range(nb):
        acc = jnp.zeros((1, tn), F32)
        for kc in range(d // MOD_KC):
            sl = slice(kc * MOD_KC, (kc + 1) * MOD_KC)
            acc = acc + jnp.sum(w_ref[sl, :] * ca[sl, r:r + 1], axis=0, keepdims=True)
        rows.append(acc)
    o_ref[...] = jnp.concatenate(rows, axis=0) + b_ref[...]


def _modulation(c, w_mod, b_mod):
    depth, d, n = w_mod.shape
    bsz = c.shape[0]
    return pl.pallas_call(
        _mod_kernel,
        grid=(depth, n // MOD_TN),
        in_specs=[
            pl.BlockSpec((d, bsz), lambda l, j: (0, 0)),
            pl.BlockSpec((None, d, MOD_TN), lambda l, j: (l, 0, j)),
            pl.BlockSpec((None, 1, MOD_TN), lambda l, j: (l, 0, j)),
        ],
        out_specs=pl.BlockSpec((None, bsz, MOD_TN), lambda l, j: (l, 0, j)),
        out_shape=jax.ShapeDtypeStruct((depth, bsz, n), F32),
        compiler_params=_params(("parallel", "parallel")),
        name="mod",
    )(c.T, w_mod, b_mod.reshape(depth, 1, n))


def _bias_table_kernel(rb_ref, bucket_ref, o_ref, *, scale):
    h = pl.program_id(0)
    b = bucket_ref[...]
    acc = jnp.where(b < 0, NEG, 0.0).astype(F32)
    for u in range(N_BUCKETS):
        acc = jnp.where(b == u, rb_ref[h, u] * scale, acc)
    o_ref[...] = acc


def _bias_tables(rel_bias_heads, bucket, *, scale):
    nh = rel_bias_heads.shape[1]
    nt, r, c = bucket.shape
    return pl.pallas_call(
        functools.partial(_bias_table_kernel, scale=scale),
        grid=(nh, nt),
        in_specs=[
            pl.BlockSpec(memory_space=pltpu.SMEM),
            pl.BlockSpec((None, r, c), lambda h, t: (t, 0, 0)),
        ],
        out_specs=pl.BlockSpec((None, None, r, c), lambda h, t: (h, t, 0, 0)),
        out_shape=jax.ShapeDtypeStruct((nh, nt, r, c), F32),
        compiler_params=_params(("parallel", "parallel")),
        name="bias_table",
    )(rel_bias_heads.T, bucket)


def _moba_buckets():
    blk = MOBA_BLOCK
    i = jnp.arange(blk)[None, :]
    j = jnp.arange(blk)[:, None]
    tabs = []
    for delta in range(MOBA_FAR):
        dist = delta * blk + i - j
        tabs.append(jnp.where(dist >= 0, _t5_bucket(dist), -1))
    tabs.append(_t5_bucket(jnp.full((blk, blk), MAX_DISTANCE, jnp.int32)))
    return jnp.stack(tabs).astype(jnp.int32)


def _swa_buckets():
    qi = jnp.arange(WINDOW)[:, None]
    kj = jnp.arange(2 * WINDOW)[None, :]
    dist = qi + WINDOW - kj
    in_win = (dist >= 0) & (dist < WINDOW)
    return jnp.where(in_win, _t5_bucket(dist), -1).astype(jnp.int32)[None]


def _norm_proj_kernel(x_ref, g_ref, sc_ref, sh_ref, w_ref, *rest, act, with_side):
    if with_side:
        ws_ref, o_ref, os_ref, h_scr = rest
    else:
        o_ref, h_scr = rest

    @pl.when(pl.program_id(2) == 0)
    def _():
        h = _modulated_norm(x_ref[...], g_ref[...], sc_ref[...], sh_ref[...]).astype(BF16)
        h_scr[...] = h
        if with_side:
            os_ref[...] = jnp.dot(h, ws_ref[...], preferred_element_type=F32)

    acc = jnp.dot(h_scr[...], w_ref[...], preferred_element_type=F32)
    if act == "sigmoid":
        acc = jax.nn.sigmoid(acc)
    o_ref[...] = acc.astype(o_ref.dtype)


def _norm_proj(x, g, sc, sh, w, w_side=None, *, act, tm, tn):
    bsz, seq, d = x.shape
    n = w.shape[1]
    with_side = w_side is not None
    in_specs = [
        pl.BlockSpec((None, tm, d), lambda b, i, j: (b, i, 0)),
        pl.BlockSpec((1, d), lambda b, i, j: (0, 0)),
        pl.BlockSpec((None, 1, d), lambda b, i, j: (b, 0, 0)),
        pl.BlockSpec((None, 1, d), lambda b, i, j: (b, 0, 0)),
        pl.BlockSpec((d, tn), lambda b, i, j: (0, j)),
    ]
    out_specs = [pl.BlockSpec((None, tm, tn), lambda b, i, j: (b, i, j))]
    out_shape = [jax.ShapeDtypeStruct((bsz, seq, n), BF16)]
    args = [x, g.reshape(1, d), sc, sh, w]
    if with_side:
        ns = w_side.shape[1]
        in_specs.append(pl.BlockSpec((d, ns), lambda b, i, j: (0, 0)))
        out_specs.append(pl.BlockSpec((None, tm, ns), lambda b, i, j: (b, i, 0)))
        out_shape.append(jax.ShapeDtypeStruct((bsz, seq, ns), F32))
        args.append(w_side)
    outs = pl.pallas_call(
        functools.partial(_norm_proj_kernel, act=act, with_side=with_side),
        grid=(bsz, seq // tm, n // tn),
        in_specs=in_specs,
        out_specs=out_specs,
        out_shape=out_shape,
        scratch_shapes=[pltpu.VMEM((tm, d), BF16)],
        compiler_params=_params(("parallel", "parallel", "arbitrary")),
        name="norm_proj_" + act,
    )(*args)
    return outs if with_side else outs[0]


def _split3(c):
    hi = c.astype(BF16).astype(F32)
    mid = (c - hi).astype(BF16).astype(F32)
    lo = (c - hi - mid).astype(BF16).astype(F32)
    return hi, mid, lo


def _fox_gate_kernel(x_ref, fb_ref, qa_ref, ka_ref, carry_scr):
    t = x_ref.shape[0]

    @pl.when(pl.program_id(1) == 0)
    def _():
        carry_scr[...] = jnp.zeros_like(carry_scr)

    lf = jax.nn.log_sigmoid(x_ref[...] + fb_ref[...])
    tri = (lax.broadcasted_iota(jnp.int32, (t, t), 0) >= lax.broadcasted_iota(jnp.int32, (t, t), 1)).astype(F32)
    c = jnp.dot(tri, lf, precision=HIGHEST, preferred_element_type=F32) + carry_scr[...]
    carry_scr[...] = c[t - 1:t, :]
    lane = lax.broadcasted_iota(jnp.int32, (t, LANES), 1)
    sub = lax.broadcasted_iota(jnp.int32, (SUBLANES, t), 0)
    ct = c.T * LOG2E
    for h in range(B_HEADS):
        hi, mid, lo = _split3(jnp.broadcast_to(c[:, h:h + 1], (t, LANES)) * LOG2E)
        ka = jnp.where(lane == 0, -hi, jnp.where(lane == 1, -mid, jnp.where(lane == 2, -lo, jnp.where(lane < 6, 1.0, 0.0))))
        ka_ref[h] = ka.astype(BF16)
        hi, mid, lo = _split3(jnp.broadcast_to(ct[h:h + 1, :], (SUBLANES, t)))
        qa = jnp.where(sub < 3, 1.0, jnp.where(sub == 3, hi, jnp.where(sub == 4, mid, jnp.where(sub == 5, lo, 0.0))))
        qa_ref[h] = jnp.concatenate([qa, jnp.zeros((LANES - SUBLANES, t), F32)], axis=0).astype(BF16)


def _fox_gate(raw, fox_b, *, t):
    bsz, seq, _ = raw.shape
    fb = jnp.zeros((1, LANES), F32).at[0, :B_HEADS].set(fox_b)
    aug = pl.BlockSpec((None, B_HEADS, t, LANES), lambda b, i: (b, 0, i, 0))
    aug_t = pl.BlockSpec((None, B_HEADS, LANES, t), lambda b, i: (b, 0, 0, i))
    return pl.pallas_call(
        _fox_gate_kernel,
        grid=(bsz, seq // t),
        in_specs=[
            pl.BlockSpec((None, t, LANES), lambda b, i: (b, i, 0)),
            pl.BlockSpec((1, LANES), lambda b, i: (0, 0)),
        ],
        out_specs=[aug_t, aug],
        out_shape=[jax.ShapeDtypeStruct((bsz, B_HEADS, LANES, seq), BF16),
                   jax.ShapeDtypeStruct((bsz, B_HEADS, seq, LANES), BF16)],
        scratch_shapes=[pltpu.VMEM((1, LANES), F32)],
        compiler_params=_params(("parallel", "arbitrary")),
        name="fox_gate",
    )(raw, fb)


ONES_ROWS = 16


def _softmax_step(zt, vt, carry):
    m, acc = carry
    m_new = jnp.maximum(m, jnp.max(zt, axis=0, keepdims=True))
    p = jnp.exp2(zt - m_new)
    acc = jnp.exp2(m - m_new) * acc + jnp.dot(vt, p.astype(BF16), preferred_element_type=F32)
    return m_new, acc


def _softmax_init(tq, dh):
    return (jnp.full((1, tq), NEG, F32), jnp.zeros((dh + ONES_ROWS, tq), F32))


def _softmax_finish(carry, dh):
    _, acc = carry
    return (acc[:dh, :] / acc[dh:dh + 1, :]).T


def _fill_vt(vt_scr, v_ref, chunk):
    seq, dh = v_ref.shape
    for cix in range(seq // chunk):
        sl = slice(cix * chunk, (cix + 1) * chunk)
        vt_scr[:dh, sl] = _transpose_bf16(v_ref[sl, :])
    vt_scr[dh:, :] = jnp.ones((ONES_ROWS, seq), BF16)


ATT_LOOKAHEAD = 6


def _run_substeps(steps, carry):
    carry = list(carry)
    pending = {}
    for s in range(min(ATT_LOOKAHEAD, len(steps))):
        pending[s] = steps[s][1]()
    for s, (j, _, fix, vt) in enumerate(steps):
        if s + ATT_LOOKAHEAD < len(steps):
            pending[s + ATT_LOOKAHEAD] = steps[s + ATT_LOOKAHEAD][1]()
        z, cj = pending.pop(s), carry[j]
        if fix is not None:
            z, cj = fix(z, cj)
        carry[j] = _softmax_step(z, vt, cj)
    return carry


def _transpose_bf16(x):
    return x.astype(F32).T.astype(BF16)


_NT = (((1,), (1,)), ((), ()))


def _dot_nt(a, b):
    return lax.dot_general(a, b, _NT, preferred_element_type=F32)


def _moba_kernel(cfar_ref, q_ref, k_ref, v_ref, tab_ref, o_ref, kmean_scr, sc_scr, qt_scr, vt_scr):
    g, t = ATT_G, ATT_T
    seq, dh = k_ref.shape
    nblk = seq // t
    rows = g * t
    i = pl.program_id(2)
    base = i * g

    @pl.when(i == 0)
    def _():
        r = lax.broadcasted_iota(jnp.int32, (nblk, seq), 0)
        c = lax.broadcasted_iota(jnp.int32, (nblk, seq), 1)
        avg = jnp.where((c >= r * t) & (c < (r + 1) * t), 1.0 / t, 0.0).astype(BF16)
        kmean_scr[...] = jnp.dot(avg, k_ref[...], preferred_element_type=F32)
        _fill_vt(vt_scr, v_ref, rows)

    q = q_ref[...]
    qt_scr[:dh, :] = _transpose_bf16(q)
    sc = lax.dot_general(kmean_scr[...], q.astype(F32), _NT, precision=HIGHEST, preferred_element_type=F32)
    row = lax.broadcasted_iota(jnp.int32, (nblk, rows), 0)
    own = base + lax.broadcasted_iota(jnp.int32, (nblk, rows), 1) // t
    past = row < own
    sc = jnp.where(past, sc, -jnp.inf)
    sc_scr[...] = sc

    def rank_body(m, rank):
        sm = sc_scr[pl.ds(m, 1), :]
        return rank + jnp.where(sm > sc, 1.0, jnp.where(sm == sc, jnp.where(row > m, 1.0, 0.0), 0.0))

    rank = lax.fori_loop(0, base + g - 1, rank_body, jnp.zeros((nblk, rows), F32))
    pen = jnp.where(past, jnp.where(rank < MOBA_TOPK, 0.0, NEG), jnp.where(row == own, 0.0, NEG))
    void = jnp.where(lax.broadcasted_iota(jnp.int32, (SUBLANES, rows), 0) == SUBLANES - 1, NEG, 0.0)
    pen = jnp.concatenate([pen, jnp.zeros((LANES - SUBLANES - nblk, rows), F32), void], axis=0)
    qt_scr[dh:, :] = pen.astype(BF16)

    lane = lax.broadcasted_iota(jnp.int32, (t, LANES), 1)

    def kv(n, col):
        start = pl.multiple_of(n * t, t)
        onehot = jnp.where(lane == col, 1.0, 0.0).astype(BF16)
        return jnp.concatenate([k_ref[pl.ds(start, t), :], onehot], axis=1), vt_scr[:, pl.ds(start, t)]

    def logits(kb, j):
        return jnp.dot(kb, qt_scr[:, j * t:(j + 1) * t], preferred_element_type=F32)

    def far_body(grp, carry):
        steps = []
        for u in range(ATT_U):
            n = grp * ATT_U + u
            kb, vb = kv(n, n)
            steps += [(j, functools.partial(logits, kb, j), None, vb) for j in range(g)]
        return tuple(_run_substeps(steps, carry))

    carry = lax.fori_loop(0, jnp.maximum(base - (MOBA_FAR - 1), 0) // ATT_U, far_body,
                          tuple(_softmax_init(t, dh) for _ in range(g)))
    cfar = cfar_ref[pl.program_id(1)]

    def near_fix(delta, z, cj):
        if delta == MOBA_FAR - 1:
            m, acc = cj
            cj = (m + cfar, acc)
        return tab_ref[delta] + z, cj

    steps = []
    for k in range(-(MOBA_FAR - 1), g):
        n = base + k
        if k < 0:
            kb, vb = kv(jnp.maximum(n, 0), jnp.where(n >= 0, n, VOID_COL))
        else:
            kb, vb = kv(n, n)
        for j in range(max(k, 0), g):
            fix = functools.partial(near_fix, j - k) if j - k < MOBA_FAR else None
            steps.append((j, functools.partial(logits, kb, j), fix, vb))
    carry = _run_substeps(steps, carry)
    for j in range(g):
        o_ref[j * t:(j + 1) * t, :] = _softmax_finish(carry[j], dh).astype(o_ref.dtype)


def _moba(proj, tab):
    bsz, seq, _ = proj.shape
    g, t = ATT_G, ATT_T
    nblk = seq // t
    assert seq % (g * t) == 0 and nblk <= LANES - SUBLANES and nblk % SUBLANES == 0
    qc, kc, vc = COL_AQ // HEAD_DIM, COL_AK // HEAD_DIM, COL_AV // HEAD_DIM
    cfar = tab[:, MOBA_FAR, 0, 0]
    return pl.pallas_call(
        _moba_kernel,
        grid=(bsz, A_HEADS, nblk // g),
        in_specs=[
            pl.BlockSpec(memory_space=pltpu.SMEM),
            pl.BlockSpec((None, g * t, HEAD_DIM), lambda b, h, i: (b, i, qc + h)),
            pl.BlockSpec((None, seq, HEAD_DIM), lambda b, h, i: (b, 0, kc + h)),
            pl.BlockSpec((None, seq, HEAD_DIM), lambda b, h, i: (b, 0, vc + h)),
            pl.BlockSpec((None, MOBA_FAR, t, t), lambda b, h, i: (h, 0, 0, 0)),
        ],
        out_specs=pl.BlockSpec((None, g * t, HEAD_DIM), lambda b, h, i: (b, i, h)),
        out_shape=jax.ShapeDtypeStruct((bsz, seq, A_W), BF16),
        scratch_shapes=[pltpu.VMEM((nblk, HEAD_DIM), F32), pltpu.VMEM((nblk, g * t), F32),
                        pltpu.VMEM((HEAD_DIM + LANES, g * t), BF16), pltpu.VMEM((HEAD_DIM + ONES_ROWS, seq), BF16)],
        compiler_params=_params(("parallel", "parallel", "arbitrary")),
        name="moba",
    )(cfar, proj, proj, proj, tab)


def _fox_kernel(q_ref, qa_ref, k_ref, ka_ref, v_ref, o_ref, qt_scr, vt_scr):
    g, t = ATT_G, ATT_T
    rows = g * t
    seq, dh = k_ref.shape
    base = pl.program_id(2) * g

    @pl.when(base == 0)
    def _():
        _fill_vt(vt_scr, v_ref, rows)

    qt_scr[:dh, :] = _transpose_bf16(q_ref[...])
    qt_scr[dh:, :] = qa_ref[...]

    def kv(n):
        start = pl.multiple_of(n * t, t)
        return (jnp.concatenate([k_ref[pl.ds(start, t), :], ka_ref[pl.ds(start, t), :]], axis=1),
                vt_scr[:, pl.ds(start, t)])

    def logits(kb, j):
        return jnp.dot(kb, qt_scr[:, j * t:(j + 1) * t], preferred_element_type=F32)

    def body(grp, carry):
        steps = []
        for u in range(ATT_U):
            kb, vb = kv(grp * ATT_U + u)
            steps += [(j, functools.partial(logits, kb, j), None, vb) for j in range(g)]
        return tuple(_run_substeps(steps, carry))

    carry = lax.fori_loop(0, base // ATT_U, body, tuple(_softmax_init(t, dh) for _ in range(g)))
    causal = lax.broadcasted_iota(jnp.int32, (t, t), 0) <= lax.broadcasted_iota(jnp.int32, (t, t), 1)

    def diag_fix(z, cj):
        return jnp.where(causal, z, NEG), cj

    steps = []
    for k in range(g):
        kb, vb = kv(base + k)
        steps += [(j, functools.partial(logits, kb, j), diag_fix if j == k else None, vb) for j in range(k, g)]
    carry = _run_substeps(steps, carry)
    for j in range(g):
        o_ref[j * t:(j + 1) * t, :] = _softmax_finish(carry[j], dh).astype(o_ref.dtype)


def _fox(proj, qa, ka):
    bsz, seq, _ = proj.shape
    g, t = ATT_G, ATT_T
    assert seq % (g * t) == 0
    qc, kc, vc = COL_BQ // HEAD_DIM, COL_BK // HEAD_DIM, COL_BV // HEAD_DIM
    return pl.pallas_call(
        _fox_kernel,
        grid=(bsz, B_HEADS, seq // (g * t)),
        in_specs=[
            pl.BlockSpec((None, g * t, HEAD_DIM), lambda b, h, i: (b, i, qc + h)),
            pl.BlockSpec((None, None, LANES, g * t), lambda b, h, i: (b, h, 0, i)),
            pl.BlockSpec((None, seq, HEAD_DIM), lambda b, h, i: (b, 0, kc + h)),
            pl.BlockSpec((None, None, seq, LANES), lambda b, h, i: (b, h, 0, 0)),
            pl.BlockSpec((None, seq, HEAD_DIM), lambda b, h, i: (b, 0, vc + h)),
        ],
        out_specs=pl.BlockSpec((None, g * t, HEAD_DIM), lambda b, h, i: (b, i, h)),
        out_shape=jax.ShapeDtypeStruct((bsz, seq, B_W), BF16),
        scratch_shapes=[pltpu.VMEM((HEAD_DIM + LANES, g * t), BF16), pltpu.VMEM((HEAD_DIM + ONES_ROWS, seq), BF16)],
        compiler_params=_params(("parallel", "parallel", "arbitrary")),
        name="fox",
    )(proj, qa, proj, ka, proj)


def _swa_kernel(sink_ref, q_ref, kp_ref, kc_ref, vp_ref, vc_ref, tab_ref, o_ref):
    w = WINDOW
    dh = D_HEAD_DIM
    grp = D_Q_HEADS // D_KV_HEADS
    i = pl.program_id(1)
    q = q_ref[...]
    kcat = jnp.concatenate([kp_ref[...], kc_ref[...]], axis=0)
    vcat = jnp.concatenate([vp_ref[...], vc_ref[...]], axis=0)
    col = lax.broadcasted_iota(jnp.int32, (w, 2 * w), 1)
    keep = col >= jnp.where(i > 0, 0, w)
    outs = []
    for h in range(D_Q_HEADS):
        g = h // grp
        s = _dot_nt(q[:, h * dh:(h + 1) * dh], kcat[:, g * dh:(g + 1) * dh])
        s = jnp.where(keep, tab_ref[h] + s, NEG)
        sink = sink_ref[h]
        m = jnp.maximum(jnp.max(s, axis=-1, keepdims=True), sink)
        p = jnp.exp(s - m)
        den = jnp.sum(p, axis=-1, keepdims=True) + jnp.exp(sink - m)
        pv = jnp.dot(p.astype(BF16), vcat[:, g * dh:(g + 1) * dh], preferred_element_type=F32)
        outs.append(pv / den)
    o_ref[...] = jnp.concatenate(outs, axis=1).astype(o_ref.dtype)


def _swa(proj, sinks, tab):
    bsz, seq, _ = proj.shape
    w = WINDOW
    qc, kc, vc = COL_DQ // D_QW, COL_DK // D_KVW, COL_DV // D_KVW
    prev = lambda i: jnp.maximum(i - 1, 0)
    return pl.pallas_call(
        _swa_kernel,
        grid=(bsz, seq // w),
        in_specs=[
            pl.BlockSpec(memory_space=pltpu.SMEM),
            pl.BlockSpec((None, w, D_QW), lambda b, i: (b, i, qc)),
            pl.BlockSpec((None, w, D_KVW), lambda b, i: (b, prev(i), kc)),
            pl.BlockSpec((None, w, D_KVW), lambda b, i: (b, i, kc)),
            pl.BlockSpec((None, w, D_KVW), lambda b, i: (b, prev(i), vc)),
            pl.BlockSpec((None, w, D_KVW), lambda b, i: (b, i, vc)),
            pl.BlockSpec((D_Q_HEADS, w, 2 * w), lambda b, i: (0, 0, 0)),
        ],
        out_specs=pl.BlockSpec((None, w, D_QW), lambda b, i: (b, i, 0)),
        out_shape=jax.ShapeDtypeStruct((bsz, seq, D_QW), BF16),
        compiler_params=_params(("parallel", "parallel")),
        name="swa",
    )(sinks, proj, proj, proj, proj, proj, tab)


CONV_T = 512
CONV_HALO = 32
CONV_ROWS = 64


def _conv_kernel(a_ref, g_ref, ha_ref, hg_ref, w_ref, b_ref, lg_ref, lb_ref, o_ref, buf):
    t = a_ref.shape[0]
    halo = CONV_HALO
    hp = ha_ref[...].astype(F32) * jax.nn.sigmoid(hg_ref[...].astype(F32))
    buf[0:halo, :] = jnp.where(pl.program_id(1) > 0, hp, 0.0)
    buf[halo:halo + t, :] = a_ref[...].astype(F32) * jax.nn.sigmoid(g_ref[...].astype(F32))
    off = halo - (CONV_WIDTH - 1)
    for r in range(t // CONV_ROWS):
        base = r * CONV_ROWS + off
        acc = jnp.broadcast_to(b_ref[...], (CONV_ROWS, CONV_CH))
        for k in range(CONV_WIDTH):
            acc = acc + w_ref[k:k + 1, :] * buf[base + k:base + k + CONV_ROWS, :]
        mu = jnp.mean(acc, axis=-1, keepdims=True)
        d = acc - mu
        var = jnp.mean(d * d, axis=-1, keepdims=True)
        y = d * lax.rsqrt(var + LN_EPS) * lg_ref[...] + lb_ref[...]
        o_ref[r * CONV_ROWS:(r + 1) * CONV_ROWS, :] = (y * jax.nn.sigmoid(y)).astype(o_ref.dtype)


def _conv(proj, conv_w, conv_b, ln_g, ln_b):
    bsz, seq, _ = proj.shape
    t = min(CONV_T, seq)
    ca, cg = COL_CU // CONV_CH, COL_CU // CONV_CH + 1
    hb = t // CONV_HALO
    prev = lambda i: jnp.maximum(i * hb - 1, 0)
    row = lambda v: v.reshape(1, CONV_CH)
    return pl.pallas_call(
        _conv_kernel,
        grid=(bsz, seq // t),
        in_specs=[
            pl.BlockSpec((None, t, CONV_CH), lambda b, i: (b, i, ca)),
            pl.BlockSpec((None, t, CONV_CH), lambda b, i: (b, i, cg)),
            pl.BlockSpec((None, CONV_HALO, CONV_CH), lambda b, i: (b, prev(i), ca)),
            pl.BlockSpec((None, CONV_HALO, CONV_CH), lambda b, i: (b, prev(i), cg)),
            pl.BlockSpec((CONV_WIDTH, CONV_CH), lambda b, i: (0, 0)),
            pl.BlockSpec((1, CONV_CH), lambda b, i: (0, 0)),
            pl.BlockSpec((1, CONV_CH), lambda b, i: (0, 0)),
            pl.BlockSpec((1, CONV_CH), lambda b, i: (0, 0)),
        ],
        out_specs=pl.BlockSpec((None, t, CONV_CH), lambda b, i: (b, i, 0)),
        out_shape=jax.ShapeDtypeStruct((bsz, seq, CONV_CH), BF16),
        scratch_shapes=[pltpu.VMEM((CONV_HALO + t, CONV_CH), F32)],
        compiler_params=_params(("parallel", "parallel")),
        name="conv",
    )(proj, proj, proj, proj, conv_w, row(conv_b), row(ln_g), row(ln_b))


def _merge_kernel(ya_ref, yb_ref, yc_ref, yd_ref, gates_ref, x_ref, gt_ref, gp_ref,
                  wa_ref, wb_ref, wc_ref, wd_ref, wo_ref, o_ref):
    d = x_ref.shape[1]
    y = None
    for i, (br, w) in enumerate(((ya_ref, wa_ref), (yb_ref, wb_ref), (yc_ref, wc_ref), (yd_ref, wd_ref))):
        t = jnp.dot(br[...], w[...], preferred_element_type=F32) * gates_ref[:, i * d:(i + 1) * d].astype(F32)
        y = t if y is None else y + t
    z = jnp.dot(y.astype(BF16), wo_ref[...], preferred_element_type=F32)
    o_ref[...] = x_ref[...] + gt_ref[...] * (_rms(z) * gp_ref[...])


def _merge(ya, yb, yc, yd, gates, x, gt, g_post, wa, wb, wc, wd, wo, *, tm):
    bsz, seq, d = x.shape
    tok = lambda width: pl.BlockSpec((None, tm, width), lambda b, i: (b, i, 0))
    full = lambda a: pl.BlockSpec(a.shape, lambda b, i: (0,) * a.ndim)
    return pl.pallas_call(
        _merge_kernel,
        grid=(bsz, seq // tm),
        in_specs=[tok(A_W), tok(B_W), tok(CONV_CH), tok(D_QW), tok(N_BRANCHES * d), tok(d),
                  pl.BlockSpec((None, 1, d), lambda b, i: (b, 0, 0)),
                  pl.BlockSpec((1, d), lambda b, i: (0, 0)),
                  full(wa), full(wb), full(wc), full(wd), full(wo)],
        out_specs=tok(d),
        out_shape=jax.ShapeDtypeStruct((bsz, seq, d), F32),
        compiler_params=_params(("parallel", "parallel")),
        name="merge",
    )(ya, yb, yc, yd, gates, x, gt, g_post.reshape(1, d), wa, wb, wc, wd, wo)


def _ffn_kernel(x_ref, g_ref, sc_ref, sh_ref, gt_ref, gp_ref, wg_ref, wu_ref, wd_ref, o_ref, h_scr, acc_scr):
    f = pl.program_id(2)

    @pl.when(f == 0)
    def _():
        h_scr[...] = _modulated_norm(x_ref[...], g_ref[...], sc_ref[...], sh_ref[...]).astype(BF16)
        acc_scr[...] = jnp.zeros_like(acc_scr)

    h = h_scr[...]
    g = jnp.dot(h, wg_ref[...], preferred_element_type=F32)
    u = jnp.dot(h, wu_ref[...], preferred_element_type=F32)
    a = ((g * jax.nn.sigmoid(g)) * u).astype(BF16)
    acc_scr[...] += jnp.dot(a, wd_ref[...], preferred_element_type=F32)

    @pl.when(f == pl.num_programs(2) - 1)
    def _():
        o_ref[...] = x_ref[...] + gt_ref[...] * (_rms(acc_scr[...]) * gp_ref[...])


def _ffn(x, g_pre, sc, sh, gt, g_post, wg, wu, wd, *, tm, tf):
    bsz, seq, d = x.shape
    dff = wg.shape[1]
    vec = lambda: pl.BlockSpec((None, 1, d), lambda b, i, f: (b, 0, 0))
    par = lambda: pl.BlockSpec((1, d), lambda b, i, f: (0, 0))
    return pl.pallas_call(
        _ffn_kernel,
        grid=(bsz, seq // tm, dff // tf),
        in_specs=[
            pl.BlockSpec((None, tm, d), lambda b, i, f: (b, i, 0)),
            par(), vec(), vec(), vec(), par(),
            pl.BlockSpec((d, tf), lambda b, i, f: (0, f)),
            pl.BlockSpec((d, tf), lambda b, i, f: (0, f)),
            pl.BlockSpec((tf, d), lambda b, i, f: (f, 0)),
        ],
        out_specs=pl.BlockSpec((None, tm, d), lambda b, i, f: (b, i, 0)),
        out_shape=jax.ShapeDtypeStruct((bsz, seq, d), F32),
        scratch_shapes=[pltpu.VMEM((tm, d), BF16), pltpu.VMEM((tm, d), F32)],
        compiler_params=_params(("parallel", "parallel", "arbitrary")),
        name="ffn",
    )(x, g_pre.reshape(1, d), sc, sh, gt, g_post.reshape(1, d), wg, wu, wd)


def _pack_in_proj(w):
    d = w.shape[0]
    splits = (A_W, A_W, A_W, B_W, B_W, B_W, B_HEADS, 2 * CONV_CH, D_QW, D_KVW, D_KVW, N_BRANCHES * d)
    offs = np.cumsum(splits)[:-1].tolist()
    aq, ak, av, bq, bk, bv, bf, cu, dq, dk, dv, gl = jnp.split(w, offs, axis=1)
    s2 = HEAD_DIM ** -0.5 * LOG2E
    main = jnp.concatenate(
        [aq * s2, ak, av, bq * s2, bk, bv, cu, dq * D_HEAD_DIM ** -0.5, dk, dv,
         jnp.zeros((d, MAIN_W_PAD - MAIN_W), w.dtype)], axis=1).astype(BF16)
    side = jnp.concatenate([bf, jnp.zeros((d, LANES - B_HEADS), w.dtype)], axis=1).astype(BF16)
    return main, side, gl.astype(BF16)


def kernel(x, c, rel_bias, w_mod, b_mod, mix_norm_pre, mix_norm_post, w_in, fox_bias, conv_w, conv_b, conv_ln_g, conv_ln_b, sinks, w_branch_a, w_branch_b, w_branch_c, w_branch_d, w_out, ffn_norm_pre, ffn_norm_post, w_ffn_gate, w_ffn_up, w_ffn_down):
    depth = w_mod.shape[0]
    bsz, seq, d = x.shape
    tm = min(1024, seq)
    tm_small = min(256, seq)
    tm_ffn = min(512, seq)

    mod = _modulation(c, w_mod, b_mod)
    tab_a = _bias_tables(rel_bias[:, :A_HEADS], _moba_buckets(), scale=LOG2E)
    tab_d = _bias_tables(rel_bias[:, A_HEADS:], _swa_buckets(), scale=1.0)[:, 0]

    for l in range(depth):
        sh_m, sc_m, gt_m, sh_f, sc_f, gt_f = [mod[l, :, None, i * d:(i + 1) * d] for i in range(6)]
        w_main, w_side, w_gl = _pack_in_proj(w_in[l])

        proj, fox_raw = _norm_proj(x, mix_norm_pre[l], sc_m, sh_m, w_main, w_side, act="none", tm=tm, tn=MAIN_TN)
        gates = _norm_proj(x, mix_norm_pre[l], sc_m, sh_m, w_gl, act="sigmoid", tm=tm, tn=1024)

        fox_qa, fox_ka = _fox_gate(fox_raw, fox_bias[l], t=min(512, seq))

        ya = _moba(proj, tab_a)
        yb = _fox(proj, fox_qa, fox_ka)
        yc = _conv(proj, conv_w[l], conv_b[l], conv_ln_g[l], conv_ln_b[l])
        yd = _swa(proj, sinks[l], tab_d)

        x = _merge(ya, yb, yc, yd, gates, x, gt_m, mix_norm_post[l],
                   w_branch_a[l].astype(BF16), w_branch_b[l].astype(BF16), w_branch_c[l].astype(BF16),
                   w_branch_d[l].astype(BF16), w_out[l].astype(BF16), tm=tm_small)
        x = _ffn(x, ffn_norm_pre[l], sc_f, sh_f, gt_f, ffn_norm_post[l],
                 w_ffn_gate[l].astype(BF16), w_ffn_up[l].astype(BF16), w_ffn_down[l].astype(BF16),
                 tm=tm_ffn, tf=512)
    return x
```

```python
import functools
import math

import jax
import jax.numpy as jnp
import numpy as np
from jax import lax
from jax.experimental import pallas as pl
from jax.experimental.pallas import tpu as pltpu

F32 = jnp.float32
BF16 = jnp.bfloat16
HIGHEST = lax.Precision.HIGHEST

HEAD_DIM = 128
A_HEADS = 4
MOBA_BLOCK = 256
MOBA_TOPK = 3
B_HEADS = 4
CONV_CH = 512
CONV_WIDTH = 31
D_Q_HEADS = 8
D_KV_HEADS = 2
D_HEAD_DIM = 64
WINDOW = 128
N_BUCKETS = 32
MAX_DISTANCE = 1024
N_BRANCHES = 4
RMS_EPS = 1e-6
LN_EPS = 1e-5

A_W = A_HEADS * HEAD_DIM
B_W = B_HEADS * HEAD_DIM
D_QW = D_Q_HEADS * D_HEAD_DIM
D_KVW = D_KV_HEADS * D_HEAD_DIM

LANES = 128
SUBLANES = 8
VMEM_LIMIT = 56 * 1024 * 1024

NEG = -1e30
LOG2E = math.log2(math.e)
ATT_G = 4
ATT_T = MOBA_BLOCK
ATT_U = 4

D_MODEL = 2048
GATE_W = N_BRANCHES * D_MODEL
COL_AQ, COL_AK, COL_AV = GATE_W, GATE_W + A_W, GATE_W + 2 * A_W
COL_BQ, COL_BK, COL_BV = COL_AV + A_W, COL_AV + A_W + B_W, COL_AV + A_W + 2 * B_W
COL_CU = COL_BV + B_W
COL_DQ = COL_CU + 2 * CONV_CH
COL_DK = COL_DQ + D_QW
COL_DV = COL_DK + D_KVW
PROJ_W = COL_DV + D_KVW
PROJ_TN = 1024
PROJ_W_PAD = -(-PROJ_W // PROJ_TN) * PROJ_TN
MOBA_FAR = -(-(MAX_DISTANCE + MOBA_BLOCK - 1) // MOBA_BLOCK)
VOID_COL = LANES - 1
assert ATT_G % ATT_U == 0 and (MOBA_FAR - 1) % ATT_U == 0


def _params(sem, vmem=VMEM_LIMIT):
    return pltpu.CompilerParams(dimension_semantics=sem, vmem_limit_bytes=vmem)


def _t5_bucket(dist):
    max_exact = N_BUCKETS // 2
    d = jnp.maximum(dist, 0)
    log_ratio = jnp.log(jnp.maximum(d, 1).astype(jnp.float32) / max_exact) / math.log(MAX_DISTANCE / max_exact)
    large = max_exact + (log_ratio * (N_BUCKETS - max_exact)).astype(jnp.int32)
    large = jnp.minimum(large, N_BUCKETS - 1)
    return jnp.where(d < max_exact, d, large)


def _rms(y):
    return y * lax.rsqrt(jnp.mean(y * y, axis=-1, keepdims=True) + RMS_EPS)


def _modulated_norm(x, g, sc, sh):
    return (_rms(x) * g) * (1.0 + sc) + sh


MOD_TN = 512
MOD_KC = 256


def _mod_kernel(ct_ref, w_ref, b_ref, o_ref):
    d, nb = ct_ref.shape
    tn = w_ref.shape[1]
    ct = ct_ref[...]
    ca = ct * jax.nn.sigmoid(ct)
    rows = []
    for r in range(nb):
        acc = jnp.zeros((1, tn), F32)
        for kc in range(d // MOD_KC):
            sl = slice(kc * MOD_KC, (kc + 1) * MOD_KC)
            acc = acc + jnp.sum(w_ref[sl, :] * ca[sl, r:r + 1], axis=0, keepdims=True)
        rows.append(acc)
    o_ref[...] = jnp.concatenate(rows, axis=0) + b_ref[...]


def _modulation(c, w_mod, b_mod):
    depth, d, n = w_mod.shape
    bsz = c.shape[0]
    return pl.pallas_call(
        _mod_kernel,
        grid=(depth, n // MOD_TN),
        in_specs=[
            pl.BlockSpec((d, bsz), lambda l, j: (0, 0)),
            pl.BlockSpec((None, d, MOD_TN), lambda l, j: (l, 0, j)),
            pl.BlockSpec((None, 1, MOD_TN), lambda l, j: (l, 0, j)),
        ],
        out_specs=pl.BlockSpec((None, bsz, MOD_TN), lambda l, j: (l, 0, j)),
        out_shape=jax.ShapeDtypeStruct((depth, bsz, n), F32),
        compiler_params=_params(("parallel", "parallel")),
        name="mod",
    )(c.T, w_mod, b_mod.reshape(depth, 1, n))


def _bias_table_kernel(rb_ref, bucket_ref, o_ref, *, scale):
    h = pl.program_id(0)
    b = bucket_ref[...]
    acc = jnp.where(b < 0, NEG, 0.0).astype(F32)
    for u in range(N_BUCKETS):
        acc = jnp.where(b == u, rb_ref[h, u] * scale, acc)
    o_ref[...] = acc


def _bias_tables(rel_bias_heads, bucket, *, scale):
    nh = rel_bias_heads.shape[1]
    nt, r, c = bucket.shape
    return pl.pallas_call(
        functools.partial(_bias_table_kernel, scale=scale),
        grid=(nh, nt),
        in_specs=[
            pl.BlockSpec(memory_space=pltpu.SMEM),
            pl.BlockSpec((None, r, c), lambda h, t: (t, 0, 0)),
        ],
        out_specs=pl.BlockSpec((None, None, r, c), lambda h, t: (h, t, 0, 0)),
        out_shape=jax.ShapeDtypeStruct((nh, nt, r, c), F32),
        compiler_params=_params(("parallel", "parallel")),
        name="bias_table",
    )(rel_bias_heads.T, bucket)


def _moba_buckets():
    blk = MOBA_BLOCK
    i = jnp.arange(blk)[None, :]
    j = jnp.arange(blk)[:, None]
    tabs = []
    for delta in range(MOBA_FAR):
        dist = delta * blk + i - j
        tabs.append(jnp.where(dist >= 0, _t5_bucket(dist), -1))
    tabs.append(_t5_bucket(jnp.full((blk, blk), MAX_DISTANCE, jnp.int32)))
    return jnp.stack(tabs).astype(jnp.int32)


def _swa_buckets():
    qi = jnp.arange(WINDOW)[:, None]
    kj = jnp.arange(2 * WINDOW)[None, :]
    dist = qi + WINDOW - kj
    in_win = (dist >= 0) & (dist < WINDOW)
    return jnp.where(in_win, _t5_bucket(dist), -1).astype(jnp.int32)[None]


def _norm_proj_kernel(x_ref, g_ref, sc_ref, sh_ref, w_ref, ws_ref, o_ref, os_ref, h_scr):
    j = pl.program_id(2)
    tn = w_ref.shape[1]

    @pl.when(j == 0)
    def _():
        h = _modulated_norm(x_ref[...], g_ref[...], sc_ref[...], sh_ref[...]).astype(BF16)
        h_scr[...] = h
        os_ref[...] = jnp.dot(h, ws_ref[...], preferred_element_type=F32)

    acc = jnp.dot(h_scr[...], w_ref[...], preferred_element_type=F32)
    acc = jnp.where(j < GATE_W // tn, jax.nn.sigmoid(acc), acc)
    o_ref[...] = acc.astype(o_ref.dtype)


def _norm_proj(x, g, sc, sh, w, w_side, *, tm, tn):
    bsz, seq, d = x.shape
    n = w.shape[1]
    ns = w_side.shape[1]
    assert GATE_W % tn == 0 and n % tn == 0
    return pl.pallas_call(
        _norm_proj_kernel,
        grid=(bsz, seq // tm, n // tn),
        in_specs=[
            pl.BlockSpec((None, tm, d), lambda b, i, j: (b, i, 0)),
            pl.BlockSpec((1, d), lambda b, i, j: (0, 0)),
            pl.BlockSpec((None, 1, d), lambda b, i, j: (b, 0, 0)),
            pl.BlockSpec((None, 1, d), lambda b, i, j: (b, 0, 0)),
            pl.BlockSpec((d, tn), lambda b, i, j: (0, j)),
            pl.BlockSpec((d, ns), lambda b, i, j: (0, 0)),
        ],
        out_specs=[pl.BlockSpec((None, tm, tn), lambda b, i, j: (b, i, j)),
                   pl.BlockSpec((None, tm, ns), lambda b, i, j: (b, i, 0))],
        out_shape=[jax.ShapeDtypeStruct((bsz, seq, n), BF16), jax.ShapeDtypeStruct((bsz, seq, ns), F32)],
        scratch_shapes=[pltpu.VMEM((tm, d), BF16)],
        compiler_params=_params(("parallel", "parallel", "arbitrary")),
        name="norm_proj",
    )(x, g.reshape(1, d), sc, sh, w, w_side)


def _split3(c):
    hi = c.astype(BF16).astype(F32)
    mid = (c - hi).astype(BF16).astype(F32)
    lo = (c - hi - mid).astype(BF16).astype(F32)
    return hi, mid, lo


def _fox_gate_kernel(x_ref, fb_ref, qa_ref, ka_ref, carry_scr):
    t = x_ref.shape[0]

    @pl.when(pl.program_id(1) == 0)
    def _():
        carry_scr[...] = jnp.zeros_like(carry_scr)

    lf = jax.nn.log_sigmoid(x_ref[...] + fb_ref[...])
    tri = (lax.broadcasted_iota(jnp.int32, (t, t), 0) >= lax.broadcasted_iota(jnp.int32, (t, t), 1)).astype(F32)
    c = jnp.dot(tri, lf, precision=HIGHEST, preferred_element_type=F32) + carry_scr[...]
    carry_scr[...] = c[t - 1:t, :]
    lane = lax.broadcasted_iota(jnp.int32, (t, LANES), 1)
    sub = lax.broadcasted_iota(jnp.int32, (SUBLANES, t), 0)
    ct = c.T * LOG2E
    for h in range(B_HEADS):
        hi, mid, lo = _split3(jnp.broadcast_to(c[:, h:h + 1], (t, LANES)) * LOG2E)
        ka = jnp.where(lane == 0, -hi, jnp.where(lane == 1, -mid, jnp.where(lane == 2, -lo, jnp.where(lane < 6, 1.0, 0.0))))
        ka_ref[h] = ka.astype(BF16)
        hi, mid, lo = _split3(jnp.broadcast_to(ct[h:h + 1, :], (SUBLANES, t)))
        qa = jnp.where(sub < 3, 1.0, jnp.where(sub == 3, hi, jnp.where(sub == 4, mid, jnp.where(sub == 5, lo, 0.0))))
        qa_ref[h] = jnp.concatenate([qa, jnp.zeros((LANES - SUBLANES, t), F32)], axis=0).astype(BF16)


def _fox_gate(raw, fox_b, *, t):
    bsz, seq, _ = raw.shape
    fb = jnp.zeros((1, LANES), F32).at[0, :B_HEADS].set(fox_b)
    aug = pl.BlockSpec((None, B_HEADS, t, LANES), lambda b, i: (b, 0, i, 0))
    aug_t = pl.BlockSpec((None, B_HEADS, LANES, t), lambda b, i: (b, 0, 0, i))
    return pl.pallas_call(
        _fox_gate_kernel,
        grid=(bsz, seq // t),
        in_specs=[
            pl.BlockSpec((None, t, LANES), lambda b, i: (b, i, 0)),
            pl.BlockSpec((1, LANES), lambda b, i: (0, 0)),
        ],
        out_specs=[aug_t, aug],
        out_shape=[jax.ShapeDtypeStruct((bsz, B_HEADS, LANES, seq), BF16),
                   jax.ShapeDtypeStruct((bsz, B_HEADS, seq, LANES), BF16)],
        scratch_shapes=[pltpu.VMEM((1, LANES), F32)],
        compiler_params=_params(("parallel", "arbitrary")),
        name="fox_gate",
    )(raw, fb)


ONES_ROWS = 16


def _softmax_step(zt, vt, carry):
    m, acc = carry
    m_new = jnp.maximum(m, jnp.max(zt, axis=0, keepdims=True))
    p = jnp.exp2(zt - m_new)
    acc = jnp.exp2(m - m_new) * acc + jnp.dot(vt, p.astype(BF16), preferred_element_type=F32)
    return m_new, acc


def _softmax_init(tq, dh):
    return (jnp.full((1, tq), NEG, F32), jnp.zeros((dh + ONES_ROWS, tq), F32))


def _softmax_finish(carry, dh):
    _, acc = carry
    return (acc[:dh, :] / acc[dh:dh + 1, :]).T


def _fill_vt(vt_scr, v_ref, chunk):
    seq, dh = v_ref.shape
    for cix in range(seq // chunk):
        sl = slice(cix * chunk, (cix + 1) * chunk)
        vt_scr[:dh, sl] = _transpose_bf16(v_ref[sl, :])
    vt_scr[dh:, :] = jnp.ones((ONES_ROWS, seq), BF16)


ATT_LOOKAHEAD = 6


def _run_substeps(steps, carry):
    carry = list(carry)
    pending = {}
    for s in range(min(ATT_LOOKAHEAD, len(steps))):
        pending[s] = steps[s][1]()
    for s, (j, _, fix, vt) in enumerate(steps):
        if s + ATT_LOOKAHEAD < len(steps):
            pending[s + ATT_LOOKAHEAD] = steps[s + ATT_LOOKAHEAD][1]()
        z, cj = pending.pop(s), carry[j]
        if fix is not None:
            z, cj = fix(z, cj)
        carry[j] = _softmax_step(z, vt, cj)
    return carry


def _transpose_bf16(x):
    return x.astype(F32).T.astype(BF16)


_NT = (((1,), (1,)), ((), ()))


def _dot_nt(a, b):
    return lax.dot_general(a, b, _NT, preferred_element_type=F32)


def _moba_kernel(cfar_ref, q_ref, k_ref, v_ref, tab_ref, o_ref, kmean_scr, sc_scr, qt_scr, vt_scr):
    g, t = ATT_G, ATT_T
    seq, dh = k_ref.shape
    nblk = seq // t
    rows = g * t
    i = pl.program_id(2)
    base = i * g

    @pl.when(i == 0)
    def _():
        r = lax.broadcasted_iota(jnp.int32, (nblk, seq), 0)
        c = lax.broadcasted_iota(jnp.int32, (nblk, seq), 1)
        avg = jnp.where((c >= r * t) & (c < (r + 1) * t), 1.0 / t, 0.0).astype(BF16)
        kmean_scr[...] = jnp.dot(avg, k_ref[...], preferred_element_type=F32)
        _fill_vt(vt_scr, v_ref, rows)

    q = q_ref[...]
    qt_scr[:dh, :] = _transpose_bf16(q)
    sc = lax.dot_general(kmean_scr[...], q.astype(F32), _NT, precision=HIGHEST, preferred_element_type=F32)
    row = lax.broadcasted_iota(jnp.int32, (nblk, rows), 0)
    own = base + lax.broadcasted_iota(jnp.int32, (nblk, rows), 1) // t
    past = row < own
    sc = jnp.where(past, sc, -jnp.inf)
    sc_scr[...] = sc

    def rank_body(m, rank):
        sm = sc_scr[pl.ds(m, 1), :]
        return rank + jnp.where(sm > sc, 1.0, jnp.where(sm == sc, jnp.where(row > m, 1.0, 0.0), 0.0))

    rank = lax.fori_loop(0, base + g - 1, rank_body, jnp.zeros((nblk, rows), F32))
    pen = jnp.where(past, jnp.where(rank < MOBA_TOPK, 0.0, NEG), jnp.where(row == own, 0.0, NEG))
    void = jnp.where(lax.broadcasted_iota(jnp.int32, (SUBLANES, rows), 0) == SUBLANES - 1, NEG, 0.0)
    pen = jnp.concatenate([pen, jnp.zeros((LANES - SUBLANES - nblk, rows), F32), void], axis=0)
    qt_scr[dh:, :] = pen.astype(BF16)

    lane = lax.broadcasted_iota(jnp.int32, (t, LANES), 1)

    def kv(n, col):
        start = pl.multiple_of(n * t, t)
        onehot = jnp.where(lane == col, 1.0, 0.0).astype(BF16)
        return jnp.concatenate([k_ref[pl.ds(start, t), :], onehot], axis=1), vt_scr[:, pl.ds(start, t)]

    def logits(kb, j):
        return jnp.dot(kb, qt_scr[:, j * t:(j + 1) * t], preferred_element_type=F32)

    def far_body(grp, carry):
        steps = []
        for u in range(ATT_U):
            n = grp * ATT_U + u
            kb, vb = kv(n, n)
            steps += [(j, functools.partial(logits, kb, j), None, vb) for j in range(g)]
        return tuple(_run_substeps(steps, carry))

    carry = lax.fori_loop(0, jnp.maximum(base - (MOBA_FAR - 1), 0) // ATT_U, far_body,
                          tuple(_softmax_init(t, dh) for _ in range(g)))
    cfar = cfar_ref[pl.program_id(1)]

    def near_fix(delta, z, cj):
        if delta == MOBA_FAR - 1:
            m, acc = cj
            cj = (m + cfar, acc)
        return tab_ref[delta] + z, cj

    steps = []
    for k in range(-(MOBA_FAR - 1), g):
        n = base + k
        if k < 0:
            kb, vb = kv(jnp.maximum(n, 0), jnp.where(n >= 0, n, VOID_COL))
        else:
            kb, vb = kv(n, n)
        for j in range(max(k, 0), g):
            fix = functools.partial(near_fix, j - k) if j - k < MOBA_FAR else None
            steps.append((j, functools.partial(logits, kb, j), fix, vb))
    carry = _run_substeps(steps, carry)
    for j in range(g):
        o_ref[j * t:(j + 1) * t, :] = _softmax_finish(carry[j], dh).astype(o_ref.dtype)


def _moba(proj, tab):
    bsz, seq, _ = proj.shape
    g, t = ATT_G, ATT_T
    nblk = seq // t
    assert seq % (g * t) == 0 and nblk <= LANES - SUBLANES and nblk % SUBLANES == 0
    qc, kc, vc = COL_AQ // HEAD_DIM, COL_AK // HEAD_DIM, COL_AV // HEAD_DIM
    cfar = tab[:, MOBA_FAR, 0, 0]
    return pl.pallas_call(
        _moba_kernel,
        grid=(bsz, A_HEADS, nblk // g),
        in_specs=[
            pl.BlockSpec(memory_space=pltpu.SMEM),
            pl.BlockSpec((None, g * t, HEAD_DIM), lambda b, h, i: (b, i, qc + h)),
            pl.BlockSpec((None, seq, HEAD_DIM), lambda b, h, i: (b, 0, kc + h)),
            pl.BlockSpec((None, seq, HEAD_DIM), lambda b, h, i: (b, 0, vc + h)),
            pl.BlockSpec((None, MOBA_FAR, t, t), lambda b, h, i: (h, 0, 0, 0)),
        ],
        out_specs=pl.BlockSpec((None, g * t, HEAD_DIM), lambda b, h, i: (b, i, h)),
        out_shape=jax.ShapeDtypeStruct((bsz, seq, A_W), BF16),
        scratch_shapes=[pltpu.VMEM((nblk, HEAD_DIM), F32), pltpu.VMEM((nblk, g * t), F32),
                        pltpu.VMEM((HEAD_DIM + LANES, g * t), BF16), pltpu.VMEM((HEAD_DIM + ONES_ROWS, seq), BF16)],
        compiler_params=_params(("parallel", "parallel", "arbitrary")),
        name="moba",
    )(cfar, proj, proj, proj, tab)


def _fox_kernel(q_ref, qa_ref, k_ref, ka_ref, v_ref, o_ref, qt_scr, vt_scr):
    g, t = ATT_G, ATT_T
    rows = g * t
    seq, dh = k_ref.shape
    base = pl.program_id(2) * g

    @pl.when(base == 0)
    def _():
        _fill_vt(vt_scr, v_ref, rows)

    qt_scr[:dh, :] = _transpose_bf16(q_ref[...])
    qt_scr[dh:, :] = qa_ref[...]

    def kv(n):
        start = pl.multiple_of(n * t, t)
        return (jnp.concatenate([k_ref[pl.ds(start, t), :], ka_ref[pl.ds(start, t), :]], axis=1),
                vt_scr[:, pl.ds(start, t)])

    def logits(kb, j):
        return jnp.dot(kb, qt_scr[:, j * t:(j + 1) * t], preferred_element_type=F32)

    def body(grp, carry):
        steps = []
        for u in range(ATT_U):
            kb, vb = kv(grp * ATT_U + u)
            steps += [(j, functools.partial(logits, kb, j), None, vb) for j in range(g)]
        return tuple(_run_substeps(steps, carry))

    carry = lax.fori_loop(0, base // ATT_U, body, tuple(_softmax_init(t, dh) for _ in range(g)))
    causal = lax.broadcasted_iota(jnp.int32, (t, t), 0) <= lax.broadcasted_iota(jnp.int32, (t, t), 1)

    def diag_fix(z, cj):
        return jnp.where(causal, z, NEG), cj

    steps = []
    for k in range(g):
        kb, vb = kv(base + k)
        steps += [(j, functools.partial(logits, kb, j), diag_fix if j == k else None, vb) for j in range(k, g)]
    carry = _run_substeps(steps, carry)
    for j in range(g):
        o_ref[j * t:(j + 1) * t, :] = _softmax_finish(carry[j], dh).astype(o_ref.dtype)


def _fox(proj, qa, ka):
    bsz, seq, _ = proj.shape
    g, t = ATT_G, ATT_T
    assert seq % (g * t) == 0
    qc, kc, vc = COL_BQ // HEAD_DIM, COL_BK // HEAD_DIM, COL_BV // HEAD_DIM
    return pl.pallas_call(
        _fox_kernel,
        grid=(bsz, B_HEADS, seq // (g * t)),
        in_specs=[
            pl.BlockSpec((None, g * t, HEAD_DIM), lambda b, h, i: (b, i, qc + h)),
            pl.BlockSpec((None, None, LANES, g * t), lambda b, h, i: (b, h, 0, i)),
            pl.BlockSpec((None, seq, HEAD_DIM), lambda b, h, i: (b, 0, kc + h)),
            pl.BlockSpec((None, None, seq, LANES), lambda b, h, i: (b, h, 0, 0)),
            pl.BlockSpec((None, seq, HEAD_DIM), lambda b, h, i: (b, 0, vc + h)),
        ],
        out_specs=pl.BlockSpec((None, g * t, HEAD_DIM), lambda b, h, i: (b, i, h)),
        out_shape=jax.ShapeDtypeStruct((bsz, seq, B_W), BF16),
        scratch_shapes=[pltpu.VMEM((HEAD_DIM + LANES, g * t), BF16), pltpu.VMEM((HEAD_DIM + ONES_ROWS, seq), BF16)],
        compiler_params=_params(("parallel", "parallel", "arbitrary")),
        name="fox",
    )(proj, qa, proj, ka, proj)


def _swa_kernel(sink_ref, q_ref, kp_ref, kc_ref, vp_ref, vc_ref, tab_ref, o_ref):
    w = WINDOW
    dh = D_HEAD_DIM
    grp = D_Q_HEADS // D_KV_HEADS
    i = pl.program_id(1)
    q = q_ref[...]
    kcat = jnp.concatenate([kp_ref[...], kc_ref[...]], axis=0)
    vcat = jnp.concatenate([vp_ref[...], vc_ref[...]], axis=0)
    col = lax.broadcasted_iota(jnp.int32, (w, 2 * w), 1)
    keep = col >= jnp.where(i > 0, 0, w)
    outs = []
    for h in range(D_Q_HEADS):
        g = h // grp
        s = _dot_nt(q[:, h * dh:(h + 1) * dh], kcat[:, g * dh:(g + 1) * dh])
        s = jnp.where(keep, tab_ref[h] + s, NEG)
        sink = sink_ref[h]
        m = jnp.maximum(jnp.max(s, axis=-1, keepdims=True), sink)
        p = jnp.exp(s - m)
        den = jnp.sum(p, axis=-1, keepdims=True) + jnp.exp(sink - m)
        pv = jnp.dot(p.astype(BF16), vcat[:, g * dh:(g + 1) * dh], preferred_element_type=F32)
        outs.append(pv / den)
    o_ref[...] = jnp.concatenate(outs, axis=1).astype(o_ref.dtype)


def _swa(proj, sinks, tab):
    bsz, seq, _ = proj.shape
    w = WINDOW
    qc, kc, vc = COL_DQ // D_QW, COL_DK // D_KVW, COL_DV // D_KVW
    prev = lambda i: jnp.maximum(i - 1, 0)
    return pl.pallas_call(
        _swa_kernel,
        grid=(bsz, seq // w),
        in_specs=[
            pl.BlockSpec(memory_space=pltpu.SMEM),
            pl.BlockSpec((None, w, D_QW), lambda b, i: (b, i, qc)),
            pl.BlockSpec((None, w, D_KVW), lambda b, i: (b, prev(i), kc)),
            pl.BlockSpec((None, w, D_KVW), lambda b, i: (b, i, kc)),
            pl.BlockSpec((None, w, D_KVW), lambda b, i: (b, prev(i), vc)),
            pl.BlockSpec((None, w, D_KVW), lambda b, i: (b, i, vc)),
            pl.BlockSpec((D_Q_HEADS, w, 2 * w), lambda b, i: (0, 0, 0)),
        ],
        out_specs=pl.BlockSpec((None, w, D_QW), lambda b, i: (b, i, 0)),
        out_shape=jax.ShapeDtypeStruct((bsz, seq, D_QW), BF16),
        compiler_params=_params(("parallel", "parallel")),
        name="swa",
    )(sinks, proj, proj, proj, proj, proj, tab)


CONV_T = 512
CONV_HALO = 32
CONV_ROWS = 64


def _conv_kernel(a_ref, g_ref, ha_ref, hg_ref, w_ref, b_ref, lg_ref, lb_ref, o_ref, buf):
    t = a_ref.shape[0]
    halo = CONV_HALO
    hp = ha_ref[...].astype(F32) * jax.nn.sigmoid(hg_ref[...].astype(F32))
    buf[0:halo, :] = jnp.where(pl.program_id(1) > 0, hp, 0.0)
    buf[halo:halo + t, :] = a_ref[...].astype(F32) * jax.nn.sigmoid(g_ref[...].astype(F32))
    off = halo - (CONV_WIDTH - 1)
    for r in range(t // CONV_ROWS):
        acc = jnp.broadcast_to(b_ref[...], (CONV_ROWS, CONV_CH))
        for rho in range(SUBLANES):
            rows = CONV_ROWS if rho == 0 else CONV_ROWS + SUBLANES
            part = None
            for s in range(off, off + CONV_WIDTH):
                if s % SUBLANES != rho:
                    continue
                start = r * CONV_ROWS + s - rho
                term = w_ref[s - off:s - off + 1, :] * buf[start:start + rows, :]
                part = term if part is None else part + term
            acc = acc + part[rho:rho + CONV_ROWS, :]
        mu = jnp.mean(acc, axis=-1, keepdims=True)
        d = acc - mu
        var = jnp.mean(d * d, axis=-1, keepdims=True)
        y = d * lax.rsqrt(var + LN_EPS) * lg_ref[...] + lb_ref[...]
        o_ref[r * CONV_ROWS:(r + 1) * CONV_ROWS, :] = (y * jax.nn.sigmoid(y)).astype(o_ref.dtype)


def _conv(proj, conv_w, conv_b, ln_g, ln_b):
    bsz, seq, _ = proj.shape
    t = min(CONV_T, seq)
    ca, cg = COL_CU // CONV_CH, COL_CU // CONV_CH + 1
    hb = t // CONV_HALO
    prev = lambda i: jnp.maximum(i * hb - 1, 0)
    row = lambda v: v.reshape(1, CONV_CH)
    return pl.pallas_call(
        _conv_kernel,
        grid=(bsz, seq // t),
        in_specs=[
            pl.BlockSpec((None, t, CONV_CH), lambda b, i: (b, i, ca)),
            pl.BlockSpec((None, t, CONV_CH), lambda b, i: (b, i, cg)),
            pl.BlockSpec((None, CONV_HALO, CONV_CH), lambda b, i: (b, prev(i), ca)),
            pl.BlockSpec((None, CONV_HALO, CONV_CH), lambda b, i: (b, prev(i), cg)),
            pl.BlockSpec((CONV_WIDTH, CONV_CH), lambda b, i: (0, 0)),
            pl.BlockSpec((1, CONV_CH), lambda b, i: (0, 0)),
            pl.BlockSpec((1, CONV_CH), lambda b, i: (0, 0)),
            pl.BlockSpec((1, CONV_CH), lambda b, i: (0, 0)),
        ],
        out_specs=pl.BlockSpec((None, t, CONV_CH), lambda b, i: (b, i, 0)),
        out_shape=jax.ShapeDtypeStruct((bsz, seq, CONV_CH), BF16),
        scratch_shapes=[pltpu.VMEM((CONV_HALO + t, CONV_CH), F32)],
        compiler_params=_params(("parallel", "parallel")),
        name="conv",
    )(proj, proj, proj, proj, conv_w, row(conv_b), row(ln_g), row(ln_b))


def _merge_kernel(ya_ref, yb_ref, yc_ref, yd_ref, gates_ref, x_ref, gt_ref, gp_ref,
                  wa_ref, wb_ref, wc_ref, wd_ref, wo_ref, o_ref):
    d = x_ref.shape[1]
    y = None
    for i, (br, w) in enumerate(((ya_ref, wa_ref), (yb_ref, wb_ref), (yc_ref, wc_ref), (yd_ref, wd_ref))):
        t = jnp.dot(br[...], w[...], preferred_element_type=F32) * gates_ref[:, i * d:(i + 1) * d].astype(F32)
        y = t if y is None else y + t
    z = jnp.dot(y.astype(BF16), wo_ref[...], preferred_element_type=F32)
    o_ref[...] = x_ref[...] + gt_ref[...] * (_rms(z) * gp_ref[...])


def _merge(ya, yb, yc, yd, gates, x, gt, g_post, wa, wb, wc, wd, wo, layer, *, tm):
    bsz, seq, d = x.shape
    tok = lambda width: pl.BlockSpec((None, tm, width), lambda b, i: (b, i, 0))
    full = lambda a: pl.BlockSpec((None,) + a.shape[1:], lambda b, i: (layer, 0, 0),
                                  pipeline_mode=pl.Buffered(1))
    return pl.pallas_call(
        _merge_kernel,
        grid=(bsz, seq // tm),
        in_specs=[tok(A_W), tok(B_W), tok(CONV_CH), tok(D_QW), tok(N_BRANCHES * d), tok(d),
                  pl.BlockSpec((None, 1, d), lambda b, i: (b, 0, 0)),
                  pl.BlockSpec((1, d), lambda b, i: (0, 0)),
                  full(wa), full(wb), full(wc), full(wd), full(wo)],
        out_specs=tok(d),
        out_shape=jax.ShapeDtypeStruct((bsz, seq, d), F32),
        compiler_params=_params(("parallel", "parallel")),
        name="merge",
    )(ya, yb, yc, yd, gates, x, gt, g_post.reshape(1, d), wa, wb, wc, wd, wo)


def _ffn_kernel(x_ref, g_ref, sc_ref, sh_ref, gt_ref, gp_ref, wg_ref, wu_ref, wd_ref, o_ref, h_scr, acc_scr):
    f = pl.program_id(2)

    @pl.when(f == 0)
    def _():
        h_scr[...] = _modulated_norm(x_ref[...], g_ref[...], sc_ref[...], sh_ref[...]).astype(BF16)
        acc_scr[...] = jnp.zeros_like(acc_scr)

    h = h_scr[...]
    g = jnp.dot(h, wg_ref[...], preferred_element_type=F32)
    u = jnp.dot(h, wu_ref[...], preferred_element_type=F32)
    a = ((g * jax.nn.sigmoid(g)) * u).astype(BF16)
    acc_scr[...] += jnp.dot(a, wd_ref[...], preferred_element_type=F32)

    @pl.when(f == pl.num_programs(2) - 1)
    def _():
        o_ref[...] = x_ref[...] + gt_ref[...] * (_rms(acc_scr[...]) * gp_ref[...])


def _ffn(x, g_pre, sc, sh, gt, g_post, wg, wu, wd, layer, *, tm, tf):
    bsz, seq, d = x.shape
    dff = wg.shape[2]
    vec = lambda: pl.BlockSpec((None, 1, d), lambda b, i, f: (b, 0, 0))
    par = lambda: pl.BlockSpec((1, d), lambda b, i, f: (0, 0))
    return pl.pallas_call(
        _ffn_kernel,
        grid=(bsz, seq // tm, dff // tf),
        in_specs=[
            pl.BlockSpec((None, tm, d), lambda b, i, f: (b, i, 0)),
            par(), vec(), vec(), vec(), par(),
            pl.BlockSpec((None, d, tf), lambda b, i, f: (layer, 0, f)),
            pl.BlockSpec((None, d, tf), lambda b, i, f: (layer, 0, f)),
            pl.BlockSpec((None, tf, d), lambda b, i, f: (layer, f, 0)),
        ],
        out_specs=pl.BlockSpec((None, tm, d), lambda b, i, f: (b, i, 0)),
        out_shape=jax.ShapeDtypeStruct((bsz, seq, d), F32),
        scratch_shapes=[pltpu.VMEM((tm, d), BF16), pltpu.VMEM((tm, d), F32)],
        compiler_params=_params(("parallel", "parallel", "arbitrary")),
        name="ffn",
    )(x, g_pre.reshape(1, d), sc, sh, gt, g_post.reshape(1, d), wg, wu, wd)


def _pack_in_proj(w):
    d = w.shape[0]
    splits = (A_W, A_W, A_W, B_W, B_W, B_W, B_HEADS, 2 * CONV_CH, D_QW, D_KVW, D_KVW, N_BRANCHES * d)
    offs = np.cumsum(splits)[:-1].tolist()
    aq, ak, av, bq, bk, bv, bf, cu, dq, dk, dv, gl = jnp.split(w, offs, axis=1)
    s2 = HEAD_DIM ** -0.5 * LOG2E
    parts = [gl, aq * s2, ak, av, bq * s2, bk, bv, cu, dq * D_HEAD_DIM ** -0.5, dk, dv]
    packed = jnp.concatenate([p.astype(BF16) for p in parts] + [jnp.zeros((d, PROJ_W_PAD - PROJ_W), BF16)], axis=1)
    side = jnp.concatenate([bf.astype(BF16), jnp.zeros((d, LANES - B_HEADS), BF16)], axis=1)
    return packed, side


CAST_BLOCK_ELEMS = 1 << 20


def _cast_kernel(w_ref, o_ref):
    o_ref[...] = w_ref[...].astype(o_ref.dtype)


def _cast_bf16(w):
    depth, r, c = w.shape
    tr = 16
    while tr * 2 * c <= CAST_BLOCK_ELEMS and r % (tr * 2) == 0:
        tr *= 2
    assert r % tr == 0
    return pl.pallas_call(
        _cast_kernel,
        grid=(depth, r // tr),
        in_specs=[pl.BlockSpec((None, tr, c), lambda l, i: (l, i, 0))],
        out_specs=pl.BlockSpec((None, tr, c), lambda l, i: (l, i, 0)),
        out_shape=jax.ShapeDtypeStruct(w.shape, BF16),
        compiler_params=_params(("parallel", "parallel")),
        name="cast_bf16",
    )(w)


def kernel(x, c, rel_bias, w_mod, b_mod, mix_norm_pre, mix_norm_post, w_in, fox_bias, conv_w, conv_b, conv_ln_g, conv_ln_b, sinks, w_branch_a, w_branch_b, w_branch_c, w_branch_d, w_out, ffn_norm_pre, ffn_norm_post, w_ffn_gate, w_ffn_up, w_ffn_down):
    depth = w_mod.shape[0]
    bsz, seq, d = x.shape
    tm = min(1024, seq)
    tm_small = min(256, seq)
    tm_ffn = min(512, seq)

    mod = _modulation(c, w_mod, b_mod)
    tab_a = _bias_tables(rel_bias[:, :A_HEADS], _moba_buckets(), scale=LOG2E)
    tab_d = _bias_tables(rel_bias[:, A_HEADS:], _swa_buckets(), scale=1.0)[:, 0]

    wa, wb, wc, wd, wo = [_cast_bf16(w) for w in (w_branch_a, w_branch_b, w_branch_c, w_branch_d, w_out)]
    wg, wu, wdn = [_cast_bf16(w) for w in (w_ffn_gate, w_ffn_up, w_ffn_down)]

    for l in range(depth):
        sh_m, sc_m, gt_m, sh_f, sc_f, gt_f = [mod[l, :, None, i * d:(i + 1) * d] for i in range(6)]
        w_proj, w_side = _pack_in_proj(w_in[l])

        proj, fox_raw = _norm_proj(x, mix_norm_pre[l], sc_m, sh_m, w_proj, w_side, tm=tm, tn=PROJ_TN)
        fox_qa, fox_ka = _fox_gate(fox_raw, fox_bias[l], t=min(512, seq))

        ya = _moba(proj, tab_a)
        yb = _fox(proj, fox_qa, fox_ka)
        yc = _conv(proj, conv_w[l], conv_b[l], conv_ln_g[l], conv_ln_b[l])
        yd = _swa(proj, sinks[l], tab_d)

        x = _merge(ya, yb, yc, yd, proj, x, gt_m, mix_norm_post[l], wa, wb, wc, wd, wo, l, tm=tm_small)
        x = _ffn(x, ffn_norm_pre[l], sc_f, sh_f, gt_f, ffn_norm_post[l], wg, wu, wdn, l, tm=tm_ffn, tf=512)
    return x
```

```python
import functools
import math

import jax
import jax.numpy as jnp
import numpy as np
from jax import lax
from jax.experimental import pallas as pl
from jax.experimental.pallas import tpu as pltpu

F32 = jnp.float32
BF16 = jnp.bfloat16
HIGHEST = lax.Precision.HIGHEST

HEAD_DIM = 128
A_HEADS = 4
MOBA_BLOCK = 256
MOBA_TOPK = 3
B_HEADS = 4
CONV_CH = 512
CONV_WIDTH = 31
D_Q_HEADS = 8
D_KV_HEADS = 2
D_HEAD_DIM = 64
WINDOW = 128
N_BUCKETS = 32
MAX_DISTANCE = 1024
N_BRANCHES = 4
RMS_EPS = 1e-6
LN_EPS = 1e-5

A_W = A_HEADS * HEAD_DIM
B_W = B_HEADS * HEAD_DIM
D_QW = D_Q_HEADS * D_HEAD_DIM
D_KVW = D_KV_HEADS * D_HEAD_DIM

LANES = 128
SUBLANES = 8
VMEM_LIMIT = 56 * 1024 * 1024

NEG = -1e30
LOG2E = math.log2(math.e)
ATT_G = 4
ATT_T = MOBA_BLOCK
ATT_U = 4

D_MODEL = 2048
GATE_W = N_BRANCHES * D_MODEL
COL_AQ, COL_AK, COL_AV = GATE_W, GATE_W + A_W, GATE_W + 2 * A_W
COL_BQ, COL_BK, COL_BV = COL_AV + A_W, COL_AV + A_W + B_W, COL_AV + A_W + 2 * B_W
COL_CU = COL_BV + B_W
COL_DQ = COL_CU + 2 * CONV_CH
COL_DK = COL_DQ + D_QW
COL_DV = COL_DK + D_KVW
PROJ_W = COL_DV + D_KVW
PROJ_TN = 1024
PROJ_W_PAD = -(-PROJ_W // PROJ_TN) * PROJ_TN
MOBA_FAR = -(-(MAX_DISTANCE + MOBA_BLOCK - 1) // MOBA_BLOCK)
VOID_COL = LANES - 1
assert ATT_G % ATT_U == 0 and (MOBA_FAR - 1) % ATT_U == 0


def _params(sem, vmem=VMEM_LIMIT):
    return pltpu.CompilerParams(dimension_semantics=sem, vmem_limit_bytes=vmem)


def _t5_bucket(dist):
    max_exact = N_BUCKETS // 2
    d = jnp.maximum(dist, 0)
    log_ratio = jnp.log(jnp.maximum(d, 1).astype(jnp.float32) / max_exact) / math.log(MAX_DISTANCE / max_exact)
    large = max_exact + (log_ratio * (N_BUCKETS - max_exact)).astype(jnp.int32)
    large = jnp.minimum(large, N_BUCKETS - 1)
    return jnp.where(d < max_exact, d, large)


def _rms(y):
    return y * lax.rsqrt(jnp.mean(y * y, axis=-1, keepdims=True) + RMS_EPS)


def _modulated_norm(x, g, sc, sh):
    return (_rms(x) * g) * (1.0 + sc) + sh


MOD_TN = 512
MOD_KC = 256


def _mod_kernel(ct_ref, w_ref, b_ref, o_ref):
    d, nb = ct_ref.shape
    tn = w_ref.shape[1]
    ct = ct_ref[...]
    ca = ct * jax.nn.sigmoid(ct)
    rows = []
    for r in range(nb):
        acc = jnp.zeros((1, tn), F32)
        for kc in range(d // MOD_KC):
            sl = slice(kc * MOD_KC, (kc + 1) * MOD_KC)
            acc = acc + jnp.sum(w_ref[sl, :] * ca[sl, r:r + 1], axis=0, keepdims=True)
        rows.append(acc)
    o_ref[...] = jnp.concatenate(rows, axis=0) + b_ref[...]


def _modulation(c, w_mod, b_mod):
    depth, d, n = w_mod.shape
    bsz = c.shape[0]
    return pl.pallas_call(
        _mod_kernel,
        grid=(depth, n // MOD_TN),
        in_specs=[
            pl.BlockSpec((d, bsz), lambda l, j: (0, 0)),
            pl.BlockSpec((None, d, MOD_TN), lambda l, j: (l, 0, j)),
            pl.BlockSpec((None, 1, MOD_TN), lambda l, j: (l, 0, j)),
        ],
        out_specs=pl.BlockSpec((None, bsz, MOD_TN), lambda l, j: (l, 0, j)),
        out_shape=jax.ShapeDtypeStruct((depth, bsz, n), F32),
        compiler_params=_params(("parallel", "parallel")),
        name="mod",
    )(c.T, w_mod, b_mod.reshape(depth, 1, n))


def _bias_table_kernel(rb_ref, bucket_ref, o_ref, *, scale):
    h = pl.program_id(0)
    b = bucket_ref[...]
    acc = jnp.where(b < 0, NEG, 0.0).astype(F32)
    for u in range(N_BUCKETS):
        acc = jnp.where(b == u, rb_ref[h, u] * scale, acc)
    o_ref[...] = acc


def _bias_tables(rel_bias_heads, bucket, *, scale):
    nh = rel_bias_heads.shape[1]
    nt, r, c = bucket.shape
    return pl.pallas_call(
        functools.partial(_bias_table_kernel, scale=scale),
        grid=(nh, nt),
        in_specs=[
            pl.BlockSpec(memory_space=pltpu.SMEM),
            pl.BlockSpec((None, r, c), lambda h, t: (t, 0, 0)),
        ],
        out_specs=pl.BlockSpec((None, None, r, c), lambda h, t: (h, t, 0, 0)),
        out_shape=jax.ShapeDtypeStruct((nh, nt, r, c), F32),
        compiler_params=_params(("parallel", "parallel")),
        name="bias_table",
    )(rel_bias_heads.T, bucket)


def _moba_buckets():
    blk = MOBA_BLOCK
    i = jnp.arange(blk)[None, :]
    j = jnp.arange(blk)[:, None]
    tabs = []
    for delta in range(MOBA_FAR):
        dist = delta * blk + i - j
        tabs.append(jnp.where(dist >= 0, _t5_bucket(dist), -1))
    tabs.append(_t5_bucket(jnp.full((blk, blk), MAX_DISTANCE, jnp.int32)))
    return jnp.stack(tabs).astype(jnp.int32)


def _swa_buckets():
    qi = jnp.arange(WINDOW)[:, None]
    kj = jnp.arange(2 * WINDOW)[None, :]
    dist = qi + WINDOW - kj
    in_win = (dist >= 0) & (dist < WINDOW)
    return jnp.where(in_win, _t5_bucket(dist), -1).astype(jnp.int32)[None]


def _norm_proj_kernel(x_ref, g_ref, sc_ref, sh_ref, w_ref, ws_ref, o_ref, os_ref, h_scr):
    j = pl.program_id(2)
    tn = w_ref.shape[1]

    @pl.when(j == 0)
    def _():
        h = _modulated_norm(x_ref[...], g_ref[...], sc_ref[...], sh_ref[...]).astype(BF16)
        h_scr[...] = h
        os_ref[...] = jnp.dot(h, ws_ref[...], preferred_element_type=F32)

    @pl.when(j < GATE_W // tn)
    def _():
        acc = jnp.dot(h_scr[...], w_ref[...], preferred_element_type=F32)
        o_ref[...] = jax.nn.sigmoid(acc).astype(o_ref.dtype)

    @pl.when(j >= GATE_W // tn)
    def _():
        o_ref[...] = jnp.dot(h_scr[...], w_ref[...], preferred_element_type=F32).astype(o_ref.dtype)


def _norm_proj(x, g, sc, sh, w, w_side, layer, *, tm, tn):
    bsz, seq, d = x.shape
    n = w.shape[2]
    ns = w_side.shape[2]
    assert GATE_W % tn == 0 and n % tn == 0
    return pl.pallas_call(
        _norm_proj_kernel,
        grid=(bsz, seq // tm, n // tn),
        in_specs=[
            pl.BlockSpec((None, tm, d), lambda b, i, j: (b, i, 0)),
            pl.BlockSpec((1, d), lambda b, i, j: (0, 0)),
            pl.BlockSpec((None, 1, d), lambda b, i, j: (b, 0, 0)),
            pl.BlockSpec((None, 1, d), lambda b, i, j: (b, 0, 0)),
            pl.BlockSpec((None, d, tn), lambda b, i, j: (layer, 0, j)),
            pl.BlockSpec((None, d, ns), lambda b, i, j: (layer, 0, 0)),
        ],
        out_specs=[pl.BlockSpec((None, tm, tn), lambda b, i, j: (b, i, j)),
                   pl.BlockSpec((None, tm, ns), lambda b, i, j: (b, i, 0))],
        out_shape=[jax.ShapeDtypeStruct((bsz, seq, n), BF16), jax.ShapeDtypeStruct((bsz, seq, ns), F32)],
        scratch_shapes=[pltpu.VMEM((tm, d), BF16)],
        compiler_params=_params(("parallel", "parallel", "arbitrary")),
        name="norm_proj",
    )(x, g.reshape(1, d), sc, sh, w, w_side)


def _split3(c):
    hi = c.astype(BF16).astype(F32)
    mid = (c - hi).astype(BF16).astype(F32)
    lo = (c - hi - mid).astype(BF16).astype(F32)
    return hi, mid, lo


def _fox_gate_kernel(x_ref, fb_ref, qa_ref, ka_ref, carry_scr):
    t = x_ref.shape[0]

    @pl.when(pl.program_id(1) == 0)
    def _():
        carry_scr[...] = jnp.zeros_like(carry_scr)

    lf = jax.nn.log_sigmoid(x_ref[...] + fb_ref[...])
    tri = (lax.broadcasted_iota(jnp.int32, (t, t), 0) >= lax.broadcasted_iota(jnp.int32, (t, t), 1)).astype(F32)
    c = jnp.dot(tri, lf, precision=HIGHEST, preferred_element_type=F32) + carry_scr[...]
    carry_scr[...] = c[t - 1:t, :]
    lane = lax.broadcasted_iota(jnp.int32, (t, LANES), 1)
    sub = lax.broadcasted_iota(jnp.int32, (SUBLANES, t), 0)
    ct = c.T * LOG2E
    for h in range(B_HEADS):
        hi, mid, lo = _split3(jnp.broadcast_to(c[:, h:h + 1], (t, LANES)) * LOG2E)
        ka = jnp.where(lane == 0, -hi, jnp.where(lane == 1, -mid, jnp.where(lane == 2, -lo, jnp.where(lane < 6, 1.0, 0.0))))
        ka_ref[h] = ka.astype(BF16)
        hi, mid, lo = _split3(jnp.broadcast_to(ct[h:h + 1, :], (SUBLANES, t)))
        qa = jnp.where(sub < 3, 1.0, jnp.where(sub == 3, hi, jnp.where(sub == 4, mid, jnp.where(sub == 5, lo, 0.0))))
        qa_ref[h] = jnp.concatenate([qa, jnp.zeros((LANES - SUBLANES, t), F32)], axis=0).astype(BF16)


def _fox_gate(raw, fox_b, *, t):
    bsz, seq, _ = raw.shape
    fb = jnp.zeros((1, LANES), F32).at[0, :B_HEADS].set(fox_b)
    aug = pl.BlockSpec((None, B_HEADS, t, LANES), lambda b, i: (b, 0, i, 0))
    aug_t = pl.BlockSpec((None, B_HEADS, LANES, t), lambda b, i: (b, 0, 0, i))
    return pl.pallas_call(
        _fox_gate_kernel,
        grid=(bsz, seq // t),
        in_specs=[
            pl.BlockSpec((None, t, LANES), lambda b, i: (b, i, 0)),
            pl.BlockSpec((1, LANES), lambda b, i: (0, 0)),
        ],
        out_specs=[aug_t, aug],
        out_shape=[jax.ShapeDtypeStruct((bsz, B_HEADS, LANES, seq), BF16),
                   jax.ShapeDtypeStruct((bsz, B_HEADS, seq, LANES), BF16)],
        scratch_shapes=[pltpu.VMEM((1, LANES), F32)],
        compiler_params=_params(("parallel", "arbitrary")),
        name="fox_gate",
    )(raw, fb)


ONES_ROWS = 16


def _softmax_step(zt, vt, carry):
    m, acc = carry
    m_new = jnp.maximum(m, jnp.max(zt, axis=0, keepdims=True))
    p = jnp.exp2(zt - m_new)
    acc = jnp.exp2(m - m_new) * acc + jnp.dot(vt, p.astype(BF16), preferred_element_type=F32)
    return m_new, acc


def _softmax_init(tq, dh):
    return (jnp.full((1, tq), NEG, F32), jnp.zeros((dh + ONES_ROWS, tq), F32))


def _softmax_finish(carry, dh):
    _, acc = carry
    return (acc[:dh, :] / acc[dh:dh + 1, :]).T


def _fill_vt(vt_scr, v_ref, chunk):
    seq, dh = v_ref.shape
    for cix in range(seq // chunk):
        sl = slice(cix * chunk, (cix + 1) * chunk)
        vt_scr[:dh, sl] = _transpose_bf16(v_ref[sl, :])
    vt_scr[dh:, :] = jnp.ones((ONES_ROWS, seq), BF16)


ATT_LOOKAHEAD = 6


def _run_substeps(steps, carry):
    carry = list(carry)
    pending = {}
    for s in range(min(ATT_LOOKAHEAD, len(steps))):
        pending[s] = steps[s][1]()
    for s, (j, _, fix, vt) in enumerate(steps):
        if s + ATT_LOOKAHEAD < len(steps):
            pending[s + ATT_LOOKAHEAD] = steps[s + ATT_LOOKAHEAD][1]()
        z, cj = pending.pop(s), carry[j]
        if fix is not None:
            z, cj = fix(z, cj)
        carry[j] = _softmax_step(z, vt, cj)
    return carry


def _transpose_bf16(x):
    return x.astype(F32).T.astype(BF16)


_NT = (((1,), (1,)), ((), ()))


def _dot_nt(a, b):
    return lax.dot_general(a, b, _NT, preferred_element_type=F32)


def _moba_kernel(cfar_ref, q_ref, k_ref, v_ref, tab_ref, o_ref, kmean_scr, sc_scr, qt_scr, vt_scr):
    g, t = ATT_G, ATT_T
    seq, dh = k_ref.shape
    nblk = seq // t
    rows = g * t
    i = pl.program_id(2)
    base = i * g

    @pl.when(i == 0)
    def _():
        r = lax.broadcasted_iota(jnp.int32, (nblk, seq), 0)
        c = lax.broadcasted_iota(jnp.int32, (nblk, seq), 1)
        avg = jnp.where((c >= r * t) & (c < (r + 1) * t), 1.0 / t, 0.0).astype(BF16)
        kmean_scr[...] = jnp.dot(avg, k_ref[...], preferred_element_type=F32)
        _fill_vt(vt_scr, v_ref, rows)

    q = q_ref[...]
    qt_scr[:dh, :] = _transpose_bf16(q)
    sc = lax.dot_general(kmean_scr[...], q.astype(F32), _NT, precision=HIGHEST, preferred_element_type=F32)
    row = lax.broadcasted_iota(jnp.int32, (nblk, rows), 0)
    own = base + lax.broadcasted_iota(jnp.int32, (nblk, rows), 1) // t
    past = row < own
    sc = jnp.where(past, sc, -jnp.inf)
    sc_scr[...] = sc

    def rank_body(m, rank):
        sm = sc_scr[pl.ds(m, 1), :]
        return rank + jnp.where(sm > sc, 1.0, jnp.where(sm == sc, jnp.where(row > m, 1.0, 0.0), 0.0))

    rank = lax.fori_loop(0, base + g - 1, rank_body, jnp.zeros((nblk, rows), F32))
    pen = jnp.where(past, jnp.where(rank < MOBA_TOPK, 0.0, NEG), jnp.where(row == own, 0.0, NEG))
    void = jnp.where(lax.broadcasted_iota(jnp.int32, (SUBLANES, rows), 0) == SUBLANES - 1, NEG, 0.0)
    pen = jnp.concatenate([pen, jnp.zeros((LANES - SUBLANES - nblk, rows), F32), void], axis=0)
    qt_scr[dh:, :] = pen.astype(BF16)

    lane = lax.broadcasted_iota(jnp.int32, (t, LANES), 1)

    def kv(n, col):
        start = pl.multiple_of(n * t, t)
        onehot = jnp.where(lane == col, 1.0, 0.0).astype(BF16)
        return jnp.concatenate([k_ref[pl.ds(start, t), :], onehot], axis=1), vt_scr[:, pl.ds(start, t)]

    def logits(kb, j):
        return jnp.dot(kb, qt_scr[:, j * t:(j + 1) * t], preferred_element_type=F32)

    def far_body(grp, carry):
        steps = []
        for u in range(ATT_U):
            n = grp * ATT_U + u
            kb, vb = kv(n, n)
            steps += [(j, functools.partial(logits, kb, j), None, vb) for j in range(g)]
        return tuple(_run_substeps(steps, carry))

    carry = lax.fori_loop(0, jnp.maximum(base - (MOBA_FAR - 1), 0) // ATT_U, far_body,
                          tuple(_softmax_init(t, dh) for _ in range(g)))
    cfar = cfar_ref[pl.program_id(1)]

    def near_fix(delta, z, cj):
        if delta == MOBA_FAR - 1:
            m, acc = cj
            cj = (m + cfar, acc)
        return tab_ref[delta] + z, cj

    steps = []
    for k in range(-(MOBA_FAR - 1), g):
        n = base + k
        if k < 0:
            kb, vb = kv(jnp.maximum(n, 0), jnp.where(n >= 0, n, VOID_COL))
        else:
            kb, vb = kv(n, n)
        for j in range(max(k, 0), g):
            fix = functools.partial(near_fix, j - k) if j - k < MOBA_FAR else None
            steps.append((j, functools.partial(logits, kb, j), fix, vb))
    carry = _run_substeps(steps, carry)
    for j in range(g):
        o_ref[j * t:(j + 1) * t, :] = _softmax_finish(carry[j], dh).astype(o_ref.dtype)


def _moba(proj, tab):
    bsz, seq, _ = proj.shape
    g, t = ATT_G, ATT_T
    nblk = seq // t
    assert seq % (g * t) == 0 and nblk <= LANES - SUBLANES and nblk % SUBLANES == 0
    qc, kc, vc = COL_AQ // HEAD_DIM, COL_AK // HEAD_DIM, COL_AV // HEAD_DIM
    cfar = tab[:, MOBA_FAR, 0, 0]
    return pl.pallas_call(
        _moba_kernel,
        grid=(bsz, A_HEADS, nblk // g),
        in_specs=[
            pl.BlockSpec(memory_space=pltpu.SMEM),
            pl.BlockSpec((None, g * t, HEAD_DIM), lambda b, h, i: (b, i, qc + h)),
            pl.BlockSpec((None, seq, HEAD_DIM), lambda b, h, i: (b, 0, kc + h)),
            pl.BlockSpec((None, seq, HEAD_DIM), lambda b, h, i: (b, 0, vc + h)),
            pl.BlockSpec((None, MOBA_FAR, t, t), lambda b, h, i: (h, 0, 0, 0)),
        ],
        out_specs=pl.BlockSpec((None, g * t, HEAD_DIM), lambda b, h, i: (b, i, h)),
        out_shape=jax.ShapeDtypeStruct((bsz, seq, A_W), BF16),
        scratch_shapes=[pltpu.VMEM((nblk, HEAD_DIM), F32), pltpu.VMEM((nblk, g * t), F32),
                        pltpu.VMEM((HEAD_DIM + LANES, g * t), BF16), pltpu.VMEM((HEAD_DIM + ONES_ROWS, seq), BF16)],
        compiler_params=_params(("parallel", "parallel", "arbitrary")),
        name="moba",
    )(cfar, proj, proj, proj, tab)


def _fox_kernel(q_ref, qa_ref, k_ref, ka_ref, v_ref, o_ref, qt_scr, vt_scr):
    g, t = ATT_G, ATT_T
    rows = g * t
    seq, dh = k_ref.shape
    base = pl.program_id(2) * g

    @pl.when(base == 0)
    def _():
        _fill_vt(vt_scr, v_ref, rows)

    qt_scr[:dh, :] = _transpose_bf16(q_ref[...])
    qt_scr[dh:, :] = qa_ref[...]

    def kv(n):
        start = pl.multiple_of(n * t, t)
        return (jnp.concatenate([k_ref[pl.ds(start, t), :], ka_ref[pl.ds(start, t), :]], axis=1),
                vt_scr[:, pl.ds(start, t)])

    def logits(kb, j):
        return jnp.dot(kb, qt_scr[:, j * t:(j + 1) * t], preferred_element_type=F32)

    def body(grp, carry):
        steps = []
        for u in range(ATT_U):
            kb, vb = kv(grp * ATT_U + u)
            steps += [(j, functools.partial(logits, kb, j), None, vb) for j in range(g)]
        return tuple(_run_substeps(steps, carry))

    carry = lax.fori_loop(0, base // ATT_U, body, tuple(_softmax_init(t, dh) for _ in range(g)))
    causal = lax.broadcasted_iota(jnp.int32, (t, t), 0) <= lax.broadcasted_iota(jnp.int32, (t, t), 1)

    def diag_fix(z, cj):
        return jnp.where(causal, z, NEG), cj

    steps = []
    for k in range(g):
        kb, vb = kv(base + k)
        steps += [(j, functools.partial(logits, kb, j), diag_fix if j == k else None, vb) for j in range(k, g)]
    carry = _run_substeps(steps, carry)
    for j in range(g):
        o_ref[j * t:(j + 1) * t, :] = _softmax_finish(carry[j], dh).astype(o_ref.dtype)


def _fox(proj, qa, ka):
    bsz, seq, _ = proj.shape
    g, t = ATT_G, ATT_T
    assert seq % (g * t) == 0
    qc, kc, vc = COL_BQ // HEAD_DIM, COL_BK // HEAD_DIM, COL_BV // HEAD_DIM
    return pl.pallas_call(
        _fox_kernel,
        grid=(bsz, B_HEADS, seq // (g * t)),
        in_specs=[
            pl.BlockSpec((None, g * t, HEAD_DIM), lambda b, h, i: (b, i, qc + h)),
            pl.BlockSpec((None, None, LANES, g * t), lambda b, h, i: (b, h, 0, i)),
            pl.BlockSpec((None, seq, HEAD_DIM), lambda b, h, i: (b, 0, kc + h)),
            pl.BlockSpec((None, None, seq, LANES), lambda b, h, i: (b, h, 0, 0)),
            pl.BlockSpec((None, seq, HEAD_DIM), lambda b, h, i: (b, 0, vc + h)),
        ],
        out_specs=pl.BlockSpec((None, g * t, HEAD_DIM), lambda b, h, i: (b, i, h)),
        out_shape=jax.ShapeDtypeStruct((bsz, seq, B_W), BF16),
        scratch_shapes=[pltpu.VMEM((HEAD_DIM + LANES, g * t), BF16), pltpu.VMEM((HEAD_DIM + ONES_ROWS, seq), BF16)],
        compiler_params=_params(("parallel", "parallel", "arbitrary")),
        name="fox",
    )(proj, qa, proj, ka, proj)


SWA_BLOCKS = 4


def _swa_kernel(sink_ref, q_ref, kp_ref, kc_ref, vp_ref, vc_ref, tab_ref, o_ref):
    w = WINDOW
    dh = D_HEAD_DIM
    grp = D_Q_HEADS // D_KV_HEADS
    i = pl.program_id(1)
    kall = jnp.concatenate([kp_ref[...], kc_ref[...]], axis=0)
    vall = jnp.concatenate([vp_ref[...], vc_ref[...]], axis=0)
    col = lax.broadcasted_iota(jnp.int32, (w, 2 * w), 1)
    first = col >= jnp.where(i > 0, 0, w)
    for blk in range(q_ref.shape[0] // w):
        q = q_ref[blk * w:(blk + 1) * w, :]
        kcat = kall[blk * w:(blk + 2) * w, :]
        vcat = vall[blk * w:(blk + 2) * w, :]
        outs = []
        for h in range(D_Q_HEADS):
            g = h // grp
            s = tab_ref[h] + _dot_nt(q[:, h * dh:(h + 1) * dh], kcat[:, g * dh:(g + 1) * dh])
            if blk == 0:
                s = jnp.where(first, s, NEG)
            sink = sink_ref[h]
            m = jnp.maximum(jnp.max(s, axis=-1, keepdims=True), sink)
            p = jnp.exp(s - m)
            den = jnp.sum(p, axis=-1, keepdims=True) + jnp.exp(sink - m)
            pv = jnp.dot(p.astype(BF16), vcat[:, g * dh:(g + 1) * dh], preferred_element_type=F32)
            outs.append(pv / den)
        o_ref[blk * w:(blk + 1) * w, :] = jnp.concatenate(outs, axis=1).astype(o_ref.dtype)


def _swa(proj, sinks, tab):
    bsz, seq, _ = proj.shape
    w = WINDOW
    nb = SWA_BLOCKS
    assert seq % (nb * w) == 0
    qc, kc, vc = COL_DQ // D_QW, COL_DK // D_KVW, COL_DV // D_KVW
    prev = lambda i: jnp.maximum(i * nb - 1, 0)
    return pl.pallas_call(
        _swa_kernel,
        grid=(bsz, seq // (nb * w)),
        in_specs=[
            pl.BlockSpec(memory_space=pltpu.SMEM),
            pl.BlockSpec((None, nb * w, D_QW), lambda b, i: (b, i, qc)),
            pl.BlockSpec((None, w, D_KVW), lambda b, i: (b, prev(i), kc)),
            pl.BlockSpec((None, nb * w, D_KVW), lambda b, i: (b, i, kc)),
            pl.BlockSpec((None, w, D_KVW), lambda b, i: (b, prev(i), vc)),
            pl.BlockSpec((None, nb * w, D_KVW), lambda b, i: (b, i, vc)),
            pl.BlockSpec((D_Q_HEADS, w, 2 * w), lambda b, i: (0, 0, 0)),
        ],
        out_specs=pl.BlockSpec((None, nb * w, D_QW), lambda b, i: (b, i, 0)),
        out_shape=jax.ShapeDtypeStruct((bsz, seq, D_QW), BF16),
        compiler_params=_params(("parallel", "parallel")),
        name="swa",
    )(sinks, proj, proj, proj, proj, proj, tab)


CONV_T = 512
CONV_HALO = 32
CONV_ROWS = 64


def _conv_kernel(a_ref, g_ref, ha_ref, hg_ref, w_ref, b_ref, lg_ref, lb_ref, o_ref, buf):
    t = a_ref.shape[0]
    halo = CONV_HALO
    hp = ha_ref[...].astype(F32) * jax.nn.sigmoid(hg_ref[...].astype(F32))
    buf[0:halo, :] = jnp.where(pl.program_id(1) > 0, hp, 0.0)
    buf[halo:halo + t, :] = a_ref[...].astype(F32) * jax.nn.sigmoid(g_ref[...].astype(F32))
    off = halo - (CONV_WIDTH - 1)
    for r in range(t // CONV_ROWS):
        acc = jnp.broadcast_to(b_ref[...], (CONV_ROWS, CONV_CH))
        for rho in range(SUBLANES):
            rows = CONV_ROWS if rho == 0 else CONV_ROWS + SUBLANES
            part = None
            for s in range(off, off + CONV_WIDTH):
                if s % SUBLANES != rho:
                    continue
                start = r * CONV_ROWS + s - rho
                term = w_ref[s - off:s - off + 1, :] * buf[start:start + rows, :]
                part = term if part is None else part + term
            acc = acc + part[rho:rho + CONV_ROWS, :]
        mu = jnp.mean(acc, axis=-1, keepdims=True)
        d = acc - mu
        var = jnp.mean(d * d, axis=-1, keepdims=True)
        y = d * lax.rsqrt(var + LN_EPS) * lg_ref[...] + lb_ref[...]
        o_ref[r * CONV_ROWS:(r + 1) * CONV_ROWS, :] = (y * jax.nn.sigmoid(y)).astype(o_ref.dtype)


def _conv(proj, conv_w, conv_b, ln_g, ln_b):
    bsz, seq, _ = proj.shape
    t = min(CONV_T, seq)
    ca, cg = COL_CU // CONV_CH, COL_CU // CONV_CH + 1
    hb = t // CONV_HALO
    prev = lambda i: jnp.maximum(i * hb - 1, 0)
    row = lambda v: v.reshape(1, CONV_CH)
    return pl.pallas_call(
        _conv_kernel,
        grid=(bsz, seq // t),
        in_specs=[
            pl.BlockSpec((None, t, CONV_CH), lambda b, i: (b, i, ca)),
            pl.BlockSpec((None, t, CONV_CH), lambda b, i: (b, i, cg)),
            pl.BlockSpec((None, CONV_HALO, CONV_CH), lambda b, i: (b, prev(i), ca)),
            pl.BlockSpec((None, CONV_HALO, CONV_CH), lambda b, i: (b, prev(i), cg)),
            pl.BlockSpec((CONV_WIDTH, CONV_CH), lambda b, i: (0, 0)),
            pl.BlockSpec((1, CONV_CH), lambda b, i: (0, 0)),
            pl.BlockSpec((1, CONV_CH), lambda b, i: (0, 0)),
            pl.BlockSpec((1, CONV_CH), lambda b, i: (0, 0)),
        ],
        out_specs=pl.BlockSpec((None, t, CONV_CH), lambda b, i: (b, i, 0)),
        out_shape=jax.ShapeDtypeStruct((bsz, seq, CONV_CH), BF16),
        scratch_shapes=[pltpu.VMEM((CONV_HALO + t, CONV_CH), F32)],
        compiler_params=_params(("parallel", "parallel")),
        name="conv",
    )(proj, proj, proj, proj, conv_w, row(conv_b), row(ln_g), row(ln_b))


def _merge_kernel(ya_ref, yb_ref, yc_ref, yd_ref, gates_ref, x_ref, gt_ref, gp_ref, gf_ref, scf_ref, shf_ref,
                  wa_ref, wb_ref, wc_ref, wd_ref, wo_ref, o_ref, h_ref):
    d = x_ref.shape[1]
    y = None
    for i, (br, w) in enumerate(((ya_ref, wa_ref), (yb_ref, wb_ref), (yc_ref, wc_ref), (yd_ref, wd_ref))):
        t = jnp.dot(br[...], w[...], preferred_element_type=F32) * gates_ref[:, i * d:(i + 1) * d].astype(F32)
        y = t if y is None else y + t
    z = jnp.dot(y.astype(BF16), wo_ref[...], preferred_element_type=F32)
    x_new = x_ref[...] + gt_ref[...] * (_rms(z) * gp_ref[...])
    o_ref[...] = x_new
    h_ref[...] = _modulated_norm(x_new, gf_ref[...], scf_ref[...], shf_ref[...]).astype(h_ref.dtype)


def _merge(ya, yb, yc, yd, gates, x, gt, g_post, g_ffn, sc_f, sh_f, wa, wb, wc, wd, wo, layer, *, tm):
    bsz, seq, d = x.shape
    tok = lambda width: pl.BlockSpec((None, tm, width), lambda b, i: (b, i, 0))
    vec = lambda: pl.BlockSpec((None, 1, d), lambda b, i: (b, 0, 0))
    par = lambda: pl.BlockSpec((1, d), lambda b, i: (0, 0))
    full = lambda a: pl.BlockSpec((None,) + a.shape[1:], lambda b, i: (layer, 0, 0),
                                  pipeline_mode=pl.Buffered(1))
    return pl.pallas_call(
        _merge_kernel,
        grid=(bsz, seq // tm),
        in_specs=[tok(A_W), tok(B_W), tok(CONV_CH), tok(D_QW), tok(N_BRANCHES * d), tok(d),
                  vec(), par(), par(), vec(), vec(),
                  full(wa), full(wb), full(wc), full(wd), full(wo)],
        out_specs=[tok(d), tok(d)],
        out_shape=[jax.ShapeDtypeStruct((bsz, seq, d), F32), jax.ShapeDtypeStruct((bsz, seq, d), BF16)],
        compiler_params=_params(("parallel", "parallel")),
        name="merge",
    )(ya, yb, yc, yd, gates, x, gt, g_post.reshape(1, d), g_ffn.reshape(1, d), sc_f, sh_f, wa, wb, wc, wd, wo)


def _ffn_kernel(x_ref, h_ref, gt_ref, gp_ref, wg_ref, wu_ref, wd_ref, o_ref, acc_scr):
    f = pl.program_id(2)

    @pl.when(f == 0)
    def _():
        acc_scr[...] = jnp.zeros_like(acc_scr)

    h = h_ref[...]
    g = jnp.dot(h, wg_ref[...], preferred_element_type=F32)
    u = jnp.dot(h, wu_ref[...], preferred_element_type=F32)
    a = ((g * jax.nn.sigmoid(g)) * u).astype(BF16)
    acc_scr[...] += jnp.dot(a, wd_ref[...], preferred_element_type=F32)

    @pl.when(f == pl.num_programs(2) - 1)
    def _():
        o_ref[...] = x_ref[...] + gt_ref[...] * (_rms(acc_scr[...]) * gp_ref[...])


def _ffn(x, h, gt, g_post, wg, wu, wd, layer, *, tm, tf):
    bsz, seq, d = x.shape
    dff = wg.shape[2]
    vec = lambda: pl.BlockSpec((None, 1, d), lambda b, i, f: (b, 0, 0))
    par = lambda: pl.BlockSpec((1, d), lambda b, i, f: (0, 0))
    return pl.pallas_call(
        _ffn_kernel,
        grid=(bsz, seq // tm, dff // tf),
        in_specs=[
            pl.BlockSpec((None, tm, d), lambda b, i, f: (b, i, 0)),
            pl.BlockSpec((None, tm, d), lambda b, i, f: (b, i, 0)),
            vec(), par(),
            pl.BlockSpec((None, d, tf), lambda b, i, f: (layer, 0, f)),
            pl.BlockSpec((None, d, tf), lambda b, i, f: (layer, 0, f)),
            pl.BlockSpec((None, tf, d), lambda b, i, f: (layer, f, 0)),
        ],
        out_specs=pl.BlockSpec((None, tm, d), lambda b, i, f: (b, i, 0)),
        out_shape=jax.ShapeDtypeStruct((bsz, seq, d), F32),
        scratch_shapes=[pltpu.VMEM((tm, d), F32)],
        compiler_params=_params(("parallel", "parallel", "arbitrary")),
        name="ffn",
    )(x, h, gt, g_post.reshape(1, d), wg, wu, wd)


def _pack_in_proj(w):
    depth, d, _ = w.shape
    splits = (A_W, A_W, A_W, B_W, B_W, B_W, B_HEADS, 2 * CONV_CH, D_QW, D_KVW, D_KVW, N_BRANCHES * d)
    offs = np.cumsum(splits)[:-1].tolist()
    aq, ak, av, bq, bk, bv, bf, cu, dq, dk, dv, gl = jnp.split(w, offs, axis=2)
    s2 = HEAD_DIM ** -0.5 * LOG2E
    parts = [gl, aq * s2, ak, av, bq * s2, bk, bv, cu, dq * D_HEAD_DIM ** -0.5, dk, dv]
    packed = jnp.concatenate([p.astype(BF16) for p in parts]
                             + [jnp.zeros((depth, d, PROJ_W_PAD - PROJ_W), BF16)], axis=2)
    side = jnp.concatenate([bf.astype(BF16), jnp.zeros((depth, d, LANES - B_HEADS), BF16)], axis=2)
    return packed, side


CAST_BLOCK_ELEMS = 1 << 20


def _cast_kernel(w_ref, o_ref):
    o_ref[...] = w_ref[...].astype(o_ref.dtype)


def _cast_bf16(w):
    depth, r, c = w.shape
    tr = 16
    while tr * 2 * c <= CAST_BLOCK_ELEMS and r % (tr * 2) == 0:
        tr *= 2
    assert r % tr == 0
    return pl.pallas_call(
        _cast_kernel,
        grid=(depth, r // tr),
        in_specs=[pl.BlockSpec((None, tr, c), lambda l, i: (l, i, 0))],
        out_specs=pl.BlockSpec((None, tr, c), lambda l, i: (l, i, 0)),
        out_shape=jax.ShapeDtypeStruct(w.shape, BF16),
        compiler_params=_params(("parallel", "parallel")),
        name="cast_bf16",
    )(w)


def kernel(x, c, rel_bias, w_mod, b_mod, mix_norm_pre, mix_norm_post, w_in, fox_bias, conv_w, conv_b, conv_ln_g, conv_ln_b, sinks, w_branch_a, w_branch_b, w_branch_c, w_branch_d, w_out, ffn_norm_pre, ffn_norm_post, w_ffn_gate, w_ffn_up, w_ffn_down):
    depth = w_mod.shape[0]
    bsz, seq, d = x.shape
    tm = min(1024, seq)
    tm_small = min(256, seq)
    tm_ffn = min(512, seq)

    mod = _modulation(c, w_mod, b_mod)
    tab_a = _bias_tables(rel_bias[:, :A_HEADS], _moba_buckets(), scale=LOG2E)
    tab_d = _bias_tables(rel_bias[:, A_HEADS:], _swa_buckets(), scale=1.0)[:, 0]

    wa, wb, wc, wd, wo = [_cast_bf16(w) for w in (w_branch_a, w_branch_b, w_branch_c, w_branch_d, w_out)]
    wg, wu, wdn = [_cast_bf16(w) for w in (w_ffn_gate, w_ffn_up, w_ffn_down)]

    w_proj, w_side = _pack_in_proj(w_in)

    for l in range(depth):
        sh_m, sc_m, gt_m, sh_f, sc_f, gt_f = [mod[l, :, None, i * d:(i + 1) * d] for i in range(6)]

        proj, fox_raw = _norm_proj(x, mix_norm_pre[l], sc_m, sh_m, w_proj, w_side, l, tm=tm, tn=PROJ_TN)
        fox_qa, fox_ka = _fox_gate(fox_raw, fox_bias[l], t=min(512, seq))

        ya = _moba(proj, tab_a)
        yb = _fox(proj, fox_qa, fox_ka)
        yc = _conv(proj, conv_w[l], conv_b[l], conv_ln_g[l], conv_ln_b[l])
        yd = _swa(proj, sinks[l], tab_d)

        x, h_ffn = _merge(ya, yb, yc, yd, proj, x, gt_m, mix_norm_post[l], ffn_norm_pre[l], sc_f, sh_f,
                          wa, wb, wc, wd, wo, l, tm=tm_small)
        x = _ffn(x, h_ffn, gt_f, ffn_norm_post[l], wg, wu, wdn, l, tm=tm_ffn, tf=512)
    return x
```

```python
import functools
import math

import jax
import jax.numpy as jnp
import numpy as np
from jax import lax
from jax.experimental import pallas as pl
from jax.experimental.pallas import tpu as pltpu

F32 = jnp.float32
BF16 = jnp.bfloat16
HIGHEST = lax.Precision.HIGHEST

HEAD_DIM = 128
A_HEADS = 4
MOBA_BLOCK = 256
MOBA_TOPK = 3
B_HEADS = 4
CONV_CH = 512
CONV_WIDTH = 31
D_Q_HEADS = 8
D_KV_HEADS = 2
D_HEAD_DIM = 64
WINDOW = 128
N_BUCKETS = 32
MAX_DISTANCE = 1024
N_BRANCHES = 4
RMS_EPS = 1e-6
LN_EPS = 1e-5

A_W = A_HEADS * HEAD_DIM
B_W = B_HEADS * HEAD_DIM
D_QW = D_Q_HEADS * D_HEAD_DIM
D_KVW = D_KV_HEADS * D_HEAD_DIM

LANES = 128
SUBLANES = 8
VMEM_LIMIT = 56 * 1024 * 1024

NEG = -1e30
LOG2E = math.log2(math.e)
ATT_G = 4
ATT_T = MOBA_BLOCK
ATT_U = 4

D_MODEL = 2048
GATE_W = N_BRANCHES * D_MODEL
COL_AQ, COL_AK, COL_AV = GATE_W, GATE_W + A_W, GATE_W + 2 * A_W
COL_BQ, COL_BK, COL_BV = COL_AV + A_W, COL_AV + A_W + B_W, COL_AV + A_W + 2 * B_W
COL_CU = COL_BV + B_W
COL_DQ = COL_CU + 2 * CONV_CH
COL_DK = COL_DQ + D_QW
COL_DV = COL_DK + D_KVW
PROJ_W = COL_DV + D_KVW
PROJ_TN = 1024
PROJ_W_PAD = -(-PROJ_W // PROJ_TN) * PROJ_TN
MOBA_FAR = -(-(MAX_DISTANCE + MOBA_BLOCK - 1) // MOBA_BLOCK)
VOID_COL = LANES - 1
assert ATT_G % ATT_U == 0 and (MOBA_FAR - 1) % ATT_U == 0


def _params(sem, vmem=VMEM_LIMIT):
    return pltpu.CompilerParams(dimension_semantics=sem, vmem_limit_bytes=vmem)


def _t5_bucket(dist):
    max_exact = N_BUCKETS // 2
    d = jnp.maximum(dist, 0)
    log_ratio = jnp.log(jnp.maximum(d, 1).astype(jnp.float32) / max_exact) / math.log(MAX_DISTANCE / max_exact)
    large = max_exact + (log_ratio * (N_BUCKETS - max_exact)).astype(jnp.int32)
    large = jnp.minimum(large, N_BUCKETS - 1)
    return jnp.where(d < max_exact, d, large)


def _rms(y):
    return y * lax.rsqrt(jnp.mean(y * y, axis=-1, keepdims=True) + RMS_EPS)


def _modulated_norm(x, g, sc, sh):
    return (_rms(x) * g) * (1.0 + sc) + sh


MOD_TN = 512
MOD_KC = 256


def _mod_kernel(ct_ref, w_ref, b_ref, o_ref):
    d, nb = ct_ref.shape
    tn = w_ref.shape[1]
    ct = ct_ref[...]
    ca = ct * jax.nn.sigmoid(ct)
    rows = []
    for r in range(nb):
        acc = jnp.zeros((1, tn), F32)
        for kc in range(d // MOD_KC):
            sl = slice(kc * MOD_KC, (kc + 1) * MOD_KC)
            acc = acc + jnp.sum(w_ref[sl, :] * ca[sl, r:r + 1], axis=0, keepdims=True)
        rows.append(acc)
    o_ref[...] = jnp.concatenate(rows, axis=0) + b_ref[...]


def _modulation(c, w_mod, b_mod):
    depth, d, n = w_mod.shape
    bsz = c.shape[0]
    return pl.pallas_call(
        _mod_kernel,
        grid=(depth, n // MOD_TN),
        in_specs=[
            pl.BlockSpec((d, bsz), lambda l, j: (0, 0)),
            pl.BlockSpec((None, d, MOD_TN), lambda l, j: (l, 0, j)),
            pl.BlockSpec((None, 1, MOD_TN), lambda l, j: (l, 0, j)),
        ],
        out_specs=pl.BlockSpec((None, bsz, MOD_TN), lambda l, j: (l, 0, j)),
        out_shape=jax.ShapeDtypeStruct((depth, bsz, n), F32),
        compiler_params=_params(("parallel", "parallel")),
        name="mod",
    )(c.T, w_mod, b_mod.reshape(depth, 1, n))


def _bias_table_kernel(rb_ref, bucket_ref, o_ref, *, scale):
    h = pl.program_id(0)
    b = bucket_ref[...]
    acc = jnp.where(b < 0, NEG, 0.0).astype(F32)
    for u in range(N_BUCKETS):
        acc = jnp.where(b == u, rb_ref[h, u] * scale, acc)
    o_ref[...] = acc


def _bias_tables(rel_bias_heads, bucket, *, scale):
    nh = rel_bias_heads.shape[1]
    nt, r, c = bucket.shape
    return pl.pallas_call(
        functools.partial(_bias_table_kernel, scale=scale),
        grid=(nh, nt),
        in_specs=[
            pl.BlockSpec(memory_space=pltpu.SMEM),
            pl.BlockSpec((None, r, c), lambda h, t: (t, 0, 0)),
        ],
        out_specs=pl.BlockSpec((None, None, r, c), lambda h, t: (h, t, 0, 0)),
        out_shape=jax.ShapeDtypeStruct((nh, nt, r, c), F32),
        compiler_params=_params(("parallel", "parallel")),
        name="bias_table",
    )(rel_bias_heads.T, bucket)


def _moba_buckets():
    blk = MOBA_BLOCK
    i = jnp.arange(blk)[None, :]
    j = jnp.arange(blk)[:, None]
    tabs = []
    for delta in range(MOBA_FAR):
        dist = delta * blk + i - j
        tabs.append(jnp.where(dist >= 0, _t5_bucket(dist), -1))
    tabs.append(_t5_bucket(jnp.full((blk, blk), MAX_DISTANCE, jnp.int32)))
    return jnp.stack(tabs).astype(jnp.int32)


def _swa_buckets():
    qi = jnp.arange(WINDOW)[:, None]
    kj = jnp.arange(2 * WINDOW)[None, :]
    dist = qi + WINDOW - kj
    in_win = (dist >= 0) & (dist < WINDOW)
    return jnp.where(in_win, _t5_bucket(dist), -1).astype(jnp.int32)[None]


def _norm_proj_kernel(x_ref, g_ref, sc_ref, sh_ref, w_ref, ws_ref, o_ref, os_ref, h_scr):
    j = pl.program_id(2)
    tn = w_ref.shape[1]

    @pl.when(j == 0)
    def _():
        h = _modulated_norm(x_ref[...], g_ref[...], sc_ref[...], sh_ref[...]).astype(BF16)
        h_scr[...] = h
        os_ref[...] = jnp.dot(h, ws_ref[...], preferred_element_type=F32)

    @pl.when(j < GATE_W // tn)
    def _():
        acc = jnp.dot(h_scr[...], w_ref[...], preferred_element_type=F32)
        o_ref[...] = jax.nn.sigmoid(acc).astype(o_ref.dtype)

    @pl.when(j >= GATE_W // tn)
    def _():
        o_ref[...] = jnp.dot(h_scr[...], w_ref[...], preferred_element_type=F32).astype(o_ref.dtype)


def _norm_proj(x, g, sc, sh, w, layer, *, tm, tn):
    bsz, seq, d = x.shape
    n = PROJ_W_PAD
    ns = LANES
    assert GATE_W % tn == 0 and n % tn == 0 and w.shape[2] == PACK_W
    return pl.pallas_call(
        _norm_proj_kernel,
        grid=(bsz, seq // tm, n // tn),
        in_specs=[
            pl.BlockSpec((None, tm, d), lambda b, i, j: (b, i, 0)),
            pl.BlockSpec((1, d), lambda b, i, j: (0, 0)),
            pl.BlockSpec((None, 1, d), lambda b, i, j: (b, 0, 0)),
            pl.BlockSpec((None, 1, d), lambda b, i, j: (b, 0, 0)),
            pl.BlockSpec((None, d, tn), lambda b, i, j: (layer, 0, j)),
            pl.BlockSpec((None, d, ns), lambda b, i, j: (layer, 0, PACK_SIDE_BLK)),
        ],
        out_specs=[pl.BlockSpec((None, tm, tn), lambda b, i, j: (b, i, j)),
                   pl.BlockSpec((None, tm, ns), lambda b, i, j: (b, i, 0))],
        out_shape=[jax.ShapeDtypeStruct((bsz, seq, n), BF16), jax.ShapeDtypeStruct((bsz, seq, ns), F32)],
        scratch_shapes=[pltpu.VMEM((tm, d), BF16)],
        compiler_params=_params(("parallel", "parallel", "arbitrary")),
        name="norm_proj",
    )(x, g.reshape(1, d), sc, sh, w, w)


def _split3(c):
    hi = c.astype(BF16).astype(F32)
    mid = (c - hi).astype(BF16).astype(F32)
    lo = (c - hi - mid).astype(BF16).astype(F32)
    return hi, mid, lo


def _fox_gate_kernel(x_ref, fb_ref, qa_ref, ka_ref, carry_scr):
    t = x_ref.shape[0]

    @pl.when(pl.program_id(1) == 0)
    def _():
        carry_scr[...] = jnp.zeros_like(carry_scr)

    lf = jax.nn.log_sigmoid(x_ref[...] + fb_ref[...])
    tri = (lax.broadcasted_iota(jnp.int32, (t, t), 0) >= lax.broadcasted_iota(jnp.int32, (t, t), 1)).astype(F32)
    c = jnp.dot(tri, lf, precision=HIGHEST, preferred_element_type=F32) + carry_scr[...]
    carry_scr[...] = c[t - 1:t, :]
    lane = lax.broadcasted_iota(jnp.int32, (t, LANES), 1)
    sub = lax.broadcasted_iota(jnp.int32, (SUBLANES, t), 0)
    ct = c.T * LOG2E
    for h in range(B_HEADS):
        hi, mid, lo = _split3(jnp.broadcast_to(c[:, h:h + 1], (t, LANES)) * LOG2E)
        ka = jnp.where(lane == 0, -hi, jnp.where(lane == 1, -mid, jnp.where(lane == 2, -lo, jnp.where(lane < 6, 1.0, 0.0))))
        ka_ref[h] = ka.astype(BF16)
        hi, mid, lo = _split3(jnp.broadcast_to(ct[h:h + 1, :], (SUBLANES, t)))
        qa = jnp.where(sub < 3, 1.0, jnp.where(sub == 3, hi, jnp.where(sub == 4, mid, jnp.where(sub == 5, lo, 0.0))))
        qa_ref[h] = jnp.concatenate([qa, jnp.zeros((LANES - SUBLANES, t), F32)], axis=0).astype(BF16)


def _fox_gate(raw, fox_b, *, t):
    bsz, seq, _ = raw.shape
    fb = jnp.zeros((1, LANES), F32).at[0, :B_HEADS].set(fox_b)
    aug = pl.BlockSpec((None, B_HEADS, t, LANES), lambda b, i: (b, 0, i, 0))
    aug_t = pl.BlockSpec((None, B_HEADS, LANES, t), lambda b, i: (b, 0, 0, i))
    return pl.pallas_call(
        _fox_gate_kernel,
        grid=(bsz, seq // t),
        in_specs=[
            pl.BlockSpec((None, t, LANES), lambda b, i: (b, i, 0)),
            pl.BlockSpec((1, LANES), lambda b, i: (0, 0)),
        ],
        out_specs=[aug_t, aug],
        out_shape=[jax.ShapeDtypeStruct((bsz, B_HEADS, LANES, seq), BF16),
                   jax.ShapeDtypeStruct((bsz, B_HEADS, seq, LANES), BF16)],
        scratch_shapes=[pltpu.VMEM((1, LANES), F32)],
        compiler_params=_params(("parallel", "arbitrary")),
        name="fox_gate",
    )(raw, fb)


ONES_ROWS = 16


def _softmax_step(zt, vt, carry):
    m, acc = carry
    m_new = jnp.maximum(m, jnp.max(zt, axis=0, keepdims=True))
    p = jnp.exp2(zt - m_new)
    acc = jnp.exp2(m - m_new) * acc + jnp.dot(vt, p.astype(BF16), preferred_element_type=F32)
    return m_new, acc


def _softmax_init(tq, dh):
    return (jnp.full((1, tq), NEG, F32), jnp.zeros((dh + ONES_ROWS, tq), F32))


def _softmax_finish(carry, dh):
    _, acc = carry
    return (acc[:dh, :] / acc[dh:dh + 1, :]).T


def _fill_vt(vt_scr, v_ref, chunk):
    seq, dh = v_ref.shape
    for cix in range(seq // chunk):
        sl = slice(cix * chunk, (cix + 1) * chunk)
        vt_scr[:dh, sl] = _transpose_bf16(v_ref[sl, :])
    vt_scr[dh:, :] = jnp.ones((ONES_ROWS, seq), BF16)


ATT_LOOKAHEAD = 6


def _run_substeps(steps, carry):
    carry = list(carry)
    pending = {}
    for s in range(min(ATT_LOOKAHEAD, len(steps))):
        pending[s] = steps[s][1]()
    for s, (j, _, fix, vt) in enumerate(steps):
        if s + ATT_LOOKAHEAD < len(steps):
            pending[s + ATT_LOOKAHEAD] = steps[s + ATT_LOOKAHEAD][1]()
        z, cj = pending.pop(s), carry[j]
        if fix is not None:
            z, cj = fix(z, cj)
        carry[j] = _softmax_step(z, vt, cj)
    return carry


def _transpose_bf16(x):
    return x.astype(F32).T.astype(BF16)


_NT = (((1,), (1,)), ((), ()))


def _dot_nt(a, b):
    return lax.dot_general(a, b, _NT, preferred_element_type=F32)


def _moba_kernel(cfar_ref, q_ref, k_ref, v_ref, tab_ref, o_ref, kmean_scr, sc_scr, qt_scr, vt_scr):
    g, t = ATT_G, ATT_T
    seq, dh = k_ref.shape
    nblk = seq // t
    rows = g * t
    i = pl.program_id(2)
    base = i * g

    @pl.when(i == 0)
    def _():
        r = lax.broadcasted_iota(jnp.int32, (nblk, seq), 0)
        c = lax.broadcasted_iota(jnp.int32, (nblk, seq), 1)
        avg = jnp.where((c >= r * t) & (c < (r + 1) * t), 1.0 / t, 0.0).astype(BF16)
        kmean_scr[...] = jnp.dot(avg, k_ref[...], preferred_element_type=F32)
        _fill_vt(vt_scr, v_ref, rows)

    q = q_ref[...]
    qt_scr[:dh, :] = _transpose_bf16(q)
    sc = lax.dot_general(kmean_scr[...], q.astype(F32), _NT, precision=HIGHEST, preferred_element_type=F32)
    row = lax.broadcasted_iota(jnp.int32, (nblk, rows), 0)
    own = base + lax.broadcasted_iota(jnp.int32, (nblk, rows), 1) // t
    past = row < own
    sc = jnp.where(past, sc, -jnp.inf)
    sc_scr[...] = sc

    def rank_body(m, rank):
        sm = sc_scr[pl.ds(m, 1), :]
        return rank + jnp.where(sm > sc, 1.0, jnp.where(sm == sc, jnp.where(row > m, 1.0, 0.0), 0.0))

    rank = lax.fori_loop(0, base + g - 1, rank_body, jnp.zeros((nblk, rows), F32))
    pen = jnp.where(past, jnp.where(rank < MOBA_TOPK, 0.0, NEG), jnp.where(row == own, 0.0, NEG))
    void = jnp.where(lax.broadcasted_iota(jnp.int32, (SUBLANES, rows), 0) == SUBLANES - 1, NEG, 0.0)
    pen = jnp.concatenate([pen, jnp.zeros((LANES - SUBLANES - nblk, rows), F32), void], axis=0)
    qt_scr[dh:, :] = pen.astype(BF16)

    lane = lax.broadcasted_iota(jnp.int32, (t, LANES), 1)

    def kv(n, col):
        start = pl.multiple_of(n * t, t)
        onehot = jnp.where(lane == col, 1.0, 0.0).astype(BF16)
        return jnp.concatenate([k_ref[pl.ds(start, t), :], onehot], axis=1), vt_scr[:, pl.ds(start, t)]

    def logits(kb, j):
        return jnp.dot(kb, qt_scr[:, j * t:(j + 1) * t], preferred_element_type=F32)

    def far_body(grp, carry):
        steps = []
        for u in range(ATT_U):
            n = grp * ATT_U + u
            kb, vb = kv(n, n)
            steps += [(j, functools.partial(logits, kb, j), None, vb) for j in range(g)]
        return tuple(_run_substeps(steps, carry))

    carry = lax.fori_loop(0, jnp.maximum(base - (MOBA_FAR - 1), 0) // ATT_U, far_body,
                          tuple(_softmax_init(t, dh) for _ in range(g)))
    cfar = cfar_ref[pl.program_id(1)]

    def near_fix(delta, z, cj):
        if delta == MOBA_FAR - 1:
            m, acc = cj
            cj = (m + cfar, acc)
        return tab_ref[delta] + z, cj

    steps = []
    for k in range(-(MOBA_FAR - 1), g):
        n = base + k
        if k < 0:
            kb, vb = kv(jnp.maximum(n, 0), jnp.where(n >= 0, n, VOID_COL))
        else:
            kb, vb = kv(n, n)
        for j in range(max(k, 0), g):
            fix = functools.partial(near_fix, j - k) if j - k < MOBA_FAR else None
            steps.append((j, functools.partial(logits, kb, j), fix, vb))
    carry = _run_substeps(steps, carry)
    for j in range(g):
        o_ref[j * t:(j + 1) * t, :] = _softmax_finish(carry[j], dh).astype(o_ref.dtype)


def _moba(proj, tab):
    bsz, seq, _ = proj.shape
    g, t = ATT_G, ATT_T
    nblk = seq // t
    assert seq % (g * t) == 0 and nblk <= LANES - SUBLANES and nblk % SUBLANES == 0
    qc, kc, vc = COL_AQ // HEAD_DIM, COL_AK // HEAD_DIM, COL_AV // HEAD_DIM
    cfar = tab[:, MOBA_FAR, 0, 0]
    return pl.pallas_call(
        _moba_kernel,
        grid=(bsz, A_HEADS, nblk // g),
        in_specs=[
            pl.BlockSpec(memory_space=pltpu.SMEM),
            pl.BlockSpec((None, g * t, HEAD_DIM), lambda b, h, i: (b, i, qc + h)),
            pl.BlockSpec((None, seq, HEAD_DIM), lambda b, h, i: (b, 0, kc + h)),
            pl.BlockSpec((None, seq, HEAD_DIM), lambda b, h, i: (b, 0, vc + h)),
            pl.BlockSpec((None, MOBA_FAR, t, t), lambda b, h, i: (h, 0, 0, 0)),
        ],
        out_specs=pl.BlockSpec((None, g * t, HEAD_DIM), lambda b, h, i: (b, i, h)),
        out_shape=jax.ShapeDtypeStruct((bsz, seq, A_W), BF16),
        scratch_shapes=[pltpu.VMEM((nblk, HEAD_DIM), F32), pltpu.VMEM((nblk, g * t), F32),
                        pltpu.VMEM((HEAD_DIM + LANES, g * t), BF16), pltpu.VMEM((HEAD_DIM + ONES_ROWS, seq), BF16)],
        compiler_params=_params(("parallel", "parallel", "arbitrary")),
        name="moba",
    )(cfar, proj, proj, proj, tab)


def _fox_kernel(q_ref, qa_ref, k_ref, ka_ref, v_ref, o_ref, qt_scr, vt_scr):
    g, t = ATT_G, ATT_T
    rows = g * t
    seq, dh = k_ref.shape
    base = pl.program_id(2) * g

    @pl.when(base == 0)
    def _():
        _fill_vt(vt_scr, v_ref, rows)

    qt_scr[:dh, :] = _transpose_bf16(q_ref[...])
    qt_scr[dh:, :] = qa_ref[...]

    def kv(n):
        start = pl.multiple_of(n * t, t)
        return (jnp.concatenate([k_ref[pl.ds(start, t), :], ka_ref[pl.ds(start, t), :]], axis=1),
                vt_scr[:, pl.ds(start, t)])

    def logits(kb, j):
        return jnp.dot(kb, qt_scr[:, j * t:(j + 1) * t], preferred_element_type=F32)

    def body(grp, carry):
        steps = []
        for u in range(ATT_U):
            kb, vb = kv(grp * ATT_U + u)
            steps += [(j, functools.partial(logits, kb, j), None, vb) for j in range(g)]
        return tuple(_run_substeps(steps, carry))

    carry = lax.fori_loop(0, base // ATT_U, body, tuple(_softmax_init(t, dh) for _ in range(g)))
    causal = lax.broadcasted_iota(jnp.int32, (t, t), 0) <= lax.broadcasted_iota(jnp.int32, (t, t), 1)

    def diag_fix(z, cj):
        return jnp.where(causal, z, NEG), cj

    steps = []
    for k in range(g):
        kb, vb = kv(base + k)
        steps += [(j, functools.partial(logits, kb, j), diag_fix if j == k else None, vb) for j in range(k, g)]
    carry = _run_substeps(steps, carry)
    for j in range(g):
        o_ref[j * t:(j + 1) * t, :] = _softmax_finish(carry[j], dh).astype(o_ref.dtype)


def _fox(proj, qa, ka):
    bsz, seq, _ = proj.shape
    g, t = ATT_G, ATT_T
    assert seq % (g * t) == 0
    qc, kc, vc = COL_BQ // HEAD_DIM, COL_BK // HEAD_DIM, COL_BV // HEAD_DIM
    return pl.pallas_call(
        _fox_kernel,
        grid=(bsz, B_HEADS, seq // (g * t)),
        in_specs=[
            pl.BlockSpec((None, g * t, HEAD_DIM), lambda b, h, i: (b, i, qc + h)),
            pl.BlockSpec((None, None, LANES, g * t), lambda b, h, i: (b, h, 0, i)),
            pl.BlockSpec((None, seq, HEAD_DIM), lambda b, h, i: (b, 0, kc + h)),
            pl.BlockSpec((None, None, seq, LANES), lambda b, h, i: (b, h, 0, 0)),
            pl.BlockSpec((None, seq, HEAD_DIM), lambda b, h, i: (b, 0, vc + h)),
        ],
        out_specs=pl.BlockSpec((None, g * t, HEAD_DIM), lambda b, h, i: (b, i, h)),
        out_shape=jax.ShapeDtypeStruct((bsz, seq, B_W), BF16),
        scratch_shapes=[pltpu.VMEM((HEAD_DIM + LANES, g * t), BF16), pltpu.VMEM((HEAD_DIM + ONES_ROWS, seq), BF16)],
        compiler_params=_params(("parallel", "parallel", "arbitrary")),
        name="fox",
    )(proj, qa, proj, ka, proj)


SWA_BLOCKS = 4


def _swa_kernel(sink_ref, q_ref, kp_ref, kc_ref, vp_ref, vc_ref, tab_ref, o_ref):
    w = WINDOW
    dh = D_HEAD_DIM
    grp = D_Q_HEADS // D_KV_HEADS
    i = pl.program_id(1)
    kall = jnp.concatenate([kp_ref[...], kc_ref[...]], axis=0)
    vall = jnp.concatenate([vp_ref[...], vc_ref[...]], axis=0)
    col = lax.broadcasted_iota(jnp.int32, (w, 2 * w), 1)
    first = col >= jnp.where(i > 0, 0, w)
    for blk in range(q_ref.shape[0] // w):
        q = q_ref[blk * w:(blk + 1) * w, :]
        kcat = kall[blk * w:(blk + 2) * w, :]
        vcat = vall[blk * w:(blk + 2) * w, :]
        outs = []
        for h in range(D_Q_HEADS):
            g = h // grp
            s = tab_ref[h] + _dot_nt(q[:, h * dh:(h + 1) * dh], kcat[:, g * dh:(g + 1) * dh])
            if blk == 0:
                s = jnp.where(first, s, NEG)
            sink = sink_ref[h]
            m = jnp.maximum(jnp.max(s, axis=-1, keepdims=True), sink)
            p = jnp.exp(s - m)
            den = jnp.sum(p, axis=-1, keepdims=True) + jnp.exp(sink - m)
            pv = jnp.dot(p.astype(BF16), vcat[:, g * dh:(g + 1) * dh], preferred_element_type=F32)
            outs.append(pv / den)
        o_ref[blk * w:(blk + 1) * w, :] = jnp.concatenate(outs, axis=1).astype(o_ref.dtype)


def _swa(proj, sinks, tab):
    bsz, seq, _ = proj.shape
    w = WINDOW
    nb = SWA_BLOCKS
    assert seq % (nb * w) == 0
    qc, kc, vc = COL_DQ // D_QW, COL_DK // D_KVW, COL_DV // D_KVW
    prev = lambda i: jnp.maximum(i * nb - 1, 0)
    return pl.pallas_call(
        _swa_kernel,
        grid=(bsz, seq // (nb * w)),
        in_specs=[
            pl.BlockSpec(memory_space=pltpu.SMEM),
            pl.BlockSpec((None, nb * w, D_QW), lambda b, i: (b, i, qc)),
            pl.BlockSpec((None, w, D_KVW), lambda b, i: (b, prev(i), kc)),
            pl.BlockSpec((None, nb * w, D_KVW), lambda b, i: (b, i, kc)),
            pl.BlockSpec((None, w, D_KVW), lambda b, i: (b, prev(i), vc)),
            pl.BlockSpec((None, nb * w, D_KVW), lambda b, i: (b, i, vc)),
            pl.BlockSpec((D_Q_HEADS, w, 2 * w), lambda b, i: (0, 0, 0)),
        ],
        out_specs=pl.BlockSpec((None, nb * w, D_QW), lambda b, i: (b, i, 0)),
        out_shape=jax.ShapeDtypeStruct((bsz, seq, D_QW), BF16),
        compiler_params=_params(("parallel", "parallel")),
        name="swa",
    )(sinks, proj, proj, proj, proj, proj, tab)


CONV_T = 512
CONV_HALO = 32
CONV_ROWS = 64


def _conv_kernel(a_ref, g_ref, ha_ref, hg_ref, w_ref, b_ref, lg_ref, lb_ref, o_ref, buf):
    t = a_ref.shape[0]
    halo = CONV_HALO
    hp = ha_ref[...].astype(F32) * jax.nn.sigmoid(hg_ref[...].astype(F32))
    buf[0:halo, :] = jnp.where(pl.program_id(1) > 0, hp, 0.0)
    buf[halo:halo + t, :] = a_ref[...].astype(F32) * jax.nn.sigmoid(g_ref[...].astype(F32))
    off = halo - (CONV_WIDTH - 1)
    for r in range(t // CONV_ROWS):
        acc = jnp.broadcast_to(b_ref[...], (CONV_ROWS, CONV_CH))
        for rho in range(SUBLANES):
            rows = CONV_ROWS if rho == 0 else CONV_ROWS + SUBLANES
            part = None
            for s in range(off, off + CONV_WIDTH):
                if s % SUBLANES != rho:
                    continue
                start = r * CONV_ROWS + s - rho
                term = w_ref[s - off:s - off + 1, :] * buf[start:start + rows, :]
                part = term if part is None else part + term
            acc = acc + part[rho:rho + CONV_ROWS, :]
        mu = jnp.mean(acc, axis=-1, keepdims=True)
        d = acc - mu
        var = jnp.mean(d * d, axis=-1, keepdims=True)
        y = d * lax.rsqrt(var + LN_EPS) * lg_ref[...] + lb_ref[...]
        o_ref[r * CONV_ROWS:(r + 1) * CONV_ROWS, :] = (y * jax.nn.sigmoid(y)).astype(o_ref.dtype)


def _conv(proj, conv_w, conv_b, ln_g, ln_b):
    bsz, seq, _ = proj.shape
    t = min(CONV_T, seq)
    ca, cg = COL_CU // CONV_CH, COL_CU // CONV_CH + 1
    hb = t // CONV_HALO
    prev = lambda i: jnp.maximum(i * hb - 1, 0)
    row = lambda v: v.reshape(1, CONV_CH)
    return pl.pallas_call(
        _conv_kernel,
        grid=(bsz, seq // t),
        in_specs=[
            pl.BlockSpec((None, t, CONV_CH), lambda b, i: (b, i, ca)),
            pl.BlockSpec((None, t, CONV_CH), lambda b, i: (b, i, cg)),
            pl.BlockSpec((None, CONV_HALO, CONV_CH), lambda b, i: (b, prev(i), ca)),
            pl.BlockSpec((None, CONV_HALO, CONV_CH), lambda b, i: (b, prev(i), cg)),
            pl.BlockSpec((CONV_WIDTH, CONV_CH), lambda b, i: (0, 0)),
            pl.BlockSpec((1, CONV_CH), lambda b, i: (0, 0)),
            pl.BlockSpec((1, CONV_CH), lambda b, i: (0, 0)),
            pl.BlockSpec((1, CONV_CH), lambda b, i: (0, 0)),
        ],
        out_specs=pl.BlockSpec((None, t, CONV_CH), lambda b, i: (b, i, 0)),
        out_shape=jax.ShapeDtypeStruct((bsz, seq, CONV_CH), BF16),
        scratch_shapes=[pltpu.VMEM((CONV_HALO + t, CONV_CH), F32)],
        compiler_params=_params(("parallel", "parallel")),
        name="conv",
    )(proj, proj, proj, proj, conv_w, row(conv_b), row(ln_g), row(ln_b))


def _merge_kernel(ya_ref, yb_ref, yc_ref, yd_ref, gates_ref, x_ref, gt_ref, gp_ref, gf_ref, scf_ref, shf_ref,
                  wa_ref, wb_ref, wc_ref, wd_ref, wo_ref, o_ref, h_ref):
    d = x_ref.shape[1]
    y = None
    for i, (br, w) in enumerate(((ya_ref, wa_ref), (yb_ref, wb_ref), (yc_ref, wc_ref), (yd_ref, wd_ref))):
        t = jnp.dot(br[...], w[...], preferred_element_type=F32) * gates_ref[:, i * d:(i + 1) * d].astype(F32)
        y = t if y is None else y + t
    z = jnp.dot(y.astype(BF16), wo_ref[...], preferred_element_type=F32)
    x_new = x_ref[...] + gt_ref[...] * (_rms(z) * gp_ref[...])
    o_ref[...] = x_new
    h_ref[...] = _modulated_norm(x_new, gf_ref[...], scf_ref[...], shf_ref[...]).astype(h_ref.dtype)


def _merge(ya, yb, yc, yd, gates, x, gt, g_post, g_ffn, sc_f, sh_f, wa, wb, wc, wd, wo, layer, *, tm):
    bsz, seq, d = x.shape
    tok = lambda width: pl.BlockSpec((None, tm, width), lambda b, i: (b, i, 0))
    vec = lambda: pl.BlockSpec((None, 1, d), lambda b, i: (b, 0, 0))
    par = lambda: pl.BlockSpec((1, d), lambda b, i: (0, 0))
    full = lambda a: pl.BlockSpec((None,) + a.shape[1:], lambda b, i: (layer, 0, 0),
                                  pipeline_mode=pl.Buffered(1))
    return pl.pallas_call(
        _merge_kernel,
        grid=(bsz, seq // tm),
        in_specs=[tok(A_W), tok(B_W), tok(CONV_CH), tok(D_QW), tok(N_BRANCHES * d), tok(d),
                  vec(), par(), par(), vec(), vec(),
                  full(wa), full(wb), full(wc), full(wd), full(wo)],
        out_specs=[tok(d), tok(d)],
        out_shape=[jax.ShapeDtypeStruct((bsz, seq, d), F32), jax.ShapeDtypeStruct((bsz, seq, d), BF16)],
        compiler_params=_params(("parallel", "parallel")),
        name="merge",
    )(ya, yb, yc, yd, gates, x, gt, g_post.reshape(1, d), g_ffn.reshape(1, d), sc_f, sh_f, wa, wb, wc, wd, wo)


def _ffn_kernel(x_ref, h_ref, gt_ref, gp_ref, wg_ref, wu_ref, wd_ref, o_ref, acc_scr):
    f = pl.program_id(2)

    @pl.when(f == 0)
    def _():
        acc_scr[...] = jnp.zeros_like(acc_scr)

    h = h_ref[...]
    g = jnp.dot(h, wg_ref[...], preferred_element_type=F32)
    u = jnp.dot(h, wu_ref[...], preferred_element_type=F32)
    a = ((g * jax.nn.sigmoid(g)) * u).astype(BF16)
    acc_scr[...] += jnp.dot(a, wd_ref[...], preferred_element_type=F32)

    @pl.when(f == pl.num_programs(2) - 1)
    def _():
        o_ref[...] = x_ref[...] + gt_ref[...] * (_rms(acc_scr[...]) * gp_ref[...])


def _ffn(x, h, gt, g_post, wg, wu, wd, layer, *, tm, tf):
    bsz, seq, d = x.shape
    dff = wg.shape[2]
    vec = lambda: pl.BlockSpec((None, 1, d), lambda b, i, f: (b, 0, 0))
    par = lambda: pl.BlockSpec((1, d), lambda b, i, f: (0, 0))
    return pl.pallas_call(
        _ffn_kernel,
        grid=(bsz, seq // tm, dff // tf),
        in_specs=[
            pl.BlockSpec((None, tm, d), lambda b, i, f: (b, i, 0)),
            pl.BlockSpec((None, tm, d), lambda b, i, f: (b, i, 0)),
            vec(), par(),
            pl.BlockSpec((None, d, tf), lambda b, i, f: (layer, 0, f)),
            pl.BlockSpec((None, d, tf), lambda b, i, f: (layer, 0, f)),
            pl.BlockSpec((None, tf, d), lambda b, i, f: (layer, f, 0)),
        ],
        out_specs=pl.BlockSpec((None, tm, d), lambda b, i, f: (b, i, 0)),
        out_shape=jax.ShapeDtypeStruct((bsz, seq, d), F32),
        scratch_shapes=[pltpu.VMEM((tm, d), F32)],
        compiler_params=_params(("parallel", "parallel", "arbitrary")),
        name="ffn",
    )(x, h, gt, g_post.reshape(1, d), wg, wu, wd)


SRC_QKV = 3 * A_W + 3 * B_W
SRC_REST = SRC_QKV + B_HEADS
SRC_GATE = SRC_REST + (PROJ_W - GATE_W - SRC_QKV)
IN_W = SRC_GATE + GATE_W
PACK_SHIFT = B_HEADS
PACK_SIDE_BLK = PROJ_W_PAD // LANES
PACK_W = PROJ_W_PAD + LANES
assert SRC_QKV % LANES == 0 and (SRC_GATE - PACK_SHIFT) % LANES == 0 and GATE_W % LANES == 0


def _pack_src_block(b):
    gate = b + (SRC_GATE - PACK_SHIFT) // LANES
    rest = b - GATE_W // LANES
    side = SRC_QKV // LANES
    return jnp.where(b < GATE_W // LANES, gate,
                     jnp.where(b < PROJ_W // LANES, rest, jnp.where(b == PACK_SIDE_BLK, side, 0)))


def _pack_kernel(a_ref, b_ref, sa_ref, sb_ref, keep_ref, o_ref):
    blk = pl.program_id(1)
    d = a_ref.shape[0]
    qkv = (blk >= GATE_W // LANES) & (blk < (GATE_W + SRC_QKV) // LANES)
    shift = jnp.where(qkv | (blk == PACK_SIDE_BLK), 0, PACK_SHIFT)
    lane = lax.broadcasted_iota(jnp.int32, (d, LANES), 1)
    in_range = (_pack_src_block(blk) + 1) * LANES + lane < IN_W
    ab = jnp.concatenate([a_ref[...] * sa_ref[...], jnp.where(in_range, b_ref[...] * sb_ref[...], 0.0)],
                         axis=1).astype(BF16)
    r = lax.broadcasted_iota(jnp.int32, (2 * LANES, LANES), 0)
    c = lax.broadcasted_iota(jnp.int32, (2 * LANES, LANES), 1)
    sel = jnp.where(r == c + shift, 1.0, 0.0).astype(BF16)
    o_ref[...] = (jnp.dot(ab, sel, preferred_element_type=F32) * keep_ref[...]).astype(o_ref.dtype)


def _pack_in_proj(w):
    depth, d, in_w = w.shape
    assert in_w == IN_W
    n_src = -(-IN_W // LANES)
    s2 = HEAD_DIM ** -0.5 * LOG2E
    src_scale = np.ones((1, (n_src + 1) * LANES), np.float32)
    src_scale[0, 0:A_W] = s2
    src_scale[0, 3 * A_W:3 * A_W + B_W] = s2
    dq0 = SRC_REST + 2 * CONV_CH
    src_scale[0, dq0:dq0 + D_QW] = D_HEAD_DIM ** -0.5
    keep = np.zeros((1, PACK_W), np.float32)
    keep[0, :PROJ_W] = 1.0
    keep[0, PACK_SIDE_BLK * LANES:PACK_SIDE_BLK * LANES + B_HEADS] = 1.0
    col = lambda off: pl.BlockSpec((None, d, LANES), lambda l, b: (l, 0, _pack_src_block(b) + off))
    scl = lambda off: pl.BlockSpec((1, LANES), lambda l, b: (0, _pack_src_block(b) + off))
    return pl.pallas_call(
        _pack_kernel,
        grid=(depth, PACK_W // LANES),
        in_specs=[col(0), col(1), scl(0), scl(1), pl.BlockSpec((1, LANES), lambda l, b: (0, b))],
        out_specs=pl.BlockSpec((None, d, LANES), lambda l, b: (l, 0, b)),
        out_shape=jax.ShapeDtypeStruct((depth, d, PACK_W), BF16),
        compiler_params=_params(("parallel", "parallel")),
        name="pack_in_proj",
    )(w, w, jnp.asarray(src_scale), jnp.asarray(src_scale), jnp.asarray(keep))


CAST_BLOCK_ELEMS = 1 << 20


def _cast_kernel(w_ref, o_ref):
    o_ref[...] = w_ref[...].astype(o_ref.dtype)


def _cast_bf16(w):
    depth, r, c = w.shape
    tr = 16
    while tr * 2 * c <= CAST_BLOCK_ELEMS and r % (tr * 2) == 0:
        tr *= 2
    assert r % tr == 0
    return pl.pallas_call(
        _cast_kernel,
        grid=(depth, r // tr),
        in_specs=[pl.BlockSpec((None, tr, c), lambda l, i: (l, i, 0))],
        out_specs=pl.BlockSpec((None, tr, c), lambda l, i: (l, i, 0)),
        out_shape=jax.ShapeDtypeStruct(w.shape, BF16),
        compiler_params=_params(("parallel", "parallel")),
        name="cast_bf16",
    )(w)


def kernel(x, c, rel_bias, w_mod, b_mod, mix_norm_pre, mix_norm_post, w_in, fox_bias, conv_w, conv_b, conv_ln_g, conv_ln_b, sinks, w_branch_a, w_branch_b, w_branch_c, w_branch_d, w_out, ffn_norm_pre, ffn_norm_post, w_ffn_gate, w_ffn_up, w_ffn_down):
    depth = w_mod.shape[0]
    bsz, seq, d = x.shape
    tm = min(1024, seq)
    tm_small = min(256, seq)
    tm_ffn = min(512, seq)

    mod = _modulation(c, w_mod, b_mod)
    tab_a = _bias_tables(rel_bias[:, :A_HEADS], _moba_buckets(), scale=LOG2E)
    tab_d = _bias_tables(rel_bias[:, A_HEADS:], _swa_buckets(), scale=1.0)[:, 0]

    wa, wb, wc, wd, wo = [_cast_bf16(w) for w in (w_branch_a, w_branch_b, w_branch_c, w_branch_d, w_out)]
    wg, wu, wdn = [_cast_bf16(w) for w in (w_ffn_gate, w_ffn_up, w_ffn_down)]

    w_proj = _pack_in_proj(w_in)

    for l in range(depth):
        sh_m, sc_m, gt_m, sh_f, sc_f, gt_f = [mod[l, :, None, i * d:(i + 1) * d] for i in range(6)]

        proj, fox_raw = _norm_proj(x, mix_norm_pre[l], sc_m, sh_m, w_proj, l, tm=tm, tn=PROJ_TN)
        fox_qa, fox_ka = _fox_gate(fox_raw, fox_bias[l], t=min(512, seq))

        ya = _moba(proj, tab_a)
        yb = _fox(proj, fox_qa, fox_ka)
        yc = _conv(proj, conv_w[l], conv_b[l], conv_ln_g[l], conv_ln_b[l])
        yd = _swa(proj, sinks[l], tab_d)

        x, h_ffn = _merge(ya, yb, yc, yd, proj, x, gt_m, mix_norm_post[l], ffn_norm_pre[l], sc_f, sh_f,
                          wa, wb, wc, wd, wo, l, tm=tm_small)
        x = _ffn(x, h_ffn, gt_f, ffn_norm_post[l], wg, wu, wdn, l, tm=tm_ffn, tf=512)
    return x
```

```python
import functools
import math

import jax
import jax.numpy as jnp
import numpy as np
from jax import lax
from jax.experimental import pallas as pl
from jax.experimental.pallas import tpu as pltpu

F32 = jnp.float32
BF16 = jnp.bfloat16
HIGHEST = lax.Precision.HIGHEST

HEAD_DIM = 128
A_HEADS = 4
MOBA_BLOCK = 256
MOBA_TOPK = 3
B_HEADS = 4
CONV_CH = 512
CONV_WIDTH = 31
D_Q_HEADS = 8
D_KV_HEADS = 2
D_HEAD_DIM = 64
WINDOW = 128
N_BUCKETS = 32
MAX_DISTANCE = 1024
N_BRANCHES = 4
RMS_EPS = 1e-6
LN_EPS = 1e-5

A_W = A_HEADS * HEAD_DIM
B_W = B_HEADS * HEAD_DIM
D_QW = D_Q_HEADS * D_HEAD_DIM
D_KVW = D_KV_HEADS * D_HEAD_DIM

LANES = 128
SUBLANES = 8
VMEM_LIMIT = 56 * 1024 * 1024

NEG = -1e30
LOG2E = math.log2(math.e)
ATT_G = 4
ATT_T = MOBA_BLOCK
ATT_U = 4

D_MODEL = 2048
GATE_W = N_BRANCHES * D_MODEL
COL_AQ, COL_AK, COL_AV = GATE_W, GATE_W + A_W, GATE_W + 2 * A_W
COL_BQ, COL_BK, COL_BV = COL_AV + A_W, COL_AV + A_W + B_W, COL_AV + A_W + 2 * B_W
COL_CU = COL_BV + B_W
COL_DQ = COL_CU + 2 * CONV_CH
COL_DK = COL_DQ + D_QW
COL_DV = COL_DK + D_KVW
PROJ_W = COL_DV + D_KVW
PROJ_TN = 1024
PROJ_W_PAD = -(-PROJ_W // PROJ_TN) * PROJ_TN
MOBA_FAR = -(-(MAX_DISTANCE + MOBA_BLOCK - 1) // MOBA_BLOCK)
VOID_COL = LANES - 1
assert ATT_G % ATT_U == 0 and (MOBA_FAR - 1) % ATT_U == 0


def _params(sem, vmem=VMEM_LIMIT):
    return pltpu.CompilerParams(dimension_semantics=sem, vmem_limit_bytes=vmem)


def _t5_bucket(dist):
    max_exact = N_BUCKETS // 2
    d = jnp.maximum(dist, 0)
    log_ratio = jnp.log(jnp.maximum(d, 1).astype(jnp.float32) / max_exact) / math.log(MAX_DISTANCE / max_exact)
    large = max_exact + (log_ratio * (N_BUCKETS - max_exact)).astype(jnp.int32)
    large = jnp.minimum(large, N_BUCKETS - 1)
    return jnp.where(d < max_exact, d, large)


def _rms(y):
    return y * lax.rsqrt(jnp.mean(y * y, axis=-1, keepdims=True) + RMS_EPS)


def _modulated_norm(x, g, sc, sh):
    return (_rms(x) * g) * (1.0 + sc) + sh


MOD_TN = 512
MOD_KC = 256


def _mod_kernel(ct_ref, w_ref, b_ref, o_ref):
    d, nb = ct_ref.shape
    tn = w_ref.shape[1]
    ct = ct_ref[...]
    ca = ct * jax.nn.sigmoid(ct)
    rows = []
    for r in range(nb):
        acc = jnp.zeros((1, tn), F32)
        for kc in range(d // MOD_KC):
            sl = slice(kc * MOD_KC, (kc + 1) * MOD_KC)
            acc = acc + jnp.sum(w_ref[sl, :] * ca[sl, r:r + 1], axis=0, keepdims=True)
        rows.append(acc)
    o_ref[...] = jnp.concatenate(rows, axis=0) + b_ref[...]


def _modulation(c, w_mod, b_mod):
    depth, d, n = w_mod.shape
    bsz = c.shape[0]
    return pl.pallas_call(
        _mod_kernel,
        grid=(depth, n // MOD_TN),
        in_specs=[
            pl.BlockSpec((d, bsz), lambda l, j: (0, 0)),
            pl.BlockSpec((None, d, MOD_TN), lambda l, j: (l, 0, j)),
            pl.BlockSpec((None, 1, MOD_TN), lambda l, j: (l, 0, j)),
        ],
        out_specs=pl.BlockSpec((None, bsz, MOD_TN), lambda l, j: (l, 0, j)),
        out_shape=jax.ShapeDtypeStruct((depth, bsz, n), F32),
        compiler_params=_params(("parallel", "parallel")),
        name="mod",
    )(c.T, w_mod, b_mod.reshape(depth, 1, n))


def _bias_table_kernel(rb_ref, bucket_ref, o_ref, *, scale):
    h = pl.program_id(0)
    b = bucket_ref[...]
    acc = jnp.where(b < 0, NEG, 0.0).astype(F32)
    for u in range(N_BUCKETS):
        acc = jnp.where(b == u, rb_ref[h, u] * scale, acc)
    o_ref[...] = acc


def _bias_tables(rel_bias_heads, bucket, *, scale):
    nh = rel_bias_heads.shape[1]
    nt, r, c = bucket.shape
    return pl.pallas_call(
        functools.partial(_bias_table_kernel, scale=scale),
        grid=(nh, nt),
        in_specs=[
            pl.BlockSpec(memory_space=pltpu.SMEM),
            pl.BlockSpec((None, r, c), lambda h, t: (t, 0, 0)),
        ],
        out_specs=pl.BlockSpec((None, None, r, c), lambda h, t: (h, t, 0, 0)),
        out_shape=jax.ShapeDtypeStruct((nh, nt, r, c), F32),
        compiler_params=_params(("parallel", "parallel")),
        name="bias_table",
    )(rel_bias_heads.T, bucket)


def _moba_buckets():
    blk = MOBA_BLOCK
    i = jnp.arange(blk)[None, :]
    j = jnp.arange(blk)[:, None]
    tabs = []
    for delta in range(MOBA_FAR):
        dist = delta * blk + i - j
        tabs.append(jnp.where(dist >= 0, _t5_bucket(dist), -1))
    tabs.append(_t5_bucket(jnp.full((blk, blk), MAX_DISTANCE, jnp.int32)))
    return jnp.stack(tabs).astype(jnp.int32)


def _swa_buckets():
    qi = jnp.arange(WINDOW)[:, None]
    kj = jnp.arange(2 * WINDOW)[None, :]
    dist = qi + WINDOW - kj
    in_win = (dist >= 0) & (dist < WINDOW)
    return jnp.where(in_win, _t5_bucket(dist), -1).astype(jnp.int32)[None]


def _norm_proj_kernel(x_ref, g_ref, sc_ref, sh_ref, w_ref, ws_ref, o_ref, os_ref, h_scr):
    j = pl.program_id(2)
    tn = w_ref.shape[0]

    @pl.when(j == 0)
    def _():
        h = _modulated_norm(x_ref[...], g_ref[...], sc_ref[...], sh_ref[...]).astype(BF16)
        h_scr[...] = h
        os_ref[...] = _dot_nt(h, ws_ref[...])

    @pl.when(j < GATE_W // tn)
    def _():
        o_ref[...] = jax.nn.sigmoid(_dot_nt(h_scr[...], w_ref[...])).astype(o_ref.dtype)

    @pl.when(j >= GATE_W // tn)
    def _():
        o_ref[...] = _dot_nt(h_scr[...], w_ref[...]).astype(o_ref.dtype)


def _norm_proj(x, g, sc, sh, w, layer, *, tm, tn):
    bsz, seq, d = x.shape
    n = PROJ_W_PAD
    ns = LANES
    assert GATE_W % tn == 0 and n % tn == 0 and w.shape[1] == PACK_ROWS
    return pl.pallas_call(
        _norm_proj_kernel,
        grid=(bsz, seq // tm, n // tn),
        in_specs=[
            pl.BlockSpec((None, tm, d), lambda b, i, j: (b, i, 0)),
            pl.BlockSpec((1, d), lambda b, i, j: (0, 0)),
            pl.BlockSpec((None, 1, d), lambda b, i, j: (b, 0, 0)),
            pl.BlockSpec((None, 1, d), lambda b, i, j: (b, 0, 0)),
            pl.BlockSpec((None, tn, d), lambda b, i, j: (layer, j, 0)),
            pl.BlockSpec((None, ns, d), lambda b, i, j: (layer, PACK_SIDE_ROW // ns, 0)),
        ],
        out_specs=[pl.BlockSpec((None, tm, tn), lambda b, i, j: (b, i, j)),
                   pl.BlockSpec((None, tm, ns), lambda b, i, j: (b, i, 0))],
        out_shape=[jax.ShapeDtypeStruct((bsz, seq, n), BF16), jax.ShapeDtypeStruct((bsz, seq, ns), F32)],
        scratch_shapes=[pltpu.VMEM((tm, d), BF16)],
        compiler_params=_params(("parallel", "parallel", "arbitrary")),
        name="norm_proj",
    )(x, g.reshape(1, d), sc, sh, w, w)


def _split3(c):
    hi = c.astype(BF16).astype(F32)
    mid = (c - hi).astype(BF16).astype(F32)
    lo = (c - hi - mid).astype(BF16).astype(F32)
    return hi, mid, lo


def _fox_gate_kernel(x_ref, fb_ref, qa_ref, ka_ref, carry_scr):
    t = x_ref.shape[0]

    @pl.when(pl.program_id(1) == 0)
    def _():
        carry_scr[...] = jnp.zeros_like(carry_scr)

    lf = jax.nn.log_sigmoid(x_ref[...] + fb_ref[...])
    tri = (lax.broadcasted_iota(jnp.int32, (t, t), 0) >= lax.broadcasted_iota(jnp.int32, (t, t), 1)).astype(F32)
    c = jnp.dot(tri, lf, precision=HIGHEST, preferred_element_type=F32) + carry_scr[...]
    carry_scr[...] = c[t - 1:t, :]
    lane = lax.broadcasted_iota(jnp.int32, (t, LANES), 1)
    sub = lax.broadcasted_iota(jnp.int32, (SUBLANES, t), 0)
    ct = c.T * LOG2E
    for h in range(B_HEADS):
        hi, mid, lo = _split3(jnp.broadcast_to(c[:, h:h + 1], (t, LANES)) * LOG2E)
        ka = jnp.where(lane == 0, -hi, jnp.where(lane == 1, -mid, jnp.where(lane == 2, -lo, jnp.where(lane < 6, 1.0, 0.0))))
        ka_ref[h] = ka.astype(BF16)
        hi, mid, lo = _split3(jnp.broadcast_to(ct[h:h + 1, :], (SUBLANES, t)))
        qa = jnp.where(sub < 3, 1.0, jnp.where(sub == 3, hi, jnp.where(sub == 4, mid, jnp.where(sub == 5, lo, 0.0))))
        qa_ref[h] = jnp.concatenate([qa, jnp.zeros((LANES - SUBLANES, t), F32)], axis=0).astype(BF16)


def _fox_gate(raw, fox_b, *, t):
    bsz, seq, _ = raw.shape
    fb = jnp.zeros((1, LANES), F32).at[0, :B_HEADS].set(fox_b)
    aug = pl.BlockSpec((None, B_HEADS, t, LANES), lambda b, i: (b, 0, i, 0))
    aug_t = pl.BlockSpec((None, B_HEADS, LANES, t), lambda b, i: (b, 0, 0, i))
    return pl.pallas_call(
        _fox_gate_kernel,
        grid=(bsz, seq // t),
        in_specs=[
            pl.BlockSpec((None, t, LANES), lambda b, i: (b, i, 0)),
            pl.BlockSpec((1, LANES), lambda b, i: (0, 0)),
        ],
        out_specs=[aug_t, aug],
        out_shape=[jax.ShapeDtypeStruct((bsz, B_HEADS, LANES, seq), BF16),
                   jax.ShapeDtypeStruct((bsz, B_HEADS, seq, LANES), BF16)],
        scratch_shapes=[pltpu.VMEM((1, LANES), F32)],
        compiler_params=_params(("parallel", "arbitrary")),
        name="fox_gate",
    )(raw, fb)


ONES_ROWS = 16


def _softmax_step(zt, vt, carry):
    m, acc = carry
    m_new = jnp.maximum(m, jnp.max(zt, axis=0, keepdims=True))
    p = jnp.exp2(zt - m_new)
    acc = jnp.exp2(m - m_new) * acc + jnp.dot(vt, p.astype(BF16), preferred_element_type=F32)
    return m_new, acc


def _softmax_init(tq, dh):
    return (jnp.full((1, tq), NEG, F32), jnp.zeros((dh + ONES_ROWS, tq), F32))


def _softmax_finish(carry, dh):
    _, acc = carry
    return (acc[:dh, :] / acc[dh:dh + 1, :]).T


def _fill_vt(vt_scr, v_ref, chunk):
    seq, dh = v_ref.shape
    for cix in range(seq // chunk):
        sl = slice(cix * chunk, (cix + 1) * chunk)
        vt_scr[:dh, sl] = _transpose_bf16(v_ref[sl, :])
    vt_scr[dh:, :] = jnp.ones((ONES_ROWS, seq), BF16)


ATT_LOOKAHEAD = 6


def _run_substeps(steps, carry):
    carry = list(carry)
    pending = {}
    for s in range(min(ATT_LOOKAHEAD, len(steps))):
        pending[s] = steps[s][1]()
    for s, (j, _, fix, vt) in enumerate(steps):
        if s + ATT_LOOKAHEAD < len(steps):
            pending[s + ATT_LOOKAHEAD] = steps[s + ATT_LOOKAHEAD][1]()
        z, cj = pending.pop(s), carry[j]
        if fix is not None:
            z, cj = fix(z, cj)
        carry[j] = _softmax_step(z, vt, cj)
    return carry


def _transpose_bf16(x):
    return x.astype(F32).T.astype(BF16)


_NT = (((1,), (1,)), ((), ()))


def _dot_nt(a, b):
    return lax.dot_general(a, b, _NT, preferred_element_type=F32)


def _moba_kernel(cfar_ref, q_ref, k_ref, v_ref, tab_ref, o_ref, kmean_scr, sc_scr, qt_scr, vt_scr):
    g, t = ATT_G, ATT_T
    seq, dh = k_ref.shape
    nblk = seq // t
    rows = g * t
    i = pl.program_id(2)
    base = i * g

    @pl.when(i == 0)
    def _():
        r = lax.broadcasted_iota(jnp.int32, (nblk, seq), 0)
        c = lax.broadcasted_iota(jnp.int32, (nblk, seq), 1)
        avg = jnp.where((c >= r * t) & (c < (r + 1) * t), 1.0 / t, 0.0).astype(BF16)
        kmean_scr[...] = jnp.dot(avg, k_ref[...], preferred_element_type=F32)
        _fill_vt(vt_scr, v_ref, rows)

    q = q_ref[...]
    qt_scr[:dh, :] = _transpose_bf16(q)
    sc = lax.dot_general(kmean_scr[...], q.astype(F32), _NT, precision=HIGHEST, preferred_element_type=F32)
    row = lax.broadcasted_iota(jnp.int32, (nblk, rows), 0)
    own = base + lax.broadcasted_iota(jnp.int32, (nblk, rows), 1) // t
    past = row < own
    sc = jnp.where(past, sc, -jnp.inf)
    sc_scr[...] = sc

    def rank_body(m, rank):
        sm = sc_scr[pl.ds(m, 1), :]
        return rank + jnp.where(sm > sc, 1.0, jnp.where(sm == sc, jnp.where(row > m, 1.0, 0.0), 0.0))

    rank = lax.fori_loop(0, base + g - 1, rank_body, jnp.zeros((nblk, rows), F32))
    pen = jnp.where(past, jnp.where(rank < MOBA_TOPK, 0.0, NEG), jnp.where(row == own, 0.0, NEG))
    void = jnp.where(lax.broadcasted_iota(jnp.int32, (SUBLANES, rows), 0) == SUBLANES - 1, NEG, 0.0)
    pen = jnp.concatenate([pen, jnp.zeros((LANES - SUBLANES - nblk, rows), F32), void], axis=0)
    qt_scr[dh:, :] = pen.astype(BF16)

    lane = lax.broadcasted_iota(jnp.int32, (t, LANES), 1)

    def kv(n, col):
        start = pl.multiple_of(n * t, t)
        onehot = jnp.where(lane == col, 1.0, 0.0).astype(BF16)
        return jnp.concatenate([k_ref[pl.ds(start, t), :], onehot], axis=1), vt_scr[:, pl.ds(start, t)]

    def logits(kb, j):
        return jnp.dot(kb, qt_scr[:, j * t:(j + 1) * t], preferred_element_type=F32)

    def far_body(grp, carry):
        steps = []
        for u in range(ATT_U):
            n = grp * ATT_U + u
            kb, vb = kv(n, n)
            steps += [(j, functools.partial(logits, kb, j), None, vb) for j in range(g)]
        return tuple(_run_substeps(steps, carry))

    carry = lax.fori_loop(0, jnp.maximum(base - (MOBA_FAR - 1), 0) // ATT_U, far_body,
                          tuple(_softmax_init(t, dh) for _ in range(g)))
    cfar = cfar_ref[pl.program_id(1)]

    def near_fix(delta, z, cj):
        if delta == MOBA_FAR - 1:
            m, acc = cj
            cj = (m + cfar, acc)
        return tab_ref[delta] + z, cj

    steps = []
    for k in range(-(MOBA_FAR - 1), g):
        n = base + k
        if k < 0:
            kb, vb = kv(jnp.maximum(n, 0), jnp.where(n >= 0, n, VOID_COL))
        else:
            kb, vb = kv(n, n)
        for j in range(max(k, 0), g):
            fix = functools.partial(near_fix, j - k) if j - k < MOBA_FAR else None
            steps.append((j, functools.partial(logits, kb, j), fix, vb))
    carry = _run_substeps(steps, carry)
    for j in range(g):
        o_ref[j * t:(j + 1) * t, :] = _softmax_finish(carry[j], dh).astype(o_ref.dtype)


def _moba(proj, tab):
    bsz, seq, _ = proj.shape
    g, t = ATT_G, ATT_T
    nblk = seq // t
    assert seq % (g * t) == 0 and nblk <= LANES - SUBLANES and nblk % SUBLANES == 0
    qc, kc, vc = COL_AQ // HEAD_DIM, COL_AK // HEAD_DIM, COL_AV // HEAD_DIM
    cfar = tab[:, MOBA_FAR, 0, 0]
    return pl.pallas_call(
        _moba_kernel,
        grid=(bsz, A_HEADS, nblk // g),
        in_specs=[
            pl.BlockSpec(memory_space=pltpu.SMEM),
            pl.BlockSpec((None, g * t, HEAD_DIM), lambda b, h, i: (b, i, qc + h)),
            pl.BlockSpec((None, seq, HEAD_DIM), lambda b, h, i: (b, 0, kc + h)),
            pl.BlockSpec((None, seq, HEAD_DIM), lambda b, h, i: (b, 0, vc + h)),
            pl.BlockSpec((None, MOBA_FAR, t, t), lambda b, h, i: (h, 0, 0, 0)),
        ],
        out_specs=pl.BlockSpec((None, g * t, HEAD_DIM), lambda b, h, i: (b, i, h)),
        out_shape=jax.ShapeDtypeStruct((bsz, seq, A_W), BF16),
        scratch_shapes=[pltpu.VMEM((nblk, HEAD_DIM), F32), pltpu.VMEM((nblk, g * t), F32),
                        pltpu.VMEM((HEAD_DIM + LANES, g * t), BF16), pltpu.VMEM((HEAD_DIM + ONES_ROWS, seq), BF16)],
        compiler_params=_params(("parallel", "parallel", "arbitrary")),
        name="moba",
    )(cfar, proj, proj, proj, tab)


def _fox_kernel(q_ref, qa_ref, k_ref, ka_ref, v_ref, o_ref, qt_scr, vt_scr):
    g, t = ATT_G, ATT_T
    rows = g * t
    seq, dh = k_ref.shape
    base = pl.program_id(2) * g

    @pl.when(base == 0)
    def _():
        _fill_vt(vt_scr, v_ref, rows)

    qt_scr[:dh, :] = _transpose_bf16(q_ref[...])
    qt_scr[dh:, :] = qa_ref[...]

    def kv(n):
        start = pl.multiple_of(n * t, t)
        return (jnp.concatenate([k_ref[pl.ds(start, t), :], ka_ref[pl.ds(start, t), :]], axis=1),
                vt_scr[:, pl.ds(start, t)])

    def logits(kb, j):
        return jnp.dot(kb, qt_scr[:, j * t:(j + 1) * t], preferred_element_type=F32)

    def body(grp, carry):
        steps = []
        for u in range(ATT_U):
            kb, vb = kv(grp * ATT_U + u)
            steps += [(j, functools.partial(logits, kb, j), None, vb) for j in range(g)]
        return tuple(_run_substeps(steps, carry))

    carry = lax.fori_loop(0, base // ATT_U, body, tuple(_softmax_init(t, dh) for _ in range(g)))
    causal = lax.broadcasted_iota(jnp.int32, (t, t), 0) <= lax.broadcasted_iota(jnp.int32, (t, t), 1)

    def diag_fix(z, cj):
        return jnp.where(causal, z, NEG), cj

    steps = []
    for k in range(g):
        kb, vb = kv(base + k)
        steps += [(j, functools.partial(logits, kb, j), diag_fix if j == k else None, vb) for j in range(k, g)]
    carry = _run_substeps(steps, carry)
    for j in range(g):
        o_ref[j * t:(j + 1) * t, :] = _softmax_finish(carry[j], dh).astype(o_ref.dtype)


def _fox(proj, qa, ka):
    bsz, seq, _ = proj.shape
    g, t = ATT_G, ATT_T
    assert seq % (g * t) == 0
    qc, kc, vc = COL_BQ // HEAD_DIM, COL_BK // HEAD_DIM, COL_BV // HEAD_DIM
    return pl.pallas_call(
        _fox_kernel,
        grid=(bsz, B_HEADS, seq // (g * t)),
        in_specs=[
            pl.BlockSpec((None, g * t, HEAD_DIM), lambda b, h, i: (b, i, qc + h)),
            pl.BlockSpec((None, None, LANES, g * t), lambda b, h, i: (b, h, 0, i)),
            pl.BlockSpec((None, seq, HEAD_DIM), lambda b, h, i: (b, 0, kc + h)),
            pl.BlockSpec((None, None, seq, LANES), lambda b, h, i: (b, h, 0, 0)),
            pl.BlockSpec((None, seq, HEAD_DIM), lambda b, h, i: (b, 0, vc + h)),
        ],
        out_specs=pl.BlockSpec((None, g * t, HEAD_DIM), lambda b, h, i: (b, i, h)),
        out_shape=jax.ShapeDtypeStruct((bsz, seq, B_W), BF16),
        scratch_shapes=[pltpu.VMEM((HEAD_DIM + LANES, g * t), BF16), pltpu.VMEM((HEAD_DIM + ONES_ROWS, seq), BF16)],
        compiler_params=_params(("parallel", "parallel", "arbitrary")),
        name="fox",
    )(proj, qa, proj, ka, proj)


SWA_BLOCKS = 4


def _swa_kernel(sink_ref, q_ref, kp_ref, kc_ref, vp_ref, vc_ref, tab_ref, o_ref):
    w = WINDOW
    dh = D_HEAD_DIM
    grp = D_Q_HEADS // D_KV_HEADS
    i = pl.program_id(1)
    kall = jnp.concatenate([kp_ref[...], kc_ref[...]], axis=0)
    vall = jnp.concatenate([vp_ref[...], vc_ref[...]], axis=0)
    col = lax.broadcasted_iota(jnp.int32, (w, 2 * w), 1)
    first = col >= jnp.where(i > 0, 0, w)
    for blk in range(q_ref.shape[0] // w):
        q = q_ref[blk * w:(blk + 1) * w, :]
        kcat = kall[blk * w:(blk + 2) * w, :]
        vcat = vall[blk * w:(blk + 2) * w, :]
        outs = []
        for h in range(D_Q_HEADS):
            g = h // grp
            s = tab_ref[h] + _dot_nt(q[:, h * dh:(h + 1) * dh], kcat[:, g * dh:(g + 1) * dh])
            if blk == 0:
                s = jnp.where(first, s, NEG)
            sink = sink_ref[h]
            m = jnp.maximum(jnp.max(s, axis=-1, keepdims=True), sink)
            p = jnp.exp(s - m)
            den = jnp.sum(p, axis=-1, keepdims=True) + jnp.exp(sink - m)
            pv = jnp.dot(p.astype(BF16), vcat[:, g * dh:(g + 1) * dh], preferred_element_type=F32)
            outs.append(pv / den)
        o_ref[blk * w:(blk + 1) * w, :] = jnp.concatenate(outs, axis=1).astype(o_ref.dtype)


def _swa(proj, sinks, tab):
    bsz, seq, _ = proj.shape
    w = WINDOW
    nb = SWA_BLOCKS
    assert seq % (nb * w) == 0
    qc, kc, vc = COL_DQ // D_QW, COL_DK // D_KVW, COL_DV // D_KVW
    prev = lambda i: jnp.maximum(i * nb - 1, 0)
    return pl.pallas_call(
        _swa_kernel,
        grid=(bsz, seq // (nb * w)),
        in_specs=[
            pl.BlockSpec(memory_space=pltpu.SMEM),
            pl.BlockSpec((None, nb * w, D_QW), lambda b, i: (b, i, qc)),
            pl.BlockSpec((None, w, D_KVW), lambda b, i: (b, prev(i), kc)),
            pl.BlockSpec((None, nb * w, D_KVW), lambda b, i: (b, i, kc)),
            pl.BlockSpec((None, w, D_KVW), lambda b, i: (b, prev(i), vc)),
            pl.BlockSpec((None, nb * w, D_KVW), lambda b, i: (b, i, vc)),
            pl.BlockSpec((D_Q_HEADS, w, 2 * w), lambda b, i: (0, 0, 0)),
        ],
        out_specs=pl.BlockSpec((None, nb * w, D_QW), lambda b, i: (b, i, 0)),
        out_shape=jax.ShapeDtypeStruct((bsz, seq, D_QW), BF16),
        compiler_params=_params(("parallel", "parallel")),
        name="swa",
    )(sinks, proj, proj, proj, proj, proj, tab)


CONV_T = 512
CONV_HALO = 32
CONV_ROWS = 64


def _conv_kernel(a_ref, g_ref, ha_ref, hg_ref, w_ref, b_ref, lg_ref, lb_ref, o_ref, buf):
    t = a_ref.shape[0]
    halo = CONV_HALO
    hp = ha_ref[...].astype(F32) * jax.nn.sigmoid(hg_ref[...].astype(F32))
    buf[0:halo, :] = jnp.where(pl.program_id(1) > 0, hp, 0.0)
    buf[halo:halo + t, :] = a_ref[...].astype(F32) * jax.nn.sigmoid(g_ref[...].astype(F32))
    off = halo - (CONV_WIDTH - 1)
    for r in range(t // CONV_ROWS):
        acc = jnp.broadcast_to(b_ref[...], (CONV_ROWS, CONV_CH))
        for rho in range(SUBLANES):
            rows = CONV_ROWS if rho == 0 else CONV_ROWS + SUBLANES
            part = None
            for s in range(off, off + CONV_WIDTH):
                if s % SUBLANES != rho:
                    continue
                start = r * CONV_ROWS + s - rho
                term = w_ref[s - off:s - off + 1, :] * buf[start:start + rows, :]
                part = term if part is None else part + term
            acc = acc + part[rho:rho + CONV_ROWS, :]
        mu = jnp.mean(acc, axis=-1, keepdims=True)
        d = acc - mu
        var = jnp.mean(d * d, axis=-1, keepdims=True)
        y = d * lax.rsqrt(var + LN_EPS) * lg_ref[...] + lb_ref[...]
        o_ref[r * CONV_ROWS:(r + 1) * CONV_ROWS, :] = (y * jax.nn.sigmoid(y)).astype(o_ref.dtype)


def _conv(proj, conv_w, conv_b, ln_g, ln_b):
    bsz, seq, _ = proj.shape
    t = min(CONV_T, seq)
    ca, cg = COL_CU // CONV_CH, COL_CU // CONV_CH + 1
    hb = t // CONV_HALO
    prev = lambda i: jnp.maximum(i * hb - 1, 0)
    row = lambda v: v.reshape(1, CONV_CH)
    return pl.pallas_call(
        _conv_kernel,
        grid=(bsz, seq // t),
        in_specs=[
            pl.BlockSpec((None, t, CONV_CH), lambda b, i: (b, i, ca)),
            pl.BlockSpec((None, t, CONV_CH), lambda b, i: (b, i, cg)),
            pl.BlockSpec((None, CONV_HALO, CONV_CH), lambda b, i: (b, prev(i), ca)),
            pl.BlockSpec((None, CONV_HALO, CONV_CH), lambda b, i: (b, prev(i), cg)),
            pl.BlockSpec((CONV_WIDTH, CONV_CH), lambda b, i: (0, 0)),
            pl.BlockSpec((1, CONV_CH), lambda b, i: (0, 0)),
            pl.BlockSpec((1, CONV_CH), lambda b, i: (0, 0)),
            pl.BlockSpec((1, CONV_CH), lambda b, i: (0, 0)),
        ],
        out_specs=pl.BlockSpec((None, t, CONV_CH), lambda b, i: (b, i, 0)),
        out_shape=jax.ShapeDtypeStruct((bsz, seq, CONV_CH), BF16),
        scratch_shapes=[pltpu.VMEM((CONV_HALO + t, CONV_CH), F32)],
        compiler_params=_params(("parallel", "parallel")),
        name="conv",
    )(proj, proj, proj, proj, conv_w, row(conv_b), row(ln_g), row(ln_b))


def _merge_kernel(ya_ref, yb_ref, yc_ref, yd_ref, gates_ref, x_ref, gt_ref, gp_ref, gf_ref, scf_ref, shf_ref,
                  wa_ref, wb_ref, wc_ref, wd_ref, wo_ref, o_ref, h_ref):
    d = x_ref.shape[1]
    y = None
    for i, (br, w) in enumerate(((ya_ref, wa_ref), (yb_ref, wb_ref), (yc_ref, wc_ref), (yd_ref, wd_ref))):
        t = jnp.dot(br[...], w[...], preferred_element_type=F32) * gates_ref[:, i * d:(i + 1) * d].astype(F32)
        y = t if y is None else y + t
    z = jnp.dot(y.astype(BF16), wo_ref[...], preferred_element_type=F32)
    x_new = x_ref[...] + gt_ref[...] * (_rms(z) * gp_ref[...])
    o_ref[...] = x_new
    h_ref[...] = _modulated_norm(x_new, gf_ref[...], scf_ref[...], shf_ref[...]).astype(h_ref.dtype)


def _merge(ya, yb, yc, yd, gates, x, gt, g_post, g_ffn, sc_f, sh_f, wa, wb, wc, wd, wo, layer, *, tm):
    bsz, seq, d = x.shape
    tok = lambda width: pl.BlockSpec((None, tm, width), lambda b, i: (b, i, 0))
    vec = lambda: pl.BlockSpec((None, 1, d), lambda b, i: (b, 0, 0))
    par = lambda: pl.BlockSpec((1, d), lambda b, i: (0, 0))
    full = lambda a: pl.BlockSpec((None,) + a.shape[1:], lambda b, i: (layer, 0, 0),
                                  pipeline_mode=pl.Buffered(1))
    return pl.pallas_call(
        _merge_kernel,
        grid=(bsz, seq // tm),
        in_specs=[tok(A_W), tok(B_W), tok(CONV_CH), tok(D_QW), tok(N_BRANCHES * d), tok(d),
                  vec(), par(), par(), vec(), vec(),
                  full(wa), full(wb), full(wc), full(wd), full(wo)],
        out_specs=[tok(d), tok(d)],
        out_shape=[jax.ShapeDtypeStruct((bsz, seq, d), F32), jax.ShapeDtypeStruct((bsz, seq, d), BF16)],
        compiler_params=_params(("parallel", "parallel")),
        name="merge",
    )(ya, yb, yc, yd, gates, x, gt, g_post.reshape(1, d), g_ffn.reshape(1, d), sc_f, sh_f, wa, wb, wc, wd, wo)


def _ffn_kernel(x_ref, h_ref, gt_ref, gp_ref, wg_ref, wu_ref, wd_ref, o_ref, acc_scr):
    f = pl.program_id(2)

    @pl.when(f == 0)
    def _():
        acc_scr[...] = jnp.zeros_like(acc_scr)

    h = h_ref[...]
    g = jnp.dot(h, wg_ref[...], preferred_element_type=F32)
    u = jnp.dot(h, wu_ref[...], preferred_element_type=F32)
    a = ((g * jax.nn.sigmoid(g)) * u).astype(BF16)
    acc_scr[...] += jnp.dot(a, wd_ref[...], preferred_element_type=F32)

    @pl.when(f == pl.num_programs(2) - 1)
    def _():
        o_ref[...] = x_ref[...] + gt_ref[...] * (_rms(acc_scr[...]) * gp_ref[...])


def _ffn(x, h, gt, g_post, wg, wu, wd, layer, *, tm, tf):
    bsz, seq, d = x.shape
    dff = wg.shape[2]
    vec = lambda: pl.BlockSpec((None, 1, d), lambda b, i, f: (b, 0, 0))
    par = lambda: pl.BlockSpec((1, d), lambda b, i, f: (0, 0))
    return pl.pallas_call(
        _ffn_kernel,
        grid=(bsz, seq // tm, dff // tf),
        in_specs=[
            pl.BlockSpec((None, tm, d), lambda b, i, f: (b, i, 0)),
            pl.BlockSpec((None, tm, d), lambda b, i, f: (b, i, 0)),
            vec(), par(),
            pl.BlockSpec((None, d, tf), lambda b, i, f: (layer, 0, f)),
            pl.BlockSpec((None, d, tf), lambda b, i, f: (layer, 0, f)),
            pl.BlockSpec((None, tf, d), lambda b, i, f: (layer, f, 0)),
        ],
        out_specs=pl.BlockSpec((None, tm, d), lambda b, i, f: (b, i, 0)),
        out_shape=jax.ShapeDtypeStruct((bsz, seq, d), F32),
        scratch_shapes=[pltpu.VMEM((tm, d), F32)],
        compiler_params=_params(("parallel", "parallel", "arbitrary")),
        name="ffn",
    )(x, h, gt, g_post.reshape(1, d), wg, wu, wd)


SRC_QKV = 3 * A_W + 3 * B_W
SRC_REST = SRC_QKV + B_HEADS
SRC_GATE = SRC_REST + (PROJ_W - GATE_W - SRC_QKV)
IN_W = SRC_GATE + GATE_W
PACK_R = 256
PACK_SIDE_ROW = PROJ_W_PAD
PACK_ROWS = PROJ_W_PAD + PACK_R
assert GATE_W % PACK_R == 0 and SRC_QKV % PACK_R == 0 and PROJ_W % PACK_R == 0 and PROJ_W_PAD % PACK_R == 0


def _pack_tables():
    s2 = HEAD_DIM ** -0.5 * LOG2E
    src, scale, rows = [], [], []
    for b in range(PACK_ROWS // PACK_R):
        row = b * PACK_R
        if row < GATE_W:
            src.append(SRC_GATE + row); scale.append(1.0); rows.append(PACK_R)
        elif row < GATE_W + SRC_QKV:
            s = row - GATE_W
            is_q = s < A_W or 3 * A_W <= s < 3 * A_W + B_W
            src.append(s); scale.append(s2 if is_q else 1.0); rows.append(PACK_R)
        elif row < PROJ_W:
            s = row - GATE_W - SRC_QKV
            is_q = 2 * CONV_CH <= s < 2 * CONV_CH + D_QW
            src.append(SRC_REST + s); scale.append(D_HEAD_DIM ** -0.5 if is_q else 1.0); rows.append(PACK_R)
        elif row == PACK_SIDE_ROW:
            src.append(SRC_QKV); scale.append(1.0); rows.append(B_HEADS)
        else:
            src.append(0); scale.append(0.0); rows.append(0)
    return np.asarray(src, np.int32), np.asarray(scale, np.float32), np.asarray(rows, np.int32)


def _pack_kernel(src_ref, scale_ref, rows_ref, a_ref, o_ref):
    layer, blk = pl.program_id(0), pl.program_id(1)
    depth = pl.num_programs(0)
    kt = o_ref.shape[1] // LANES
    cols = [a_ref[pl.ds(t * depth + layer, PACK_R, stride=kt * depth), :] for t in range(kt)]
    keep = lax.broadcasted_iota(jnp.int32, o_ref.shape, 0) < rows_ref[blk]
    o_ref[...] = jnp.where(keep, jnp.concatenate(cols, axis=1) * scale_ref[blk], 0.0).astype(o_ref.dtype)


def _pack_in_proj(w):
    depth, d, in_w = w.shape
    assert in_w == IN_W and d % LANES == 0
    kt = d // LANES
    view = jnp.transpose(w.reshape(depth, kt, LANES, in_w), (3, 1, 0, 2)).reshape(in_w * kt * depth, LANES)
    src, scale, rows = _pack_tables()
    per_feature = kt * depth
    return pl.pallas_call(
        _pack_kernel,
        grid_spec=pltpu.PrefetchScalarGridSpec(
            num_scalar_prefetch=3,
            grid=(depth, PACK_ROWS // PACK_R),
            in_specs=[pl.BlockSpec((pl.Element(PACK_R * per_feature), pl.Element(LANES)),
                                   lambda l, b, src, scale, rows: (src[b] * per_feature, 0))],
            out_specs=pl.BlockSpec((None, PACK_R, d), lambda l, b, src, scale, rows: (l, b, 0)),
        ),
        out_shape=jax.ShapeDtypeStruct((depth, PACK_ROWS, d), BF16),
        compiler_params=_params(("parallel", "parallel")),
        name="pack_in_proj",
    )(jnp.asarray(src), jnp.asarray(scale), jnp.asarray(rows), view)


CAST_BLOCK_ELEMS = 1 << 20


def _cast_kernel(w_ref, o_ref):
    o_ref[...] = w_ref[...].astype(o_ref.dtype)


def _cast_bf16(w):
    depth, r, c = w.shape
    tr = 16
    while tr * 2 * c <= CAST_BLOCK_ELEMS and r % (tr * 2) == 0:
        tr *= 2
    assert r % tr == 0
    return pl.pallas_call(
        _cast_kernel,
        grid=(depth, r // tr),
        in_specs=[pl.BlockSpec((None, tr, c), lambda l, i: (l, i, 0))],
        out_specs=pl.BlockSpec((None, tr, c), lambda l, i: (l, i, 0)),
        out_shape=jax.ShapeDtypeStruct(w.shape, BF16),
        compiler_params=_params(("parallel", "parallel")),
        name="cast_bf16",
    )(w)


def kernel(x, c, rel_bias, w_mod, b_mod, mix_norm_pre, mix_norm_post, w_in, fox_bias, conv_w, conv_b, conv_ln_g, conv_ln_b, sinks, w_branch_a, w_branch_b, w_branch_c, w_branch_d, w_out, ffn_norm_pre, ffn_norm_post, w_ffn_gate, w_ffn_up, w_ffn_down):
    depth = w_mod.shape[0]
    bsz, seq, d = x.shape
    tm = min(1024, seq)
    tm_small = min(256, seq)
    tm_ffn = min(512, seq)

    mod = _modulation(c, w_mod, b_mod)
    tab_a = _bias_tables(rel_bias[:, :A_HEADS], _moba_buckets(), scale=LOG2E)
    tab_d = _bias_tables(rel_bias[:, A_HEADS:], _swa_buckets(), scale=1.0)[:, 0]

    wa, wb, wc, wd, wo = [_cast_bf16(w) for w in (w_branch_a, w_branch_b, w_branch_c, w_branch_d, w_out)]
    wg, wu, wdn = [_cast_bf16(w) for w in (w_ffn_gate, w_ffn_up, w_ffn_down)]

    w_proj = _pack_in_proj(w_in)

    for l in range(depth):
        sh_m, sc_m, gt_m, sh_f, sc_f, gt_f = [mod[l, :, None, i * d:(i + 1) * d] for i in range(6)]

        proj, fox_raw = _norm_proj(x, mix_norm_pre[l], sc_m, sh_m, w_proj, l, tm=tm, tn=PROJ_TN)
        fox_qa, fox_ka = _fox_gate(fox_raw, fox_bias[l], t=min(512, seq))

        ya = _moba(proj, tab_a)
        yb = _fox(proj, fox_qa, fox_ka)
        yc = _conv(proj, conv_w[l], conv_b[l], conv_ln_g[l], conv_ln_b[l])
        yd = _swa(proj, sinks[l], tab_d)

        x, h_ffn = _merge(ya, yb, yc, yd, proj, x, gt_m, mix_norm_post[l], ffn_norm_pre[l], sc_f, sh_f,
                          wa, wb, wc, wd, wo, l, tm=tm_small)
        x = _ffn(x, h_ffn, gt_f, ffn_norm_post[l], wg, wu, wdn, l, tm=tm_ffn, tf=512)
    return x
```

```python
import functools
import math

import jax
import jax.numpy as jnp
import numpy as np
from jax import lax
from jax.experimental import pallas as pl
from jax.experimental.pallas import tpu as pltpu

F32 = jnp.float32
BF16 = jnp.bfloat16
HIGHEST = lax.Precision.HIGHEST

HEAD_DIM = 128
A_HEADS = 4
MOBA_BLOCK = 256
MOBA_TOPK = 3
B_HEADS = 4
CONV_CH = 512
CONV_WIDTH = 31
D_Q_HEADS = 8
D_KV_HEADS = 2
D_HEAD_DIM = 64
WINDOW = 128
N_BUCKETS = 32
MAX_DISTANCE = 1024
N_BRANCHES = 4
RMS_EPS = 1e-6
LN_EPS = 1e-5

A_W = A_HEADS * HEAD_DIM
B_W = B_HEADS * HEAD_DIM
D_QW = D_Q_HEADS * D_HEAD_DIM
D_KVW = D_KV_HEADS * D_HEAD_DIM

LANES = 128
SUBLANES = 8
VMEM_LIMIT = 56 * 1024 * 1024

NEG = -1e30
LOG2E = math.log2(math.e)
ATT_G = 4
ATT_T = MOBA_BLOCK
ATT_U = 4

D_MODEL = 2048
GATE_W = N_BRANCHES * D_MODEL
COL_AQ, COL_AK, COL_AV = GATE_W, GATE_W + A_W, GATE_W + 2 * A_W
COL_BQ, COL_BK, COL_BV = COL_AV + A_W, COL_AV + A_W + B_W, COL_AV + A_W + 2 * B_W
COL_CU = COL_BV + B_W
COL_DQ = COL_CU + 2 * CONV_CH
COL_DK = COL_DQ + D_QW
COL_DV = COL_DK + D_KVW
PROJ_W = COL_DV + D_KVW
PROJ_TN = 1024
PROJ_W_PAD = -(-PROJ_W // PROJ_TN) * PROJ_TN
MOBA_FAR = -(-(MAX_DISTANCE + MOBA_BLOCK - 1) // MOBA_BLOCK)
VOID_COL = LANES - 1
assert ATT_G % ATT_U == 0 and (MOBA_FAR - 1) % ATT_U == 0


def _params(sem, vmem=VMEM_LIMIT):
    return pltpu.CompilerParams(dimension_semantics=sem, vmem_limit_bytes=vmem)


def _t5_bucket(dist):
    max_exact = N_BUCKETS // 2
    d = jnp.maximum(dist, 0)
    log_ratio = jnp.log(jnp.maximum(d, 1).astype(jnp.float32) / max_exact) / math.log(MAX_DISTANCE / max_exact)
    large = max_exact + (log_ratio * (N_BUCKETS - max_exact)).astype(jnp.int32)
    large = jnp.minimum(large, N_BUCKETS - 1)
    return jnp.where(d < max_exact, d, large)


def _rms(y):
    return y * lax.rsqrt(jnp.mean(y * y, axis=-1, keepdims=True) + RMS_EPS)


def _modulated_norm(x, g, sc, sh):
    return (_rms(x) * g) * (1.0 + sc) + sh


MOD_TN = 512
MOD_KC = 256


def _mod_kernel(ct_ref, w_ref, b_ref, o_ref):
    d, nb = ct_ref.shape
    tn = w_ref.shape[1]
    ct = ct_ref[...]
    ca = ct * jax.nn.sigmoid(ct)
    rows = []
    for r in range(nb):
        acc = jnp.zeros((1, tn), F32)
        for kc in range(d // MOD_KC):
            sl = slice(kc * MOD_KC, (kc + 1) * MOD_KC)
            acc = acc + jnp.sum(w_ref[sl, :] * ca[sl, r:r + 1], axis=0, keepdims=True)
        rows.append(acc)
    o_ref[...] = jnp.concatenate(rows, axis=0) + b_ref[...]


def _modulation(c, w_mod, b_mod):
    depth, d, n = w_mod.shape
    bsz = c.shape[0]
    return pl.pallas_call(
        _mod_kernel,
        grid=(depth, n // MOD_TN),
        in_specs=[
            pl.BlockSpec((d, bsz), lambda l, j: (0, 0)),
            pl.BlockSpec((None, d, MOD_TN), lambda l, j: (l, 0, j)),
            pl.BlockSpec((None, 1, MOD_TN), lambda l, j: (l, 0, j)),
        ],
        out_specs=pl.BlockSpec((None, bsz, MOD_TN), lambda l, j: (l, 0, j)),
        out_shape=jax.ShapeDtypeStruct((depth, bsz, n), F32),
        compiler_params=_params(("parallel", "parallel")),
        name="mod",
    )(c.T, w_mod, b_mod.reshape(depth, 1, n))


def _bias_table_kernel(rb_ref, bucket_ref, o_ref, *, scale):
    h = pl.program_id(0)
    b = bucket_ref[...]
    acc = jnp.where(b < 0, NEG, 0.0).astype(F32)
    for u in range(N_BUCKETS):
        acc = jnp.where(b == u, rb_ref[h, u] * scale, acc)
    o_ref[...] = acc


def _bias_tables(rel_bias_heads, bucket, *, scale):
    nh = rel_bias_heads.shape[1]
    nt, r, c = bucket.shape
    return pl.pallas_call(
        functools.partial(_bias_table_kernel, scale=scale),
        grid=(nh, nt),
        in_specs=[
            pl.BlockSpec(memory_space=pltpu.SMEM),
            pl.BlockSpec((None, r, c), lambda h, t: (t, 0, 0)),
        ],
        out_specs=pl.BlockSpec((None, None, r, c), lambda h, t: (h, t, 0, 0)),
        out_shape=jax.ShapeDtypeStruct((nh, nt, r, c), F32),
        compiler_params=_params(("parallel", "parallel")),
        name="bias_table",
    )(rel_bias_heads.T, bucket)


def _moba_buckets():
    blk = MOBA_BLOCK
    i = jnp.arange(blk)[None, :]
    j = jnp.arange(blk)[:, None]
    tabs = []
    for delta in range(MOBA_FAR):
        dist = delta * blk + i - j
        tabs.append(jnp.where(dist >= 0, _t5_bucket(dist), -1))
    tabs.append(_t5_bucket(jnp.full((blk, blk), MAX_DISTANCE, jnp.int32)))
    return jnp.stack(tabs).astype(jnp.int32)


def _swa_buckets():
    qi = jnp.arange(WINDOW)[:, None]
    kj = jnp.arange(2 * WINDOW)[None, :]
    dist = qi + WINDOW - kj
    in_win = (dist >= 0) & (dist < WINDOW)
    return jnp.where(in_win, _t5_bucket(dist), -1).astype(jnp.int32)[None]


def _norm_proj_kernel(x_ref, g_ref, sc_ref, sh_ref, w_ref, ws_ref, o_ref, os_ref, h_scr):
    j = pl.program_id(2)
    tn = w_ref.shape[0]

    @pl.when(j == 0)
    def _():
        h = _modulated_norm(x_ref[...], g_ref[...], sc_ref[...], sh_ref[...]).astype(BF16)
        h_scr[...] = h
        os_ref[...] = _dot_nt(h, ws_ref[...])

    @pl.when(j < GATE_W // tn)
    def _():
        o_ref[...] = jax.nn.sigmoid(_dot_nt(h_scr[...], w_ref[...])).astype(o_ref.dtype)

    @pl.when(j >= GATE_W // tn)
    def _():
        o_ref[...] = _dot_nt(h_scr[...], w_ref[...]).astype(o_ref.dtype)


def _norm_proj(x, g, sc, sh, w, layer, *, tm, tn):
    bsz, seq, d = x.shape
    n = PROJ_W_PAD
    ns = LANES
    assert GATE_W % tn == 0 and n % tn == 0 and w.shape[1] == PACK_ROWS
    return pl.pallas_call(
        _norm_proj_kernel,
        grid=(bsz, seq // tm, n // tn),
        in_specs=[
            pl.BlockSpec((None, tm, d), lambda b, i, j: (b, i, 0)),
            pl.BlockSpec((1, d), lambda b, i, j: (0, 0)),
            pl.BlockSpec((None, 1, d), lambda b, i, j: (b, 0, 0)),
            pl.BlockSpec((None, 1, d), lambda b, i, j: (b, 0, 0)),
            pl.BlockSpec((None, tn, d), lambda b, i, j: (layer, j, 0)),
            pl.BlockSpec((None, ns, d), lambda b, i, j: (layer, PACK_SIDE_ROW // ns, 0)),
        ],
        out_specs=[pl.BlockSpec((None, tm, tn), lambda b, i, j: (b, i, j)),
                   pl.BlockSpec((None, tm, ns), lambda b, i, j: (b, i, 0))],
        out_shape=[jax.ShapeDtypeStruct((bsz, seq, n), BF16), jax.ShapeDtypeStruct((bsz, seq, ns), F32)],
        scratch_shapes=[pltpu.VMEM((tm, d), BF16)],
        compiler_params=_params(("parallel", "parallel", "arbitrary")),
        name="norm_proj",
    )(x, g.reshape(1, d), sc, sh, w, w)


def _split3(c):
    hi = c.astype(BF16).astype(F32)
    mid = (c - hi).astype(BF16).astype(F32)
    lo = (c - hi - mid).astype(BF16).astype(F32)
    return hi, mid, lo


def _fox_gate_kernel(x_ref, fb_ref, qa_ref, ka_ref, carry_scr):
    t = x_ref.shape[0]

    @pl.when(pl.program_id(1) == 0)
    def _():
        carry_scr[...] = jnp.zeros_like(carry_scr)

    lf = jax.nn.log_sigmoid(x_ref[...] + fb_ref[...])
    tri = (lax.broadcasted_iota(jnp.int32, (t, t), 0) >= lax.broadcasted_iota(jnp.int32, (t, t), 1)).astype(F32)
    c = jnp.dot(tri, lf, precision=HIGHEST, preferred_element_type=F32) + carry_scr[...]
    carry_scr[...] = c[t - 1:t, :]
    lane = lax.broadcasted_iota(jnp.int32, (t, LANES), 1)
    sub = lax.broadcasted_iota(jnp.int32, (SUBLANES, t), 0)
    ct = c.T * LOG2E
    for h in range(B_HEADS):
        hi, mid, lo = _split3(jnp.broadcast_to(c[:, h:h + 1], (t, LANES)) * LOG2E)
        ka = jnp.where(lane == 0, -hi, jnp.where(lane == 1, -mid, jnp.where(lane == 2, -lo, jnp.where(lane < 6, 1.0, 0.0))))
        ka_ref[h] = ka.astype(BF16)
        hi, mid, lo = _split3(jnp.broadcast_to(ct[h:h + 1, :], (SUBLANES, t)))
        qa = jnp.where(sub < 3, 1.0, jnp.where(sub == 3, hi, jnp.where(sub == 4, mid, jnp.where(sub == 5, lo, 0.0))))
        qa_ref[h] = jnp.concatenate([qa, jnp.zeros((LANES - SUBLANES, t), F32)], axis=0).astype(BF16)


def _fox_gate(raw, fox_b, *, t):
    bsz, seq, _ = raw.shape
    fb = jnp.zeros((1, LANES), F32).at[0, :B_HEADS].set(fox_b)
    aug = pl.BlockSpec((None, B_HEADS, t, LANES), lambda b, i: (b, 0, i, 0))
    aug_t = pl.BlockSpec((None, B_HEADS, LANES, t), lambda b, i: (b, 0, 0, i))
    return pl.pallas_call(
        _fox_gate_kernel,
        grid=(bsz, seq // t),
        in_specs=[
            pl.BlockSpec((None, t, LANES), lambda b, i: (b, i, 0)),
            pl.BlockSpec((1, LANES), lambda b, i: (0, 0)),
        ],
        out_specs=[aug_t, aug],
        out_shape=[jax.ShapeDtypeStruct((bsz, B_HEADS, LANES, seq), BF16),
                   jax.ShapeDtypeStruct((bsz, B_HEADS, seq, LANES), BF16)],
        scratch_shapes=[pltpu.VMEM((1, LANES), F32)],
        compiler_params=_params(("parallel", "arbitrary")),
        name="fox_gate",
    )(raw, fb)


ONES_ROWS = 16


def _softmax_step(zt, vt, carry):
    m, acc = carry
    m_new = jnp.maximum(m, jnp.max(zt, axis=0, keepdims=True))
    p = jnp.exp2(zt - m_new)
    acc = jnp.exp2(m - m_new) * acc + jnp.dot(vt, p.astype(BF16), preferred_element_type=F32)
    return m_new, acc


def _softmax_init(tq, dh):
    return (jnp.full((1, tq), NEG, F32), jnp.zeros((dh + ONES_ROWS, tq), F32))


def _softmax_finish(carry, dh):
    _, acc = carry
    return (acc[:dh, :] / acc[dh:dh + 1, :]).T


def _fill_vt(vt_scr, v_ref, chunk):
    seq, dh = v_ref.shape
    for cix in range(seq // chunk):
        sl = slice(cix * chunk, (cix + 1) * chunk)
        vt_scr[:dh, sl] = _transpose_bf16(v_ref[sl, :])
    vt_scr[dh:, :] = jnp.ones((ONES_ROWS, seq), BF16)


ATT_LOOKAHEAD = 6


def _run_substeps(steps, carry):
    carry = list(carry)
    pending = {}
    for s in range(min(ATT_LOOKAHEAD, len(steps))):
        pending[s] = steps[s][1]()
    for s, (j, _, fix, vt) in enumerate(steps):
        if s + ATT_LOOKAHEAD < len(steps):
            pending[s + ATT_LOOKAHEAD] = steps[s + ATT_LOOKAHEAD][1]()
        z, cj = pending.pop(s), carry[j]
        if fix is not None:
            z, cj = fix(z, cj)
        carry[j] = _softmax_step(z, vt, cj)
    return carry


def _dense_loop(n_blocks, make_steps, carry):
    big = 2 * ATT_U
    carry = lax.fori_loop(0, n_blocks // big,
                          lambda grp, c: tuple(_run_substeps(make_steps(grp * big, big), c)), carry)
    first = (n_blocks // big) * big
    return lax.fori_loop(0, (n_blocks - first) // ATT_U,
                         lambda grp, c: tuple(_run_substeps(make_steps(first + grp * ATT_U, ATT_U), c)), carry)


def _transpose_bf16(x):
    return x.astype(F32).T.astype(BF16)


_NT = (((1,), (1,)), ((), ()))


def _dot_nt(a, b):
    return lax.dot_general(a, b, _NT, preferred_element_type=F32)


def _moba_kernel(cfar_ref, q_ref, k_ref, v_ref, tab_ref, o_ref, kmean_scr, sc_scr, qt_scr, vt_scr):
    g, t = ATT_G, ATT_T
    seq, dh = k_ref.shape
    nblk = seq // t
    rows = g * t
    i = pl.program_id(2)
    base = i * g

    @pl.when(i == 0)
    def _():
        r = lax.broadcasted_iota(jnp.int32, (nblk, seq), 0)
        c = lax.broadcasted_iota(jnp.int32, (nblk, seq), 1)
        avg = jnp.where((c >= r * t) & (c < (r + 1) * t), 1.0 / t, 0.0).astype(BF16)
        kmean_scr[...] = jnp.dot(avg, k_ref[...], preferred_element_type=F32)
        _fill_vt(vt_scr, v_ref, rows)

    q = q_ref[...]
    qt_scr[:dh, :] = _transpose_bf16(q)
    sc = lax.dot_general(kmean_scr[...], q.astype(F32), _NT, precision=HIGHEST, preferred_element_type=F32)
    row = lax.broadcasted_iota(jnp.int32, (nblk, rows), 0)
    own = base + lax.broadcasted_iota(jnp.int32, (nblk, rows), 1) // t
    past = row < own
    sc = jnp.where(past, sc, -jnp.inf)
    sc_scr[...] = sc

    def rank_body(m, rank):
        sm = sc_scr[pl.ds(m, 1), :]
        return rank + jnp.where(sm > sc, 1.0, jnp.where(sm == sc, jnp.where(row > m, 1.0, 0.0), 0.0))

    rank = lax.fori_loop(0, base + g - 1, rank_body, jnp.zeros((nblk, rows), F32))
    pen = jnp.where(past, jnp.where(rank < MOBA_TOPK, 0.0, NEG), jnp.where(row == own, 0.0, NEG))
    void = jnp.where(lax.broadcasted_iota(jnp.int32, (SUBLANES, rows), 0) == SUBLANES - 1, NEG, 0.0)
    pen = jnp.concatenate([pen, jnp.zeros((LANES - SUBLANES - nblk, rows), F32), void], axis=0)
    qt_scr[dh:, :] = pen.astype(BF16)

    lane = lax.broadcasted_iota(jnp.int32, (t, LANES), 1)

    def kv(n, col):
        start = pl.multiple_of(n * t, t)
        onehot = jnp.where(lane == col, 1.0, 0.0).astype(BF16)
        return jnp.concatenate([k_ref[pl.ds(start, t), :], onehot], axis=1), vt_scr[:, pl.ds(start, t)]

    def logits(kb, j):
        return jnp.dot(kb, qt_scr[:, j * t:(j + 1) * t], preferred_element_type=F32)

    def far_steps(first, count):
        steps = []
        for u in range(count):
            kb, vb = kv(first + u, first + u)
            steps += [(j, functools.partial(logits, kb, j), None, vb) for j in range(g)]
        return steps

    carry = _dense_loop(jnp.maximum(base - (MOBA_FAR - 1), 0), far_steps,
                        tuple(_softmax_init(t, dh) for _ in range(g)))
    cfar = cfar_ref[pl.program_id(1)]

    def near_fix(delta, z, cj):
        if delta == MOBA_FAR - 1:
            m, acc = cj
            cj = (m + cfar, acc)
        return tab_ref[delta] + z, cj

    steps = []
    for k in range(-(MOBA_FAR - 1), g):
        n = base + k
        if k < 0:
            kb, vb = kv(jnp.maximum(n, 0), jnp.where(n >= 0, n, VOID_COL))
        else:
            kb, vb = kv(n, n)
        for j in range(max(k, 0), g):
            fix = functools.partial(near_fix, j - k) if j - k < MOBA_FAR else None
            steps.append((j, functools.partial(logits, kb, j), fix, vb))
    carry = _run_substeps(steps, carry)
    for j in range(g):
        o_ref[j * t:(j + 1) * t, :] = _softmax_finish(carry[j], dh).astype(o_ref.dtype)


def _moba(proj, tab):
    bsz, seq, _ = proj.shape
    g, t = ATT_G, ATT_T
    nblk = seq // t
    assert seq % (g * t) == 0 and nblk <= LANES - SUBLANES and nblk % SUBLANES == 0
    qc, kc, vc = COL_AQ // HEAD_DIM, COL_AK // HEAD_DIM, COL_AV // HEAD_DIM
    cfar = tab[:, MOBA_FAR, 0, 0]
    return pl.pallas_call(
        _moba_kernel,
        grid=(bsz, A_HEADS, nblk // g),
        in_specs=[
            pl.BlockSpec(memory_space=pltpu.SMEM),
            pl.BlockSpec((None, g * t, HEAD_DIM), lambda b, h, i: (b, i, qc + h)),
            pl.BlockSpec((None, seq, HEAD_DIM), lambda b, h, i: (b, 0, kc + h)),
            pl.BlockSpec((None, seq, HEAD_DIM), lambda b, h, i: (b, 0, vc + h)),
            pl.BlockSpec((None, MOBA_FAR, t, t), lambda b, h, i: (h, 0, 0, 0)),
        ],
        out_specs=pl.BlockSpec((None, g * t, HEAD_DIM), lambda b, h, i: (b, i, h)),
        out_shape=jax.ShapeDtypeStruct((bsz, seq, A_W), BF16),
        scratch_shapes=[pltpu.VMEM((nblk, HEAD_DIM), F32), pltpu.VMEM((nblk, g * t), F32),
                        pltpu.VMEM((HEAD_DIM + LANES, g * t), BF16), pltpu.VMEM((HEAD_DIM + ONES_ROWS, seq), BF16)],
        compiler_params=_params(("parallel", "parallel", "arbitrary")),
        name="moba",
    )(cfar, proj, proj, proj, tab)


def _fox_kernel(q_ref, qa_ref, k_ref, ka_ref, v_ref, o_ref, qt_scr, vt_scr):
    g, t = ATT_G, ATT_T
    rows = g * t
    seq, dh = k_ref.shape
    base = pl.program_id(2) * g

    @pl.when(base == 0)
    def _():
        _fill_vt(vt_scr, v_ref, rows)

    qt_scr[:dh, :] = _transpose_bf16(q_ref[...])
    qt_scr[dh:, :] = qa_ref[...]

    def kv(n):
        start = pl.multiple_of(n * t, t)
        return (jnp.concatenate([k_ref[pl.ds(start, t), :], ka_ref[pl.ds(start, t), :]], axis=1),
                vt_scr[:, pl.ds(start, t)])

    def logits(kb, j):
        return jnp.dot(kb, qt_scr[:, j * t:(j + 1) * t], preferred_element_type=F32)

    def dense_steps(first, count):
        steps = []
        for u in range(count):
            kb, vb = kv(first + u)
            steps += [(j, functools.partial(logits, kb, j), None, vb) for j in range(g)]
        return steps

    carry = _dense_loop(base, dense_steps, tuple(_softmax_init(t, dh) for _ in range(g)))
    causal = lax.broadcasted_iota(jnp.int32, (t, t), 0) <= lax.broadcasted_iota(jnp.int32, (t, t), 1)

    def diag_fix(z, cj):
        return jnp.where(causal, z, NEG), cj

    steps = []
    for k in range(g):
        kb, vb = kv(base + k)
        steps += [(j, functools.partial(logits, kb, j), diag_fix if j == k else None, vb) for j in range(k, g)]
    carry = _run_substeps(steps, carry)
    for j in range(g):
        o_ref[j * t:(j + 1) * t, :] = _softmax_finish(carry[j], dh).astype(o_ref.dtype)


def _fox(proj, qa, ka):
    bsz, seq, _ = proj.shape
    g, t = ATT_G, ATT_T
    assert seq % (g * t) == 0
    qc, kc, vc = COL_BQ // HEAD_DIM, COL_BK // HEAD_DIM, COL_BV // HEAD_DIM
    return pl.pallas_call(
        _fox_kernel,
        grid=(bsz, B_HEADS, seq // (g * t)),
        in_specs=[
            pl.BlockSpec((None, g * t, HEAD_DIM), lambda b, h, i: (b, i, qc + h)),
            pl.BlockSpec((None, None, LANES, g * t), lambda b, h, i: (b, h, 0, i)),
            pl.BlockSpec((None, seq, HEAD_DIM), lambda b, h, i: (b, 0, kc + h)),
            pl.BlockSpec((None, None, seq, LANES), lambda b, h, i: (b, h, 0, 0)),
            pl.BlockSpec((None, seq, HEAD_DIM), lambda b, h, i: (b, 0, vc + h)),
        ],
        out_specs=pl.BlockSpec((None, g * t, HEAD_DIM), lambda b, h, i: (b, i, h)),
        out_shape=jax.ShapeDtypeStruct((bsz, seq, B_W), BF16),
        scratch_shapes=[pltpu.VMEM((HEAD_DIM + LANES, g * t), BF16), pltpu.VMEM((HEAD_DIM + ONES_ROWS, seq), BF16)],
        compiler_params=_params(("parallel", "parallel", "arbitrary")),
        name="fox",
    )(proj, qa, proj, ka, proj)


SWA_BLOCKS = 4


def _swa_kernel(sink_ref, q_ref, kp_ref, kc_ref, vp_ref, vc_ref, tab_ref, o_ref):
    w = WINDOW
    dh = D_HEAD_DIM
    grp = D_Q_HEADS // D_KV_HEADS
    i = pl.program_id(1)
    kall = jnp.concatenate([kp_ref[...], kc_ref[...]], axis=0)
    vall = jnp.concatenate([vp_ref[...], vc_ref[...]], axis=0)
    col = lax.broadcasted_iota(jnp.int32, (w, 2 * w), 1)
    first = col >= jnp.where(i > 0, 0, w)
    nblk = q_ref.shape[0] // w

    def block_logits(blk):
        q = q_ref[blk * w:(blk + 1) * w, :]
        kcat = kall[blk * w:(blk + 2) * w, :]
        return [_dot_nt(q[:, h * dh:(h + 1) * dh], kcat[:, (h // grp) * dh:(h // grp + 1) * dh])
                for h in range(D_Q_HEADS)]

    logits = block_logits(0)
    for blk in range(nblk):
        nxt = block_logits(blk + 1) if blk + 1 < nblk else None
        vcat = vall[blk * w:(blk + 2) * w, :]
        outs = []
        for h in range(D_Q_HEADS):
            g = h // grp
            s = tab_ref[h] + logits[h]
            if blk == 0:
                s = jnp.where(first, s, NEG)
            sink = sink_ref[h]
            m = jnp.maximum(jnp.max(s, axis=-1, keepdims=True), sink)
            p = jnp.exp(s - m)
            den = jnp.sum(p, axis=-1, keepdims=True) + jnp.exp(sink - m)
            pv = jnp.dot(p.astype(BF16), vcat[:, g * dh:(g + 1) * dh], preferred_element_type=F32)
            outs.append(pv / den)
        o_ref[blk * w:(blk + 1) * w, :] = jnp.concatenate(outs, axis=1).astype(o_ref.dtype)
        logits = nxt


def _swa(proj, sinks, tab):
    bsz, seq, _ = proj.shape
    w = WINDOW
    nb = SWA_BLOCKS
    assert seq % (nb * w) == 0
    qc, kc, vc = COL_DQ // D_QW, COL_DK // D_KVW, COL_DV // D_KVW
    prev = lambda i: jnp.maximum(i * nb - 1, 0)
    return pl.pallas_call(
        _swa_kernel,
        grid=(bsz, seq // (nb * w)),
        in_specs=[
            pl.BlockSpec(memory_space=pltpu.SMEM),
            pl.BlockSpec((None, nb * w, D_QW), lambda b, i: (b, i, qc)),
            pl.BlockSpec((None, w, D_KVW), lambda b, i: (b, prev(i), kc)),
            pl.BlockSpec((None, nb * w, D_KVW), lambda b, i: (b, i, kc)),
            pl.BlockSpec((None, w, D_KVW), lambda b, i: (b, prev(i), vc)),
            pl.BlockSpec((None, nb * w, D_KVW), lambda b, i: (b, i, vc)),
            pl.BlockSpec((D_Q_HEADS, w, 2 * w), lambda b, i: (0, 0, 0)),
        ],
        out_specs=pl.BlockSpec((None, nb * w, D_QW), lambda b, i: (b, i, 0)),
        out_shape=jax.ShapeDtypeStruct((bsz, seq, D_QW), BF16),
        compiler_params=_params(("parallel", "parallel")),
        name="swa",
    )(sinks, proj, proj, proj, proj, proj, tab)


CONV_T = 512
CONV_HALO = 32
CONV_ROWS = 64


def _conv_kernel(a_ref, g_ref, ha_ref, hg_ref, w_ref, b_ref, lg_ref, lb_ref, o_ref, buf):
    t = a_ref.shape[0]
    halo = CONV_HALO
    hp = ha_ref[...].astype(F32) * jax.nn.sigmoid(hg_ref[...].astype(F32))
    buf[0:halo, :] = jnp.where(pl.program_id(1) > 0, hp, 0.0)
    buf[halo:halo + t, :] = a_ref[...].astype(F32) * jax.nn.sigmoid(g_ref[...].astype(F32))
    off = halo - (CONV_WIDTH - 1)
    for r in range(t // CONV_ROWS):
        acc = jnp.broadcast_to(b_ref[...], (CONV_ROWS, CONV_CH))
        for rho in range(SUBLANES):
            rows = CONV_ROWS if rho == 0 else CONV_ROWS + SUBLANES
            part = None
            for s in range(off, off + CONV_WIDTH):
                if s % SUBLANES != rho:
                    continue
                start = r * CONV_ROWS + s - rho
                term = w_ref[s - off:s - off + 1, :] * buf[start:start + rows, :]
                part = term if part is None else part + term
            acc = acc + part[rho:rho + CONV_ROWS, :]
        mu = jnp.mean(acc, axis=-1, keepdims=True)
        d = acc - mu
        var = jnp.mean(d * d, axis=-1, keepdims=True)
        y = d * lax.rsqrt(var + LN_EPS) * lg_ref[...] + lb_ref[...]
        o_ref[r * CONV_ROWS:(r + 1) * CONV_ROWS, :] = (y * jax.nn.sigmoid(y)).astype(o_ref.dtype)


def _conv(proj, conv_w, conv_b, ln_g, ln_b):
    bsz, seq, _ = proj.shape
    t = min(CONV_T, seq)
    ca, cg = COL_CU // CONV_CH, COL_CU // CONV_CH + 1
    hb = t // CONV_HALO
    prev = lambda i: jnp.maximum(i * hb - 1, 0)
    row = lambda v: v.reshape(1, CONV_CH)
    return pl.pallas_call(
        _conv_kernel,
        grid=(bsz, seq // t),
        in_specs=[
            pl.BlockSpec((None, t, CONV_CH), lambda b, i: (b, i, ca)),
            pl.BlockSpec((None, t, CONV_CH), lambda b, i: (b, i, cg)),
            pl.BlockSpec((None, CONV_HALO, CONV_CH), lambda b, i: (b, prev(i), ca)),
            pl.BlockSpec((None, CONV_HALO, CONV_CH), lambda b, i: (b, prev(i), cg)),
            pl.BlockSpec((CONV_WIDTH, CONV_CH), lambda b, i: (0, 0)),
            pl.BlockSpec((1, CONV_CH), lambda b, i: (0, 0)),
            pl.BlockSpec((1, CONV_CH), lambda b, i: (0, 0)),
            pl.BlockSpec((1, CONV_CH), lambda b, i: (0, 0)),
        ],
        out_specs=pl.BlockSpec((None, t, CONV_CH), lambda b, i: (b, i, 0)),
        out_shape=jax.ShapeDtypeStruct((bsz, seq, CONV_CH), BF16),
        scratch_shapes=[pltpu.VMEM((CONV_HALO + t, CONV_CH), F32)],
        compiler_params=_params(("parallel", "parallel")),
        name="conv",
    )(proj, proj, proj, proj, conv_w, row(conv_b), row(ln_g), row(ln_b))


def _merge_kernel(ya_ref, yb_ref, yc_ref, yd_ref, gates_ref, x_ref, gt_ref, gp_ref, gf_ref, scf_ref, shf_ref,
                  wa_ref, wb_ref, wc_ref, wd_ref, wo_ref, o_ref, h_ref):
    d = x_ref.shape[1]
    y = None
    for i, (br, w) in enumerate(((ya_ref, wa_ref), (yb_ref, wb_ref), (yc_ref, wc_ref), (yd_ref, wd_ref))):
        t = jnp.dot(br[...], w[...], preferred_element_type=F32) * gates_ref[:, i * d:(i + 1) * d].astype(F32)
        y = t if y is None else y + t
    z = jnp.dot(y.astype(BF16), wo_ref[...], preferred_element_type=F32)
    x_new = x_ref[...] + gt_ref[...] * (_rms(z) * gp_ref[...])
    o_ref[...] = x_new
    h_ref[...] = _modulated_norm(x_new, gf_ref[...], scf_ref[...], shf_ref[...]).astype(h_ref.dtype)


def _merge(ya, yb, yc, yd, gates, x, gt, g_post, g_ffn, sc_f, sh_f, wa, wb, wc, wd, wo, layer, *, tm):
    bsz, seq, d = x.shape
    tok = lambda width: pl.BlockSpec((None, tm, width), lambda b, i: (b, i, 0))
    vec = lambda: pl.BlockSpec((None, 1, d), lambda b, i: (b, 0, 0))
    par = lambda: pl.BlockSpec((1, d), lambda b, i: (0, 0))
    full = lambda a: pl.BlockSpec((None,) + a.shape[1:], lambda b, i: (layer, 0, 0),
                                  pipeline_mode=pl.Buffered(1))
    return pl.pallas_call(
        _merge_kernel,
        grid=(bsz, seq // tm),
        in_specs=[tok(A_W), tok(B_W), tok(CONV_CH), tok(D_QW), tok(N_BRANCHES * d), tok(d),
                  vec(), par(), par(), vec(), vec(),
                  full(wa), full(wb), full(wc), full(wd), full(wo)],
        out_specs=[tok(d), tok(d)],
        out_shape=[jax.ShapeDtypeStruct((bsz, seq, d), F32), jax.ShapeDtypeStruct((bsz, seq, d), BF16)],
        compiler_params=_params(("parallel", "parallel")),
        name="merge",
    )(ya, yb, yc, yd, gates, x, gt, g_post.reshape(1, d), g_ffn.reshape(1, d), sc_f, sh_f, wa, wb, wc, wd, wo)


def _ffn_kernel(x_ref, h_ref, gt_ref, gp_ref, wg_ref, wu_ref, wd_ref, o_ref, acc_scr):
    f = pl.program_id(2)

    @pl.when(f == 0)
    def _():
        acc_scr[...] = jnp.zeros_like(acc_scr)

    h = h_ref[...]
    g = jnp.dot(h, wg_ref[...], preferred_element_type=F32)
    u = jnp.dot(h, wu_ref[...], preferred_element_type=F32)
    a = ((g * jax.nn.sigmoid(g)) * u).astype(BF16)
    acc_scr[...] += jnp.dot(a, wd_ref[...], preferred_element_type=F32)

    @pl.when(f == pl.num_programs(2) - 1)
    def _():
        o_ref[...] = x_ref[...] + gt_ref[...] * (_rms(acc_scr[...]) * gp_ref[...])


def _ffn(x, h, gt, g_post, wg, wu, wd, layer, *, tm, tf):
    bsz, seq, d = x.shape
    dff = wg.shape[2]
    vec = lambda: pl.BlockSpec((None, 1, d), lambda b, i, f: (b, 0, 0))
    par = lambda: pl.BlockSpec((1, d), lambda b, i, f: (0, 0))
    return pl.pallas_call(
        _ffn_kernel,
        grid=(bsz, seq // tm, dff // tf),
        in_specs=[
            pl.BlockSpec((None, tm, d), lambda b, i, f: (b, i, 0)),
            pl.BlockSpec((None, tm, d), lambda b, i, f: (b, i, 0)),
            vec(), par(),
            pl.BlockSpec((None, d, tf), lambda b, i, f: (layer, 0, f)),
            pl.BlockSpec((None, d, tf), lambda b, i, f: (layer, 0, f)),
            pl.BlockSpec((None, tf, d), lambda b, i, f: (layer, f, 0)),
        ],
        out_specs=pl.BlockSpec((None, tm, d), lambda b, i, f: (b, i, 0)),
        out_shape=jax.ShapeDtypeStruct((bsz, seq, d), F32),
        scratch_shapes=[pltpu.VMEM((tm, d), F32)],
        compiler_params=_params(("parallel", "parallel", "arbitrary")),
        name="ffn",
    )(x, h, gt, g_post.reshape(1, d), wg, wu, wd)


SRC_QKV = 3 * A_W + 3 * B_W
SRC_REST = SRC_QKV + B_HEADS
SRC_GATE = SRC_REST + (PROJ_W - GATE_W - SRC_QKV)
IN_W = SRC_GATE + GATE_W
PACK_R = 256
PACK_SIDE_ROW = PROJ_W_PAD
PACK_ROWS = PROJ_W_PAD + PACK_R
assert GATE_W % PACK_R == 0 and SRC_QKV % PACK_R == 0 and PROJ_W % PACK_R == 0 and PROJ_W_PAD % PACK_R == 0


def _pack_tables():
    s2 = HEAD_DIM ** -0.5 * LOG2E
    src, scale, rows = [], [], []
    for b in range(PACK_ROWS // PACK_R):
        row = b * PACK_R
        if row < GATE_W:
            src.append(SRC_GATE + row); scale.append(1.0); rows.append(PACK_R)
        elif row < GATE_W + SRC_QKV:
            s = row - GATE_W
            is_q = s < A_W or 3 * A_W <= s < 3 * A_W + B_W
            src.append(s); scale.append(s2 if is_q else 1.0); rows.append(PACK_R)
        elif row < PROJ_W:
            s = row - GATE_W - SRC_QKV
            is_q = 2 * CONV_CH <= s < 2 * CONV_CH + D_QW
            src.append(SRC_REST + s); scale.append(D_HEAD_DIM ** -0.5 if is_q else 1.0); rows.append(PACK_R)
        elif row == PACK_SIDE_ROW:
            src.append(SRC_QKV); scale.append(1.0); rows.append(B_HEADS)
        else:
            src.append(0); scale.append(0.0); rows.append(0)
    return np.asarray(src, np.int32), np.asarray(scale, np.float32), np.asarray(rows, np.int32)


def _pack_kernel(src_ref, scale_ref, rows_ref, a_ref, o_ref):
    blk = pl.program_id(0)
    depth, _, d = o_ref.shape
    kt = d // LANES
    keep = lax.broadcasted_iota(jnp.int32, (PACK_R, d), 0) < rows_ref[blk]
    for layer in range(depth):
        cols = [a_ref[pl.ds(t * depth + layer, PACK_R, stride=kt * depth), :] for t in range(kt)]
        o_ref[layer] = jnp.where(keep, jnp.concatenate(cols, axis=1) * scale_ref[blk], 0.0).astype(o_ref.dtype)


def _pack_in_proj(w):
    depth, d, in_w = w.shape
    assert in_w == IN_W and d % LANES == 0
    kt = d // LANES
    view = jnp.transpose(w.reshape(depth, kt, LANES, in_w), (3, 1, 0, 2)).reshape(in_w * kt * depth, LANES)
    src, scale, rows = _pack_tables()
    per_feature = kt * depth
    return pl.pallas_call(
        _pack_kernel,
        grid_spec=pltpu.PrefetchScalarGridSpec(
            num_scalar_prefetch=3,
            grid=(PACK_ROWS // PACK_R,),
            in_specs=[pl.BlockSpec((pl.Element(PACK_R * per_feature), pl.Element(LANES)),
                                   lambda b, src, scale, rows: (src[b] * per_feature, 0))],
            out_specs=pl.BlockSpec((depth, PACK_R, d), lambda b, src, scale, rows: (0, b, 0)),
        ),
        out_shape=jax.ShapeDtypeStruct((depth, PACK_ROWS, d), BF16),
        compiler_params=_params(("parallel",)),
        name="pack_in_proj",
    )(jnp.asarray(src), jnp.asarray(scale), jnp.asarray(rows), view)


CAST_BLOCK_ELEMS = 1 << 20


def _cast_kernel(w_ref, o_ref):
    o_ref[...] = w_ref[...].astype(o_ref.dtype)


def _cast_bf16(w):
    depth, r, c = w.shape
    tr = 16
    while tr * 2 * c <= CAST_BLOCK_ELEMS and r % (tr * 2) == 0:
        tr *= 2
    assert r % tr == 0
    return pl.pallas_call(
        _cast_kernel,
        grid=(depth, r // tr),
        in_specs=[pl.BlockSpec((None, tr, c), lambda l, i: (l, i, 0))],
        out_specs=pl.BlockSpec((None, tr, c), lambda l, i: (l, i, 0)),
        out_shape=jax.ShapeDtypeStruct(w.shape, BF16),
        compiler_params=_params(("parallel", "parallel")),
        name="cast_bf16",
    )(w)


def kernel(x, c, rel_bias, w_mod, b_mod, mix_norm_pre, mix_norm_post, w_in, fox_bias, conv_w, conv_b, conv_ln_g, conv_ln_b, sinks, w_branch_a, w_branch_b, w_branch_c, w_branch_d, w_out, ffn_norm_pre, ffn_norm_post, w_ffn_gate, w_ffn_up, w_ffn_down):
    depth = w_mod.shape[0]
    bsz, seq, d = x.shape
    tm = min(1024, seq)
    tm_small = min(256, seq)
    tm_ffn = min(512, seq)

    mod = _modulation(c, w_mod, b_mod)
    tab_a = _bias_tables(rel_bias[:, :A_HEADS], _moba_buckets(), scale=LOG2E)
    tab_d = _bias_tables(rel_bias[:, A_HEADS:], _swa_buckets(), scale=1.0)[:, 0]

    wa, wb, wc, wd, wo = [_cast_bf16(w) for w in (w_branch_a, w_branch_b, w_branch_c, w_branch_d, w_out)]
    wg, wu, wdn = [_cast_bf16(w) for w in (w_ffn_gate, w_ffn_up, w_ffn_down)]

    w_proj = _pack_in_proj(w_in)

    for l in range(depth):
        sh_m, sc_m, gt_m, sh_f, sc_f, gt_f = [mod[l, :, None, i * d:(i + 1) * d] for i in range(6)]

        proj, fox_raw = _norm_proj(x, mix_norm_pre[l], sc_m, sh_m, w_proj, l, tm=tm, tn=PROJ_TN)
        fox_qa, fox_ka = _fox_gate(fox_raw, fox_bias[l], t=min(512, seq))

        ya = _moba(proj, tab_a)
        yb = _fox(proj, fox_qa, fox_ka)
        yc = _conv(proj, conv_w[l], conv_b[l], conv_ln_g[l], conv_ln_b[l])
        yd = _swa(proj, sinks[l], tab_d)

        x, h_ffn = _merge(ya, yb, yc, yd, proj, x, gt_m, mix_norm_post[l], ffn_norm_pre[l], sc_f, sh_f,
                          wa, wb, wc, wd, wo, l, tm=tm_small)
        x = _ffn(x, h_ffn, gt_f, ffn_norm_post[l], wg, wu, wdn, l, tm=tm_ffn, tf=512)
    return x
```

```python
import functools
import math

import jax
import jax.numpy as jnp
import numpy as np
from jax import lax
from jax.experimental import pallas as pl
from jax.experimental.pallas import tpu as pltpu

F32 = jnp.float32
BF16 = jnp.bfloat16
HIGHEST = lax.Precision.HIGHEST

HEAD_DIM = 128
A_HEADS = 4
MOBA_BLOCK = 256
MOBA_TOPK = 3
B_HEADS = 4
CONV_CH = 512
CONV_WIDTH = 31
D_Q_HEADS = 8
D_KV_HEADS = 2
D_HEAD_DIM = 64
WINDOW = 128
N_BUCKETS = 32
MAX_DISTANCE = 1024
N_BRANCHES = 4
RMS_EPS = 1e-6
LN_EPS = 1e-5

A_W = A_HEADS * HEAD_DIM
B_W = B_HEADS * HEAD_DIM
D_QW = D_Q_HEADS * D_HEAD_DIM
D_KVW = D_KV_HEADS * D_HEAD_DIM

LANES = 128
SUBLANES = 8
VMEM_LIMIT = 56 * 1024 * 1024

NEG = -1e30
LOG2E = math.log2(math.e)
ATT_G = 4
ATT_T = MOBA_BLOCK
ATT_U = 4

D_MODEL = 2048
GATE_W = N_BRANCHES * D_MODEL
COL_AQ, COL_AK, COL_AV = GATE_W, GATE_W + A_W, GATE_W + 2 * A_W
COL_BQ, COL_BK, COL_BV = COL_AV + A_W, COL_AV + A_W + B_W, COL_AV + A_W + 2 * B_W
COL_CU = COL_BV + B_W
COL_DQ = COL_CU + 2 * CONV_CH
COL_DK = COL_DQ + D_QW
COL_DV = COL_DK + D_KVW
PROJ_W = COL_DV + D_KVW
PROJ_TN = 1024
PROJ_W_PAD = -(-PROJ_W // PROJ_TN) * PROJ_TN
MOBA_FAR = -(-(MAX_DISTANCE + MOBA_BLOCK - 1) // MOBA_BLOCK)
VOID_COL = LANES - 1
assert ATT_G % ATT_U == 0 and (MOBA_FAR - 1) % ATT_U == 0


def _params(sem, vmem=VMEM_LIMIT):
    return pltpu.CompilerParams(dimension_semantics=sem, vmem_limit_bytes=vmem)


def _t5_bucket(dist):
    max_exact = N_BUCKETS // 2
    d = jnp.maximum(dist, 0)
    log_ratio = jnp.log(jnp.maximum(d, 1).astype(jnp.float32) / max_exact) / math.log(MAX_DISTANCE / max_exact)
    large = max_exact + (log_ratio * (N_BUCKETS - max_exact)).astype(jnp.int32)
    large = jnp.minimum(large, N_BUCKETS - 1)
    return jnp.where(d < max_exact, d, large)


def _sigmoid(x):
    return 0.5 * jnp.tanh(0.5 * x) + 0.5


def _rms(y):
    return y * lax.rsqrt(jnp.mean(y * y, axis=-1, keepdims=True) + RMS_EPS)


def _modulated_norm(x, g, sc, sh):
    return (_rms(x) * g) * (1.0 + sc) + sh


MOD_TN = 512
MOD_KC = 256


def _mod_kernel(ct_ref, w_ref, b_ref, o_ref):
    d, nb = ct_ref.shape
    tn = w_ref.shape[1]
    ct = ct_ref[...]
    ca = ct * jax.nn.sigmoid(ct)
    rows = []
    for r in range(nb):
        acc = jnp.zeros((1, tn), F32)
        for kc in range(d // MOD_KC):
            sl = slice(kc * MOD_KC, (kc + 1) * MOD_KC)
            acc = acc + jnp.sum(w_ref[sl, :] * ca[sl, r:r + 1], axis=0, keepdims=True)
        rows.append(acc)
    o_ref[...] = jnp.concatenate(rows, axis=0) + b_ref[...]


def _modulation(c, w_mod, b_mod):
    depth, d, n = w_mod.shape
    bsz = c.shape[0]
    return pl.pallas_call(
        _mod_kernel,
        grid=(depth, n // MOD_TN),
        in_specs=[
            pl.BlockSpec((d, bsz), lambda l, j: (0, 0)),
            pl.BlockSpec((None, d, MOD_TN), lambda l, j: (l, 0, j)),
            pl.BlockSpec((None, 1, MOD_TN), lambda l, j: (l, 0, j)),
        ],
        out_specs=pl.BlockSpec((None, bsz, MOD_TN), lambda l, j: (l, 0, j)),
        out_shape=jax.ShapeDtypeStruct((depth, bsz, n), F32),
        compiler_params=_params(("parallel", "parallel")),
        name="mod",
    )(c.T, w_mod, b_mod.reshape(depth, 1, n))


def _bias_table_kernel(rb_ref, bucket_ref, o_ref, *, scale):
    h = pl.program_id(0)
    b = bucket_ref[...]
    acc = jnp.where(b < 0, NEG, 0.0).astype(F32)
    for u in range(N_BUCKETS):
        acc = jnp.where(b == u, rb_ref[h, u] * scale, acc)
    o_ref[...] = acc


def _bias_tables(rel_bias_heads, bucket, *, scale):
    nh = rel_bias_heads.shape[1]
    nt, r, c = bucket.shape
    return pl.pallas_call(
        functools.partial(_bias_table_kernel, scale=scale),
        grid=(nh, nt),
        in_specs=[
            pl.BlockSpec(memory_space=pltpu.SMEM),
            pl.BlockSpec((None, r, c), lambda h, t: (t, 0, 0)),
        ],
        out_specs=pl.BlockSpec((None, None, r, c), lambda h, t: (h, t, 0, 0)),
        out_shape=jax.ShapeDtypeStruct((nh, nt, r, c), F32),
        compiler_params=_params(("parallel", "parallel")),
        name="bias_table",
    )(rel_bias_heads.T, bucket)


def _moba_buckets():
    blk = MOBA_BLOCK
    i = jnp.arange(blk)[None, :]
    j = jnp.arange(blk)[:, None]
    tabs = []
    for delta in range(MOBA_FAR):
        dist = delta * blk + i - j
        tabs.append(jnp.where(dist >= 0, _t5_bucket(dist), -1))
    tabs.append(_t5_bucket(jnp.full((blk, blk), MAX_DISTANCE, jnp.int32)))
    return jnp.stack(tabs).astype(jnp.int32)


def _swa_buckets():
    qi = jnp.arange(WINDOW)[None, :]
    kj = jnp.arange(2 * WINDOW)[:, None]
    dist = qi + WINDOW - kj
    in_win = (dist >= 0) & (dist < WINDOW)
    return jnp.where(in_win, _t5_bucket(dist), -1).astype(jnp.int32)[None]


def _norm_proj_kernel(x_ref, g_ref, sc_ref, sh_ref, w_ref, ws_ref, o_ref, os_ref, h_scr):
    j = pl.program_id(2)
    tn = w_ref.shape[0]

    @pl.when(j == 0)
    def _():
        h = _modulated_norm(x_ref[...], g_ref[...], sc_ref[...], sh_ref[...]).astype(BF16)
        h_scr[...] = h
        os_ref[...] = _dot_nt(h, ws_ref[...])

    @pl.when(j < GATE_W // tn)
    def _():
        o_ref[...] = _sigmoid(_dot_nt(h_scr[...], w_ref[...])).astype(o_ref.dtype)

    @pl.when(j >= GATE_W // tn)
    def _():
        o_ref[...] = _dot_nt(h_scr[...], w_ref[...]).astype(o_ref.dtype)


def _norm_proj(x, g, sc, sh, w, layer, *, tm, tn):
    bsz, seq, d = x.shape
    n = PROJ_W_PAD
    ns = LANES
    assert GATE_W % tn == 0 and n % tn == 0 and w.shape[1] == PACK_ROWS
    return pl.pallas_call(
        _norm_proj_kernel,
        grid=(bsz, seq // tm, n // tn),
        in_specs=[
            pl.BlockSpec((None, tm, d), lambda b, i, j: (b, i, 0)),
            pl.BlockSpec((1, d), lambda b, i, j: (0, 0)),
            pl.BlockSpec((None, 1, d), lambda b, i, j: (b, 0, 0)),
            pl.BlockSpec((None, 1, d), lambda b, i, j: (b, 0, 0)),
            pl.BlockSpec((None, tn, d), lambda b, i, j: (layer, j, 0)),
            pl.BlockSpec((None, ns, d), lambda b, i, j: (layer, PACK_SIDE_ROW // ns, 0)),
        ],
        out_specs=[pl.BlockSpec((None, tm, tn), lambda b, i, j: (b, i, j)),
                   pl.BlockSpec((None, tm, ns), lambda b, i, j: (b, i, 0))],
        out_shape=[jax.ShapeDtypeStruct((bsz, seq, n), BF16), jax.ShapeDtypeStruct((bsz, seq, ns), F32)],
        scratch_shapes=[pltpu.VMEM((tm, d), BF16)],
        compiler_params=_params(("parallel", "parallel", "arbitrary")),
        name="norm_proj",
    )(x, g.reshape(1, d), sc, sh, w, w)


def _split3(c):
    hi = c.astype(BF16).astype(F32)
    mid = (c - hi).astype(BF16).astype(F32)
    lo = (c - hi - mid).astype(BF16).astype(F32)
    return hi, mid, lo


def _fox_gate_kernel(x_ref, fb_ref, qa_ref, ka_ref, carry_scr):
    t = x_ref.shape[0]

    @pl.when(pl.program_id(1) == 0)
    def _():
        carry_scr[...] = jnp.zeros_like(carry_scr)

    lf = jax.nn.log_sigmoid(x_ref[...] + fb_ref[...])
    tri = (lax.broadcasted_iota(jnp.int32, (t, t), 0) >= lax.broadcasted_iota(jnp.int32, (t, t), 1)).astype(F32)
    c = jnp.dot(tri, lf, precision=HIGHEST, preferred_element_type=F32) + carry_scr[...]
    carry_scr[...] = c[t - 1:t, :]
    lane = lax.broadcasted_iota(jnp.int32, (t, LANES), 1)
    sub = lax.broadcasted_iota(jnp.int32, (SUBLANES, t), 0)
    ct = c.T * LOG2E
    for h in range(B_HEADS):
        hi, mid, lo = _split3(jnp.broadcast_to(c[:, h:h + 1], (t, LANES)) * LOG2E)
        ka = jnp.where(lane == 0, -hi, jnp.where(lane == 1, -mid, jnp.where(lane == 2, -lo, jnp.where(lane < 6, 1.0, 0.0))))
        ka_ref[h] = ka.astype(BF16)
        hi, mid, lo = _split3(jnp.broadcast_to(ct[h:h + 1, :], (SUBLANES, t)))
        qa = jnp.where(sub < 3, 1.0, jnp.where(sub == 3, hi, jnp.where(sub == 4, mid, jnp.where(sub == 5, lo, 0.0))))
        qa_ref[h] = jnp.concatenate([qa, jnp.zeros((LANES - SUBLANES, t), F32)], axis=0).astype(BF16)


def _fox_gate(raw, fox_b, *, t):
    bsz, seq, _ = raw.shape
    fb = jnp.zeros((1, LANES), F32).at[0, :B_HEADS].set(fox_b)
    aug = pl.BlockSpec((None, B_HEADS, t, LANES), lambda b, i: (b, 0, i, 0))
    aug_t = pl.BlockSpec((None, B_HEADS, LANES, t), lambda b, i: (b, 0, 0, i))
    return pl.pallas_call(
        _fox_gate_kernel,
        grid=(bsz, seq // t),
        in_specs=[
            pl.BlockSpec((None, t, LANES), lambda b, i: (b, i, 0)),
            pl.BlockSpec((1, LANES), lambda b, i: (0, 0)),
        ],
        out_specs=[aug_t, aug],
        out_shape=[jax.ShapeDtypeStruct((bsz, B_HEADS, LANES, seq), BF16),
                   jax.ShapeDtypeStruct((bsz, B_HEADS, seq, LANES), BF16)],
        scratch_shapes=[pltpu.VMEM((1, LANES), F32)],
        compiler_params=_params(("parallel", "arbitrary")),
        name="fox_gate",
    )(raw, fb)


ONES_ROWS = 16


def _softmax_step(zt, vt, carry):
    m, acc = carry
    m_new = jnp.maximum(m, jnp.max(zt, axis=0, keepdims=True))
    p = jnp.exp2(zt - m_new)
    acc = jnp.exp2(m - m_new) * acc + jnp.dot(vt, p.astype(BF16), preferred_element_type=F32)
    return m_new, acc


def _softmax_init(tq, dh):
    return (jnp.full((1, tq), NEG, F32), jnp.zeros((dh + ONES_ROWS, tq), F32))


def _softmax_finish(carry, dh):
    _, acc = carry
    return (acc[:dh, :] / acc[dh:dh + 1, :]).T


def _fill_vt(vt_scr, v_ref, chunk):
    seq, dh = v_ref.shape
    for cix in range(seq // chunk):
        sl = slice(cix * chunk, (cix + 1) * chunk)
        vt_scr[:dh, sl] = _transpose_bf16(v_ref[sl, :])
    vt_scr[dh:, :] = jnp.ones((ONES_ROWS, seq), BF16)


ATT_LOOKAHEAD = 6


def _run_substeps(steps, carry):
    carry = list(carry)
    pending = {}
    for s in range(min(ATT_LOOKAHEAD, len(steps))):
        pending[s] = steps[s][1]()
    for s, (j, _, fix, vt) in enumerate(steps):
        if s + ATT_LOOKAHEAD < len(steps):
            pending[s + ATT_LOOKAHEAD] = steps[s + ATT_LOOKAHEAD][1]()
        z, cj = pending.pop(s), carry[j]
        if fix is not None:
            z, cj = fix(z, cj)
        carry[j] = _softmax_step(z, vt, cj)
    return carry


def _dense_loop(n_blocks, make_steps, carry):
    big = 2 * ATT_U
    carry = lax.fori_loop(0, n_blocks // big,
                          lambda grp, c: tuple(_run_substeps(make_steps(grp * big, big), c)), carry)
    first = (n_blocks // big) * big
    return lax.fori_loop(0, (n_blocks - first) // ATT_U,
                         lambda grp, c: tuple(_run_substeps(make_steps(first + grp * ATT_U, ATT_U), c)), carry)


def _transpose_bf16(x):
    return x.astype(F32).T.astype(BF16)


_NT = (((1,), (1,)), ((), ()))


def _dot_nt(a, b):
    return lax.dot_general(a, b, _NT, preferred_element_type=F32)


def _moba_kernel(cfar_ref, q_ref, k_ref, v_ref, tab_ref, o_ref, kmean_scr, sc_scr, qt_scr, vt_scr):
    g, t = ATT_G, ATT_T
    seq, dh = k_ref.shape
    nblk = seq // t
    rows = g * t
    i = pl.program_id(2)
    base = i * g

    @pl.when(i == 0)
    def _():
        r = lax.broadcasted_iota(jnp.int32, (nblk, seq), 0)
        c = lax.broadcasted_iota(jnp.int32, (nblk, seq), 1)
        avg = jnp.where((c >= r * t) & (c < (r + 1) * t), 1.0 / t, 0.0).astype(BF16)
        kmean_scr[...] = jnp.dot(avg, k_ref[...], preferred_element_type=F32)
        _fill_vt(vt_scr, v_ref, rows)

    q = q_ref[...]
    qt_scr[:dh, :] = _transpose_bf16(q)
    sc = lax.dot_general(kmean_scr[...], q.astype(F32), _NT, precision=HIGHEST, preferred_element_type=F32)
    row = lax.broadcasted_iota(jnp.int32, (nblk, rows), 0)
    own = base + lax.broadcasted_iota(jnp.int32, (nblk, rows), 1) // t
    past = row < own
    sc = jnp.where(past, sc, -jnp.inf)
    sc_scr[...] = sc

    def rank_body(m, rank):
        sm = sc_scr[pl.ds(m, 1), :]
        return rank + jnp.where(sm > sc, 1.0, jnp.where(sm == sc, jnp.where(row > m, 1.0, 0.0), 0.0))

    rank = lax.fori_loop(0, base + g - 1, rank_body, jnp.zeros((nblk, rows), F32))
    pen = jnp.where(past, jnp.where(rank < MOBA_TOPK, 0.0, NEG), jnp.where(row == own, 0.0, NEG))
    void = jnp.where(lax.broadcasted_iota(jnp.int32, (SUBLANES, rows), 0) == SUBLANES - 1, NEG, 0.0)
    pen = jnp.concatenate([pen, jnp.zeros((LANES - SUBLANES - nblk, rows), F32), void], axis=0)
    qt_scr[dh:, :] = pen.astype(BF16)

    lane = lax.broadcasted_iota(jnp.int32, (t, LANES), 1)

    def kv(n, col):
        start = pl.multiple_of(n * t, t)
        onehot = jnp.where(lane == col, 1.0, 0.0).astype(BF16)
        return jnp.concatenate([k_ref[pl.ds(start, t), :], onehot], axis=1), vt_scr[:, pl.ds(start, t)]

    def logits(kb, j):
        return jnp.dot(kb, qt_scr[:, j * t:(j + 1) * t], preferred_element_type=F32)

    def far_steps(first, count):
        steps = []
        for u in range(count):
            kb, vb = kv(first + u, first + u)
            steps += [(j, functools.partial(logits, kb, j), None, vb) for j in range(g)]
        return steps

    carry = _dense_loop(jnp.maximum(base - (MOBA_FAR - 1), 0), far_steps,
                        tuple(_softmax_init(t, dh) for _ in range(g)))
    cfar = cfar_ref[pl.program_id(1)]

    def near_fix(delta, z, cj):
        if delta == MOBA_FAR - 1:
            m, acc = cj
            cj = (m + cfar, acc)
        return tab_ref[delta] + z, cj

    steps = []
    for k in range(-(MOBA_FAR - 1), g):
        n = base + k
        if k < 0:
            kb, vb = kv(jnp.maximum(n, 0), jnp.where(n >= 0, n, VOID_COL))
        else:
            kb, vb = kv(n, n)
        for j in range(max(k, 0), g):
            fix = functools.partial(near_fix, j - k) if j - k < MOBA_FAR else None
            steps.append((j, functools.partial(logits, kb, j), fix, vb))
    carry = _run_substeps(steps, carry)
    for j in range(g):
        o_ref[j * t:(j + 1) * t, :] = _softmax_finish(carry[j], dh).astype(o_ref.dtype)


def _moba(proj, tab):
    bsz, seq, _ = proj.shape
    g, t = ATT_G, ATT_T
    nblk = seq // t
    assert seq % (g * t) == 0 and nblk <= LANES - SUBLANES and nblk % SUBLANES == 0
    qc, kc, vc = COL_AQ // HEAD_DIM, COL_AK // HEAD_DIM, COL_AV // HEAD_DIM
    cfar = tab[:, MOBA_FAR, 0, 0]
    return pl.pallas_call(
        _moba_kernel,
        grid=(bsz, A_HEADS, nblk // g),
        in_specs=[
            pl.BlockSpec(memory_space=pltpu.SMEM),
            pl.BlockSpec((None, g * t, HEAD_DIM), lambda b, h, i: (b, i, qc + h)),
            pl.BlockSpec((None, seq, HEAD_DIM), lambda b, h, i: (b, 0, kc + h)),
            pl.BlockSpec((None, seq, HEAD_DIM), lambda b, h, i: (b, 0, vc + h)),
            pl.BlockSpec((None, MOBA_FAR, t, t), lambda b, h, i: (h, 0, 0, 0)),
        ],
        out_specs=pl.BlockSpec((None, g * t, HEAD_DIM), lambda b, h, i: (b, i, h)),
        out_shape=jax.ShapeDtypeStruct((bsz, seq, A_W), BF16),
        scratch_shapes=[pltpu.VMEM((nblk, HEAD_DIM), F32), pltpu.VMEM((nblk, g * t), F32),
                        pltpu.VMEM((HEAD_DIM + LANES, g * t), BF16), pltpu.VMEM((HEAD_DIM + ONES_ROWS, seq), BF16)],
        compiler_params=_params(("parallel", "parallel", "arbitrary")),
        name="moba",
    )(cfar, proj, proj, proj, tab)


def _fox_kernel(q_ref, qa_ref, k_ref, ka_ref, v_ref, o_ref, qt_scr, vt_scr):
    g, t = ATT_G, ATT_T
    rows = g * t
    seq, dh = k_ref.shape
    base = pl.program_id(2) * g

    @pl.when(base == 0)
    def _():
        _fill_vt(vt_scr, v_ref, rows)

    qt_scr[:dh, :] = _transpose_bf16(q_ref[...])
    qt_scr[dh:, :] = qa_ref[...]

    def kv(n):
        start = pl.multiple_of(n * t, t)
        return (jnp.concatenate([k_ref[pl.ds(start, t), :], ka_ref[pl.ds(start, t), :]], axis=1),
                vt_scr[:, pl.ds(start, t)])

    def logits(kb, j):
        return jnp.dot(kb, qt_scr[:, j * t:(j + 1) * t], preferred_element_type=F32)

    def dense_steps(first, count):
        steps = []
        for u in range(count):
            kb, vb = kv(first + u)
            steps += [(j, functools.partial(logits, kb, j), None, vb) for j in range(g)]
        return steps

    carry = _dense_loop(base, dense_steps, tuple(_softmax_init(t, dh) for _ in range(g)))
    causal = lax.broadcasted_iota(jnp.int32, (t, t), 0) <= lax.broadcasted_iota(jnp.int32, (t, t), 1)

    def diag_fix(z, cj):
        return jnp.where(causal, z, NEG), cj

    steps = []
    for k in range(g):
        kb, vb = kv(base + k)
        steps += [(j, functools.partial(logits, kb, j), diag_fix if j == k else None, vb) for j in range(k, g)]
    carry = _run_substeps(steps, carry)
    for j in range(g):
        o_ref[j * t:(j + 1) * t, :] = _softmax_finish(carry[j], dh).astype(o_ref.dtype)


def _fox(proj, qa, ka):
    bsz, seq, _ = proj.shape
    g, t = ATT_G, ATT_T
    assert seq % (g * t) == 0
    qc, kc, vc = COL_BQ // HEAD_DIM, COL_BK // HEAD_DIM, COL_BV // HEAD_DIM
    return pl.pallas_call(
        _fox_kernel,
        grid=(bsz, B_HEADS, seq // (g * t)),
        in_specs=[
            pl.BlockSpec((None, g * t, HEAD_DIM), lambda b, h, i: (b, i, qc + h)),
            pl.BlockSpec((None, None, LANES, g * t), lambda b, h, i: (b, h, 0, i)),
            pl.BlockSpec((None, seq, HEAD_DIM), lambda b, h, i: (b, 0, kc + h)),
            pl.BlockSpec((None, None, seq, LANES), lambda b, h, i: (b, h, 0, 0)),
            pl.BlockSpec((None, seq, HEAD_DIM), lambda b, h, i: (b, 0, vc + h)),
        ],
        out_specs=pl.BlockSpec((None, g * t, HEAD_DIM), lambda b, h, i: (b, i, h)),
        out_shape=jax.ShapeDtypeStruct((bsz, seq, B_W), BF16),
        scratch_shapes=[pltpu.VMEM((HEAD_DIM + LANES, g * t), BF16), pltpu.VMEM((HEAD_DIM + ONES_ROWS, seq), BF16)],
        compiler_params=_params(("parallel", "parallel", "arbitrary")),
        name="fox",
    )(proj, qa, proj, ka, proj)


SWA_BLOCKS = 4


def _swa_kernel(sink_ref, q_ref, kp_ref, kc_ref, vp_ref, vc_ref, tab_ref, o_ref):
    w = WINDOW
    dh = D_HEAD_DIM
    grp = D_Q_HEADS // D_KV_HEADS
    i = pl.program_id(1)
    nblk = q_ref.shape[0] // w
    span = (nblk + 1) * w
    kall = jnp.concatenate([kp_ref[...], kc_ref[...]], axis=0)
    vt = jnp.concatenate([_transpose_bf16(vp_ref[...]), _transpose_bf16(vc_ref[...])], axis=1)
    ones = jnp.ones((ONES_ROWS, span), BF16)
    kg = [kall[:, g * dh:(g + 1) * dh] for g in range(D_KV_HEADS)]
    vg = [jnp.concatenate([vt[g * dh:(g + 1) * dh, :], ones], axis=0) for g in range(D_KV_HEADS)]
    qt = _transpose_bf16(q_ref[...])
    key = lax.broadcasted_iota(jnp.int32, (2 * w, w), 0)
    first = key >= jnp.where(i > 0, 0, w)

    def block_logits(blk):
        return [jnp.dot(kg[h // grp][blk * w:(blk + 2) * w, :], qt[h * dh:(h + 1) * dh, blk * w:(blk + 1) * w],
                        preferred_element_type=F32) for h in range(D_Q_HEADS)]

    logits = block_logits(0)
    for blk in range(nblk):
        nxt = block_logits(blk + 1) if blk + 1 < nblk else None
        outs = []
        for h in range(D_Q_HEADS):
            s = tab_ref[h] + logits[h]
            if blk == 0:
                s = jnp.where(first, s, NEG)
            sink = sink_ref[h]
            m = jnp.maximum(jnp.max(s, axis=0, keepdims=True), sink)
            p = jnp.exp(s - m)
            acc = jnp.dot(vg[h // grp][:, blk * w:(blk + 2) * w], p.astype(BF16), preferred_element_type=F32)
            den = acc[dh:dh + 1, :] + jnp.exp(sink - m)
            outs.append(acc[:dh, :] / den)
        o_ref[blk * w:(blk + 1) * w, :] = jnp.concatenate(outs, axis=0).T.astype(o_ref.dtype)
        logits = nxt


def _swa(proj, sinks, tab):
    bsz, seq, _ = proj.shape
    w = WINDOW
    nb = SWA_BLOCKS
    assert seq % (nb * w) == 0
    qc, kc, vc = COL_DQ // D_QW, COL_DK // D_KVW, COL_DV // D_KVW
    prev = lambda i: jnp.maximum(i * nb - 1, 0)
    return pl.pallas_call(
        _swa_kernel,
        grid=(bsz, seq // (nb * w)),
        in_specs=[
            pl.BlockSpec(memory_space=pltpu.SMEM),
            pl.BlockSpec((None, nb * w, D_QW), lambda b, i: (b, i, qc)),
            pl.BlockSpec((None, w, D_KVW), lambda b, i: (b, prev(i), kc)),
            pl.BlockSpec((None, nb * w, D_KVW), lambda b, i: (b, i, kc)),
            pl.BlockSpec((None, w, D_KVW), lambda b, i: (b, prev(i), vc)),
            pl.BlockSpec((None, nb * w, D_KVW), lambda b, i: (b, i, vc)),
            pl.BlockSpec((D_Q_HEADS, 2 * w, w), lambda b, i: (0, 0, 0)),
        ],
        out_specs=pl.BlockSpec((None, nb * w, D_QW), lambda b, i: (b, i, 0)),
        out_shape=jax.ShapeDtypeStruct((bsz, seq, D_QW), BF16),
        compiler_params=_params(("parallel", "parallel")),
        name="swa",
    )(sinks, proj, proj, proj, proj, proj, tab)


CONV_T = 512
CONV_HALO = 32
CONV_ROWS = 64


def _conv_kernel(a_ref, g_ref, ha_ref, hg_ref, w_ref, b_ref, lg_ref, lb_ref, o_ref, buf):
    t = a_ref.shape[0]
    halo = CONV_HALO
    hp = ha_ref[...].astype(F32) * jax.nn.sigmoid(hg_ref[...].astype(F32))
    buf[0:halo, :] = jnp.where(pl.program_id(1) > 0, hp, 0.0)
    buf[halo:halo + t, :] = a_ref[...].astype(F32) * jax.nn.sigmoid(g_ref[...].astype(F32))
    off = halo - (CONV_WIDTH - 1)
    for r in range(t // CONV_ROWS):
        acc = jnp.broadcast_to(b_ref[...], (CONV_ROWS, CONV_CH))
        for rho in range(SUBLANES):
            rows = CONV_ROWS if rho == 0 else CONV_ROWS + SUBLANES
            part = None
            for s in range(off, off + CONV_WIDTH):
                if s % SUBLANES != rho:
                    continue
                start = r * CONV_ROWS + s - rho
                term = w_ref[s - off:s - off + 1, :] * buf[start:start + rows, :]
                part = term if part is None else part + term
            acc = acc + part[rho:rho + CONV_ROWS, :]
        mu = jnp.mean(acc, axis=-1, keepdims=True)
        d = acc - mu
        var = jnp.mean(d * d, axis=-1, keepdims=True)
        y = d * lax.rsqrt(var + LN_EPS) * lg_ref[...] + lb_ref[...]
        o_ref[r * CONV_ROWS:(r + 1) * CONV_ROWS, :] = (y * jax.nn.sigmoid(y)).astype(o_ref.dtype)


def _conv(proj, conv_w, conv_b, ln_g, ln_b):
    bsz, seq, _ = proj.shape
    t = min(CONV_T, seq)
    ca, cg = COL_CU // CONV_CH, COL_CU // CONV_CH + 1
    hb = t // CONV_HALO
    prev = lambda i: jnp.maximum(i * hb - 1, 0)
    row = lambda v: v.reshape(1, CONV_CH)
    return pl.pallas_call(
        _conv_kernel,
        grid=(bsz, seq // t),
        in_specs=[
            pl.BlockSpec((None, t, CONV_CH), lambda b, i: (b, i, ca)),
            pl.BlockSpec((None, t, CONV_CH), lambda b, i: (b, i, cg)),
            pl.BlockSpec((None, CONV_HALO, CONV_CH), lambda b, i: (b, prev(i), ca)),
            pl.BlockSpec((None, CONV_HALO, CONV_CH), lambda b, i: (b, prev(i), cg)),
            pl.BlockSpec((CONV_WIDTH, CONV_CH), lambda b, i: (0, 0)),
            pl.BlockSpec((1, CONV_CH), lambda b, i: (0, 0)),
            pl.BlockSpec((1, CONV_CH), lambda b, i: (0, 0)),
            pl.BlockSpec((1, CONV_CH), lambda b, i: (0, 0)),
        ],
        out_specs=pl.BlockSpec((None, t, CONV_CH), lambda b, i: (b, i, 0)),
        out_shape=jax.ShapeDtypeStruct((bsz, seq, CONV_CH), BF16),
        scratch_shapes=[pltpu.VMEM((CONV_HALO + t, CONV_CH), F32)],
        compiler_params=_params(("parallel", "parallel")),
        name="conv",
    )(proj, proj, proj, proj, conv_w, row(conv_b), row(ln_g), row(ln_b))


def _merge_kernel(ya_ref, yb_ref, yc_ref, yd_ref, gates_ref, x_ref, gt_ref, gp_ref, gf_ref, scf_ref, shf_ref,
                  wa_ref, wb_ref, wc_ref, wd_ref, wo_ref, o_ref, h_ref):
    d = x_ref.shape[1]
    y = None
    for i, (br, w) in enumerate(((ya_ref, wa_ref), (yb_ref, wb_ref), (yc_ref, wc_ref), (yd_ref, wd_ref))):
        t = jnp.dot(br[...], w[...], preferred_element_type=F32) * gates_ref[:, i * d:(i + 1) * d].astype(F32)
        y = t if y is None else y + t
    z = jnp.dot(y.astype(BF16), wo_ref[...], preferred_element_type=F32)
    x_new = x_ref[...] + gt_ref[...] * (_rms(z) * gp_ref[...])
    o_ref[...] = x_new
    h_ref[...] = _modulated_norm(x_new, gf_ref[...], scf_ref[...], shf_ref[...]).astype(h_ref.dtype)


def _merge(ya, yb, yc, yd, gates, x, gt, g_post, g_ffn, sc_f, sh_f, wa, wb, wc, wd, wo, layer, *, tm):
    bsz, seq, d = x.shape
    tok = lambda width: pl.BlockSpec((None, tm, width), lambda b, i: (b, i, 0))
    vec = lambda: pl.BlockSpec((None, 1, d), lambda b, i: (b, 0, 0))
    par = lambda: pl.BlockSpec((1, d), lambda b, i: (0, 0))
    full = lambda a: pl.BlockSpec((None,) + a.shape[1:], lambda b, i: (layer, 0, 0),
                                  pipeline_mode=pl.Buffered(1))
    return pl.pallas_call(
        _merge_kernel,
        grid=(bsz, seq // tm),
        in_specs=[tok(A_W), tok(B_W), tok(CONV_CH), tok(D_QW), tok(N_BRANCHES * d), tok(d),
                  vec(), par(), par(), vec(), vec(),
                  full(wa), full(wb), full(wc), full(wd), full(wo)],
        out_specs=[tok(d), tok(d)],
        out_shape=[jax.ShapeDtypeStruct((bsz, seq, d), F32), jax.ShapeDtypeStruct((bsz, seq, d), BF16)],
        compiler_params=_params(("parallel", "parallel")),
        name="merge",
    )(ya, yb, yc, yd, gates, x, gt, g_post.reshape(1, d), g_ffn.reshape(1, d), sc_f, sh_f, wa, wb, wc, wd, wo)


def _ffn_kernel(x_ref, h_ref, gt_ref, gp_ref, wg_ref, wu_ref, wd_ref, o_ref, acc_scr):
    f = pl.program_id(2)

    @pl.when(f == 0)
    def _():
        acc_scr[...] = jnp.zeros_like(acc_scr)

    h = h_ref[...]
    g = jnp.dot(h, wg_ref[...], preferred_element_type=F32)
    u = jnp.dot(h, wu_ref[...], preferred_element_type=F32)
    a = ((g * jax.nn.sigmoid(g)) * u).astype(BF16)
    acc_scr[...] += jnp.dot(a, wd_ref[...], preferred_element_type=F32)

    @pl.when(f == pl.num_programs(2) - 1)
    def _():
        o_ref[...] = x_ref[...] + gt_ref[...] * (_rms(acc_scr[...]) * gp_ref[...])


def _ffn(x, h, gt, g_post, wg, wu, wd, layer, *, tm, tf):
    bsz, seq, d = x.shape
    dff = wg.shape[2]
    vec = lambda: pl.BlockSpec((None, 1, d), lambda b, i, f: (b, 0, 0))
    par = lambda: pl.BlockSpec((1, d), lambda b, i, f: (0, 0))
    return pl.pallas_call(
        _ffn_kernel,
        grid=(bsz, seq // tm, dff // tf),
        in_specs=[
            pl.BlockSpec((None, tm, d), lambda b, i, f: (b, i, 0)),
            pl.BlockSpec((None, tm, d), lambda b, i, f: (b, i, 0)),
            vec(), par(),
            pl.BlockSpec((None, d, tf), lambda b, i, f: (layer, 0, f)),
            pl.BlockSpec((None, d, tf), lambda b, i, f: (layer, 0, f)),
            pl.BlockSpec((None, tf, d), lambda b, i, f: (layer, f, 0)),
        ],
        out_specs=pl.BlockSpec((None, tm, d), lambda b, i, f: (b, i, 0)),
        out_shape=jax.ShapeDtypeStruct((bsz, seq, d), F32),
        scratch_shapes=[pltpu.VMEM((tm, d), F32)],
        compiler_params=_params(("parallel", "parallel", "arbitrary")),
        name="ffn",
    )(x, h, gt, g_post.reshape(1, d), wg, wu, wd)


SRC_QKV = 3 * A_W + 3 * B_W
SRC_REST = SRC_QKV + B_HEADS
SRC_GATE = SRC_REST + (PROJ_W - GATE_W - SRC_QKV)
IN_W = SRC_GATE + GATE_W
PACK_R = 256
PACK_SIDE_ROW = PROJ_W_PAD
PACK_ROWS = PROJ_W_PAD + PACK_R
assert GATE_W % PACK_R == 0 and SRC_QKV % PACK_R == 0 and PROJ_W % PACK_R == 0 and PROJ_W_PAD % PACK_R == 0


def _pack_tables():
    s2 = HEAD_DIM ** -0.5 * LOG2E
    src, scale, rows = [], [], []
    for b in range(PACK_ROWS // PACK_R):
        row = b * PACK_R
        if row < GATE_W:
            src.append(SRC_GATE + row); scale.append(1.0); rows.append(PACK_R)
        elif row < GATE_W + SRC_QKV:
            s = row - GATE_W
            is_q = s < A_W or 3 * A_W <= s < 3 * A_W + B_W
            src.append(s); scale.append(s2 if is_q else 1.0); rows.append(PACK_R)
        elif row < PROJ_W:
            s = row - GATE_W - SRC_QKV
            is_q = 2 * CONV_CH <= s < 2 * CONV_CH + D_QW
            src.append(SRC_REST + s); scale.append(D_HEAD_DIM ** -0.5 if is_q else 1.0); rows.append(PACK_R)
        elif row == PACK_SIDE_ROW:
            src.append(SRC_QKV); scale.append(1.0); rows.append(B_HEADS)
        else:
            src.append(0); scale.append(0.0); rows.append(0)
    return np.asarray(src, np.int32), np.asarray(scale, np.float32), np.asarray(rows, np.int32)


def _pack_kernel(src_ref, scale_ref, rows_ref, a_ref, o_ref):
    blk = pl.program_id(0)
    depth, _, d = o_ref.shape
    kt = d // LANES
    keep = lax.broadcasted_iota(jnp.int32, (PACK_R, d), 0) < rows_ref[blk]
    for layer in range(depth):
        cols = [a_ref[pl.ds(t * depth + layer, PACK_R, stride=kt * depth), :] for t in range(kt)]
        o_ref[layer] = jnp.where(keep, jnp.concatenate(cols, axis=1) * scale_ref[blk], 0.0).astype(o_ref.dtype)


def _pack_in_proj(w):
    depth, d, in_w = w.shape
    assert in_w == IN_W and d % LANES == 0
    kt = d // LANES
    view = jnp.transpose(w.reshape(depth, kt, LANES, in_w), (3, 1, 0, 2)).reshape(in_w * kt * depth, LANES)
    src, scale, rows = _pack_tables()
    per_feature = kt * depth
    return pl.pallas_call(
        _pack_kernel,
        grid_spec=pltpu.PrefetchScalarGridSpec(
            num_scalar_prefetch=3,
            grid=(PACK_ROWS // PACK_R,),
            in_specs=[pl.BlockSpec((pl.Element(PACK_R * per_feature), pl.Element(LANES)),
                                   lambda b, src, scale, rows: (src[b] * per_feature, 0))],
            out_specs=pl.BlockSpec((depth, PACK_R, d), lambda b, src, scale, rows: (0, b, 0)),
        ),
        out_shape=jax.ShapeDtypeStruct((depth, PACK_ROWS, d), BF16),
        compiler_params=_params(("parallel",)),
        name="pack_in_proj",
    )(jnp.asarray(src), jnp.asarray(scale), jnp.asarray(rows), view)


CAST_BLOCK_ELEMS = 1 << 20


def _cast_kernel(w_ref, o_ref):
    o_ref[...] = w_ref[...].astype(o_ref.dtype)


def _cast_bf16(w):
    depth, r, c = w.shape
    tr = 16
    while tr * 2 * c <= CAST_BLOCK_ELEMS and r % (tr * 2) == 0:
        tr *= 2
    assert r % tr == 0
    return pl.pallas_call(
        _cast_kernel,
        grid=(depth, r // tr),
        in_specs=[pl.BlockSpec((None, tr, c), lambda l, i: (l, i, 0))],
        out_specs=pl.BlockSpec((None, tr, c), lambda l, i: (l, i, 0)),
        out_shape=jax.ShapeDtypeStruct(w.shape, BF16),
        compiler_params=_params(("parallel", "parallel")),
        name="cast_bf16",
    )(w)


def kernel(x, c, rel_bias, w_mod, b_mod, mix_norm_pre, mix_norm_post, w_in, fox_bias, conv_w, conv_b, conv_ln_g, conv_ln_b, sinks, w_branch_a, w_branch_b, w_branch_c, w_branch_d, w_out, ffn_norm_pre, ffn_norm_post, w_ffn_gate, w_ffn_up, w_ffn_down):
    depth = w_mod.shape[0]
    bsz, seq, d = x.shape
    tm = min(1024, seq)
    tm_small = min(256, seq)
    tm_ffn = min(512, seq)

    mod = _modulation(c, w_mod, b_mod)
    tab_a = _bias_tables(rel_bias[:, :A_HEADS], _moba_buckets(), scale=LOG2E)
    tab_d = _bias_tables(rel_bias[:, A_HEADS:], _swa_buckets(), scale=1.0)[:, 0]

    wa, wb, wc, wd, wo = [_cast_bf16(w) for w in (w_branch_a, w_branch_b, w_branch_c, w_branch_d, w_out)]
    wg, wu, wdn = [_cast_bf16(w) for w in (w_ffn_gate, w_ffn_up, w_ffn_down)]

    w_proj = _pack_in_proj(w_in)

    for l in range(depth):
        sh_m, sc_m, gt_m, sh_f, sc_f, gt_f = [mod[l, :, None, i * d:(i + 1) * d] for i in range(6)]

        proj, fox_raw = _norm_proj(x, mix_norm_pre[l], sc_m, sh_m, w_proj, l, tm=tm, tn=PROJ_TN)
        fox_qa, fox_ka = _fox_gate(fox_raw, fox_bias[l], t=min(512, seq))

        ya = _moba(proj, tab_a)
        yb = _fox(proj, fox_qa, fox_ka)
        yc = _conv(proj, conv_w[l], conv_b[l], conv_ln_g[l], conv_ln_b[l])
        yd = _swa(proj, sinks[l], tab_d)

        x, h_ffn = _merge(ya, yb, yc, yd, proj, x, gt_m, mix_norm_post[l], ffn_norm_pre[l], sc_f, sh_f,
                          wa, wb, wc, wd, wo, l, tm=tm_small)
        x = _ffn(x, h_ffn, gt_f, ffn_norm_post[l], wg, wu, wdn, l, tm=tm_ffn, tf=512)
    return x
```

```python
import functools
import math

import jax
import jax.numpy as jnp
import numpy as np
from jax import lax
from jax.experimental import pallas as pl
from jax.experimental.pallas import tpu as pltpu

F32 = jnp.float32
BF16 = jnp.bfloat16
HIGHEST = lax.Precision.HIGHEST

HEAD_DIM = 128
A_HEADS = 4
MOBA_BLOCK = 256
MOBA_TOPK = 3
B_HEADS = 4
CONV_CH = 512
CONV_WIDTH = 31
D_Q_HEADS = 8
D_KV_HEADS = 2
D_HEAD_DIM = 64
WINDOW = 128
N_BUCKETS = 32
MAX_DISTANCE = 1024
N_BRANCHES = 4
RMS_EPS = 1e-6
LN_EPS = 1e-5

A_W = A_HEADS * HEAD_DIM
B_W = B_HEADS * HEAD_DIM
D_QW = D_Q_HEADS * D_HEAD_DIM
D_KVW = D_KV_HEADS * D_HEAD_DIM

LANES = 128
SUBLANES = 8
VMEM_LIMIT = 56 * 1024 * 1024

NEG = -1e30
LOG2E = math.log2(math.e)
ATT_G = 4
ATT_T = MOBA_BLOCK
ATT_U = 4

D_MODEL = 2048
GATE_W = N_BRANCHES * D_MODEL
COL_AQ, COL_AK, COL_AV = GATE_W, GATE_W + A_W, GATE_W + 2 * A_W
COL_BQ, COL_BK, COL_BV = COL_AV + A_W, COL_AV + A_W + B_W, COL_AV + A_W + 2 * B_W
COL_CU = COL_BV + B_W
COL_DQ = COL_CU + 2 * CONV_CH
COL_DK = COL_DQ + D_QW
COL_DV = COL_DK + D_KVW
PROJ_W = COL_DV + D_KVW
PROJ_TN = 1024
PROJ_W_PAD = -(-PROJ_W // PROJ_TN) * PROJ_TN
MOBA_FAR = -(-(MAX_DISTANCE + MOBA_BLOCK - 1) // MOBA_BLOCK)
VOID_COL = LANES - 1
assert ATT_G % ATT_U == 0 and (MOBA_FAR - 1) % ATT_U == 0


def _params(sem, vmem=VMEM_LIMIT):
    return pltpu.CompilerParams(dimension_semantics=sem, vmem_limit_bytes=vmem)


def _t5_bucket(dist):
    max_exact = N_BUCKETS // 2
    d = jnp.maximum(dist, 0)
    log_ratio = jnp.log(jnp.maximum(d, 1).astype(jnp.float32) / max_exact) / math.log(MAX_DISTANCE / max_exact)
    large = max_exact + (log_ratio * (N_BUCKETS - max_exact)).astype(jnp.int32)
    large = jnp.minimum(large, N_BUCKETS - 1)
    return jnp.where(d < max_exact, d, large)


def _sigmoid(x):
    return 0.5 * jnp.tanh(0.5 * x) + 0.5


def _rms(y):
    return y * lax.rsqrt(jnp.mean(y * y, axis=-1, keepdims=True) + RMS_EPS)


def _modulated_norm(x, g, sc, sh):
    return (_rms(x) * g) * (1.0 + sc) + sh


MOD_TN = 512
MOD_KC = 256


def _mod_kernel(ct_ref, w_ref, b_ref, o_ref):
    d, nb = ct_ref.shape
    tn = w_ref.shape[1]
    ct = ct_ref[...]
    ca = ct * jax.nn.sigmoid(ct)
    rows = []
    for r in range(nb):
        acc = jnp.zeros((1, tn), F32)
        for kc in range(d // MOD_KC):
            sl = slice(kc * MOD_KC, (kc + 1) * MOD_KC)
            acc = acc + jnp.sum(w_ref[sl, :] * ca[sl, r:r + 1], axis=0, keepdims=True)
        rows.append(acc)
    o_ref[...] = jnp.concatenate(rows, axis=0) + b_ref[...]


def _modulation(c, w_mod, b_mod):
    depth, d, n = w_mod.shape
    bsz = c.shape[0]
    return pl.pallas_call(
        _mod_kernel,
        grid=(depth, n // MOD_TN),
        in_specs=[
            pl.BlockSpec((d, bsz), lambda l, j: (0, 0)),
            pl.BlockSpec((None, d, MOD_TN), lambda l, j: (l, 0, j)),
            pl.BlockSpec((None, 1, MOD_TN), lambda l, j: (l, 0, j)),
        ],
        out_specs=pl.BlockSpec((None, bsz, MOD_TN), lambda l, j: (l, 0, j)),
        out_shape=jax.ShapeDtypeStruct((depth, bsz, n), F32),
        compiler_params=_params(("parallel", "parallel")),
        name="mod",
    )(c.T, w_mod, b_mod.reshape(depth, 1, n))


def _bias_table_kernel(rb_ref, bucket_ref, o_ref, *, scale):
    h = pl.program_id(0)
    b = bucket_ref[...]
    acc = jnp.where(b < 0, NEG, 0.0).astype(F32)
    for u in range(N_BUCKETS):
        acc = jnp.where(b == u, rb_ref[h, u] * scale, acc)
    o_ref[...] = acc


def _bias_tables(rel_bias_heads, bucket, *, scale):
    nh = rel_bias_heads.shape[1]
    nt, r, c = bucket.shape
    return pl.pallas_call(
        functools.partial(_bias_table_kernel, scale=scale),
        grid=(nh, nt),
        in_specs=[
            pl.BlockSpec(memory_space=pltpu.SMEM),
            pl.BlockSpec((None, r, c), lambda h, t: (t, 0, 0)),
        ],
        out_specs=pl.BlockSpec((None, None, r, c), lambda h, t: (h, t, 0, 0)),
        out_shape=jax.ShapeDtypeStruct((nh, nt, r, c), F32),
        compiler_params=_params(("parallel", "parallel")),
        name="bias_table",
    )(rel_bias_heads.T, bucket)


def _moba_buckets():
    blk = MOBA_BLOCK
    i = jnp.arange(blk)[None, :]
    j = jnp.arange(blk)[:, None]
    tabs = []
    for delta in range(MOBA_FAR):
        dist = delta * blk + i - j
        tabs.append(jnp.where(dist >= 0, _t5_bucket(dist), -1))
    tabs.append(_t5_bucket(jnp.full((blk, blk), MAX_DISTANCE, jnp.int32)))
    return jnp.stack(tabs).astype(jnp.int32)


def _swa_buckets():
    qi = jnp.arange(WINDOW)[None, :]
    kj = jnp.arange(2 * WINDOW)[:, None]
    dist = qi + WINDOW - kj
    in_win = (dist >= 0) & (dist < WINDOW)
    return jnp.where(in_win, _t5_bucket(dist), -1).astype(jnp.int32)[None]


def _norm_proj_kernel(x_ref, g_ref, sc_ref, sh_ref, w_ref, ws_ref, o_ref, os_ref, h_scr):
    j = pl.program_id(2)
    tn = w_ref.shape[0]

    @pl.when(j == 0)
    def _():
        h = _modulated_norm(x_ref[...], g_ref[...], sc_ref[...], sh_ref[...]).astype(BF16)
        h_scr[...] = h
        os_ref[...] = _dot_nt(h, ws_ref[...])

    @pl.when(j < GATE_W // tn)
    def _():
        o_ref[...] = _sigmoid(_dot_nt(h_scr[...], w_ref[...])).astype(o_ref.dtype)

    @pl.when(j >= GATE_W // tn)
    def _():
        o_ref[...] = _dot_nt(h_scr[...], w_ref[...]).astype(o_ref.dtype)


def _norm_proj(x, g, sc, sh, w, layer, *, tm, tn):
    bsz, seq, d = x.shape
    n = PROJ_W_PAD
    ns = LANES
    assert GATE_W % tn == 0 and n % tn == 0 and w.shape[1] == PACK_ROWS
    return pl.pallas_call(
        _norm_proj_kernel,
        grid=(bsz, seq // tm, n // tn),
        in_specs=[
            pl.BlockSpec((None, tm, d), lambda b, i, j: (b, i, 0)),
            pl.BlockSpec((1, d), lambda b, i, j: (0, 0)),
            pl.BlockSpec((None, 1, d), lambda b, i, j: (b, 0, 0)),
            pl.BlockSpec((None, 1, d), lambda b, i, j: (b, 0, 0)),
            pl.BlockSpec((None, tn, d), lambda b, i, j: (layer, j, 0)),
            pl.BlockSpec((None, ns, d), lambda b, i, j: (layer, PACK_SIDE_ROW // ns, 0)),
        ],
        out_specs=[pl.BlockSpec((None, tm, tn), lambda b, i, j: (b, i, j)),
                   pl.BlockSpec((None, tm, ns), lambda b, i, j: (b, i, 0))],
        out_shape=[jax.ShapeDtypeStruct((bsz, seq, n), BF16), jax.ShapeDtypeStruct((bsz, seq, ns), F32)],
        scratch_shapes=[pltpu.VMEM((tm, d), BF16)],
        compiler_params=_params(("parallel", "parallel", "arbitrary")),
        name="norm_proj",
    )(x, g.reshape(1, d), sc, sh, w, w)


def _split3(c):
    hi = c.astype(BF16).astype(F32)
    mid = (c - hi).astype(BF16).astype(F32)
    lo = (c - hi - mid).astype(BF16).astype(F32)
    return hi, mid, lo


def _fox_gate_kernel(x_ref, fb_ref, qa_ref, ka_ref, cum_ref, carry_scr):
    t = x_ref.shape[0]

    @pl.when(pl.program_id(1) == 0)
    def _():
        carry_scr[...] = jnp.zeros_like(carry_scr)

    lf = jax.nn.log_sigmoid(x_ref[...] + fb_ref[...])
    tri = (lax.broadcasted_iota(jnp.int32, (t, t), 0) >= lax.broadcasted_iota(jnp.int32, (t, t), 1)).astype(F32)
    c = jnp.dot(tri, lf, precision=HIGHEST, preferred_element_type=F32) + carry_scr[...]
    carry_scr[...] = c[t - 1:t, :]
    cum_ref[...] = c * LOG2E
    lane = lax.broadcasted_iota(jnp.int32, (t, LANES), 1)
    sub = lax.broadcasted_iota(jnp.int32, (SUBLANES, t), 0)
    ct = c.T * LOG2E
    for h in range(B_HEADS):
        hi, mid, lo = _split3(jnp.broadcast_to(c[:, h:h + 1], (t, LANES)) * LOG2E)
        ka = jnp.where(lane == 0, -hi, jnp.where(lane == 1, -mid, jnp.where(lane == 2, -lo, jnp.where(lane < 6, 1.0, 0.0))))
        ka_ref[h] = ka.astype(BF16)
        hi, mid, lo = _split3(jnp.broadcast_to(ct[h:h + 1, :], (SUBLANES, t)))
        qa = jnp.where(sub < 3, 1.0, jnp.where(sub == 3, hi, jnp.where(sub == 4, mid, jnp.where(sub == 5, lo, 0.0))))
        qa_ref[h] = jnp.concatenate([qa, jnp.zeros((LANES - SUBLANES, t), F32)], axis=0).astype(BF16)


def _fox_gate(raw, fox_b, *, t):
    bsz, seq, _ = raw.shape
    fb = jnp.zeros((1, LANES), F32).at[0, :B_HEADS].set(fox_b)
    aug = pl.BlockSpec((None, B_HEADS, t, LANES), lambda b, i: (b, 0, i, 0))
    aug_t = pl.BlockSpec((None, B_HEADS, LANES, t), lambda b, i: (b, 0, 0, i))
    return pl.pallas_call(
        _fox_gate_kernel,
        grid=(bsz, seq // t),
        in_specs=[
            pl.BlockSpec((None, t, LANES), lambda b, i: (b, i, 0)),
            pl.BlockSpec((1, LANES), lambda b, i: (0, 0)),
        ],
        out_specs=[aug_t, aug, pl.BlockSpec((None, t, LANES), lambda b, i: (b, i, 0))],
        out_shape=[jax.ShapeDtypeStruct((bsz, B_HEADS, LANES, seq), BF16),
                   jax.ShapeDtypeStruct((bsz, B_HEADS, seq, LANES), BF16),
                   jax.ShapeDtypeStruct((bsz, seq, LANES), F32)],
        scratch_shapes=[pltpu.VMEM((1, LANES), F32)],
        compiler_params=_params(("parallel", "arbitrary")),
        name="fox_gate",
    )(raw, fb)


ONES_ROWS = 16


def _softmax_step(zt, vt, carry):
    m, acc = carry
    m_new = jnp.maximum(m, jnp.max(zt, axis=0, keepdims=True))
    p = jnp.exp2(zt - m_new)
    acc = jnp.exp2(m - m_new) * acc + jnp.dot(vt, p.astype(BF16), preferred_element_type=F32)
    return m_new, acc


def _softmax_init(tq, dh):
    return (jnp.full((1, tq), NEG, F32), jnp.zeros((dh + ONES_ROWS, tq), F32))


def _softmax_finish(carry, dh):
    _, acc = carry
    return (acc[:dh, :] / acc[dh:dh + 1, :]).T


def _fill_vt(vt_scr, v_ref, chunk):
    seq, dh = v_ref.shape
    for cix in range(seq // chunk):
        sl = slice(cix * chunk, (cix + 1) * chunk)
        vt_scr[:dh, sl] = _transpose_bf16(v_ref[sl, :])
    vt_scr[dh:, :] = jnp.ones((ONES_ROWS, seq), BF16)


ATT_LOOKAHEAD = 6


def _run_substeps(steps, carry):
    carry = list(carry)
    pending = {}
    for s in range(min(ATT_LOOKAHEAD, len(steps))):
        pending[s] = steps[s][1]()
    for s, (j, _, fix, vt) in enumerate(steps):
        if s + ATT_LOOKAHEAD < len(steps):
            pending[s + ATT_LOOKAHEAD] = steps[s + ATT_LOOKAHEAD][1]()
        z, cj = pending.pop(s), carry[j]
        if fix is not None:
            z, cj = fix(z, cj)
        carry[j] = _softmax_step(z, vt, cj)
    return carry


def _dense_loop(start, n_blocks, make_steps, carry):
    big = 2 * ATT_U
    carry = lax.fori_loop(0, n_blocks // big,
                          lambda grp, c: tuple(_run_substeps(make_steps(start + grp * big, big), c)), carry)
    first = start + (n_blocks // big) * big
    return lax.fori_loop(0, (start + n_blocks - first) // ATT_U,
                         lambda grp, c: tuple(_run_substeps(make_steps(first + grp * ATT_U, ATT_U), c)), carry)


def _transpose_bf16(x):
    return x.astype(F32).T.astype(BF16)


_NT = (((1,), (1,)), ((), ()))


def _dot_nt(a, b):
    return lax.dot_general(a, b, _NT, preferred_element_type=F32)


def _moba_kernel(cfar_ref, q_ref, k_ref, v_ref, tab_ref, o_ref, kmean_scr, sc_scr, qt_scr, vt_scr):
    g, t = ATT_G, ATT_T
    seq, dh = k_ref.shape
    nblk = seq // t
    rows = g * t
    i = pl.program_id(2)
    base = i * g

    @pl.when(i == 0)
    def _():
        r = lax.broadcasted_iota(jnp.int32, (nblk, seq), 0)
        c = lax.broadcasted_iota(jnp.int32, (nblk, seq), 1)
        avg = jnp.where((c >= r * t) & (c < (r + 1) * t), 1.0 / t, 0.0).astype(BF16)
        kmean_scr[...] = jnp.dot(avg, k_ref[...], preferred_element_type=F32)
        _fill_vt(vt_scr, v_ref, rows)

    q = q_ref[...]
    qt_scr[:dh, :] = _transpose_bf16(q)
    sc = lax.dot_general(kmean_scr[...], q.astype(F32), _NT, precision=HIGHEST, preferred_element_type=F32)
    row = lax.broadcasted_iota(jnp.int32, (nblk, rows), 0)
    own = base + lax.broadcasted_iota(jnp.int32, (nblk, rows), 1) // t
    past = row < own
    sc = jnp.where(past, sc, -jnp.inf)
    sc_scr[...] = sc

    def rank_body(m, rank):
        sm = sc_scr[pl.ds(m, 1), :]
        return rank + jnp.where(sm > sc, 1.0, jnp.where(sm == sc, jnp.where(row > m, 1.0, 0.0), 0.0))

    rank = lax.fori_loop(0, base + g - 1, rank_body, jnp.zeros((nblk, rows), F32))
    pen = jnp.where(past, jnp.where(rank < MOBA_TOPK, 0.0, NEG), jnp.where(row == own, 0.0, NEG))
    void = jnp.where(lax.broadcasted_iota(jnp.int32, (SUBLANES, rows), 0) == SUBLANES - 1, NEG, 0.0)
    pen = jnp.concatenate([pen, jnp.zeros((LANES - SUBLANES - nblk, rows), F32), void], axis=0)
    qt_scr[dh:, :] = pen.astype(BF16)

    lane = lax.broadcasted_iota(jnp.int32, (t, LANES), 1)

    def kv(n, col):
        start = pl.multiple_of(n * t, t)
        onehot = jnp.where(lane == col, 1.0, 0.0).astype(BF16)
        return jnp.concatenate([k_ref[pl.ds(start, t), :], onehot], axis=1), vt_scr[:, pl.ds(start, t)]

    def logits(kb, j):
        return jnp.dot(kb, qt_scr[:, j * t:(j + 1) * t], preferred_element_type=F32)

    def far_steps(first, count):
        steps = []
        for u in range(count):
            kb, vb = kv(first + u, first + u)
            steps += [(j, functools.partial(logits, kb, j), None, vb) for j in range(g)]
        return steps

    carry = _dense_loop(0, jnp.maximum(base - (MOBA_FAR - 1), 0), far_steps,
                        tuple(_softmax_init(t, dh) for _ in range(g)))
    cfar = cfar_ref[pl.program_id(1)]

    def near_fix(delta, z, cj):
        if delta == MOBA_FAR - 1:
            m, acc = cj
            cj = (m + cfar, acc)
        return tab_ref[delta] + z, cj

    steps = []
    for k in range(-(MOBA_FAR - 1), g):
        n = base + k
        if k < 0:
            kb, vb = kv(jnp.maximum(n, 0), jnp.where(n >= 0, n, VOID_COL))
        else:
            kb, vb = kv(n, n)
        for j in range(max(k, 0), g):
            fix = functools.partial(near_fix, j - k) if j - k < MOBA_FAR else None
            steps.append((j, functools.partial(logits, kb, j), fix, vb))
    carry = _run_substeps(steps, carry)
    for j in range(g):
        o_ref[j * t:(j + 1) * t, :] = _softmax_finish(carry[j], dh).astype(o_ref.dtype)


def _moba(proj, tab):
    bsz, seq, _ = proj.shape
    g, t = ATT_G, ATT_T
    nblk = seq // t
    assert seq % (g * t) == 0 and nblk <= LANES - SUBLANES and nblk % SUBLANES == 0
    qc, kc, vc = COL_AQ // HEAD_DIM, COL_AK // HEAD_DIM, COL_AV // HEAD_DIM
    cfar = tab[:, MOBA_FAR, 0, 0]
    return pl.pallas_call(
        _moba_kernel,
        grid=(bsz, A_HEADS, nblk // g),
        in_specs=[
            pl.BlockSpec(memory_space=pltpu.SMEM),
            pl.BlockSpec((None, g * t, HEAD_DIM), lambda b, h, i: (b, i, qc + h)),
            pl.BlockSpec((None, seq, HEAD_DIM), lambda b, h, i: (b, 0, kc + h)),
            pl.BlockSpec((None, seq, HEAD_DIM), lambda b, h, i: (b, 0, vc + h)),
            pl.BlockSpec((None, MOBA_FAR, t, t), lambda b, h, i: (h, 0, 0, 0)),
        ],
        out_specs=pl.BlockSpec((None, g * t, HEAD_DIM), lambda b, h, i: (b, i, h)),
        out_shape=jax.ShapeDtypeStruct((bsz, seq, A_W), BF16),
        scratch_shapes=[pltpu.VMEM((nblk, HEAD_DIM), F32), pltpu.VMEM((nblk, g * t), F32),
                        pltpu.VMEM((HEAD_DIM + LANES, g * t), BF16), pltpu.VMEM((HEAD_DIM + ONES_ROWS, seq), BF16)],
        compiler_params=_params(("parallel", "parallel", "arbitrary")),
        name="moba",
    )(cfar, proj, proj, proj, tab)


def _fox_kernel(start_ref, q_ref, qa_ref, k_ref, ka_ref, v_ref, o_ref, qt_scr, vt_scr):
    g, t = ATT_G, ATT_T
    rows = g * t
    seq, dh = k_ref.shape
    base = pl.program_id(2) * g
    first = start_ref[(pl.program_id(0) * pl.num_programs(1) + pl.program_id(1)) * pl.num_programs(2)
                      + pl.program_id(2)]

    @pl.when(base == 0)
    def _():
        _fill_vt(vt_scr, v_ref, rows)

    qt_scr[:dh, :] = _transpose_bf16(q_ref[...])
    qt_scr[dh:, :] = qa_ref[...]

    def kv(n):
        start = pl.multiple_of(n * t, t)
        return (jnp.concatenate([k_ref[pl.ds(start, t), :], ka_ref[pl.ds(start, t), :]], axis=1),
                vt_scr[:, pl.ds(start, t)])

    def logits(kb, j):
        return jnp.dot(kb, qt_scr[:, j * t:(j + 1) * t], preferred_element_type=F32)

    def dense_steps(first, count):
        steps = []
        for u in range(count):
            kb, vb = kv(first + u)
            steps += [(j, functools.partial(logits, kb, j), None, vb) for j in range(g)]
        return steps

    carry = _dense_loop(first, base - first, dense_steps, tuple(_softmax_init(t, dh) for _ in range(g)))
    causal = lax.broadcasted_iota(jnp.int32, (t, t), 0) <= lax.broadcasted_iota(jnp.int32, (t, t), 1)

    def diag_fix(z, cj):
        return jnp.where(causal, z, NEG), cj

    steps = []
    for k in range(g):
        kb, vb = kv(base + k)
        steps += [(j, functools.partial(logits, kb, j), diag_fix if j == k else None, vb) for j in range(k, g)]
    carry = _run_substeps(steps, carry)
    for j in range(g):
        o_ref[j * t:(j + 1) * t, :] = _softmax_finish(carry[j], dh).astype(o_ref.dtype)


FOX_SKIP_BITS = 200.0


def _fox_first_block(proj, cum2):
    bsz, seq, _ = proj.shape
    g, t = ATT_G, ATT_T
    nblk, ntile = seq // t, seq // (g * t)
    sq_norm = lambda col: jnp.sum(jnp.square(proj[:, :, col:col + B_W].astype(F32).reshape(bsz, seq, B_HEADS, HEAD_DIM)), -1)
    slack = 1.0 + 2.0 ** -8
    kmax = jnp.sqrt(jnp.max(sq_norm(COL_BK).reshape(bsz, nblk, t, B_HEADS), axis=2)) * slack
    qmax = jnp.sqrt(jnp.max(sq_norm(COL_BQ).reshape(bsz, ntile, g * t, B_HEADS), axis=2)) * slack
    kpref = lax.cummax(kmax, axis=1)
    ktile = jnp.max(kmax.reshape(bsz, ntile, g, B_HEADS), axis=2)
    c_first = cum2[:, 0::g * t, :B_HEADS]
    c_end = cum2[:, t - 1::t, :B_HEADS]
    cand = jnp.arange(ATT_U - 1, nblk, ATT_U)
    ub = (qmax[:, :, None, :] * kpref[:, None, cand, :] + (c_first[:, :, None, :] - c_end[:, None, cand, :])
          + (qmax * ktile)[:, :, None, :])
    n_skip = (cand + 1)[None, None, :, None]
    ok = (ub < -FOX_SKIP_BITS) & (n_skip <= (jnp.arange(ntile) * g)[None, :, None, None])
    first = jnp.max(jnp.where(ok, n_skip, 0), axis=2)
    return first.transpose(0, 2, 1).reshape(-1).astype(jnp.int32)


def _fox(proj, qa, ka, first):
    bsz, seq, _ = proj.shape
    g, t = ATT_G, ATT_T
    assert seq % (g * t) == 0
    qc, kc, vc = COL_BQ // HEAD_DIM, COL_BK // HEAD_DIM, COL_BV // HEAD_DIM
    return pl.pallas_call(
        _fox_kernel,
        grid_spec=pltpu.PrefetchScalarGridSpec(
            num_scalar_prefetch=1,
            grid=(bsz, B_HEADS, seq // (g * t)),
            in_specs=[
                pl.BlockSpec((None, g * t, HEAD_DIM), lambda b, h, i, s: (b, i, qc + h)),
                pl.BlockSpec((None, None, LANES, g * t), lambda b, h, i, s: (b, h, 0, i)),
                pl.BlockSpec((None, seq, HEAD_DIM), lambda b, h, i, s: (b, 0, kc + h)),
                pl.BlockSpec((None, None, seq, LANES), lambda b, h, i, s: (b, h, 0, 0)),
                pl.BlockSpec((None, seq, HEAD_DIM), lambda b, h, i, s: (b, 0, vc + h)),
            ],
            out_specs=pl.BlockSpec((None, g * t, HEAD_DIM), lambda b, h, i, s: (b, i, h)),
            scratch_shapes=[pltpu.VMEM((HEAD_DIM + LANES, g * t), BF16),
                            pltpu.VMEM((HEAD_DIM + ONES_ROWS, seq), BF16)],
        ),
        out_shape=jax.ShapeDtypeStruct((bsz, seq, B_W), BF16),
        compiler_params=_params(("parallel", "parallel", "arbitrary")),
        name="fox",
    )(first, proj, qa, proj, ka, proj)


SWA_BLOCKS = 4


def _swa_kernel(sink_ref, q_ref, kp_ref, kc_ref, vp_ref, vc_ref, tab_ref, o_ref):
    w = WINDOW
    dh = D_HEAD_DIM
    grp = D_Q_HEADS // D_KV_HEADS
    i = pl.program_id(1)
    nblk = q_ref.shape[0] // w
    span = (nblk + 1) * w
    kall = jnp.concatenate([kp_ref[...], kc_ref[...]], axis=0)
    vt = jnp.concatenate([_transpose_bf16(vp_ref[...]), _transpose_bf16(vc_ref[...])], axis=1)
    ones = jnp.ones((ONES_ROWS, span), BF16)
    kg = [kall[:, g * dh:(g + 1) * dh] for g in range(D_KV_HEADS)]
    vg = [jnp.concatenate([vt[g * dh:(g + 1) * dh, :], ones], axis=0) for g in range(D_KV_HEADS)]
    qt = _transpose_bf16(q_ref[...])
    key = lax.broadcasted_iota(jnp.int32, (2 * w, w), 0)
    first = key >= jnp.where(i > 0, 0, w)

    def block_logits(blk):
        return [jnp.dot(kg[h // grp][blk * w:(blk + 2) * w, :], qt[h * dh:(h + 1) * dh, blk * w:(blk + 1) * w],
                        preferred_element_type=F32) for h in range(D_Q_HEADS)]

    logits = block_logits(0)
    for blk in range(nblk):
        nxt = block_logits(blk + 1) if blk + 1 < nblk else None
        outs = []
        for h in range(D_Q_HEADS):
            s = tab_ref[h] + logits[h]
            if blk == 0:
                s = jnp.where(first, s, NEG)
            sink = sink_ref[h]
            m = jnp.maximum(jnp.max(s, axis=0, keepdims=True), sink)
            p = jnp.exp(s - m)
            acc = jnp.dot(vg[h // grp][:, blk * w:(blk + 2) * w], p.astype(BF16), preferred_element_type=F32)
            den = acc[dh:dh + 1, :] + jnp.exp(sink - m)
            outs.append(acc[:dh, :] / den)
        o_ref[blk * w:(blk + 1) * w, :] = jnp.concatenate(outs, axis=0).T.astype(o_ref.dtype)
        logits = nxt


def _swa(proj, sinks, tab):
    bsz, seq, _ = proj.shape
    w = WINDOW
    nb = SWA_BLOCKS
    assert seq % (nb * w) == 0
    qc, kc, vc = COL_DQ // D_QW, COL_DK // D_KVW, COL_DV // D_KVW
    prev = lambda i: jnp.maximum(i * nb - 1, 0)
    return pl.pallas_call(
        _swa_kernel,
        grid=(bsz, seq // (nb * w)),
        in_specs=[
            pl.BlockSpec(memory_space=pltpu.SMEM),
            pl.BlockSpec((None, nb * w, D_QW), lambda b, i: (b, i, qc)),
            pl.BlockSpec((None, w, D_KVW), lambda b, i: (b, prev(i), kc)),
            pl.BlockSpec((None, nb * w, D_KVW), lambda b, i: (b, i, kc)),
            pl.BlockSpec((None, w, D_KVW), lambda b, i: (b, prev(i), vc)),
            pl.BlockSpec((None, nb * w, D_KVW), lambda b, i: (b, i, vc)),
            pl.BlockSpec((D_Q_HEADS, 2 * w, w), lambda b, i: (0, 0, 0)),
        ],
        out_specs=pl.BlockSpec((None, nb * w, D_QW), lambda b, i: (b, i, 0)),
        out_shape=jax.ShapeDtypeStruct((bsz, seq, D_QW), BF16),
        compiler_params=_params(("parallel", "parallel")),
        name="swa",
    )(sinks, proj, proj, proj, proj, proj, tab)


CONV_T = 512
CONV_HALO = 32
CONV_ROWS = 64


def _conv_kernel(a_ref, g_ref, ha_ref, hg_ref, w_ref, b_ref, lg_ref, lb_ref, o_ref, buf):
    t = a_ref.shape[0]
    halo = CONV_HALO
    hp = ha_ref[...].astype(F32) * jax.nn.sigmoid(hg_ref[...].astype(F32))
    buf[0:halo, :] = jnp.where(pl.program_id(1) > 0, hp, 0.0)
    buf[halo:halo + t, :] = a_ref[...].astype(F32) * jax.nn.sigmoid(g_ref[...].astype(F32))
    off = halo - (CONV_WIDTH - 1)
    for r in range(t // CONV_ROWS):
        acc = jnp.broadcast_to(b_ref[...], (CONV_ROWS, CONV_CH))
        for rho in range(SUBLANES):
            rows = CONV_ROWS if rho == 0 else CONV_ROWS + SUBLANES
            part = None
            for s in range(off, off + CONV_WIDTH):
                if s % SUBLANES != rho:
                    continue
                start = r * CONV_ROWS + s - rho
                term = w_ref[s - off:s - off + 1, :] * buf[start:start + rows, :]
                part = term if part is None else part + term
            acc = acc + part[rho:rho + CONV_ROWS, :]
        mu = jnp.mean(acc, axis=-1, keepdims=True)
        d = acc - mu
        var = jnp.mean(d * d, axis=-1, keepdims=True)
        y = d * lax.rsqrt(var + LN_EPS) * lg_ref[...] + lb_ref[...]
        o_ref[r * CONV_ROWS:(r + 1) * CONV_ROWS, :] = (y * jax.nn.sigmoid(y)).astype(o_ref.dtype)


def _conv(proj, conv_w, conv_b, ln_g, ln_b):
    bsz, seq, _ = proj.shape
    t = min(CONV_T, seq)
    ca, cg = COL_CU // CONV_CH, COL_CU // CONV_CH + 1
    hb = t // CONV_HALO
    prev = lambda i: jnp.maximum(i * hb - 1, 0)
    row = lambda v: v.reshape(1, CONV_CH)
    return pl.pallas_call(
        _conv_kernel,
        grid=(bsz, seq // t),
        in_specs=[
            pl.BlockSpec((None, t, CONV_CH), lambda b, i: (b, i, ca)),
            pl.BlockSpec((None, t, CONV_CH), lambda b, i: (b, i, cg)),
            pl.BlockSpec((None, CONV_HALO, CONV_CH), lambda b, i: (b, prev(i), ca)),
            pl.BlockSpec((None, CONV_HALO, CONV_CH), lambda b, i: (b, prev(i), cg)),
            pl.BlockSpec((CONV_WIDTH, CONV_CH), lambda b, i: (0, 0)),
            pl.BlockSpec((1, CONV_CH), lambda b, i: (0, 0)),
            pl.BlockSpec((1, CONV_CH), lambda b, i: (0, 0)),
            pl.BlockSpec((1, CONV_CH), lambda b, i: (0, 0)),
        ],
        out_specs=pl.BlockSpec((None, t, CONV_CH), lambda b, i: (b, i, 0)),
        out_shape=jax.ShapeDtypeStruct((bsz, seq, CONV_CH), BF16),
        scratch_shapes=[pltpu.VMEM((CONV_HALO + t, CONV_CH), F32)],
        compiler_params=_params(("parallel", "parallel")),
        name="conv",
    )(proj, proj, proj, proj, conv_w, row(conv_b), row(ln_g), row(ln_b))


def _merge_kernel(ya_ref, yb_ref, yc_ref, yd_ref, gates_ref, x_ref, gt_ref, gp_ref, gf_ref, scf_ref, shf_ref,
                  wa_ref, wb_ref, wc_ref, wd_ref, wo_ref, o_ref, h_ref):
    d = x_ref.shape[1]
    y = None
    for i, (br, w) in enumerate(((ya_ref, wa_ref), (yb_ref, wb_ref), (yc_ref, wc_ref), (yd_ref, wd_ref))):
        t = jnp.dot(br[...], w[...], preferred_element_type=F32) * gates_ref[:, i * d:(i + 1) * d].astype(F32)
        y = t if y is None else y + t
    z = jnp.dot(y.astype(BF16), wo_ref[...], preferred_element_type=F32)
    x_new = x_ref[...] + gt_ref[...] * (_rms(z) * gp_ref[...])
    o_ref[...] = x_new
    h_ref[...] = _modulated_norm(x_new, gf_ref[...], scf_ref[...], shf_ref[...]).astype(h_ref.dtype)


def _merge(ya, yb, yc, yd, gates, x, gt, g_post, g_ffn, sc_f, sh_f, wa, wb, wc, wd, wo, layer, *, tm):
    bsz, seq, d = x.shape
    tok = lambda width: pl.BlockSpec((None, tm, width), lambda b, i: (b, i, 0))
    vec = lambda: pl.BlockSpec((None, 1, d), lambda b, i: (b, 0, 0))
    par = lambda: pl.BlockSpec((1, d), lambda b, i: (0, 0))
    full = lambda a: pl.BlockSpec((None,) + a.shape[1:], lambda b, i: (layer, 0, 0),
                                  pipeline_mode=pl.Buffered(1))
    return pl.pallas_call(
        _merge_kernel,
        grid=(bsz, seq // tm),
        in_specs=[tok(A_W), tok(B_W), tok(CONV_CH), tok(D_QW), tok(N_BRANCHES * d), tok(d),
                  vec(), par(), par(), vec(), vec(),
                  full(wa), full(wb), full(wc), full(wd), full(wo)],
        out_specs=[tok(d), tok(d)],
        out_shape=[jax.ShapeDtypeStruct((bsz, seq, d), F32), jax.ShapeDtypeStruct((bsz, seq, d), BF16)],
        compiler_params=_params(("parallel", "parallel")),
        name="merge",
    )(ya, yb, yc, yd, gates, x, gt, g_post.reshape(1, d), g_ffn.reshape(1, d), sc_f, sh_f, wa, wb, wc, wd, wo)


def _ffn_kernel(x_ref, h_ref, gt_ref, gp_ref, wg_ref, wu_ref, wd_ref, o_ref, acc_scr):
    f = pl.program_id(2)

    @pl.when(f == 0)
    def _():
        acc_scr[...] = jnp.zeros_like(acc_scr)

    h = h_ref[...]
    g = jnp.dot(h, wg_ref[...], preferred_element_type=F32)
    u = jnp.dot(h, wu_ref[...], preferred_element_type=F32)
    a = ((g * jax.nn.sigmoid(g)) * u).astype(BF16)
    acc_scr[...] += jnp.dot(a, wd_ref[...], preferred_element_type=F32)

    @pl.when(f == pl.num_programs(2) - 1)
    def _():
        o_ref[...] = x_ref[...] + gt_ref[...] * (_rms(acc_scr[...]) * gp_ref[...])


def _ffn(x, h, gt, g_post, wg, wu, wd, layer, *, tm, tf):
    bsz, seq, d = x.shape
    dff = wg.shape[2]
    vec = lambda: pl.BlockSpec((None, 1, d), lambda b, i, f: (b, 0, 0))
    par = lambda: pl.BlockSpec((1, d), lambda b, i, f: (0, 0))
    return pl.pallas_call(
        _ffn_kernel,
        grid=(bsz, seq // tm, dff // tf),
        in_specs=[
            pl.BlockSpec((None, tm, d), lambda b, i, f: (b, i, 0)),
            pl.BlockSpec((None, tm, d), lambda b, i, f: (b, i, 0)),
            vec(), par(),
            pl.BlockSpec((None, d, tf), lambda b, i, f: (layer, 0, f)),
            pl.BlockSpec((None, d, tf), lambda b, i, f: (layer, 0, f)),
            pl.BlockSpec((None, tf, d), lambda b, i, f: (layer, f, 0)),
        ],
        out_specs=pl.BlockSpec((None, tm, d), lambda b, i, f: (b, i, 0)),
        out_shape=jax.ShapeDtypeStruct((bsz, seq, d), F32),
        scratch_shapes=[pltpu.VMEM((tm, d), F32)],
        compiler_params=_params(("parallel", "parallel", "arbitrary")),
        name="ffn",
    )(x, h, gt, g_post.reshape(1, d), wg, wu, wd)


SRC_QKV = 3 * A_W + 3 * B_W
SRC_REST = SRC_QKV + B_HEADS
SRC_GATE = SRC_REST + (PROJ_W - GATE_W - SRC_QKV)
IN_W = SRC_GATE + GATE_W
PACK_R = 256
PACK_SIDE_ROW = PROJ_W_PAD
PACK_ROWS = PROJ_W_PAD + PACK_R
assert GATE_W % PACK_R == 0 and SRC_QKV % PACK_R == 0 and PROJ_W % PACK_R == 0 and PROJ_W_PAD % PACK_R == 0


def _pack_tables():
    s2 = HEAD_DIM ** -0.5 * LOG2E
    src, scale, rows = [], [], []
    for b in range(PACK_ROWS // PACK_R):
        row = b * PACK_R
        if row < GATE_W:
            src.append(SRC_GATE + row); scale.append(1.0); rows.append(PACK_R)
        elif row < GATE_W + SRC_QKV:
            s = row - GATE_W
            is_q = s < A_W or 3 * A_W <= s < 3 * A_W + B_W
            src.append(s); scale.append(s2 if is_q else 1.0); rows.append(PACK_R)
        elif row < PROJ_W:
            s = row - GATE_W - SRC_QKV
            is_q = 2 * CONV_CH <= s < 2 * CONV_CH + D_QW
            src.append(SRC_REST + s); scale.append(D_HEAD_DIM ** -0.5 if is_q else 1.0); rows.append(PACK_R)
        elif row == PACK_SIDE_ROW:
            src.append(SRC_QKV); scale.append(1.0); rows.append(B_HEADS)
        else:
            src.append(0); scale.append(0.0); rows.append(0)
    return np.asarray(src, np.int32), np.asarray(scale, np.float32), np.asarray(rows, np.int32)


def _pack_kernel(src_ref, scale_ref, rows_ref, a_ref, o_ref):
    blk = pl.program_id(0)
    depth, _, d = o_ref.shape
    kt = d // LANES
    keep = lax.broadcasted_iota(jnp.int32, (PACK_R, d), 0) < rows_ref[blk]
    for layer in range(depth):
        cols = [a_ref[pl.ds(t * depth + layer, PACK_R, stride=kt * depth), :] for t in range(kt)]
        o_ref[layer] = jnp.where(keep, jnp.concatenate(cols, axis=1) * scale_ref[blk], 0.0).astype(o_ref.dtype)


def _pack_in_proj(w):
    depth, d, in_w = w.shape
    assert in_w == IN_W and d % LANES == 0
    kt = d // LANES
    view = jnp.transpose(w.reshape(depth, kt, LANES, in_w), (3, 1, 0, 2)).reshape(in_w * kt * depth, LANES)
    src, scale, rows = _pack_tables()
    per_feature = kt * depth
    return pl.pallas_call(
        _pack_kernel,
        grid_spec=pltpu.PrefetchScalarGridSpec(
            num_scalar_prefetch=3,
            grid=(PACK_ROWS // PACK_R,),
            in_specs=[pl.BlockSpec((pl.Element(PACK_R * per_feature), pl.Element(LANES)),
                                   lambda b, src, scale, rows: (src[b] * per_feature, 0))],
            out_specs=pl.BlockSpec((depth, PACK_R, d), lambda b, src, scale, rows: (0, b, 0)),
        ),
        out_shape=jax.ShapeDtypeStruct((depth, PACK_ROWS, d), BF16),
        compiler_params=_params(("parallel",)),
        name="pack_in_proj",
    )(jnp.asarray(src), jnp.asarray(scale), jnp.asarray(rows), view)


CAST_BLOCK_ELEMS = 1 << 20


def _cast_kernel(w_ref, o_ref):
    o_ref[...] = w_ref[...].astype(o_ref.dtype)


def _cast_bf16(w):
    depth, r, c = w.shape
    tr = 16
    while tr * 2 * c <= CAST_BLOCK_ELEMS and r % (tr * 2) == 0:
        tr *= 2
    assert r % tr == 0
    return pl.pallas_call(
        _cast_kernel,
        grid=(depth, r // tr),
        in_specs=[pl.BlockSpec((None, tr, c), lambda l, i: (l, i, 0))],
        out_specs=pl.BlockSpec((None, tr, c), lambda l, i: (l, i, 0)),
        out_shape=jax.ShapeDtypeStruct(w.shape, BF16),
        compiler_params=_params(("parallel", "parallel")),
        name="cast_bf16",
    )(w)


def kernel(x, c, rel_bias, w_mod, b_mod, mix_norm_pre, mix_norm_post, w_in, fox_bias, conv_w, conv_b, conv_ln_g, conv_ln_b, sinks, w_branch_a, w_branch_b, w_branch_c, w_branch_d, w_out, ffn_norm_pre, ffn_norm_post, w_ffn_gate, w_ffn_up, w_ffn_down):
    depth = w_mod.shape[0]
    bsz, seq, d = x.shape
    tm = min(1024, seq)
    tm_small = min(256, seq)
    tm_ffn = min(512, seq)

    mod = _modulation(c, w_mod, b_mod)
    tab_a = _bias_tables(rel_bias[:, :A_HEADS], _moba_buckets(), scale=LOG2E)
    tab_d = _bias_tables(rel_bias[:, A_HEADS:], _swa_buckets(), scale=1.0)[:, 0]

    wa, wb, wc, wd, wo = [_cast_bf16(w) for w in (w_branch_a, w_branch_b, w_branch_c, w_branch_d, w_out)]
    wg, wu, wdn = [_cast_bf16(w) for w in (w_ffn_gate, w_ffn_up, w_ffn_down)]

    w_proj = _pack_in_proj(w_in)

    for l in range(depth):
        sh_m, sc_m, gt_m, sh_f, sc_f, gt_f = [mod[l, :, None, i * d:(i + 1) * d] for i in range(6)]

        proj, fox_raw = _norm_proj(x, mix_norm_pre[l], sc_m, sh_m, w_proj, l, tm=tm, tn=PROJ_TN)
        fox_qa, fox_ka, fox_cum = _fox_gate(fox_raw, fox_bias[l], t=min(512, seq))

        ya = _moba(proj, tab_a)
        yb = _fox(proj, fox_qa, fox_ka, _fox_first_block(proj, fox_cum))
        yc = _conv(proj, conv_w[l], conv_b[l], conv_ln_g[l], conv_ln_b[l])
        yd = _swa(proj, sinks[l], tab_d)

        x, h_ffn = _merge(ya, yb, yc, yd, proj, x, gt_m, mix_norm_post[l], ffn_norm_pre[l], sc_f, sh_f,
                          wa, wb, wc, wd, wo, l, tm=tm_small)
        x = _ffn(x, h_ffn, gt_f, ffn_norm_post[l], wg, wu, wdn, l, tm=tm_ffn, tf=512)
    return x
```

```python
import functools
import math

import jax
import jax.numpy as jnp
import numpy as np
from jax import lax
from jax.experimental import pallas as pl
from jax.experimental.pallas import tpu as pltpu

F32 = jnp.float32
BF16 = jnp.bfloat16
HIGHEST = lax.Precision.HIGHEST

HEAD_DIM = 128
A_HEADS = 4
MOBA_BLOCK = 256
MOBA_TOPK = 3
B_HEADS = 4
CONV_CH = 512
CONV_WIDTH = 31
D_Q_HEADS = 8
D_KV_HEADS = 2
D_HEAD_DIM = 64
WINDOW = 128
N_BUCKETS = 32
MAX_DISTANCE = 1024
N_BRANCHES = 4
RMS_EPS = 1e-6
LN_EPS = 1e-5

A_W = A_HEADS * HEAD_DIM
B_W = B_HEADS * HEAD_DIM
D_QW = D_Q_HEADS * D_HEAD_DIM
D_KVW = D_KV_HEADS * D_HEAD_DIM

LANES = 128
SUBLANES = 8
VMEM_LIMIT = 56 * 1024 * 1024

NEG = -1e30
LOG2E = math.log2(math.e)
ATT_G = 4
ATT_T = MOBA_BLOCK
ATT_U = 4

D_MODEL = 2048
GATE_W = N_BRANCHES * D_MODEL
COL_AQ, COL_AK, COL_AV = GATE_W, GATE_W + A_W, GATE_W + 2 * A_W
COL_BQ, COL_BK, COL_BV = COL_AV + A_W, COL_AV + A_W + B_W, COL_AV + A_W + 2 * B_W
COL_CU = COL_BV + B_W
COL_DQ = COL_CU + 2 * CONV_CH
COL_DK = COL_DQ + D_QW
COL_DV = COL_DK + D_KVW
PROJ_W = COL_DV + D_KVW
PROJ_TN = 1024
PROJ_W_PAD = -(-PROJ_W // PROJ_TN) * PROJ_TN
MOBA_FAR = -(-(MAX_DISTANCE + MOBA_BLOCK - 1) // MOBA_BLOCK)
VOID_COL = LANES - 1
assert ATT_G % ATT_U == 0 and (MOBA_FAR - 1) % ATT_U == 0


def _params(sem, vmem=VMEM_LIMIT):
    return pltpu.CompilerParams(dimension_semantics=sem, vmem_limit_bytes=vmem)


def _t5_bucket(dist):
    max_exact = N_BUCKETS // 2
    d = jnp.maximum(dist, 0)
    log_ratio = jnp.log(jnp.maximum(d, 1).astype(jnp.float32) / max_exact) / math.log(MAX_DISTANCE / max_exact)
    large = max_exact + (log_ratio * (N_BUCKETS - max_exact)).astype(jnp.int32)
    large = jnp.minimum(large, N_BUCKETS - 1)
    return jnp.where(d < max_exact, d, large)


def _sigmoid(x):
    return 0.5 * jnp.tanh(0.5 * x) + 0.5


def _rms(y):
    return y * lax.rsqrt(jnp.mean(y * y, axis=-1, keepdims=True) + RMS_EPS)


def _modulated_norm(x, g, sc, sh):
    return (_rms(x) * g) * (1.0 + sc) + sh


MOD_TN = 512
MOD_KC = 256


def _mod_kernel(ct_ref, w_ref, b_ref, o_ref):
    d, nb = ct_ref.shape
    tn = w_ref.shape[1]
    ct = ct_ref[...]
    ca = ct * jax.nn.sigmoid(ct)
    rows = []
    for r in range(nb):
        acc = jnp.zeros((1, tn), F32)
        for kc in range(d // MOD_KC):
            sl = slice(kc * MOD_KC, (kc + 1) * MOD_KC)
            acc = acc + jnp.sum(w_ref[sl, :] * ca[sl, r:r + 1], axis=0, keepdims=True)
        rows.append(acc)
    o_ref[...] = jnp.concatenate(rows, axis=0) + b_ref[...]


def _modulation(c, w_mod, b_mod):
    depth, d, n = w_mod.shape
    bsz = c.shape[0]
    return pl.pallas_call(
        _mod_kernel,
        grid=(depth, n // MOD_TN),
        in_specs=[
            pl.BlockSpec((d, bsz), lambda l, j: (0, 0)),
            pl.BlockSpec((None, d, MOD_TN), lambda l, j: (l, 0, j)),
            pl.BlockSpec((None, 1, MOD_TN), lambda l, j: (l, 0, j)),
        ],
        out_specs=pl.BlockSpec((None, bsz, MOD_TN), lambda l, j: (l, 0, j)),
        out_shape=jax.ShapeDtypeStruct((depth, bsz, n), F32),
        compiler_params=_params(("parallel", "parallel")),
        name="mod",
    )(c.T, w_mod, b_mod.reshape(depth, 1, n))


def _bias_table_kernel(rb_ref, bucket_ref, o_ref, *, scale):
    h = pl.program_id(0)
    b = bucket_ref[...]
    acc = jnp.where(b < 0, NEG, 0.0).astype(F32)
    for u in range(N_BUCKETS):
        acc = jnp.where(b == u, rb_ref[h, u] * scale, acc)
    o_ref[...] = acc


def _bias_tables(rel_bias_heads, bucket, *, scale):
    nh = rel_bias_heads.shape[1]
    nt, r, c = bucket.shape
    return pl.pallas_call(
        functools.partial(_bias_table_kernel, scale=scale),
        grid=(nh, nt),
        in_specs=[
            pl.BlockSpec(memory_space=pltpu.SMEM),
            pl.BlockSpec((None, r, c), lambda h, t: (t, 0, 0)),
        ],
        out_specs=pl.BlockSpec((None, None, r, c), lambda h, t: (h, t, 0, 0)),
        out_shape=jax.ShapeDtypeStruct((nh, nt, r, c), F32),
        compiler_params=_params(("parallel", "parallel")),
        name="bias_table",
    )(rel_bias_heads.T, bucket)


def _moba_buckets():
    blk = MOBA_BLOCK
    i = jnp.arange(blk)[None, :]
    j = jnp.arange(blk)[:, None]
    tabs = []
    for delta in range(MOBA_FAR):
        dist = delta * blk + i - j
        tabs.append(jnp.where(dist >= 0, _t5_bucket(dist), -1))
    tabs.append(_t5_bucket(jnp.full((blk, blk), MAX_DISTANCE, jnp.int32)))
    return jnp.stack(tabs).astype(jnp.int32)


def _swa_buckets():
    qi = jnp.arange(WINDOW)[None, :]
    kj = jnp.arange(2 * WINDOW)[:, None]
    dist = qi + WINDOW - kj
    in_win = (dist >= 0) & (dist < WINDOW)
    return jnp.where(in_win, _t5_bucket(dist), -1).astype(jnp.int32)[None]


def _norm_proj_kernel(x_ref, g_ref, sc_ref, sh_ref, w_ref, ws_ref, o_ref, os_ref, h_scr):
    j = pl.program_id(2)
    tn = w_ref.shape[0]

    @pl.when(j == 0)
    def _():
        h = _modulated_norm(x_ref[...], g_ref[...], sc_ref[...], sh_ref[...]).astype(BF16)
        h_scr[...] = h
        os_ref[...] = _dot_nt(h, ws_ref[...])

    @pl.when(j < GATE_W // tn)
    def _():
        o_ref[...] = _sigmoid(_dot_nt(h_scr[...], w_ref[...])).astype(o_ref.dtype)

    @pl.when(j >= GATE_W // tn)
    def _():
        o_ref[...] = _dot_nt(h_scr[...], w_ref[...]).astype(o_ref.dtype)


def _norm_proj(x, g, sc, sh, w, layer, *, tm, tn):
    bsz, seq, d = x.shape
    n = PROJ_W_PAD
    ns = LANES
    assert GATE_W % tn == 0 and n % tn == 0 and w.shape[1] == PACK_ROWS
    return pl.pallas_call(
        _norm_proj_kernel,
        grid=(bsz, seq // tm, n // tn),
        in_specs=[
            pl.BlockSpec((None, tm, d), lambda b, i, j: (b, i, 0)),
            pl.BlockSpec((1, d), lambda b, i, j: (0, 0)),
            pl.BlockSpec((None, 1, d), lambda b, i, j: (b, 0, 0)),
            pl.BlockSpec((None, 1, d), lambda b, i, j: (b, 0, 0)),
            pl.BlockSpec((None, tn, d), lambda b, i, j: (layer, j, 0)),
            pl.BlockSpec((None, ns, d), lambda b, i, j: (layer, PACK_SIDE_ROW // ns, 0)),
        ],
        out_specs=[pl.BlockSpec((None, tm, tn), lambda b, i, j: (b, i, j)),
                   pl.BlockSpec((None, tm, ns), lambda b, i, j: (b, i, 0))],
        out_shape=[jax.ShapeDtypeStruct((bsz, seq, n), BF16), jax.ShapeDtypeStruct((bsz, seq, ns), F32)],
        scratch_shapes=[pltpu.VMEM((tm, d), BF16)],
        compiler_params=_params(("parallel", "parallel", "arbitrary")),
        name="norm_proj",
    )(x, g.reshape(1, d), sc, sh, w, w)


def _split3(c):
    hi = c.astype(BF16).astype(F32)
    mid = (c - hi).astype(BF16).astype(F32)
    lo = (c - hi - mid).astype(BF16).astype(F32)
    return hi, mid, lo


def _fox_gate_kernel(x_ref, fb_ref, qa_ref, ka_ref, cum_ref, carry_scr):
    t = x_ref.shape[0]

    @pl.when(pl.program_id(1) == 0)
    def _():
        carry_scr[...] = jnp.zeros_like(carry_scr)

    lf = jax.nn.log_sigmoid(x_ref[...] + fb_ref[...])
    tri = (lax.broadcasted_iota(jnp.int32, (t, t), 0) >= lax.broadcasted_iota(jnp.int32, (t, t), 1)).astype(F32)
    c = jnp.dot(tri, lf, precision=HIGHEST, preferred_element_type=F32) + carry_scr[...]
    carry_scr[...] = c[t - 1:t, :]
    cum_ref[...] = c * LOG2E
    lane = lax.broadcasted_iota(jnp.int32, (t, LANES), 1)
    sub = lax.broadcasted_iota(jnp.int32, (SUBLANES, t), 0)
    ct = c.T * LOG2E
    for h in range(B_HEADS):
        hi, mid, lo = _split3(jnp.broadcast_to(c[:, h:h + 1], (t, LANES)) * LOG2E)
        ka = jnp.where(lane == 0, -hi, jnp.where(lane == 1, -mid, jnp.where(lane == 2, -lo, jnp.where(lane < 6, 1.0, 0.0))))
        ka_ref[h] = ka.astype(BF16)
        hi, mid, lo = _split3(jnp.broadcast_to(ct[h:h + 1, :], (SUBLANES, t)))
        qa = jnp.where(sub < 3, 1.0, jnp.where(sub == 3, hi, jnp.where(sub == 4, mid, jnp.where(sub == 5, lo, 0.0))))
        qa_ref[h] = jnp.concatenate([qa, jnp.zeros((LANES - SUBLANES, t), F32)], axis=0).astype(BF16)


def _fox_gate(raw, fox_b, *, t):
    bsz, seq, _ = raw.shape
    fb = jnp.zeros((1, LANES), F32).at[0, :B_HEADS].set(fox_b)
    aug = pl.BlockSpec((None, B_HEADS, t, LANES), lambda b, i: (b, 0, i, 0))
    aug_t = pl.BlockSpec((None, B_HEADS, LANES, t), lambda b, i: (b, 0, 0, i))
    return pl.pallas_call(
        _fox_gate_kernel,
        grid=(bsz, seq // t),
        in_specs=[
            pl.BlockSpec((None, t, LANES), lambda b, i: (b, i, 0)),
            pl.BlockSpec((1, LANES), lambda b, i: (0, 0)),
        ],
        out_specs=[aug_t, aug, pl.BlockSpec((None, t, LANES), lambda b, i: (b, i, 0))],
        out_shape=[jax.ShapeDtypeStruct((bsz, B_HEADS, LANES, seq), BF16),
                   jax.ShapeDtypeStruct((bsz, B_HEADS, seq, LANES), BF16),
                   jax.ShapeDtypeStruct((bsz, seq, LANES), F32)],
        scratch_shapes=[pltpu.VMEM((1, LANES), F32)],
        compiler_params=_params(("parallel", "arbitrary")),
        name="fox_gate",
    )(raw, fb)


ONES_ROWS = 16


def _softmax_step(zt, vt, carry):
    m, acc = carry
    m_new = jnp.maximum(m, jnp.max(zt, axis=0, keepdims=True))
    p = jnp.exp2(zt - m_new)
    acc = jnp.exp2(m - m_new) * acc + jnp.dot(vt, p.astype(BF16), preferred_element_type=F32)
    return m_new, acc


def _softmax_init(tq, dh):
    return (jnp.full((1, tq), NEG, F32), jnp.zeros((dh + ONES_ROWS, tq), F32))


def _softmax_finish(carry, dh):
    _, acc = carry
    return (acc[:dh, :] / acc[dh:dh + 1, :]).T


def _fill_vt(vt_scr, v_ref, chunk):
    seq, dh = v_ref.shape
    for cix in range(seq // chunk):
        sl = slice(cix * chunk, (cix + 1) * chunk)
        vt_scr[:dh, sl] = _transpose_bf16(v_ref[sl, :])
    vt_scr[dh:, :] = jnp.ones((ONES_ROWS, seq), BF16)


ATT_LOOKAHEAD = 6


def _run_substeps(steps, carry):
    carry = list(carry)
    pending = {}
    for s in range(min(ATT_LOOKAHEAD, len(steps))):
        pending[s] = steps[s][1]()
    for s, (j, _, fix, vt) in enumerate(steps):
        if s + ATT_LOOKAHEAD < len(steps):
            pending[s + ATT_LOOKAHEAD] = steps[s + ATT_LOOKAHEAD][1]()
        z, cj = pending.pop(s), carry[j]
        if fix is not None:
            z, cj = fix(z, cj)
        carry[j] = _softmax_step(z, vt, cj)
    return carry


def _dense_loop(start, n_blocks, make_steps, carry):
    big = 2 * ATT_U
    carry = lax.fori_loop(0, n_blocks // big,
                          lambda grp, c: tuple(_run_substeps(make_steps(start + grp * big, big), c)), carry)
    first = start + (n_blocks // big) * big
    return lax.fori_loop(0, (start + n_blocks - first) // ATT_U,
                         lambda grp, c: tuple(_run_substeps(make_steps(first + grp * ATT_U, ATT_U), c)), carry)


def _transpose_bf16(x):
    return x.astype(F32).T.astype(BF16)


_NT = (((1,), (1,)), ((), ()))


def _dot_nt(a, b):
    return lax.dot_general(a, b, _NT, preferred_element_type=F32)


def _moba_kernel(cfar_ref, q_ref, k_ref, v_ref, tab_ref, o_ref, kmean_scr, sc_scr, qt_scr, vt_scr):
    g, t = ATT_G, ATT_T
    seq, dh = k_ref.shape
    nblk = seq // t
    rows = g * t
    i = pl.program_id(2)
    base = i * g

    @pl.when(i == 0)
    def _():
        r = lax.broadcasted_iota(jnp.int32, (nblk, seq), 0)
        c = lax.broadcasted_iota(jnp.int32, (nblk, seq), 1)
        avg = jnp.where((c >= r * t) & (c < (r + 1) * t), 1.0 / t, 0.0).astype(BF16)
        kmean_scr[...] = jnp.dot(avg, k_ref[...], preferred_element_type=F32)
        _fill_vt(vt_scr, v_ref, rows)

    q = q_ref[...]
    qt_scr[:dh, :] = _transpose_bf16(q)
    sc = lax.dot_general(kmean_scr[...], q.astype(F32), _NT, precision=HIGHEST, preferred_element_type=F32)
    row = lax.broadcasted_iota(jnp.int32, (nblk, rows), 0)
    own = base + lax.broadcasted_iota(jnp.int32, (nblk, rows), 1) // t
    past = row < own
    sc = jnp.where(past, sc, -jnp.inf)
    sc_scr[...] = sc

    def rank_body(m, rank):
        sm = sc_scr[pl.ds(m, 1), :]
        return rank + jnp.where(sm > sc, 1.0, jnp.where(sm == sc, jnp.where(row > m, 1.0, 0.0), 0.0))

    rank = lax.fori_loop(0, base + g - 1, rank_body, jnp.zeros((nblk, rows), F32))
    pen = jnp.where(past, jnp.where(rank < MOBA_TOPK, 0.0, NEG), jnp.where(row == own, 0.0, NEG))
    void = jnp.where(lax.broadcasted_iota(jnp.int32, (SUBLANES, rows), 0) == SUBLANES - 1, NEG, 0.0)
    pen = jnp.concatenate([pen, jnp.zeros((LANES - SUBLANES - nblk, rows), F32), void], axis=0)
    qt_scr[dh:, :] = pen.astype(BF16)

    lane = lax.broadcasted_iota(jnp.int32, (t, LANES), 1)

    def kv(n, col):
        start = pl.multiple_of(n * t, t)
        onehot = jnp.where(lane == col, 1.0, 0.0).astype(BF16)
        return jnp.concatenate([k_ref[pl.ds(start, t), :], onehot], axis=1), vt_scr[:, pl.ds(start, t)]

    def logits(kb, j):
        return jnp.dot(kb, qt_scr[:, j * t:(j + 1) * t], preferred_element_type=F32)

    def far_steps(first, count):
        steps = []
        for u in range(count):
            kb, vb = kv(first + u, first + u)
            steps += [(j, functools.partial(logits, kb, j), None, vb) for j in range(g)]
        return steps

    carry = _dense_loop(0, jnp.maximum(base - (MOBA_FAR - 1), 0), far_steps,
                        tuple(_softmax_init(t, dh) for _ in range(g)))
    cfar = cfar_ref[pl.program_id(1)]

    def near_fix(delta, z, cj):
        if delta == MOBA_FAR - 1:
            m, acc = cj
            cj = (m + cfar, acc)
        return tab_ref[delta] + z, cj

    steps = []
    for k in range(-(MOBA_FAR - 1), g):
        n = base + k
        if k < 0:
            kb, vb = kv(jnp.maximum(n, 0), jnp.where(n >= 0, n, VOID_COL))
        else:
            kb, vb = kv(n, n)
        for j in range(max(k, 0), g):
            fix = functools.partial(near_fix, j - k) if j - k < MOBA_FAR else None
            steps.append((j, functools.partial(logits, kb, j), fix, vb))
    carry = _run_substeps(steps, carry)
    for j in range(g):
        o_ref[j * t:(j + 1) * t, :] = _softmax_finish(carry[j], dh).astype(o_ref.dtype)


def _moba(proj, tab):
    bsz, seq, _ = proj.shape
    g, t = ATT_G, ATT_T
    nblk = seq // t
    assert seq % (g * t) == 0 and nblk <= LANES - SUBLANES and nblk % SUBLANES == 0
    qc, kc, vc = COL_AQ // HEAD_DIM, COL_AK // HEAD_DIM, COL_AV // HEAD_DIM
    cfar = tab[:, MOBA_FAR, 0, 0]
    return pl.pallas_call(
        _moba_kernel,
        grid=(bsz, A_HEADS, nblk // g),
        in_specs=[
            pl.BlockSpec(memory_space=pltpu.SMEM),
            pl.BlockSpec((None, g * t, HEAD_DIM), lambda b, h, i: (b, i, qc + h)),
            pl.BlockSpec((None, seq, HEAD_DIM), lambda b, h, i: (b, 0, kc + h)),
            pl.BlockSpec((None, seq, HEAD_DIM), lambda b, h, i: (b, 0, vc + h)),
            pl.BlockSpec((None, MOBA_FAR, t, t), lambda b, h, i: (h, 0, 0, 0)),
        ],
        out_specs=pl.BlockSpec((None, g * t, HEAD_DIM), lambda b, h, i: (b, i, h)),
        out_shape=jax.ShapeDtypeStruct((bsz, seq, A_W), BF16),
        scratch_shapes=[pltpu.VMEM((nblk, HEAD_DIM), F32), pltpu.VMEM((nblk, g * t), F32),
                        pltpu.VMEM((HEAD_DIM + LANES, g * t), BF16), pltpu.VMEM((HEAD_DIM + ONES_ROWS, seq), BF16)],
        compiler_params=_params(("parallel", "parallel", "arbitrary")),
        name="moba",
    )(cfar, proj, proj, proj, tab)


def _fox_kernel(start_ref, q_ref, qa_ref, k_ref, ka_ref, v_ref, o_ref, qt_scr, vt_scr):
    g, t = ATT_G, ATT_T
    rows = g * t
    seq, dh = k_ref.shape
    base = pl.program_id(2) * g
    first = start_ref[(pl.program_id(0) * pl.num_programs(1) + pl.program_id(1)) * pl.num_programs(2)
                      + pl.program_id(2)]

    @pl.when(base == 0)
    def _():
        _fill_vt(vt_scr, v_ref, rows)

    qt_scr[:dh, :] = _transpose_bf16(q_ref[...])
    qt_scr[dh:, :] = qa_ref[...]

    def kv(n):
        start = pl.multiple_of(n * t, t)
        return (jnp.concatenate([k_ref[pl.ds(start, t), :], ka_ref[pl.ds(start, t), :]], axis=1),
                vt_scr[:, pl.ds(start, t)])

    def logits(kb, j):
        return jnp.dot(kb, qt_scr[:, j * t:(j + 1) * t], preferred_element_type=F32)

    def dense_steps(first, count):
        steps = []
        for u in range(count):
            kb, vb = kv(first + u)
            steps += [(j, functools.partial(logits, kb, j), None, vb) for j in range(g)]
        return steps

    carry = _dense_loop(first, base - first, dense_steps, tuple(_softmax_init(t, dh) for _ in range(g)))
    causal = lax.broadcasted_iota(jnp.int32, (t, t), 0) <= lax.broadcasted_iota(jnp.int32, (t, t), 1)

    def diag_fix(z, cj):
        return jnp.where(causal, z, NEG), cj

    steps = []
    for k in range(g):
        kb, vb = kv(base + k)
        steps += [(j, functools.partial(logits, kb, j), diag_fix if j == k else None, vb) for j in range(k, g)]
    carry = _run_substeps(steps, carry)
    for j in range(g):
        o_ref[j * t:(j + 1) * t, :] = _softmax_finish(carry[j], dh).astype(o_ref.dtype)


FOX_SKIP_BITS = 200.0
FOX_NORM_SLACK = 1.0 + 2.0 ** -6


def _fox_bounds_kernel(q_ref, k_ref, cum_ref, o_ref, krun_scr, kpref_scr, clast_scr):
    i = pl.program_id(1)
    rows, width = q_ref.shape
    g = rows // ATT_T

    @pl.when(i == 0)
    def _():
        krun_scr[...] = jnp.zeros_like(krun_scr)
        kpref_scr[...] = jnp.zeros_like(kpref_scr)
        clast_scr[...] = jnp.zeros_like(clast_scr)

    head_of = lax.broadcasted_iota(jnp.int32, (width, LANES), 0) // HEAD_DIM
    head_sum = jnp.where(head_of == lax.broadcasted_iota(jnp.int32, (width, LANES), 1), 1.0, 0.0).astype(BF16)

    def max_norm(ref):
        x = ref[...].astype(F32)
        n2 = jnp.dot((x * x).astype(BF16), head_sum, preferred_element_type=F32)
        return jnp.sqrt(jnp.max(n2, axis=0, keepdims=True)) * FOX_NORM_SLACK

    qn, kn = max_norm(q_ref), max_norm(k_ref)
    c_first, c_last = cum_ref[0:1, :], cum_ref[rows - 1:rows, :]
    tile = lax.broadcasted_iota(jnp.int32, kpref_scr.shape, 0)
    ub = qn * kpref_scr[...] + (c_first - clast_scr[...]) + qn * kn
    ok = (ub < -FOX_SKIP_BITS) & (tile < i)
    o_ref[...] = jnp.max(jnp.where(ok, (tile + 1) * g, 0), axis=0, keepdims=True)
    krun = jnp.maximum(krun_scr[...], kn)
    krun_scr[...] = krun
    kpref_scr[pl.ds(i, 1), :] = krun
    clast_scr[pl.ds(i, 1), :] = c_last


def _fox_first_block(proj, cum2):
    bsz, seq, _ = proj.shape
    rows = ATT_G * ATT_T
    ntile = seq // rows
    hist = pltpu.VMEM((-(-ntile // SUBLANES) * SUBLANES, LANES), F32)
    first = pl.pallas_call(
        _fox_bounds_kernel,
        grid=(bsz, ntile),
        in_specs=[
            pl.BlockSpec((None, rows, B_W), lambda b, i: (b, i, COL_BQ // B_W)),
            pl.BlockSpec((None, rows, B_W), lambda b, i: (b, i, COL_BK // B_W)),
            pl.BlockSpec((None, rows, LANES), lambda b, i: (b, i, 0)),
        ],
        out_specs=pl.BlockSpec((None, None, 1, LANES), lambda b, i: (b, i, 0, 0)),
        out_shape=jax.ShapeDtypeStruct((bsz, ntile, 1, LANES), jnp.int32),
        scratch_shapes=[pltpu.VMEM((1, LANES), F32), hist, hist],
        compiler_params=_params(("parallel", "arbitrary")),
        name="fox_bounds",
    )(proj, proj, cum2)
    return first[:, :, 0, :B_HEADS].transpose(0, 2, 1).reshape(-1)


def _fox(proj, qa, ka, first):
    bsz, seq, _ = proj.shape
    g, t = ATT_G, ATT_T
    assert seq % (g * t) == 0
    qc, kc, vc = COL_BQ // HEAD_DIM, COL_BK // HEAD_DIM, COL_BV // HEAD_DIM
    return pl.pallas_call(
        _fox_kernel,
        grid_spec=pltpu.PrefetchScalarGridSpec(
            num_scalar_prefetch=1,
            grid=(bsz, B_HEADS, seq // (g * t)),
            in_specs=[
                pl.BlockSpec((None, g * t, HEAD_DIM), lambda b, h, i, s: (b, i, qc + h)),
                pl.BlockSpec((None, None, LANES, g * t), lambda b, h, i, s: (b, h, 0, i)),
                pl.BlockSpec((None, seq, HEAD_DIM), lambda b, h, i, s: (b, 0, kc + h)),
                pl.BlockSpec((None, None, seq, LANES), lambda b, h, i, s: (b, h, 0, 0)),
                pl.BlockSpec((None, seq, HEAD_DIM), lambda b, h, i, s: (b, 0, vc + h)),
            ],
            out_specs=pl.BlockSpec((None, g * t, HEAD_DIM), lambda b, h, i, s: (b, i, h)),
            scratch_shapes=[pltpu.VMEM((HEAD_DIM + LANES, g * t), BF16),
                            pltpu.VMEM((HEAD_DIM + ONES_ROWS, seq), BF16)],
        ),
        out_shape=jax.ShapeDtypeStruct((bsz, seq, B_W), BF16),
        compiler_params=_params(("parallel", "parallel", "arbitrary")),
        name="fox",
    )(first, proj, qa, proj, ka, proj)


SWA_BLOCKS = 4


def _swa_kernel(sink_ref, q_ref, kp_ref, kc_ref, vp_ref, vc_ref, tab_ref, o_ref):
    w = WINDOW
    dh = D_HEAD_DIM
    grp = D_Q_HEADS // D_KV_HEADS
    i = pl.program_id(1)
    nblk = q_ref.shape[0] // w
    span = (nblk + 1) * w
    kall = jnp.concatenate([kp_ref[...], kc_ref[...]], axis=0)
    vt = jnp.concatenate([_transpose_bf16(vp_ref[...]), _transpose_bf16(vc_ref[...])], axis=1)
    ones = jnp.ones((ONES_ROWS, span), BF16)
    kg = [kall[:, g * dh:(g + 1) * dh] for g in range(D_KV_HEADS)]
    vg = [jnp.concatenate([vt[g * dh:(g + 1) * dh, :], ones], axis=0) for g in range(D_KV_HEADS)]
    qt = _transpose_bf16(q_ref[...])
    key = lax.broadcasted_iota(jnp.int32, (2 * w, w), 0)
    first = key >= jnp.where(i > 0, 0, w)

    def block_logits(blk):
        return [jnp.dot(kg[h // grp][blk * w:(blk + 2) * w, :], qt[h * dh:(h + 1) * dh, blk * w:(blk + 1) * w],
                        preferred_element_type=F32) for h in range(D_Q_HEADS)]

    logits = block_logits(0)
    for blk in range(nblk):
        nxt = block_logits(blk + 1) if blk + 1 < nblk else None
        outs = []
        for h in range(D_Q_HEADS):
            s = tab_ref[h] + logits[h]
            if blk == 0:
                s = jnp.where(first, s, NEG)
            sink = sink_ref[h]
            m = jnp.maximum(jnp.max(s, axis=0, keepdims=True), sink)
            p = jnp.exp(s - m)
            acc = jnp.dot(vg[h // grp][:, blk * w:(blk + 2) * w], p.astype(BF16), preferred_element_type=F32)
            den = acc[dh:dh + 1, :] + jnp.exp(sink - m)
            outs.append(acc[:dh, :] / den)
        o_ref[blk * w:(blk + 1) * w, :] = jnp.concatenate(outs, axis=0).T.astype(o_ref.dtype)
        logits = nxt


def _swa(proj, sinks, tab):
    bsz, seq, _ = proj.shape
    w = WINDOW
    nb = SWA_BLOCKS
    assert seq % (nb * w) == 0
    qc, kc, vc = COL_DQ // D_QW, COL_DK // D_KVW, COL_DV // D_KVW
    prev = lambda i: jnp.maximum(i * nb - 1, 0)
    return pl.pallas_call(
        _swa_kernel,
        grid=(bsz, seq // (nb * w)),
        in_specs=[
            pl.BlockSpec(memory_space=pltpu.SMEM),
            pl.BlockSpec((None, nb * w, D_QW), lambda b, i: (b, i, qc)),
            pl.BlockSpec((None, w, D_KVW), lambda b, i: (b, prev(i), kc)),
            pl.BlockSpec((None, nb * w, D_KVW), lambda b, i: (b, i, kc)),
            pl.BlockSpec((None, w, D_KVW), lambda b, i: (b, prev(i), vc)),
            pl.BlockSpec((None, nb * w, D_KVW), lambda b, i: (b, i, vc)),
            pl.BlockSpec((D_Q_HEADS, 2 * w, w), lambda b, i: (0, 0, 0)),
        ],
        out_specs=pl.BlockSpec((None, nb * w, D_QW), lambda b, i: (b, i, 0)),
        out_shape=jax.ShapeDtypeStruct((bsz, seq, D_QW), BF16),
        compiler_params=_params(("parallel", "parallel")),
        name="swa",
    )(sinks, proj, proj, proj, proj, proj, tab)


CONV_T = 512
CONV_HALO = 32
CONV_ROWS = 64


def _conv_kernel(a_ref, g_ref, ha_ref, hg_ref, w_ref, b_ref, lg_ref, lb_ref, o_ref, buf):
    t = a_ref.shape[0]
    halo = CONV_HALO
    hp = ha_ref[...].astype(F32) * jax.nn.sigmoid(hg_ref[...].astype(F32))
    buf[0:halo, :] = jnp.where(pl.program_id(1) > 0, hp, 0.0)
    buf[halo:halo + t, :] = a_ref[...].astype(F32) * jax.nn.sigmoid(g_ref[...].astype(F32))
    off = halo - (CONV_WIDTH - 1)
    for r in range(t // CONV_ROWS):
        acc = jnp.broadcast_to(b_ref[...], (CONV_ROWS, CONV_CH))
        for rho in range(SUBLANES):
            rows = CONV_ROWS if rho == 0 else CONV_ROWS + SUBLANES
            part = None
            for s in range(off, off + CONV_WIDTH):
                if s % SUBLANES != rho:
                    continue
                start = r * CONV_ROWS + s - rho
                term = w_ref[s - off:s - off + 1, :] * buf[start:start + rows, :]
                part = term if part is None else part + term
            acc = acc + part[rho:rho + CONV_ROWS, :]
        mu = jnp.mean(acc, axis=-1, keepdims=True)
        d = acc - mu
        var = jnp.mean(d * d, axis=-1, keepdims=True)
        y = d * lax.rsqrt(var + LN_EPS) * lg_ref[...] + lb_ref[...]
        o_ref[r * CONV_ROWS:(r + 1) * CONV_ROWS, :] = (y * jax.nn.sigmoid(y)).astype(o_ref.dtype)


def _conv(proj, conv_w, conv_b, ln_g, ln_b):
    bsz, seq, _ = proj.shape
    t = min(CONV_T, seq)
    ca, cg = COL_CU // CONV_CH, COL_CU // CONV_CH + 1
    hb = t // CONV_HALO
    prev = lambda i: jnp.maximum(i * hb - 1, 0)
    row = lambda v: v.reshape(1, CONV_CH)
    return pl.pallas_call(
        _conv_kernel,
        grid=(bsz, seq // t),
        in_specs=[
            pl.BlockSpec((None, t, CONV_CH), lambda b, i: (b, i, ca)),
            pl.BlockSpec((None, t, CONV_CH), lambda b, i: (b, i, cg)),
            pl.BlockSpec((None, CONV_HALO, CONV_CH), lambda b, i: (b, prev(i), ca)),
            pl.BlockSpec((None, CONV_HALO, CONV_CH), lambda b, i: (b, prev(i), cg)),
            pl.BlockSpec((CONV_WIDTH, CONV_CH), lambda b, i: (0, 0)),
            pl.BlockSpec((1, CONV_CH), lambda b, i: (0, 0)),
            pl.BlockSpec((1, CONV_CH), lambda b, i: (0, 0)),
            pl.BlockSpec((1, CONV_CH), lambda b, i: (0, 0)),
        ],
        out_specs=pl.BlockSpec((None, t, CONV_CH), lambda b, i: (b, i, 0)),
        out_shape=jax.ShapeDtypeStruct((bsz, seq, CONV_CH), BF16),
        scratch_shapes=[pltpu.VMEM((CONV_HALO + t, CONV_CH), F32)],
        compiler_params=_params(("parallel", "parallel")),
        name="conv",
    )(proj, proj, proj, proj, conv_w, row(conv_b), row(ln_g), row(ln_b))


def _merge_kernel(ya_ref, yb_ref, yc_ref, yd_ref, gates_ref, x_ref, gt_ref, gp_ref, gf_ref, scf_ref, shf_ref,
                  wa_ref, wb_ref, wc_ref, wd_ref, wo_ref, o_ref, h_ref):
    d = x_ref.shape[1]
    y = None
    for i, (br, w) in enumerate(((ya_ref, wa_ref), (yb_ref, wb_ref), (yc_ref, wc_ref), (yd_ref, wd_ref))):
        t = jnp.dot(br[...], w[...], preferred_element_type=F32) * gates_ref[:, i * d:(i + 1) * d].astype(F32)
        y = t if y is None else y + t
    z = jnp.dot(y.astype(BF16), wo_ref[...], preferred_element_type=F32)
    x_new = x_ref[...] + gt_ref[...] * (_rms(z) * gp_ref[...])
    o_ref[...] = x_new
    h_ref[...] = _modulated_norm(x_new, gf_ref[...], scf_ref[...], shf_ref[...]).astype(h_ref.dtype)


def _merge(ya, yb, yc, yd, gates, x, gt, g_post, g_ffn, sc_f, sh_f, wa, wb, wc, wd, wo, layer, *, tm):
    bsz, seq, d = x.shape
    tok = lambda width: pl.BlockSpec((None, tm, width), lambda b, i: (b, i, 0))
    vec = lambda: pl.BlockSpec((None, 1, d), lambda b, i: (b, 0, 0))
    par = lambda: pl.BlockSpec((1, d), lambda b, i: (0, 0))
    full = lambda a: pl.BlockSpec((None,) + a.shape[1:], lambda b, i: (layer, 0, 0),
                                  pipeline_mode=pl.Buffered(1))
    return pl.pallas_call(
        _merge_kernel,
        grid=(bsz, seq // tm),
        in_specs=[tok(A_W), tok(B_W), tok(CONV_CH), tok(D_QW), tok(N_BRANCHES * d), tok(d),
                  vec(), par(), par(), vec(), vec(),
                  full(wa), full(wb), full(wc), full(wd), full(wo)],
        out_specs=[tok(d), tok(d)],
        out_shape=[jax.ShapeDtypeStruct((bsz, seq, d), F32), jax.ShapeDtypeStruct((bsz, seq, d), BF16)],
        compiler_params=_params(("parallel", "parallel")),
        name="merge",
    )(ya, yb, yc, yd, gates, x, gt, g_post.reshape(1, d), g_ffn.reshape(1, d), sc_f, sh_f, wa, wb, wc, wd, wo)


def _ffn_kernel(x_ref, h_ref, gt_ref, gp_ref, wg_ref, wu_ref, wd_ref, o_ref, acc_scr):
    f = pl.program_id(2)

    @pl.when(f == 0)
    def _():
        acc_scr[...] = jnp.zeros_like(acc_scr)

    h = h_ref[...]
    g = jnp.dot(h, wg_ref[...], preferred_element_type=F32)
    u = jnp.dot(h, wu_ref[...], preferred_element_type=F32)
    a = ((g * jax.nn.sigmoid(g)) * u).astype(BF16)
    acc_scr[...] += jnp.dot(a, wd_ref[...], preferred_element_type=F32)

    @pl.when(f == pl.num_programs(2) - 1)
    def _():
        o_ref[...] = x_ref[...] + gt_ref[...] * (_rms(acc_scr[...]) * gp_ref[...])


def _ffn(x, h, gt, g_post, wg, wu, wd, layer, *, tm, tf):
    bsz, seq, d = x.shape
    dff = wg.shape[2]
    vec = lambda: pl.BlockSpec((None, 1, d), lambda b, i, f: (b, 0, 0))
    par = lambda: pl.BlockSpec((1, d), lambda b, i, f: (0, 0))
    return pl.pallas_call(
        _ffn_kernel,
        grid=(bsz, seq // tm, dff // tf),
        in_specs=[
            pl.BlockSpec((None, tm, d), lambda b, i, f: (b, i, 0)),
            pl.BlockSpec((None, tm, d), lambda b, i, f: (b, i, 0)),
            vec(), par(),
            pl.BlockSpec((None, d, tf), lambda b, i, f: (layer, 0, f)),
            pl.BlockSpec((None, d, tf), lambda b, i, f: (layer, 0, f)),
            pl.BlockSpec((None, tf, d), lambda b, i, f: (layer, f, 0)),
        ],
        out_specs=pl.BlockSpec((None, tm, d), lambda b, i, f: (b, i, 0)),
        out_shape=jax.ShapeDtypeStruct((bsz, seq, d), F32),
        scratch_shapes=[pltpu.VMEM((tm, d), F32)],
        compiler_params=_params(("parallel", "parallel", "arbitrary")),
        name="ffn",
    )(x, h, gt, g_post.reshape(1, d), wg, wu, wd)


SRC_QKV = 3 * A_W + 3 * B_W
SRC_REST = SRC_QKV + B_HEADS
SRC_GATE = SRC_REST + (PROJ_W - GATE_W - SRC_QKV)
IN_W = SRC_GATE + GATE_W
PACK_R = 256
PACK_SIDE_ROW = PROJ_W_PAD
PACK_ROWS = PROJ_W_PAD + PACK_R
assert GATE_W % PACK_R == 0 and SRC_QKV % PACK_R == 0 and PROJ_W % PACK_R == 0 and PROJ_W_PAD % PACK_R == 0


def _pack_tables():
    s2 = HEAD_DIM ** -0.5 * LOG2E
    src, scale, rows = [], [], []
    for b in range(PACK_ROWS // PACK_R):
        row = b * PACK_R
        if row < GATE_W:
            src.append(SRC_GATE + row); scale.append(1.0); rows.append(PACK_R)
        elif row < GATE_W + SRC_QKV:
            s = row - GATE_W
            is_q = s < A_W or 3 * A_W <= s < 3 * A_W + B_W
            src.append(s); scale.append(s2 if is_q else 1.0); rows.append(PACK_R)
        elif row < PROJ_W:
            s = row - GATE_W - SRC_QKV
            is_q = 2 * CONV_CH <= s < 2 * CONV_CH + D_QW
            src.append(SRC_REST + s); scale.append(D_HEAD_DIM ** -0.5 if is_q else 1.0); rows.append(PACK_R)
        elif row == PACK_SIDE_ROW:
            src.append(SRC_QKV); scale.append(1.0); rows.append(B_HEADS)
        else:
            src.append(0); scale.append(0.0); rows.append(0)
    return np.asarray(src, np.int32), np.asarray(scale, np.float32), np.asarray(rows, np.int32)


def _pack_kernel(src_ref, scale_ref, rows_ref, a_ref, o_ref):
    blk = pl.program_id(0)
    depth, _, d = o_ref.shape
    kt = d // LANES
    keep = lax.broadcasted_iota(jnp.int32, (PACK_R, d), 0) < rows_ref[blk]
    for layer in range(depth):
        cols = [a_ref[pl.ds(t * depth + layer, PACK_R, stride=kt * depth), :] for t in range(kt)]
        o_ref[layer] = jnp.where(keep, jnp.concatenate(cols, axis=1) * scale_ref[blk], 0.0).astype(o_ref.dtype)


def _pack_in_proj(w):
    depth, d, in_w = w.shape
    assert in_w == IN_W and d % LANES == 0
    kt = d // LANES
    view = jnp.transpose(w.reshape(depth, kt, LANES, in_w), (3, 1, 0, 2)).reshape(in_w * kt * depth, LANES)
    src, scale, rows = _pack_tables()
    per_feature = kt * depth
    return pl.pallas_call(
        _pack_kernel,
        grid_spec=pltpu.PrefetchScalarGridSpec(
            num_scalar_prefetch=3,
            grid=(PACK_ROWS // PACK_R,),
            in_specs=[pl.BlockSpec((pl.Element(PACK_R * per_feature), pl.Element(LANES)),
                                   lambda b, src, scale, rows: (src[b] * per_feature, 0))],
            out_specs=pl.BlockSpec((depth, PACK_R, d), lambda b, src, scale, rows: (0, b, 0)),
        ),
        out_shape=jax.ShapeDtypeStruct((depth, PACK_ROWS, d), BF16),
        compiler_params=_params(("parallel",)),
        name="pack_in_proj",
    )(jnp.asarray(src), jnp.asarray(scale), jnp.asarray(rows), view)


CAST_BLOCK_ELEMS = 1 << 20


def _cast_kernel(w_ref, o_ref):
    o_ref[...] = w_ref[...].astype(o_ref.dtype)


def _cast_bf16(w):
    depth, r, c = w.shape
    tr = 16
    while tr * 2 * c <= CAST_BLOCK_ELEMS and r % (tr * 2) == 0:
        tr *= 2
    assert r % tr == 0
    return pl.pallas_call(
        _cast_kernel,
        grid=(depth, r // tr),
        in_specs=[pl.BlockSpec((None, tr, c), lambda l, i: (l, i, 0))],
        out_specs=pl.BlockSpec((None, tr, c), lambda l, i: (l, i, 0)),
        out_shape=jax.ShapeDtypeStruct(w.shape, BF16),
        compiler_params=_params(("parallel", "parallel")),
        name="cast_bf16",
    )(w)


def kernel(x, c, rel_bias, w_mod, b_mod, mix_norm_pre, mix_norm_post, w_in, fox_bias, conv_w, conv_b, conv_ln_g, conv_ln_b, sinks, w_branch_a, w_branch_b, w_branch_c, w_branch_d, w_out, ffn_norm_pre, ffn_norm_post, w_ffn_gate, w_ffn_up, w_ffn_down):
    depth = w_mod.shape[0]
    bsz, seq, d = x.shape
    tm = min(1024, seq)
    tm_small = min(256, seq)
    tm_ffn = min(512, seq)

    mod = _modulation(c, w_mod, b_mod)
    tab_a = _bias_tables(rel_bias[:, :A_HEADS], _moba_buckets(), scale=LOG2E)
    tab_d = _bias_tables(rel_bias[:, A_HEADS:], _swa_buckets(), scale=1.0)[:, 0]

    wa, wb, wc, wd, wo = [_cast_bf16(w) for w in (w_branch_a, w_branch_b, w_branch_c, w_branch_d, w_out)]
    wg, wu, wdn = [_cast_bf16(w) for w in (w_ffn_gate, w_ffn_up, w_ffn_down)]

    w_proj = _pack_in_proj(w_in)

    for l in range(depth):
        sh_m, sc_m, gt_m, sh_f, sc_f, gt_f = [mod[l, :, None, i * d:(i + 1) * d] for i in range(6)]

        proj, fox_raw = _norm_proj(x, mix_norm_pre[l], sc_m, sh_m, w_proj, l, tm=tm, tn=PROJ_TN)
        fox_qa, fox_ka, fox_cum = _fox_gate(fox_raw, fox_bias[l], t=min(512, seq))

        ya = _moba(proj, tab_a)
        yb = _fox(proj, fox_qa, fox_ka, _fox_first_block(proj, fox_cum))
        yc = _conv(proj, conv_w[l], conv_b[l], conv_ln_g[l], conv_ln_b[l])
        yd = _swa(proj, sinks[l], tab_d)

        x, h_ffn = _merge(ya, yb, yc, yd, proj, x, gt_m, mix_norm_post[l], ffn_norm_pre[l], sc_f, sh_f,
                          wa, wb, wc, wd, wo, l, tm=tm_small)
        x = _ffn(x, h_ffn, gt_f, ffn_norm_post[l], wg, wu, wdn, l, tm=tm_ffn, tf=512)
    return x
```

```python
import functools
import math

import jax
import jax.numpy as jnp
import numpy as np
from jax import lax
from jax.experimental import pallas as pl
from jax.experimental.pallas import tpu as pltpu

F32 = jnp.float32
BF16 = jnp.bfloat16
HIGHEST = lax.Precision.HIGHEST

HEAD_DIM = 128
A_HEADS = 4
MOBA_BLOCK = 256
MOBA_TOPK = 3
B_HEADS = 4
CONV_CH = 512
CONV_WIDTH = 31
D_Q_HEADS = 8
D_KV_HEADS = 2
D_HEAD_DIM = 64
WINDOW = 128
N_BUCKETS = 32
MAX_DISTANCE = 1024
N_BRANCHES = 4
RMS_EPS = 1e-6
LN_EPS = 1e-5

A_W = A_HEADS * HEAD_DIM
B_W = B_HEADS * HEAD_DIM
D_QW = D_Q_HEADS * D_HEAD_DIM
D_KVW = D_KV_HEADS * D_HEAD_DIM

LANES = 128
SUBLANES = 8
VMEM_LIMIT = 56 * 1024 * 1024

NEG = -1e30
LOG2E = math.log2(math.e)
ATT_G = 4
ATT_T = MOBA_BLOCK
ATT_U = 4

D_MODEL = 2048
GATE_W = N_BRANCHES * D_MODEL
COL_AQ, COL_AK, COL_AV = GATE_W, GATE_W + A_W, GATE_W + 2 * A_W
COL_BQ, COL_BK, COL_BV = COL_AV + A_W, COL_AV + A_W + B_W, COL_AV + A_W + 2 * B_W
COL_CU = COL_BV + B_W
COL_DQ = COL_CU + 2 * CONV_CH
COL_DK = COL_DQ + D_QW
COL_DV = COL_DK + D_KVW
PROJ_W = COL_DV + D_KVW
PROJ_TN = 1024
PROJ_W_PAD = -(-PROJ_W // PROJ_TN) * PROJ_TN
MOBA_FAR = -(-(MAX_DISTANCE + MOBA_BLOCK - 1) // MOBA_BLOCK)
VOID_COL = LANES - 1
assert ATT_G % ATT_U == 0 and (MOBA_FAR - 1) % ATT_U == 0


def _params(sem, vmem=VMEM_LIMIT):
    return pltpu.CompilerParams(dimension_semantics=sem, vmem_limit_bytes=vmem)


def _t5_bucket(dist):
    max_exact = N_BUCKETS // 2
    d = jnp.maximum(dist, 0)
    log_ratio = jnp.log(jnp.maximum(d, 1).astype(jnp.float32) / max_exact) / math.log(MAX_DISTANCE / max_exact)
    large = max_exact + (log_ratio * (N_BUCKETS - max_exact)).astype(jnp.int32)
    large = jnp.minimum(large, N_BUCKETS - 1)
    return jnp.where(d < max_exact, d, large)


def _sigmoid(x):
    return 0.5 * jnp.tanh(0.5 * x) + 0.5


def _rms(y):
    return y * lax.rsqrt(jnp.mean(y * y, axis=-1, keepdims=True) + RMS_EPS)


def _modulated_norm(x, g, sc, sh):
    return (_rms(x) * g) * (1.0 + sc) + sh


MOD_TN = 512
MOD_KC = 256


def _mod_kernel(ct_ref, w_ref, b_ref, o_ref):
    d, nb = ct_ref.shape
    tn = w_ref.shape[1]
    ct = ct_ref[...]
    ca = ct * jax.nn.sigmoid(ct)
    rows = []
    for r in range(nb):
        acc = jnp.zeros((1, tn), F32)
        for kc in range(d // MOD_KC):
            sl = slice(kc * MOD_KC, (kc + 1) * MOD_KC)
            acc = acc + jnp.sum(w_ref[sl, :] * ca[sl, r:r + 1], axis=0, keepdims=True)
        rows.append(acc)
    o_ref[...] = jnp.concatenate(rows, axis=0) + b_ref[...]


def _modulation(c, w_mod, b_mod):
    depth, d, n = w_mod.shape
    bsz = c.shape[0]
    return pl.pallas_call(
        _mod_kernel,
        grid=(depth, n // MOD_TN),
        in_specs=[
            pl.BlockSpec((d, bsz), lambda l, j: (0, 0)),
            pl.BlockSpec((None, d, MOD_TN), lambda l, j: (l, 0, j)),
            pl.BlockSpec((None, 1, MOD_TN), lambda l, j: (l, 0, j)),
        ],
        out_specs=pl.BlockSpec((None, bsz, MOD_TN), lambda l, j: (l, 0, j)),
        out_shape=jax.ShapeDtypeStruct((depth, bsz, n), F32),
        compiler_params=_params(("parallel", "parallel")),
        name="mod",
    )(c.T, w_mod, b_mod.reshape(depth, 1, n))


def _bias_table_kernel(rb_ref, bucket_ref, o_ref, *, scale):
    h = pl.program_id(0)
    b = bucket_ref[...]
    acc = jnp.where(b < 0, NEG, 0.0).astype(F32)
    for u in range(N_BUCKETS):
        acc = jnp.where(b == u, rb_ref[h, u] * scale, acc)
    o_ref[...] = acc


def _bias_tables(rel_bias_heads, bucket, *, scale):
    nh = rel_bias_heads.shape[1]
    nt, r, c = bucket.shape
    return pl.pallas_call(
        functools.partial(_bias_table_kernel, scale=scale),
        grid=(nh, nt),
        in_specs=[
            pl.BlockSpec(memory_space=pltpu.SMEM),
            pl.BlockSpec((None, r, c), lambda h, t: (t, 0, 0)),
        ],
        out_specs=pl.BlockSpec((None, None, r, c), lambda h, t: (h, t, 0, 0)),
        out_shape=jax.ShapeDtypeStruct((nh, nt, r, c), F32),
        compiler_params=_params(("parallel", "parallel")),
        name="bias_table",
    )(rel_bias_heads.T, bucket)


def _moba_buckets():
    blk = MOBA_BLOCK
    i = jnp.arange(blk)[None, :]
    j = jnp.arange(blk)[:, None]
    tabs = []
    for delta in range(MOBA_FAR):
        dist = delta * blk + i - j
        tabs.append(jnp.where(dist >= 0, _t5_bucket(dist), -1))
    tabs.append(_t5_bucket(jnp.full((blk, blk), MAX_DISTANCE, jnp.int32)))
    return jnp.stack(tabs).astype(jnp.int32)


def _swa_buckets():
    qi = jnp.arange(WINDOW)[None, :]
    kj = jnp.arange(2 * WINDOW)[:, None]
    dist = qi + WINDOW - kj
    in_win = (dist >= 0) & (dist < WINDOW)
    return jnp.where(in_win, _t5_bucket(dist), -1).astype(jnp.int32)[None]


def _norm_proj_kernel(x_ref, g_ref, sc_ref, sh_ref, w_ref, ws_ref, o_ref, os_ref, h_scr):
    j = pl.program_id(2)
    tn = w_ref.shape[0]

    @pl.when(j == 0)
    def _():
        h = _modulated_norm(x_ref[...], g_ref[...], sc_ref[...], sh_ref[...]).astype(BF16)
        h_scr[...] = h
        os_ref[...] = _dot_nt(h, ws_ref[...])

    @pl.when(j < GATE_W // tn)
    def _():
        o_ref[...] = _sigmoid(_dot_nt(h_scr[...], w_ref[...])).astype(o_ref.dtype)

    @pl.when(j >= GATE_W // tn)
    def _():
        o_ref[...] = _dot_nt(h_scr[...], w_ref[...]).astype(o_ref.dtype)


def _norm_proj(x, g, sc, sh, w, layer, *, tm, tn):
    bsz, seq, d = x.shape
    n = PROJ_W_PAD
    ns = LANES
    assert GATE_W % tn == 0 and n % tn == 0 and w.shape[1] == PACK_ROWS
    return pl.pallas_call(
        _norm_proj_kernel,
        grid=(bsz, seq // tm, n // tn),
        in_specs=[
            pl.BlockSpec((None, tm, d), lambda b, i, j: (b, i, 0)),
            pl.BlockSpec((1, d), lambda b, i, j: (0, 0)),
            pl.BlockSpec((None, 1, d), lambda b, i, j: (b, 0, 0)),
            pl.BlockSpec((None, 1, d), lambda b, i, j: (b, 0, 0)),
            pl.BlockSpec((None, tn, d), lambda b, i, j: (layer, j, 0)),
            pl.BlockSpec((None, ns, d), lambda b, i, j: (layer, PACK_SIDE_ROW // ns, 0)),
        ],
        out_specs=[pl.BlockSpec((None, tm, tn), lambda b, i, j: (b, i, j)),
                   pl.BlockSpec((None, tm, ns), lambda b, i, j: (b, i, 0))],
        out_shape=[jax.ShapeDtypeStruct((bsz, seq, n), BF16), jax.ShapeDtypeStruct((bsz, seq, ns), F32)],
        scratch_shapes=[pltpu.VMEM((tm, d), BF16)],
        compiler_params=_params(("parallel", "parallel", "arbitrary")),
        name="norm_proj",
    )(x, g.reshape(1, d), sc, sh, w, w)


def _split3(c):
    hi = c.astype(BF16).astype(F32)
    mid = (c - hi).astype(BF16).astype(F32)
    lo = (c - hi - mid).astype(BF16).astype(F32)
    return hi, mid, lo


def _fox_gate_kernel(x_ref, fb_ref, qa_ref, ka_ref, cum_ref, carry_scr):
    t = x_ref.shape[0]

    @pl.when(pl.program_id(1) == 0)
    def _():
        carry_scr[...] = jnp.zeros_like(carry_scr)

    lf = jax.nn.log_sigmoid(x_ref[...] + fb_ref[...])
    tri = (lax.broadcasted_iota(jnp.int32, (t, t), 0) >= lax.broadcasted_iota(jnp.int32, (t, t), 1)).astype(F32)
    c = jnp.dot(tri, lf, precision=HIGHEST, preferred_element_type=F32) + carry_scr[...]
    carry_scr[...] = c[t - 1:t, :]
    cum_ref[...] = c * LOG2E
    lane = lax.broadcasted_iota(jnp.int32, (t, LANES), 1)
    sub = lax.broadcasted_iota(jnp.int32, (SUBLANES, t), 0)
    ct = c.T * LOG2E
    for h in range(B_HEADS):
        hi, mid, lo = _split3(jnp.broadcast_to(c[:, h:h + 1], (t, LANES)) * LOG2E)
        ka = jnp.where(lane == 0, -hi, jnp.where(lane == 1, -mid, jnp.where(lane == 2, -lo, jnp.where(lane < 6, 1.0, 0.0))))
        ka_ref[h] = ka.astype(BF16)
        hi, mid, lo = _split3(jnp.broadcast_to(ct[h:h + 1, :], (SUBLANES, t)))
        qa = jnp.where(sub < 3, 1.0, jnp.where(sub == 3, hi, jnp.where(sub == 4, mid, jnp.where(sub == 5, lo, 0.0))))
        qa_ref[h] = jnp.concatenate([qa, jnp.zeros((LANES - SUBLANES, t), F32)], axis=0).astype(BF16)


def _fox_gate(raw, fox_b, *, t):
    bsz, seq, _ = raw.shape
    fb = jnp.zeros((1, LANES), F32).at[0, :B_HEADS].set(fox_b)
    aug = pl.BlockSpec((None, B_HEADS, t, LANES), lambda b, i: (b, 0, i, 0))
    aug_t = pl.BlockSpec((None, B_HEADS, LANES, t), lambda b, i: (b, 0, 0, i))
    return pl.pallas_call(
        _fox_gate_kernel,
        grid=(bsz, seq // t),
        in_specs=[
            pl.BlockSpec((None, t, LANES), lambda b, i: (b, i, 0)),
            pl.BlockSpec((1, LANES), lambda b, i: (0, 0)),
        ],
        out_specs=[aug_t, aug, pl.BlockSpec((None, t, LANES), lambda b, i: (b, i, 0))],
        out_shape=[jax.ShapeDtypeStruct((bsz, B_HEADS, LANES, seq), BF16),
                   jax.ShapeDtypeStruct((bsz, B_HEADS, seq, LANES), BF16),
                   jax.ShapeDtypeStruct((bsz, seq, LANES), F32)],
        scratch_shapes=[pltpu.VMEM((1, LANES), F32)],
        compiler_params=_params(("parallel", "arbitrary")),
        name="fox_gate",
    )(raw, fb)


ONES_ROWS = 16


def _softmax_step(zt, vt, carry):
    m, acc = carry
    m_new = jnp.maximum(m, jnp.max(zt, axis=0, keepdims=True))
    p = jnp.exp2(zt - m_new)
    acc = jnp.exp2(m - m_new) * acc + jnp.dot(vt, p.astype(BF16), preferred_element_type=F32)
    return m_new, acc


def _softmax_init(tq, dh):
    return (jnp.full((1, tq), NEG, F32), jnp.zeros((dh + ONES_ROWS, tq), F32))


def _softmax_finish(carry, dh):
    _, acc = carry
    return (acc[:dh, :] / acc[dh:dh + 1, :]).T


def _fill_vt(vt_scr, v_ref, chunk):
    seq, dh = v_ref.shape
    for cix in range(seq // chunk):
        sl = slice(cix * chunk, (cix + 1) * chunk)
        vt_scr[:dh, sl] = _transpose_bf16(v_ref[sl, :])
    vt_scr[dh:, :] = jnp.ones((ONES_ROWS, seq), BF16)


ATT_LOOKAHEAD = 6


def _run_substeps(steps, carry):
    carry = list(carry)
    pending = {}
    for s in range(min(ATT_LOOKAHEAD, len(steps))):
        pending[s] = steps[s][1]()
    for s, (j, _, fix, vt) in enumerate(steps):
        if s + ATT_LOOKAHEAD < len(steps):
            pending[s + ATT_LOOKAHEAD] = steps[s + ATT_LOOKAHEAD][1]()
        z, cj = pending.pop(s), carry[j]
        if fix is not None:
            z, cj = fix(z, cj)
        carry[j] = _softmax_step(z, vt, cj)
    return carry


def _dense_loop(start, n_blocks, make_steps, carry):
    big = 2 * ATT_U
    carry = lax.fori_loop(0, n_blocks // big,
                          lambda grp, c: tuple(_run_substeps(make_steps(start + grp * big, big), c)), carry)
    first = start + (n_blocks // big) * big
    return lax.fori_loop(0, (start + n_blocks - first) // ATT_U,
                         lambda grp, c: tuple(_run_substeps(make_steps(first + grp * ATT_U, ATT_U), c)), carry)


def _transpose_bf16(x):
    return x.astype(F32).T.astype(BF16)


_NT = (((1,), (1,)), ((), ()))


def _dot_nt(a, b):
    return lax.dot_general(a, b, _NT, preferred_element_type=F32)


def _moba_kernel(cfar_ref, q_ref, k_ref, v_ref, tab_ref, o_ref, kmean_scr, sc_scr, qt_scr, vt_scr):
    g, t = ATT_G, ATT_T
    seq, dh = k_ref.shape
    nblk = seq // t
    rows = g * t
    i = pl.program_id(2)
    base = i * g

    @pl.when(i == 0)
    def _():
        r = lax.broadcasted_iota(jnp.int32, (nblk, seq), 0)
        c = lax.broadcasted_iota(jnp.int32, (nblk, seq), 1)
        avg = jnp.where((c >= r * t) & (c < (r + 1) * t), 1.0 / t, 0.0).astype(BF16)
        kmean_scr[...] = jnp.dot(avg, k_ref[...], preferred_element_type=F32)
        _fill_vt(vt_scr, v_ref, rows)

    q = q_ref[...]
    qt_scr[:dh, :] = _transpose_bf16(q)
    sc = lax.dot_general(kmean_scr[...], q.astype(F32), _NT, precision=HIGHEST, preferred_element_type=F32)
    row = lax.broadcasted_iota(jnp.int32, (nblk, rows), 0)
    own = base + lax.broadcasted_iota(jnp.int32, (nblk, rows), 1) // t
    past = row < own
    sc = jnp.where(past, sc, -jnp.inf)
    sc_scr[...] = sc

    def rank_body(m, rank):
        sm = sc_scr[pl.ds(m, 1), :]
        return rank + jnp.where(sm > sc, 1.0, jnp.where(sm == sc, jnp.where(row > m, 1.0, 0.0), 0.0))

    rank = lax.fori_loop(0, base + g - 1, rank_body, jnp.zeros((nblk, rows), F32))
    pen = jnp.where(past, jnp.where(rank < MOBA_TOPK, 0.0, NEG), jnp.where(row == own, 0.0, NEG))
    void = jnp.where(lax.broadcasted_iota(jnp.int32, (SUBLANES, rows), 0) == SUBLANES - 1, NEG, 0.0)
    pen = jnp.concatenate([pen, jnp.zeros((LANES - SUBLANES - nblk, rows), F32), void], axis=0)
    qt_scr[dh:, :] = pen.astype(BF16)

    lane = lax.broadcasted_iota(jnp.int32, (t, LANES), 1)

    def kv(n, col):
        start = pl.multiple_of(n * t, t)
        onehot = jnp.where(lane == col, 1.0, 0.0).astype(BF16)
        return jnp.concatenate([k_ref[pl.ds(start, t), :], onehot], axis=1), vt_scr[:, pl.ds(start, t)]

    def logits(kb, j):
        return jnp.dot(kb, qt_scr[:, j * t:(j + 1) * t], preferred_element_type=F32)

    def far_steps(first, count):
        steps = []
        for u in range(count):
            kb, vb = kv(first + u, first + u)
            steps += [(j, functools.partial(logits, kb, j), None, vb) for j in range(g)]
        return steps

    carry = _dense_loop(0, jnp.maximum(base - (MOBA_FAR - 1), 0), far_steps,
                        tuple(_softmax_init(t, dh) for _ in range(g)))
    cfar = cfar_ref[pl.program_id(1)]

    def near_fix(delta, z, cj):
        if delta == MOBA_FAR - 1:
            m, acc = cj
            cj = (m + cfar, acc)
        return tab_ref[delta] + z, cj

    steps = []
    for k in range(-(MOBA_FAR - 1), g):
        n = base + k
        if k < 0:
            kb, vb = kv(jnp.maximum(n, 0), jnp.where(n >= 0, n, VOID_COL))
        else:
            kb, vb = kv(n, n)
        for j in range(max(k, 0), g):
            fix = functools.partial(near_fix, j - k) if j - k < MOBA_FAR else None
            steps.append((j, functools.partial(logits, kb, j), fix, vb))
    carry = _run_substeps(steps, carry)
    for j in range(g):
        o_ref[j * t:(j + 1) * t, :] = _softmax_finish(carry[j], dh).astype(o_ref.dtype)


def _moba(proj, tab):
    bsz, seq, _ = proj.shape
    g, t = ATT_G, ATT_T
    nblk = seq // t
    assert seq % (g * t) == 0 and nblk <= LANES - SUBLANES and nblk % SUBLANES == 0
    qc, kc, vc = COL_AQ // HEAD_DIM, COL_AK // HEAD_DIM, COL_AV // HEAD_DIM
    cfar = tab[:, MOBA_FAR, 0, 0]
    return pl.pallas_call(
        _moba_kernel,
        grid=(bsz, A_HEADS, nblk // g),
        in_specs=[
            pl.BlockSpec(memory_space=pltpu.SMEM),
            pl.BlockSpec((None, g * t, HEAD_DIM), lambda b, h, i: (b, i, qc + h)),
            pl.BlockSpec((None, seq, HEAD_DIM), lambda b, h, i: (b, 0, kc + h)),
            pl.BlockSpec((None, seq, HEAD_DIM), lambda b, h, i: (b, 0, vc + h)),
            pl.BlockSpec((None, MOBA_FAR, t, t), lambda b, h, i: (h, 0, 0, 0)),
        ],
        out_specs=pl.BlockSpec((None, g * t, HEAD_DIM), lambda b, h, i: (b, i, h)),
        out_shape=jax.ShapeDtypeStruct((bsz, seq, A_W), BF16),
        scratch_shapes=[pltpu.VMEM((nblk, HEAD_DIM), F32), pltpu.VMEM((nblk, g * t), F32),
                        pltpu.VMEM((HEAD_DIM + LANES, g * t), BF16), pltpu.VMEM((HEAD_DIM + ONES_ROWS, seq), BF16)],
        compiler_params=_params(("parallel", "parallel", "arbitrary")),
        name="moba",
    )(cfar, proj, proj, proj, tab)


def _fox_kernel(start_ref, q_ref, qa_ref, k_ref, ka_ref, v_ref, o_ref, qt_scr, vt_scr):
    g, t = ATT_G, ATT_T
    rows = g * t
    seq, dh = k_ref.shape
    base = pl.program_id(2) * g
    first = start_ref[(pl.program_id(0) * pl.num_programs(1) + pl.program_id(1)) * pl.num_programs(2)
                      + pl.program_id(2)]

    @pl.when(base == 0)
    def _():
        _fill_vt(vt_scr, v_ref, rows)

    qt_scr[:dh, :] = _transpose_bf16(q_ref[...])
    qt_scr[dh:, :] = qa_ref[...]

    def kv(n):
        start = pl.multiple_of(n * t, t)
        return (jnp.concatenate([k_ref[pl.ds(start, t), :], ka_ref[pl.ds(start, t), :]], axis=1),
                vt_scr[:, pl.ds(start, t)])

    def logits(kb, j):
        return jnp.dot(kb, qt_scr[:, j * t:(j + 1) * t], preferred_element_type=F32)

    def dense_steps(first, count):
        steps = []
        for u in range(count):
            kb, vb = kv(first + u)
            steps += [(j, functools.partial(logits, kb, j), None, vb) for j in range(g)]
        return steps

    carry = _dense_loop(first, base - first, dense_steps, tuple(_softmax_init(t, dh) for _ in range(g)))
    causal = lax.broadcasted_iota(jnp.int32, (t, t), 0) <= lax.broadcasted_iota(jnp.int32, (t, t), 1)

    def diag_fix(z, cj):
        return jnp.where(causal, z, NEG), cj

    steps = []
    for k in range(g):
        kb, vb = kv(base + k)
        steps += [(j, functools.partial(logits, kb, j), diag_fix if j == k else None, vb) for j in range(k, g)]
    carry = _run_substeps(steps, carry)
    for j in range(g):
        o_ref[j * t:(j + 1) * t, :] = _softmax_finish(carry[j], dh).astype(o_ref.dtype)


FOX_SKIP_BITS = 200.0
FOX_NORM_SLACK = 1.0 + 2.0 ** -6


def _fox_bounds_kernel(q_ref, k_ref, cum_ref, o_ref, krun_scr, kpref_scr, clast_scr):
    i = pl.program_id(1)
    rows, width = q_ref.shape
    g = rows // ATT_T

    @pl.when(i == 0)
    def _():
        krun_scr[...] = jnp.zeros_like(krun_scr)
        kpref_scr[...] = jnp.zeros_like(kpref_scr)
        clast_scr[...] = jnp.zeros_like(clast_scr)

    head_of = lax.broadcasted_iota(jnp.int32, (width, LANES), 0) // HEAD_DIM
    head_sum = jnp.where(head_of == lax.broadcasted_iota(jnp.int32, (width, LANES), 1), 1.0, 0.0).astype(BF16)

    def max_norm(ref):
        x = ref[...].astype(F32)
        n2 = jnp.dot((x * x).astype(BF16), head_sum, preferred_element_type=F32)
        return jnp.sqrt(jnp.max(n2, axis=0, keepdims=True)) * FOX_NORM_SLACK

    qn, kn = max_norm(q_ref), max_norm(k_ref)
    c_first, c_last = cum_ref[0:1, :], cum_ref[rows - 1:rows, :]
    tile = lax.broadcasted_iota(jnp.int32, kpref_scr.shape, 0)
    ub = qn * kpref_scr[...] + (c_first - clast_scr[...]) + qn * kn
    ok = (ub < -FOX_SKIP_BITS) & (tile < i)
    o_ref[...] = jnp.max(jnp.where(ok, (tile + 1) * g, 0), axis=0, keepdims=True)
    krun = jnp.maximum(krun_scr[...], kn)
    krun_scr[...] = krun
    kpref_scr[pl.ds(i, 1), :] = krun
    clast_scr[pl.ds(i, 1), :] = c_last


def _fox_first_block(proj, cum2):
    bsz, seq, _ = proj.shape
    rows = ATT_G * ATT_T
    ntile = seq // rows
    hist = pltpu.VMEM((-(-ntile // SUBLANES) * SUBLANES, LANES), F32)
    first = pl.pallas_call(
        _fox_bounds_kernel,
        grid=(bsz, ntile),
        in_specs=[
            pl.BlockSpec((None, rows, B_W), lambda b, i: (b, i, COL_BQ // B_W)),
            pl.BlockSpec((None, rows, B_W), lambda b, i: (b, i, COL_BK // B_W)),
            pl.BlockSpec((None, rows, LANES), lambda b, i: (b, i, 0)),
        ],
        out_specs=pl.BlockSpec((None, None, 1, LANES), lambda b, i: (b, i, 0, 0)),
        out_shape=jax.ShapeDtypeStruct((bsz, ntile, 1, LANES), jnp.int32),
        scratch_shapes=[pltpu.VMEM((1, LANES), F32), hist, hist],
        compiler_params=_params(("parallel", "arbitrary")),
        name="fox_bounds",
    )(proj, proj, cum2)
    return first[:, :, 0, :B_HEADS].transpose(0, 2, 1).reshape(-1)


def _fox(proj, qa, ka, first):
    bsz, seq, _ = proj.shape
    g, t = ATT_G, ATT_T
    assert seq % (g * t) == 0
    qc, kc, vc = COL_BQ // HEAD_DIM, COL_BK // HEAD_DIM, COL_BV // HEAD_DIM
    return pl.pallas_call(
        _fox_kernel,
        grid_spec=pltpu.PrefetchScalarGridSpec(
            num_scalar_prefetch=1,
            grid=(bsz, B_HEADS, seq // (g * t)),
            in_specs=[
                pl.BlockSpec((None, g * t, HEAD_DIM), lambda b, h, i, s: (b, i, qc + h)),
                pl.BlockSpec((None, None, LANES, g * t), lambda b, h, i, s: (b, h, 0, i)),
                pl.BlockSpec((None, seq, HEAD_DIM), lambda b, h, i, s: (b, 0, kc + h)),
                pl.BlockSpec((None, None, seq, LANES), lambda b, h, i, s: (b, h, 0, 0)),
                pl.BlockSpec((None, seq, HEAD_DIM), lambda b, h, i, s: (b, 0, vc + h)),
            ],
            out_specs=pl.BlockSpec((None, g * t, HEAD_DIM), lambda b, h, i, s: (b, i, h)),
            scratch_shapes=[pltpu.VMEM((HEAD_DIM + LANES, g * t), BF16),
                            pltpu.VMEM((HEAD_DIM + ONES_ROWS, seq), BF16)],
        ),
        out_shape=jax.ShapeDtypeStruct((bsz, seq, B_W), BF16),
        compiler_params=_params(("parallel", "parallel", "arbitrary")),
        name="fox",
    )(first, proj, qa, proj, ka, proj)


SWA_BLOCKS = 4


def _swa_kernel(sink_ref, q_ref, kp_ref, kc_ref, vp_ref, vc_ref, tab_ref, o_ref):
    w = WINDOW
    dh = D_HEAD_DIM
    grp = D_Q_HEADS // D_KV_HEADS
    i = pl.program_id(1)
    nblk = q_ref.shape[0] // w
    span = (nblk + 1) * w
    kall = jnp.concatenate([kp_ref[...], kc_ref[...]], axis=0)
    vt = jnp.concatenate([_transpose_bf16(vp_ref[...]), _transpose_bf16(vc_ref[...])], axis=1)
    ones = jnp.ones((ONES_ROWS, span), BF16)
    kg = [kall[:, g * dh:(g + 1) * dh] for g in range(D_KV_HEADS)]
    vg = [jnp.concatenate([vt[g * dh:(g + 1) * dh, :], ones], axis=0) for g in range(D_KV_HEADS)]
    qt = _transpose_bf16(q_ref[...])
    key = lax.broadcasted_iota(jnp.int32, (2 * w, w), 0)
    first = key >= jnp.where(i > 0, 0, w)

    def block_logits(blk):
        return [jnp.dot(kg[h // grp][blk * w:(blk + 2) * w, :], qt[h * dh:(h + 1) * dh, blk * w:(blk + 1) * w],
                        preferred_element_type=F32) for h in range(D_Q_HEADS)]

    logits = block_logits(0)
    for blk in range(nblk):
        nxt = block_logits(blk + 1) if blk + 1 < nblk else None
        outs = []
        for h in range(D_Q_HEADS):
            s = tab_ref[h] + logits[h]
            if blk == 0:
                s = jnp.where(first, s, NEG)
            sink = sink_ref[h]
            m = jnp.maximum(jnp.max(s, axis=0, keepdims=True), sink)
            p = jnp.exp(s - m)
            acc = jnp.dot(vg[h // grp][:, blk * w:(blk + 2) * w], p.astype(BF16), preferred_element_type=F32)
            den = acc[dh:dh + 1, :] + jnp.exp(sink - m)
            outs.append(acc[:dh, :] / den)
        o_ref[blk * w:(blk + 1) * w, :] = jnp.concatenate(outs, axis=0).T.astype(o_ref.dtype)
        logits = nxt


def _swa(proj, sinks, tab):
    bsz, seq, _ = proj.shape
    w = WINDOW
    nb = SWA_BLOCKS
    assert seq % (nb * w) == 0
    qc, kc, vc = COL_DQ // D_QW, COL_DK // D_KVW, COL_DV // D_KVW
    prev = lambda i: jnp.maximum(i * nb - 1, 0)
    return pl.pallas_call(
        _swa_kernel,
        grid=(bsz, seq // (nb * w)),
        in_specs=[
            pl.BlockSpec(memory_space=pltpu.SMEM),
            pl.BlockSpec((None, nb * w, D_QW), lambda b, i: (b, i, qc)),
            pl.BlockSpec((None, w, D_KVW), lambda b, i: (b, prev(i), kc)),
            pl.BlockSpec((None, nb * w, D_KVW), lambda b, i: (b, i, kc)),
            pl.BlockSpec((None, w, D_KVW), lambda b, i: (b, prev(i), vc)),
            pl.BlockSpec((None, nb * w, D_KVW), lambda b, i: (b, i, vc)),
            pl.BlockSpec((D_Q_HEADS, 2 * w, w), lambda b, i: (0, 0, 0)),
        ],
        out_specs=pl.BlockSpec((None, nb * w, D_QW), lambda b, i: (b, i, 0)),
        out_shape=jax.ShapeDtypeStruct((bsz, seq, D_QW), BF16),
        compiler_params=_params(("parallel", "parallel")),
        name="swa",
    )(sinks, proj, proj, proj, proj, proj, tab)


CONV_T = 512
CONV_HALO = 32
CONV_ROWS = 64


def _conv_kernel(a_ref, g_ref, ha_ref, hg_ref, w_ref, b_ref, lg_ref, lb_ref, o_ref, buf):
    t = a_ref.shape[0]
    halo = CONV_HALO
    hp = ha_ref[...].astype(F32) * jax.nn.sigmoid(hg_ref[...].astype(F32))
    buf[0:halo, :] = jnp.where(pl.program_id(1) > 0, hp, 0.0)
    buf[halo:halo + t, :] = a_ref[...].astype(F32) * jax.nn.sigmoid(g_ref[...].astype(F32))
    off = halo - (CONV_WIDTH - 1)
    for r in range(t // CONV_ROWS):
        acc = jnp.broadcast_to(b_ref[...], (CONV_ROWS, CONV_CH))
        for rho in range(SUBLANES):
            rows = CONV_ROWS if rho == 0 else CONV_ROWS + SUBLANES
            part = None
            for s in range(off, off + CONV_WIDTH):
                if s % SUBLANES != rho:
                    continue
                start = r * CONV_ROWS + s - rho
                term = w_ref[s - off:s - off + 1, :] * buf[start:start + rows, :]
                part = term if part is None else part + term
            acc = acc + part[rho:rho + CONV_ROWS, :]
        mu = jnp.mean(acc, axis=-1, keepdims=True)
        d = acc - mu
        var = jnp.mean(d * d, axis=-1, keepdims=True)
        y = d * lax.rsqrt(var + LN_EPS) * lg_ref[...] + lb_ref[...]
        o_ref[r * CONV_ROWS:(r + 1) * CONV_ROWS, :] = (y * jax.nn.sigmoid(y)).astype(o_ref.dtype)


def _conv(proj, conv_w, conv_b, ln_g, ln_b):
    bsz, seq, _ = proj.shape
    t = min(CONV_T, seq)
    ca, cg = COL_CU // CONV_CH, COL_CU // CONV_CH + 1
    hb = t // CONV_HALO
    prev = lambda i: jnp.maximum(i * hb - 1, 0)
    row = lambda v: v.reshape(1, CONV_CH)
    return pl.pallas_call(
        _conv_kernel,
        grid=(bsz, seq // t),
        in_specs=[
            pl.BlockSpec((None, t, CONV_CH), lambda b, i: (b, i, ca)),
            pl.BlockSpec((None, t, CONV_CH), lambda b, i: (b, i, cg)),
            pl.BlockSpec((None, CONV_HALO, CONV_CH), lambda b, i: (b, prev(i), ca)),
            pl.BlockSpec((None, CONV_HALO, CONV_CH), lambda b, i: (b, prev(i), cg)),
            pl.BlockSpec((CONV_WIDTH, CONV_CH), lambda b, i: (0, 0)),
            pl.BlockSpec((1, CONV_CH), lambda b, i: (0, 0)),
            pl.BlockSpec((1, CONV_CH), lambda b, i: (0, 0)),
            pl.BlockSpec((1, CONV_CH), lambda b, i: (0, 0)),
        ],
        out_specs=pl.BlockSpec((None, t, CONV_CH), lambda b, i: (b, i, 0)),
        out_shape=jax.ShapeDtypeStruct((bsz, seq, CONV_CH), BF16),
        scratch_shapes=[pltpu.VMEM((CONV_HALO + t, CONV_CH), F32)],
        compiler_params=_params(("parallel", "parallel")),
        name="conv",
    )(proj, proj, proj, proj, conv_w, row(conv_b), row(ln_g), row(ln_b))


def _merge_kernel(ya_ref, yb_ref, yc_ref, yd_ref, gates_ref, x_ref, gt_ref, gp_ref, gf_ref, scf_ref, shf_ref,
                  wa_ref, wb_ref, wc_ref, wd_ref, wo_ref, o_ref, h_ref):
    d = x_ref.shape[1]
    y = None
    for i, (br, w) in enumerate(((ya_ref, wa_ref), (yb_ref, wb_ref), (yc_ref, wc_ref), (yd_ref, wd_ref))):
        t = jnp.dot(br[...], w[...], preferred_element_type=F32) * gates_ref[:, i * d:(i + 1) * d].astype(F32)
        y = t if y is None else y + t
    z = jnp.dot(y.astype(BF16), wo_ref[...], preferred_element_type=F32)
    x_new = x_ref[...] + gt_ref[...] * (_rms(z) * gp_ref[...])
    o_ref[...] = x_new
    h_ref[...] = _modulated_norm(x_new, gf_ref[...], scf_ref[...], shf_ref[...]).astype(h_ref.dtype)


def _merge(ya, yb, yc, yd, gates, x, gt, g_post, g_ffn, sc_f, sh_f, wa, wb, wc, wd, wo, layer, *, tm):
    bsz, seq, d = x.shape
    tok = lambda width: pl.BlockSpec((None, tm, width), lambda b, i: (b, i, 0))
    vec = lambda: pl.BlockSpec((None, 1, d), lambda b, i: (b, 0, 0))
    par = lambda: pl.BlockSpec((1, d), lambda b, i: (0, 0))
    full = lambda a: pl.BlockSpec((None,) + a.shape[1:], lambda b, i: (layer, 0, 0),
                                  pipeline_mode=pl.Buffered(1))
    return pl.pallas_call(
        _merge_kernel,
        grid=(bsz, seq // tm),
        in_specs=[tok(A_W), tok(B_W), tok(CONV_CH), tok(D_QW), tok(N_BRANCHES * d), tok(d),
                  vec(), par(), par(), vec(), vec(),
                  full(wa), full(wb), full(wc), full(wd), full(wo)],
        out_specs=[tok(d), tok(d)],
        out_shape=[jax.ShapeDtypeStruct((bsz, seq, d), F32), jax.ShapeDtypeStruct((bsz, seq, d), BF16)],
        compiler_params=_params(("parallel", "parallel")),
        name="merge",
    )(ya, yb, yc, yd, gates, x, gt, g_post.reshape(1, d), g_ffn.reshape(1, d), sc_f, sh_f, wa, wb, wc, wd, wo)


def _ffn_kernel(x_ref, h_ref, gt_ref, gp_ref, wgu_ref, wd_ref, o_ref, acc_scr):
    f = pl.program_id(2)
    tf = wd_ref.shape[0]

    @pl.when(f == 0)
    def _():
        acc_scr[...] = jnp.zeros_like(acc_scr)

    gu = jnp.dot(h_ref[...], wgu_ref[...], preferred_element_type=F32)
    g, u = gu[:, :tf], gu[:, tf:]
    a = ((g * jax.nn.sigmoid(g)) * u).astype(BF16)
    acc_scr[...] += jnp.dot(a, wd_ref[...], preferred_element_type=F32)

    @pl.when(f == pl.num_programs(2) - 1)
    def _():
        o_ref[...] = x_ref[...] + gt_ref[...] * (_rms(acc_scr[...]) * gp_ref[...])


def _ffn(x, h, gt, g_post, wgu, wd, layer, *, tm, tf):
    bsz, seq, d = x.shape
    dff = wd.shape[1]
    vec = lambda: pl.BlockSpec((None, 1, d), lambda b, i, f: (b, 0, 0))
    par = lambda: pl.BlockSpec((1, d), lambda b, i, f: (0, 0))
    return pl.pallas_call(
        _ffn_kernel,
        grid=(bsz, seq // tm, dff // tf),
        in_specs=[
            pl.BlockSpec((None, tm, d), lambda b, i, f: (b, i, 0)),
            pl.BlockSpec((None, tm, d), lambda b, i, f: (b, i, 0)),
            vec(), par(),
            pl.BlockSpec((None, d, 2 * tf), lambda b, i, f: (layer, 0, f)),
            pl.BlockSpec((None, tf, d), lambda b, i, f: (layer, f, 0)),
        ],
        out_specs=pl.BlockSpec((None, tm, d), lambda b, i, f: (b, i, 0)),
        out_shape=jax.ShapeDtypeStruct((bsz, seq, d), F32),
        scratch_shapes=[pltpu.VMEM((tm, d), F32)],
        compiler_params=_params(("parallel", "parallel", "arbitrary")),
        name="ffn",
    )(x, h, gt, g_post.reshape(1, d), wgu, wd)


SRC_QKV = 3 * A_W + 3 * B_W
SRC_REST = SRC_QKV + B_HEADS
SRC_GATE = SRC_REST + (PROJ_W - GATE_W - SRC_QKV)
IN_W = SRC_GATE + GATE_W
PACK_R = 256
PACK_SIDE_ROW = PROJ_W_PAD
PACK_ROWS = PROJ_W_PAD + PACK_R
assert GATE_W % PACK_R == 0 and SRC_QKV % PACK_R == 0 and PROJ_W % PACK_R == 0 and PROJ_W_PAD % PACK_R == 0


def _pack_tables():
    s2 = HEAD_DIM ** -0.5 * LOG2E
    src, scale, rows = [], [], []
    for b in range(PACK_ROWS // PACK_R):
        row = b * PACK_R
        if row < GATE_W:
            src.append(SRC_GATE + row); scale.append(1.0); rows.append(PACK_R)
        elif row < GATE_W + SRC_QKV:
            s = row - GATE_W
            is_q = s < A_W or 3 * A_W <= s < 3 * A_W + B_W
            src.append(s); scale.append(s2 if is_q else 1.0); rows.append(PACK_R)
        elif row < PROJ_W:
            s = row - GATE_W - SRC_QKV
            is_q = 2 * CONV_CH <= s < 2 * CONV_CH + D_QW
            src.append(SRC_REST + s); scale.append(D_HEAD_DIM ** -0.5 if is_q else 1.0); rows.append(PACK_R)
        elif row == PACK_SIDE_ROW:
            src.append(SRC_QKV); scale.append(1.0); rows.append(B_HEADS)
        else:
            src.append(0); scale.append(0.0); rows.append(0)
    return np.asarray(src, np.int32), np.asarray(scale, np.float32), np.asarray(rows, np.int32)


def _pack_kernel(src_ref, scale_ref, rows_ref, a_ref, o_ref):
    blk = pl.program_id(0)
    depth, _, d = o_ref.shape
    kt = d // LANES
    keep = lax.broadcasted_iota(jnp.int32, (PACK_R, d), 0) < rows_ref[blk]
    for layer in range(depth):
        cols = [a_ref[pl.ds(t * depth + layer, PACK_R, stride=kt * depth), :] for t in range(kt)]
        o_ref[layer] = jnp.where(keep, jnp.concatenate(cols, axis=1) * scale_ref[blk], 0.0).astype(o_ref.dtype)


def _pack_in_proj(w):
    depth, d, in_w = w.shape
    assert in_w == IN_W and d % LANES == 0
    kt = d // LANES
    view = jnp.transpose(w.reshape(depth, kt, LANES, in_w), (3, 1, 0, 2)).reshape(in_w * kt * depth, LANES)
    src, scale, rows = _pack_tables()
    per_feature = kt * depth
    return pl.pallas_call(
        _pack_kernel,
        grid_spec=pltpu.PrefetchScalarGridSpec(
            num_scalar_prefetch=3,
            grid=(PACK_ROWS // PACK_R,),
            in_specs=[pl.BlockSpec((pl.Element(PACK_R * per_feature), pl.Element(LANES)),
                                   lambda b, src, scale, rows: (src[b] * per_feature, 0))],
            out_specs=pl.BlockSpec((depth, PACK_R, d), lambda b, src, scale, rows: (0, b, 0)),
        ),
        out_shape=jax.ShapeDtypeStruct((depth, PACK_ROWS, d), BF16),
        compiler_params=_params(("parallel",)),
        name="pack_in_proj",
    )(jnp.asarray(src), jnp.asarray(scale), jnp.asarray(rows), view)


CAST_BLOCK_ELEMS = 1 << 20


def _cast_kernel(w_ref, o_ref):
    o_ref[...] = w_ref[...].astype(o_ref.dtype)


def _cast_bf16(w):
    depth, r, c = w.shape
    tr = 16
    while tr * 2 * c <= CAST_BLOCK_ELEMS and r % (tr * 2) == 0:
        tr *= 2
    assert r % tr == 0
    return pl.pallas_call(
        _cast_kernel,
        grid=(depth, r // tr),
        in_specs=[pl.BlockSpec((None, tr, c), lambda l, i: (l, i, 0))],
        out_specs=pl.BlockSpec((None, tr, c), lambda l, i: (l, i, 0)),
        out_shape=jax.ShapeDtypeStruct(w.shape, BF16),
        compiler_params=_params(("parallel", "parallel")),
        name="cast_bf16",
    )(w)


def _cast_gate_up_kernel(wg_ref, wu_ref, o_ref):
    tf = wg_ref.shape[1]
    o_ref[:, :tf] = wg_ref[...].astype(o_ref.dtype)
    o_ref[:, tf:] = wu_ref[...].astype(o_ref.dtype)


def _cast_gate_up(wg, wu, tf):
    depth, d, dff = wg.shape
    assert dff % tf == 0
    src = pl.BlockSpec((None, d, tf), lambda l, f: (l, 0, f))
    return pl.pallas_call(
        _cast_gate_up_kernel,
        grid=(depth, dff // tf),
        in_specs=[src, src],
        out_specs=pl.BlockSpec((None, d, 2 * tf), lambda l, f: (l, 0, f)),
        out_shape=jax.ShapeDtypeStruct((depth, d, 2 * dff), BF16),
        compiler_params=_params(("parallel", "parallel")),
        name="cast_gate_up",
    )(wg, wu)


def kernel(x, c, rel_bias, w_mod, b_mod, mix_norm_pre, mix_norm_post, w_in, fox_bias, conv_w, conv_b, conv_ln_g, conv_ln_b, sinks, w_branch_a, w_branch_b, w_branch_c, w_branch_d, w_out, ffn_norm_pre, ffn_norm_post, w_ffn_gate, w_ffn_up, w_ffn_down):
    depth = w_mod.shape[0]
    bsz, seq, d = x.shape
    tm = min(1024, seq)
    tm_small = min(256, seq)
    tm_ffn = min(512, seq)

    mod = _modulation(c, w_mod, b_mod)
    tab_a = _bias_tables(rel_bias[:, :A_HEADS], _moba_buckets(), scale=LOG2E)
    tab_d = _bias_tables(rel_bias[:, A_HEADS:], _swa_buckets(), scale=1.0)[:, 0]

    wa, wb, wc, wd, wo = [_cast_bf16(w) for w in (w_branch_a, w_branch_b, w_branch_c, w_branch_d, w_out)]
    tf_ffn = 512
    wgu, wdn = _cast_gate_up(w_ffn_gate, w_ffn_up, tf_ffn), _cast_bf16(w_ffn_down)

    w_proj = _pack_in_proj(w_in)

    for l in range(depth):
        sh_m, sc_m, gt_m, sh_f, sc_f, gt_f = [mod[l, :, None, i * d:(i + 1) * d] for i in range(6)]

        proj, fox_raw = _norm_proj(x, mix_norm_pre[l], sc_m, sh_m, w_proj, l, tm=tm, tn=PROJ_TN)
        fox_qa, fox_ka, fox_cum = _fox_gate(fox_raw, fox_bias[l], t=min(512, seq))

        ya = _moba(proj, tab_a)
        yb = _fox(proj, fox_qa, fox_ka, _fox_first_block(proj, fox_cum))
        yc = _conv(proj, conv_w[l], conv_b[l], conv_ln_g[l], conv_ln_b[l])
        yd = _swa(proj, sinks[l], tab_d)

        x, h_ffn = _merge(ya, yb, yc, yd, proj, x, gt_m, mix_norm_post[l], ffn_norm_pre[l], sc_f, sh_f,
                          wa, wb, wc, wd, wo, l, tm=tm_small)
        x = _ffn(x, h_ffn, gt_f, ffn_norm_post[l], wgu, wdn, l, tm=tm_ffn, tf=tf_ffn)
    return x
```

```python
import functools
import math

import jax
import jax.numpy as jnp
import numpy as np
from jax import lax
from jax.experimental import pallas as pl
from jax.experimental.pallas import tpu as pltpu

F32 = jnp.float32
BF16 = jnp.bfloat16
HIGHEST = lax.Precision.HIGHEST

HEAD_DIM = 128
A_HEADS = 4
MOBA_BLOCK = 256
MOBA_TOPK = 3
B_HEADS = 4
CONV_CH = 512
CONV_WIDTH = 31
D_Q_HEADS = 8
D_KV_HEADS = 2
D_HEAD_DIM = 64
WINDOW = 128
N_BUCKETS = 32
MAX_DISTANCE = 1024
N_BRANCHES = 4
RMS_EPS = 1e-6
LN_EPS = 1e-5

A_W = A_HEADS * HEAD_DIM
B_W = B_HEADS * HEAD_DIM
D_QW = D_Q_HEADS * D_HEAD_DIM
D_KVW = D_KV_HEADS * D_HEAD_DIM

LANES = 128
SUBLANES = 8
VMEM_LIMIT = 56 * 1024 * 1024

NEG = -1e30
LOG2E = math.log2(math.e)
ATT_G = 4
ATT_T = MOBA_BLOCK
ATT_U = 4
FFN_TF = 512

D_MODEL = 2048
GATE_W = N_BRANCHES * D_MODEL
COL_AQ, COL_AK, COL_AV = GATE_W, GATE_W + A_W, GATE_W + 2 * A_W
COL_BQ, COL_BK, COL_BV = COL_AV + A_W, COL_AV + A_W + B_W, COL_AV + A_W + 2 * B_W
COL_CU = COL_BV + B_W
COL_DQ = COL_CU + 2 * CONV_CH
COL_DK = COL_DQ + D_QW
COL_DV = COL_DK + D_KVW
PROJ_W = COL_DV + D_KVW
PROJ_TN = 1024
PROJ_W_PAD = -(-PROJ_W // PROJ_TN) * PROJ_TN
MOBA_FAR = -(-(MAX_DISTANCE + MOBA_BLOCK - 1) // MOBA_BLOCK)
VOID_COL = LANES - 1
assert ATT_G % ATT_U == 0 and (MOBA_FAR - 1) % ATT_U == 0


def _params(sem, vmem=VMEM_LIMIT):
    return pltpu.CompilerParams(dimension_semantics=sem, vmem_limit_bytes=vmem)


def _t5_bucket(dist):
    max_exact = N_BUCKETS // 2
    d = jnp.maximum(dist, 0)
    log_ratio = jnp.log(jnp.maximum(d, 1).astype(jnp.float32) / max_exact) / math.log(MAX_DISTANCE / max_exact)
    large = max_exact + (log_ratio * (N_BUCKETS - max_exact)).astype(jnp.int32)
    large = jnp.minimum(large, N_BUCKETS - 1)
    return jnp.where(d < max_exact, d, large)


def _sigmoid(x):
    return 0.5 * jnp.tanh(0.5 * x) + 0.5


def _rms(y):
    return y * lax.rsqrt(jnp.mean(y * y, axis=-1, keepdims=True) + RMS_EPS)


def _modulated_norm(x, g, sc, sh):
    return (_rms(x) * g) * (1.0 + sc) + sh


MOD_TN = 512
MOD_KC = 256


def _mod_kernel(ct_ref, w_ref, b_ref, o_ref):
    d, nb = ct_ref.shape
    tn = w_ref.shape[1]
    ct = ct_ref[...]
    ca = ct * jax.nn.sigmoid(ct)
    rows = []
    for r in range(nb):
        acc = jnp.zeros((1, tn), F32)
        for kc in range(d // MOD_KC):
            sl = slice(kc * MOD_KC, (kc + 1) * MOD_KC)
            acc = acc + jnp.sum(w_ref[sl, :] * ca[sl, r:r + 1], axis=0, keepdims=True)
        rows.append(acc)
    o_ref[...] = jnp.concatenate(rows, axis=0) + b_ref[...]


def _modulation(c, w_mod, b_mod):
    depth, d, n = w_mod.shape
    bsz = c.shape[0]
    return pl.pallas_call(
        _mod_kernel,
        grid=(depth, n // MOD_TN),
        in_specs=[
            pl.BlockSpec((d, bsz), lambda l, j: (0, 0)),
            pl.BlockSpec((None, d, MOD_TN), lambda l, j: (l, 0, j)),
            pl.BlockSpec((None, 1, MOD_TN), lambda l, j: (l, 0, j)),
        ],
        out_specs=pl.BlockSpec((None, bsz, MOD_TN), lambda l, j: (l, 0, j)),
        out_shape=jax.ShapeDtypeStruct((depth, bsz, n), F32),
        compiler_params=_params(("parallel", "parallel")),
        name="mod",
    )(c.T, w_mod, b_mod.reshape(depth, 1, n))


def _bias_table_kernel(rb_ref, bucket_ref, o_ref, *, scale):
    h = pl.program_id(0)
    b = bucket_ref[...]
    acc = jnp.where(b < 0, NEG, 0.0).astype(F32)
    for u in range(N_BUCKETS):
        acc = jnp.where(b == u, rb_ref[h, u] * scale, acc)
    o_ref[...] = acc


def _bias_tables(rel_bias_heads, bucket, *, scale):
    nh = rel_bias_heads.shape[1]
    nt, r, c = bucket.shape
    return pl.pallas_call(
        functools.partial(_bias_table_kernel, scale=scale),
        grid=(nh, nt),
        in_specs=[
            pl.BlockSpec(memory_space=pltpu.SMEM),
            pl.BlockSpec((None, r, c), lambda h, t: (t, 0, 0)),
        ],
        out_specs=pl.BlockSpec((None, None, r, c), lambda h, t: (h, t, 0, 0)),
        out_shape=jax.ShapeDtypeStruct((nh, nt, r, c), F32),
        compiler_params=_params(("parallel", "parallel")),
        name="bias_table",
    )(rel_bias_heads.T, bucket)


def _moba_buckets():
    blk = MOBA_BLOCK
    i = jnp.arange(blk)[None, :]
    j = jnp.arange(blk)[:, None]
    tabs = []
    for delta in range(MOBA_FAR):
        dist = delta * blk + i - j
        tabs.append(jnp.where(dist >= 0, _t5_bucket(dist), -1))
    tabs.append(_t5_bucket(jnp.full((blk, blk), MAX_DISTANCE, jnp.int32)))
    return jnp.stack(tabs).astype(jnp.int32)


def _swa_buckets():
    qi = jnp.arange(WINDOW)[None, :]
    kj = jnp.arange(2 * WINDOW)[:, None]
    dist = qi + WINDOW - kj
    in_win = (dist >= 0) & (dist < WINDOW)
    return jnp.where(in_win, _t5_bucket(dist), -1).astype(jnp.int32)[None]


def _norm_proj_kernel(x_ref, g_ref, sc_ref, sh_ref, w_ref, ws_ref, wg_ref, wu_ref, wdn_ref,
                      o_ref, os_ref, wgu_out, wdn_out, h_scr):
    j = pl.program_id(2)
    tn = w_ref.shape[0]

    def cast_ffn_weights():
        tf = FFN_TF
        for f in range(wg_ref.shape[1] // tf):
            wgu_out[:, 2 * f * tf:(2 * f + 1) * tf] = wg_ref[:, f * tf:(f + 1) * tf].astype(BF16)
            wgu_out[:, (2 * f + 1) * tf:(2 * f + 2) * tf] = wu_ref[:, f * tf:(f + 1) * tf].astype(BF16)
        wdn_out[...] = wdn_ref[...].astype(BF16)

    @pl.when(j == 0)
    def _():
        h = _modulated_norm(x_ref[...], g_ref[...], sc_ref[...], sh_ref[...]).astype(BF16)
        h_scr[...] = h
        os_ref[...] = _dot_nt(h, ws_ref[...])

    @pl.when(j < GATE_W // tn)
    def _():
        o_ref[...] = _sigmoid(_dot_nt(h_scr[...], w_ref[...])).astype(o_ref.dtype)
        cast_ffn_weights()

    @pl.when(j >= GATE_W // tn)
    def _():
        o_ref[...] = _dot_nt(h_scr[...], w_ref[...]).astype(o_ref.dtype)
        cast_ffn_weights()


def _rows_per_step(rows, steps, tile):
    r = tile
    while rows % r or rows // r > steps:
        r += tile
    return r


def _norm_proj(x, g, sc, sh, w, w_gate, w_up, w_down, layer, *, tm, tn):
    bsz, seq, d = x.shape
    dff = w_down.shape[1]
    n = PROJ_W_PAD
    ns = LANES
    ni, nj = seq // tm, n // tn
    assert GATE_W % tn == 0 and n % tn == 0 and w.shape[1] == PACK_ROWS and dff % FFN_TF == 0
    r_gu = _rows_per_step(d, bsz * ni * nj, 2 * SUBLANES)
    r_dn = _rows_per_step(dff, bsz * ni * nj, 2 * SUBLANES)
    step = lambda b, i, j: (b * ni + i) * nj + j
    gu_blk = lambda b, i, j: (layer, jnp.minimum(step(b, i, j), d // r_gu - 1), 0)
    dn_blk = lambda b, i, j: (layer, jnp.minimum(step(b, i, j), dff // r_dn - 1), 0)
    first = lambda idx: (lambda b, i, j: (0,) + idx(b, i, j)[1:])
    return pl.pallas_call(
        _norm_proj_kernel,
        grid=(bsz, ni, nj),
        in_specs=[
            pl.BlockSpec((None, tm, d), lambda b, i, j: (b, i, 0)),
            pl.BlockSpec((1, d), lambda b, i, j: (0, 0)),
            pl.BlockSpec((None, 1, d), lambda b, i, j: (b, 0, 0)),
            pl.BlockSpec((None, 1, d), lambda b, i, j: (b, 0, 0)),
            pl.BlockSpec((None, tn, d), lambda b, i, j: (layer, j, 0)),
            pl.BlockSpec((None, ns, d), lambda b, i, j: (layer, PACK_SIDE_ROW // ns, 0)),
            pl.BlockSpec((None, r_gu, dff), gu_blk),
            pl.BlockSpec((None, r_gu, dff), gu_blk),
            pl.BlockSpec((None, r_dn, d), dn_blk),
        ],
        out_specs=[pl.BlockSpec((None, tm, tn), lambda b, i, j: (b, i, j)),
                   pl.BlockSpec((None, tm, ns), lambda b, i, j: (b, i, 0)),
                   pl.BlockSpec((None, r_gu, 2 * dff), first(gu_blk)),
                   pl.BlockSpec((None, r_dn, d), first(dn_blk))],
        out_shape=[jax.ShapeDtypeStruct((bsz, seq, n), BF16), jax.ShapeDtypeStruct((bsz, seq, ns), F32),
                   jax.ShapeDtypeStruct((1, d, 2 * dff), BF16), jax.ShapeDtypeStruct((1, dff, d), BF16)],
        scratch_shapes=[pltpu.VMEM((tm, d), BF16)],
        compiler_params=_params(("arbitrary", "arbitrary", "arbitrary")),
        name="norm_proj",
    )(x, g.reshape(1, d), sc, sh, w, w, w_gate, w_up, w_down)


def _split3(c):
    hi = c.astype(BF16).astype(F32)
    mid = (c - hi).astype(BF16).astype(F32)
    lo = (c - hi - mid).astype(BF16).astype(F32)
    return hi, mid, lo


def _fox_gate_kernel(x_ref, fb_ref, qa_ref, ka_ref, cum_ref, carry_scr):
    t = x_ref.shape[0]

    @pl.when(pl.program_id(1) == 0)
    def _():
        carry_scr[...] = jnp.zeros_like(carry_scr)

    lf = jax.nn.log_sigmoid(x_ref[...] + fb_ref[...])
    tri = (lax.broadcasted_iota(jnp.int32, (t, t), 0) >= lax.broadcasted_iota(jnp.int32, (t, t), 1)).astype(F32)
    c = jnp.dot(tri, lf, precision=HIGHEST, preferred_element_type=F32) + carry_scr[...]
    carry_scr[...] = c[t - 1:t, :]
    cum_ref[...] = c * LOG2E
    lane = lax.broadcasted_iota(jnp.int32, (t, LANES), 1)
    sub = lax.broadcasted_iota(jnp.int32, (SUBLANES, t), 0)
    ct = c.T * LOG2E
    for h in range(B_HEADS):
        hi, mid, lo = _split3(jnp.broadcast_to(c[:, h:h + 1], (t, LANES)) * LOG2E)
        ka = jnp.where(lane == 0, -hi, jnp.where(lane == 1, -mid, jnp.where(lane == 2, -lo, jnp.where(lane < 6, 1.0, 0.0))))
        ka_ref[h] = ka.astype(BF16)
        hi, mid, lo = _split3(jnp.broadcast_to(ct[h:h + 1, :], (SUBLANES, t)))
        qa = jnp.where(sub < 3, 1.0, jnp.where(sub == 3, hi, jnp.where(sub == 4, mid, jnp.where(sub == 5, lo, 0.0))))
        qa_ref[h] = jnp.concatenate([qa, jnp.zeros((LANES - SUBLANES, t), F32)], axis=0).astype(BF16)


def _fox_gate(raw, fox_b, *, t):
    bsz, seq, _ = raw.shape
    fb = jnp.zeros((1, LANES), F32).at[0, :B_HEADS].set(fox_b)
    aug = pl.BlockSpec((None, B_HEADS, t, LANES), lambda b, i: (b, 0, i, 0))
    aug_t = pl.BlockSpec((None, B_HEADS, LANES, t), lambda b, i: (b, 0, 0, i))
    return pl.pallas_call(
        _fox_gate_kernel,
        grid=(bsz, seq // t),
        in_specs=[
            pl.BlockSpec((None, t, LANES), lambda b, i: (b, i, 0)),
            pl.BlockSpec((1, LANES), lambda b, i: (0, 0)),
        ],
        out_specs=[aug_t, aug, pl.BlockSpec((None, t, LANES), lambda b, i: (b, i, 0))],
        out_shape=[jax.ShapeDtypeStruct((bsz, B_HEADS, LANES, seq), BF16),
                   jax.ShapeDtypeStruct((bsz, B_HEADS, seq, LANES), BF16),
                   jax.ShapeDtypeStruct((bsz, seq, LANES), F32)],
        scratch_shapes=[pltpu.VMEM((1, LANES), F32)],
        compiler_params=_params(("parallel", "arbitrary")),
        name="fox_gate",
    )(raw, fb)


ONES_ROWS = 16


def _softmax_step(zt, vt, carry):
    m, acc = carry
    m_new = jnp.maximum(m, jnp.max(zt, axis=0, keepdims=True))
    p = jnp.exp2(zt - m_new)
    acc = jnp.exp2(m - m_new) * acc + jnp.dot(vt, p.astype(BF16), preferred_element_type=F32)
    return m_new, acc


def _softmax_init(tq, dh):
    return (jnp.full((1, tq), NEG, F32), jnp.zeros((dh + ONES_ROWS, tq), F32))


def _softmax_finish(carry, dh):
    _, acc = carry
    return (acc[:dh, :] / acc[dh:dh + 1, :]).T


def _fill_vt(vt_scr, v_ref, chunk):
    seq, dh = v_ref.shape
    for cix in range(seq // chunk):
        sl = slice(cix * chunk, (cix + 1) * chunk)
        vt_scr[:dh, sl] = _transpose_bf16(v_ref[sl, :])
    vt_scr[dh:, :] = jnp.ones((ONES_ROWS, seq), BF16)


ATT_LOOKAHEAD = 6


def _run_substeps(steps, carry):
    carry = list(carry)
    pending = {}
    for s in range(min(ATT_LOOKAHEAD, len(steps))):
        pending[s] = steps[s][1]()
    for s, (j, _, fix, vt) in enumerate(steps):
        if s + ATT_LOOKAHEAD < len(steps):
            pending[s + ATT_LOOKAHEAD] = steps[s + ATT_LOOKAHEAD][1]()
        z, cj = pending.pop(s), carry[j]
        if fix is not None:
            z, cj = fix(z, cj)
        carry[j] = _softmax_step(z, vt, cj)
    return carry


def _dense_loop(start, n_blocks, make_steps, carry):
    big = 2 * ATT_U
    carry = lax.fori_loop(0, n_blocks // big,
                          lambda grp, c: tuple(_run_substeps(make_steps(start + grp * big, big), c)), carry)
    first = start + (n_blocks // big) * big
    return lax.fori_loop(0, (start + n_blocks - first) // ATT_U,
                         lambda grp, c: tuple(_run_substeps(make_steps(first + grp * ATT_U, ATT_U), c)), carry)


def _transpose_bf16(x):
    return x.astype(F32).T.astype(BF16)


_NT = (((1,), (1,)), ((), ()))


def _dot_nt(a, b):
    return lax.dot_general(a, b, _NT, preferred_element_type=F32)


def _moba_kernel(cfar_ref, q_ref, k_ref, v_ref, tab_ref, o_ref, kmean_scr, sc_scr, qt_scr, vt_scr):
    g, t = ATT_G, ATT_T
    seq, dh = k_ref.shape
    nblk = seq // t
    rows = g * t
    i = pl.program_id(2)
    base = i * g

    @pl.when(i == 0)
    def _():
        r = lax.broadcasted_iota(jnp.int32, (nblk, seq), 0)
        c = lax.broadcasted_iota(jnp.int32, (nblk, seq), 1)
        avg = jnp.where((c >= r * t) & (c < (r + 1) * t), 1.0 / t, 0.0).astype(BF16)
        kmean_scr[...] = jnp.dot(avg, k_ref[...], preferred_element_type=F32)
        _fill_vt(vt_scr, v_ref, rows)

    q = q_ref[...]
    qt_scr[:dh, :] = _transpose_bf16(q)
    sc = lax.dot_general(kmean_scr[...], q.astype(F32), _NT, precision=HIGHEST, preferred_element_type=F32)
    row = lax.broadcasted_iota(jnp.int32, (nblk, rows), 0)
    own = base + lax.broadcasted_iota(jnp.int32, (nblk, rows), 1) // t
    past = row < own
    sc = jnp.where(past, sc, -jnp.inf)
    sc_scr[...] = sc

    def rank_body(m, rank):
        sm = sc_scr[pl.ds(m, 1), :]
        return rank + jnp.where(sm > sc, 1.0, jnp.where(sm == sc, jnp.where(row > m, 1.0, 0.0), 0.0))

    rank = lax.fori_loop(0, base + g - 1, rank_body, jnp.zeros((nblk, rows), F32))
    pen = jnp.where(past, jnp.where(rank < MOBA_TOPK, 0.0, NEG), jnp.where(row == own, 0.0, NEG))
    void = jnp.where(lax.broadcasted_iota(jnp.int32, (SUBLANES, rows), 0) == SUBLANES - 1, NEG, 0.0)
    pen = jnp.concatenate([pen, jnp.zeros((LANES - SUBLANES - nblk, rows), F32), void], axis=0)
    qt_scr[dh:, :] = pen.astype(BF16)

    lane = lax.broadcasted_iota(jnp.int32, (t, LANES), 1)

    def kv(n, col):
        start = pl.multiple_of(n * t, t)
        onehot = jnp.where(lane == col, 1.0, 0.0).astype(BF16)
        return jnp.concatenate([k_ref[pl.ds(start, t), :], onehot], axis=1), vt_scr[:, pl.ds(start, t)]

    def logits(kb, j):
        return jnp.dot(kb, qt_scr[:, j * t:(j + 1) * t], preferred_element_type=F32)

    def far_steps(first, count):
        steps = []
        for u in range(count):
            kb, vb = kv(first + u, first + u)
            steps += [(j, functools.partial(logits, kb, j), None, vb) for j in range(g)]
        return steps

    carry = _dense_loop(0, jnp.maximum(base - (MOBA_FAR - 1), 0), far_steps,
                        tuple(_softmax_init(t, dh) for _ in range(g)))
    cfar = cfar_ref[pl.program_id(1)]

    def near_fix(delta, z, cj):
        if delta == MOBA_FAR - 1:
            m, acc = cj
            cj = (m + cfar, acc)
        return tab_ref[delta] + z, cj

    steps = []
    for k in range(-(MOBA_FAR - 1), g):
        n = base + k
        if k < 0:
            kb, vb = kv(jnp.maximum(n, 0), jnp.where(n >= 0, n, VOID_COL))
        else:
            kb, vb = kv(n, n)
        for j in range(max(k, 0), g):
            fix = functools.partial(near_fix, j - k) if j - k < MOBA_FAR else None
            steps.append((j, functools.partial(logits, kb, j), fix, vb))
    carry = _run_substeps(steps, carry)
    for j in range(g):
        o_ref[j * t:(j + 1) * t, :] = _softmax_finish(carry[j], dh).astype(o_ref.dtype)


def _moba(proj, tab):
    bsz, seq, _ = proj.shape
    g, t = ATT_G, ATT_T
    nblk = seq // t
    assert seq % (g * t) == 0 and nblk <= LANES - SUBLANES and nblk % SUBLANES == 0
    qc, kc, vc = COL_AQ // HEAD_DIM, COL_AK // HEAD_DIM, COL_AV // HEAD_DIM
    cfar = tab[:, MOBA_FAR, 0, 0]
    return pl.pallas_call(
        _moba_kernel,
        grid=(bsz, A_HEADS, nblk // g),
        in_specs=[
            pl.BlockSpec(memory_space=pltpu.SMEM),
            pl.BlockSpec((None, g * t, HEAD_DIM), lambda b, h, i: (b, i, qc + h)),
            pl.BlockSpec((None, seq, HEAD_DIM), lambda b, h, i: (b, 0, kc + h)),
            pl.BlockSpec((None, seq, HEAD_DIM), lambda b, h, i: (b, 0, vc + h)),
            pl.BlockSpec((None, MOBA_FAR, t, t), lambda b, h, i: (h, 0, 0, 0)),
        ],
        out_specs=pl.BlockSpec((None, g * t, HEAD_DIM), lambda b, h, i: (b, i, h)),
        out_shape=jax.ShapeDtypeStruct((bsz, seq, A_W), BF16),
        scratch_shapes=[pltpu.VMEM((nblk, HEAD_DIM), F32), pltpu.VMEM((nblk, g * t), F32),
                        pltpu.VMEM((HEAD_DIM + LANES, g * t), BF16), pltpu.VMEM((HEAD_DIM + ONES_ROWS, seq), BF16)],
        compiler_params=_params(("parallel", "parallel", "arbitrary")),
        name="moba",
    )(cfar, proj, proj, proj, tab)


def _fox_kernel(start_ref, q_ref, qa_ref, k_ref, ka_ref, v_ref, o_ref, qt_scr, vt_scr):
    g, t = ATT_G, ATT_T
    rows = g * t
    seq, dh = k_ref.shape
    base = pl.program_id(2) * g
    first = start_ref[(pl.program_id(0) * pl.num_programs(1) + pl.program_id(1)) * pl.num_programs(2)
                      + pl.program_id(2)]

    @pl.when(base == 0)
    def _():
        _fill_vt(vt_scr, v_ref, rows)

    qt_scr[:dh, :] = _transpose_bf16(q_ref[...])
    qt_scr[dh:, :] = qa_ref[...]

    def kv(n):
        start = pl.multiple_of(n * t, t)
        return (jnp.concatenate([k_ref[pl.ds(start, t), :], ka_ref[pl.ds(start, t), :]], axis=1),
                vt_scr[:, pl.ds(start, t)])

    def logits(kb, j):
        return jnp.dot(kb, qt_scr[:, j * t:(j + 1) * t], preferred_element_type=F32)

    def dense_steps(first, count):
        steps = []
        for u in range(count):
            kb, vb = kv(first + u)
            steps += [(j, functools.partial(logits, kb, j), None, vb) for j in range(g)]
        return steps

    carry = _dense_loop(first, base - first, dense_steps, tuple(_softmax_init(t, dh) for _ in range(g)))
    causal = lax.broadcasted_iota(jnp.int32, (t, t), 0) <= lax.broadcasted_iota(jnp.int32, (t, t), 1)

    def diag_fix(z, cj):
        return jnp.where(causal, z, NEG), cj

    steps = []
    for k in range(g):
        kb, vb = kv(base + k)
        steps += [(j, functools.partial(logits, kb, j), diag_fix if j == k else None, vb) for j in range(k, g)]
    carry = _run_substeps(steps, carry)
    for j in range(g):
        o_ref[j * t:(j + 1) * t, :] = _softmax_finish(carry[j], dh).astype(o_ref.dtype)


FOX_SKIP_BITS = 200.0
FOX_NORM_SLACK = 1.0 + 2.0 ** -6


def _fox_bounds_kernel(q_ref, k_ref, cum_ref, o_ref, krun_scr, kpref_scr, clast_scr):
    i = pl.program_id(1)
    rows, width = q_ref.shape
    g = rows // ATT_T

    @pl.when(i == 0)
    def _():
        krun_scr[...] = jnp.zeros_like(krun_scr)
        kpref_scr[...] = jnp.zeros_like(kpref_scr)
        clast_scr[...] = jnp.zeros_like(clast_scr)

    head_of = lax.broadcasted_iota(jnp.int32, (width, LANES), 0) // HEAD_DIM
    head_sum = jnp.where(head_of == lax.broadcasted_iota(jnp.int32, (width, LANES), 1), 1.0, 0.0).astype(BF16)

    def max_norm(ref):
        x = ref[...].astype(F32)
        n2 = jnp.dot((x * x).astype(BF16), head_sum, preferred_element_type=F32)
        return jnp.sqrt(jnp.max(n2, axis=0, keepdims=True)) * FOX_NORM_SLACK

    qn, kn = max_norm(q_ref), max_norm(k_ref)
    c_first, c_last = cum_ref[0:1, :], cum_ref[rows - 1:rows, :]
    tile = lax.broadcasted_iota(jnp.int32, kpref_scr.shape, 0)
    ub = qn * kpref_scr[...] + (c_first - clast_scr[...]) + qn * kn
    ok = (ub < -FOX_SKIP_BITS) & (tile < i)
    o_ref[...] = jnp.max(jnp.where(ok, (tile + 1) * g, 0), axis=0, keepdims=True)
    krun = jnp.maximum(krun_scr[...], kn)
    krun_scr[...] = krun
    kpref_scr[pl.ds(i, 1), :] = krun
    clast_scr[pl.ds(i, 1), :] = c_last


def _fox_first_block(proj, cum2):
    bsz, seq, _ = proj.shape
    rows = ATT_G * ATT_T
    ntile = seq // rows
    hist = pltpu.VMEM((-(-ntile // SUBLANES) * SUBLANES, LANES), F32)
    first = pl.pallas_call(
        _fox_bounds_kernel,
        grid=(bsz, ntile),
        in_specs=[
            pl.BlockSpec((None, rows, B_W), lambda b, i: (b, i, COL_BQ // B_W)),
            pl.BlockSpec((None, rows, B_W), lambda b, i: (b, i, COL_BK // B_W)),
            pl.BlockSpec((None, rows, LANES), lambda b, i: (b, i, 0)),
        ],
        out_specs=pl.BlockSpec((None, None, 1, LANES), lambda b, i: (b, i, 0, 0)),
        out_shape=jax.ShapeDtypeStruct((bsz, ntile, 1, LANES), jnp.int32),
        scratch_shapes=[pltpu.VMEM((1, LANES), F32), hist, hist],
        compiler_params=_params(("parallel", "arbitrary")),
        name="fox_bounds",
    )(proj, proj, cum2)
    return first[:, :, 0, :B_HEADS].transpose(0, 2, 1).reshape(-1)


def _fox(proj, qa, ka, first):
    bsz, seq, _ = proj.shape
    g, t = ATT_G, ATT_T
    assert seq % (g * t) == 0
    qc, kc, vc = COL_BQ // HEAD_DIM, COL_BK // HEAD_DIM, COL_BV // HEAD_DIM
    return pl.pallas_call(
        _fox_kernel,
        grid_spec=pltpu.PrefetchScalarGridSpec(
            num_scalar_prefetch=1,
            grid=(bsz, B_HEADS, seq // (g * t)),
            in_specs=[
                pl.BlockSpec((None, g * t, HEAD_DIM), lambda b, h, i, s: (b, i, qc + h)),
                pl.BlockSpec((None, None, LANES, g * t), lambda b, h, i, s: (b, h, 0, i)),
                pl.BlockSpec((None, seq, HEAD_DIM), lambda b, h, i, s: (b, 0, kc + h)),
                pl.BlockSpec((None, None, seq, LANES), lambda b, h, i, s: (b, h, 0, 0)),
                pl.BlockSpec((None, seq, HEAD_DIM), lambda b, h, i, s: (b, 0, vc + h)),
            ],
            out_specs=pl.BlockSpec((None, g * t, HEAD_DIM), lambda b, h, i, s: (b, i, h)),
            scratch_shapes=[pltpu.VMEM((HEAD_DIM + LANES, g * t), BF16),
                            pltpu.VMEM((HEAD_DIM + ONES_ROWS, seq), BF16)],
        ),
        out_shape=jax.ShapeDtypeStruct((bsz, seq, B_W), BF16),
        compiler_params=_params(("parallel", "parallel", "arbitrary")),
        name="fox",
    )(first, proj, qa, proj, ka, proj)


SWA_BLOCKS = 4


def _swa_kernel(sink_ref, q_ref, kp_ref, kc_ref, vp_ref, vc_ref, tab_ref, o_ref):
    w = WINDOW
    dh = D_HEAD_DIM
    grp = D_Q_HEADS // D_KV_HEADS
    i = pl.program_id(1)
    nblk = q_ref.shape[0] // w
    span = (nblk + 1) * w
    kall = jnp.concatenate([kp_ref[...], kc_ref[...]], axis=0)
    vt = jnp.concatenate([_transpose_bf16(vp_ref[...]), _transpose_bf16(vc_ref[...])], axis=1)
    ones = jnp.ones((ONES_ROWS, span), BF16)
    kg = [kall[:, g * dh:(g + 1) * dh] for g in range(D_KV_HEADS)]
    vg = [jnp.concatenate([vt[g * dh:(g + 1) * dh, :], ones], axis=0) for g in range(D_KV_HEADS)]
    qt = _transpose_bf16(q_ref[...])
    key = lax.broadcasted_iota(jnp.int32, (2 * w, w), 0)
    first = key >= jnp.where(i > 0, 0, w)

    def block_logits(blk):
        return [jnp.dot(kg[h // grp][blk * w:(blk + 2) * w, :], qt[h * dh:(h + 1) * dh, blk * w:(blk + 1) * w],
                        preferred_element_type=F32) for h in range(D_Q_HEADS)]

    logits = block_logits(0)
    for blk in range(nblk):
        nxt = block_logits(blk + 1) if blk + 1 < nblk else None
        outs = []
        for h in range(D_Q_HEADS):
            s = tab_ref[h] + logits[h]
            if blk == 0:
                s = jnp.where(first, s, NEG)
            sink = sink_ref[h]
            m = jnp.maximum(jnp.max(s, axis=0, keepdims=True), sink)
            p = jnp.exp(s - m)
            acc = jnp.dot(vg[h // grp][:, blk * w:(blk + 2) * w], p.astype(BF16), preferred_element_type=F32)
            den = acc[dh:dh + 1, :] + jnp.exp(sink - m)
            outs.append(acc[:dh, :] / den)
        o_ref[blk * w:(blk + 1) * w, :] = jnp.concatenate(outs, axis=0).T.astype(o_ref.dtype)
        logits = nxt


def _swa(proj, sinks, tab):
    bsz, seq, _ = proj.shape
    w = WINDOW
    nb = SWA_BLOCKS
    assert seq % (nb * w) == 0
    qc, kc, vc = COL_DQ // D_QW, COL_DK // D_KVW, COL_DV // D_KVW
    prev = lambda i: jnp.maximum(i * nb - 1, 0)
    return pl.pallas_call(
        _swa_kernel,
        grid=(bsz, seq // (nb * w)),
        in_specs=[
            pl.BlockSpec(memory_space=pltpu.SMEM),
            pl.BlockSpec((None, nb * w, D_QW), lambda b, i: (b, i, qc)),
            pl.BlockSpec((None, w, D_KVW), lambda b, i: (b, prev(i), kc)),
            pl.BlockSpec((None, nb * w, D_KVW), lambda b, i: (b, i, kc)),
            pl.BlockSpec((None, w, D_KVW), lambda b, i: (b, prev(i), vc)),
            pl.BlockSpec((None, nb * w, D_KVW), lambda b, i: (b, i, vc)),
            pl.BlockSpec((D_Q_HEADS, 2 * w, w), lambda b, i: (0, 0, 0)),
        ],
        out_specs=pl.BlockSpec((None, nb * w, D_QW), lambda b, i: (b, i, 0)),
        out_shape=jax.ShapeDtypeStruct((bsz, seq, D_QW), BF16),
        compiler_params=_params(("parallel", "parallel")),
        name="swa",
    )(sinks, proj, proj, proj, proj, proj, tab)


CONV_T = 512
CONV_HALO = 32
CONV_ROWS = 64


def _conv_kernel(a_ref, g_ref, ha_ref, hg_ref, w_ref, b_ref, lg_ref, lb_ref, o_ref, buf):
    t = a_ref.shape[0]
    halo = CONV_HALO
    hp = ha_ref[...].astype(F32) * jax.nn.sigmoid(hg_ref[...].astype(F32))
    buf[0:halo, :] = jnp.where(pl.program_id(1) > 0, hp, 0.0)
    buf[halo:halo + t, :] = a_ref[...].astype(F32) * jax.nn.sigmoid(g_ref[...].astype(F32))
    off = halo - (CONV_WIDTH - 1)
    for r in range(t // CONV_ROWS):
        acc = jnp.broadcast_to(b_ref[...], (CONV_ROWS, CONV_CH))
        for rho in range(SUBLANES):
            rows = CONV_ROWS if rho == 0 else CONV_ROWS + SUBLANES
            part = None
            for s in range(off, off + CONV_WIDTH):
                if s % SUBLANES != rho:
                    continue
                start = r * CONV_ROWS + s - rho
                term = w_ref[s - off:s - off + 1, :] * buf[start:start + rows, :]
                part = term if part is None else part + term
            acc = acc + part[rho:rho + CONV_ROWS, :]
        mu = jnp.mean(acc, axis=-1, keepdims=True)
        d = acc - mu
        var = jnp.mean(d * d, axis=-1, keepdims=True)
        y = d * lax.rsqrt(var + LN_EPS) * lg_ref[...] + lb_ref[...]
        o_ref[r * CONV_ROWS:(r + 1) * CONV_ROWS, :] = (y * jax.nn.sigmoid(y)).astype(o_ref.dtype)


def _conv(proj, conv_w, conv_b, ln_g, ln_b):
    bsz, seq, _ = proj.shape
    t = min(CONV_T, seq)
    ca, cg = COL_CU // CONV_CH, COL_CU // CONV_CH + 1
    hb = t // CONV_HALO
    prev = lambda i: jnp.maximum(i * hb - 1, 0)
    row = lambda v: v.reshape(1, CONV_CH)
    return pl.pallas_call(
        _conv_kernel,
        grid=(bsz, seq // t),
        in_specs=[
            pl.BlockSpec((None, t, CONV_CH), lambda b, i: (b, i, ca)),
            pl.BlockSpec((None, t, CONV_CH), lambda b, i: (b, i, cg)),
            pl.BlockSpec((None, CONV_HALO, CONV_CH), lambda b, i: (b, prev(i), ca)),
            pl.BlockSpec((None, CONV_HALO, CONV_CH), lambda b, i: (b, prev(i), cg)),
            pl.BlockSpec((CONV_WIDTH, CONV_CH), lambda b, i: (0, 0)),
            pl.BlockSpec((1, CONV_CH), lambda b, i: (0, 0)),
            pl.BlockSpec((1, CONV_CH), lambda b, i: (0, 0)),
            pl.BlockSpec((1, CONV_CH), lambda b, i: (0, 0)),
        ],
        out_specs=pl.BlockSpec((None, t, CONV_CH), lambda b, i: (b, i, 0)),
        out_shape=jax.ShapeDtypeStruct((bsz, seq, CONV_CH), BF16),
        scratch_shapes=[pltpu.VMEM((CONV_HALO + t, CONV_CH), F32)],
        compiler_params=_params(("parallel", "parallel")),
        name="conv",
    )(proj, proj, proj, proj, conv_w, row(conv_b), row(ln_g), row(ln_b))


def _merge_kernel(ya_ref, yb_ref, yc_ref, yd_ref, gates_ref, x_ref, gt_ref, gp_ref, gf_ref, scf_ref, shf_ref,
                  wa_ref, wb_ref, wc_ref, wd_ref, wo_ref, o_ref, h_ref):
    d = x_ref.shape[1]
    y = None
    for i, (br, w) in enumerate(((ya_ref, wa_ref), (yb_ref, wb_ref), (yc_ref, wc_ref), (yd_ref, wd_ref))):
        t = jnp.dot(br[...], w[...], preferred_element_type=F32) * gates_ref[:, i * d:(i + 1) * d].astype(F32)
        y = t if y is None else y + t
    z = jnp.dot(y.astype(BF16), wo_ref[...], preferred_element_type=F32)
    x_new = x_ref[...] + gt_ref[...] * (_rms(z) * gp_ref[...])
    o_ref[...] = x_new
    h_ref[...] = _modulated_norm(x_new, gf_ref[...], scf_ref[...], shf_ref[...]).astype(h_ref.dtype)


def _merge(ya, yb, yc, yd, gates, x, gt, g_post, g_ffn, sc_f, sh_f, wa, wb, wc, wd, wo, layer, *, tm):
    bsz, seq, d = x.shape
    tok = lambda width: pl.BlockSpec((None, tm, width), lambda b, i: (b, i, 0))
    vec = lambda: pl.BlockSpec((None, 1, d), lambda b, i: (b, 0, 0))
    par = lambda: pl.BlockSpec((1, d), lambda b, i: (0, 0))
    full = lambda a: pl.BlockSpec((None,) + a.shape[1:], lambda b, i: (layer, 0, 0),
                                  pipeline_mode=pl.Buffered(1))
    return pl.pallas_call(
        _merge_kernel,
        grid=(bsz, seq // tm),
        in_specs=[tok(A_W), tok(B_W), tok(CONV_CH), tok(D_QW), tok(N_BRANCHES * d), tok(d),
                  vec(), par(), par(), vec(), vec(),
                  full(wa), full(wb), full(wc), full(wd), full(wo)],
        out_specs=[tok(d), tok(d)],
        out_shape=[jax.ShapeDtypeStruct((bsz, seq, d), F32), jax.ShapeDtypeStruct((bsz, seq, d), BF16)],
        compiler_params=_params(("parallel", "parallel")),
        name="merge",
    )(ya, yb, yc, yd, gates, x, gt, g_post.reshape(1, d), g_ffn.reshape(1, d), sc_f, sh_f, wa, wb, wc, wd, wo)


def _ffn_kernel(x_ref, h_ref, gt_ref, gp_ref, wgu_ref, wd_ref, o_ref, acc_scr):
    f = pl.program_id(2)
    tf = wd_ref.shape[0]

    @pl.when(f == 0)
    def _():
        acc_scr[...] = jnp.zeros_like(acc_scr)

    gu = jnp.dot(h_ref[...], wgu_ref[...], preferred_element_type=F32)
    g, u = gu[:, :tf], gu[:, tf:]
    a = ((g * jax.nn.sigmoid(g)) * u).astype(BF16)
    acc_scr[...] += jnp.dot(a, wd_ref[...], preferred_element_type=F32)

    @pl.when(f == pl.num_programs(2) - 1)
    def _():
        o_ref[...] = x_ref[...] + gt_ref[...] * (_rms(acc_scr[...]) * gp_ref[...])


def _ffn(x, h, gt, g_post, wgu, wd, layer, *, tm, tf):
    bsz, seq, d = x.shape
    dff = wd.shape[1]
    vec = lambda: pl.BlockSpec((None, 1, d), lambda b, i, f: (b, 0, 0))
    par = lambda: pl.BlockSpec((1, d), lambda b, i, f: (0, 0))
    return pl.pallas_call(
        _ffn_kernel,
        grid=(bsz, seq // tm, dff // tf),
        in_specs=[
            pl.BlockSpec((None, tm, d), lambda b, i, f: (b, i, 0)),
            pl.BlockSpec((None, tm, d), lambda b, i, f: (b, i, 0)),
            vec(), par(),
            pl.BlockSpec((None, d, 2 * tf), lambda b, i, f: (layer, 0, f)),
            pl.BlockSpec((None, tf, d), lambda b, i, f: (layer, f, 0)),
        ],
        out_specs=pl.BlockSpec((None, tm, d), lambda b, i, f: (b, i, 0)),
        out_shape=jax.ShapeDtypeStruct((bsz, seq, d), F32),
        scratch_shapes=[pltpu.VMEM((tm, d), F32)],
        compiler_params=_params(("parallel", "parallel", "arbitrary")),
        name="ffn",
    )(x, h, gt, g_post.reshape(1, d), wgu, wd)


SRC_QKV = 3 * A_W + 3 * B_W
SRC_REST = SRC_QKV + B_HEADS
SRC_GATE = SRC_REST + (PROJ_W - GATE_W - SRC_QKV)
IN_W = SRC_GATE + GATE_W
PACK_R = 256
PACK_SIDE_ROW = PROJ_W_PAD
PACK_ROWS = PROJ_W_PAD + PACK_R
assert GATE_W % PACK_R == 0 and SRC_QKV % PACK_R == 0 and PROJ_W % PACK_R == 0 and PROJ_W_PAD % PACK_R == 0


def _pack_tables():
    s2 = HEAD_DIM ** -0.5 * LOG2E
    src, scale, rows = [], [], []
    for b in range(PACK_ROWS // PACK_R):
        row = b * PACK_R
        if row < GATE_W:
            src.append(SRC_GATE + row); scale.append(1.0); rows.append(PACK_R)
        elif row < GATE_W + SRC_QKV:
            s = row - GATE_W
            is_q = s < A_W or 3 * A_W <= s < 3 * A_W + B_W
            src.append(s); scale.append(s2 if is_q else 1.0); rows.append(PACK_R)
        elif row < PROJ_W:
            s = row - GATE_W - SRC_QKV
            is_q = 2 * CONV_CH <= s < 2 * CONV_CH + D_QW
            src.append(SRC_REST + s); scale.append(D_HEAD_DIM ** -0.5 if is_q else 1.0); rows.append(PACK_R)
        elif row == PACK_SIDE_ROW:
            src.append(SRC_QKV); scale.append(1.0); rows.append(B_HEADS)
        else:
            src.append(0); scale.append(0.0); rows.append(0)
    return np.asarray(src, np.int32), np.asarray(scale, np.float32), np.asarray(rows, np.int32)


def _pack_kernel(src_ref, scale_ref, rows_ref, a_ref, o_ref):
    blk = pl.program_id(0)
    depth, _, d = o_ref.shape
    kt = d // LANES
    keep = lax.broadcasted_iota(jnp.int32, (PACK_R, d), 0) < rows_ref[blk]
    for layer in range(depth):
        cols = [a_ref[pl.ds(t * depth + layer, PACK_R, stride=kt * depth), :] for t in range(kt)]
        o_ref[layer] = jnp.where(keep, jnp.concatenate(cols, axis=1) * scale_ref[blk], 0.0).astype(o_ref.dtype)


def _pack_in_proj(w):
    depth, d, in_w = w.shape
    assert in_w == IN_W and d % LANES == 0
    kt = d // LANES
    view = jnp.transpose(w.reshape(depth, kt, LANES, in_w), (3, 1, 0, 2)).reshape(in_w * kt * depth, LANES)
    src, scale, rows = _pack_tables()
    per_feature = kt * depth
    return pl.pallas_call(
        _pack_kernel,
        grid_spec=pltpu.PrefetchScalarGridSpec(
            num_scalar_prefetch=3,
            grid=(PACK_ROWS // PACK_R,),
            in_specs=[pl.BlockSpec((pl.Element(PACK_R * per_feature), pl.Element(LANES)),
                                   lambda b, src, scale, rows: (src[b] * per_feature, 0))],
            out_specs=pl.BlockSpec((depth, PACK_R, d), lambda b, src, scale, rows: (0, b, 0)),
        ),
        out_shape=jax.ShapeDtypeStruct((depth, PACK_ROWS, d), BF16),
        compiler_params=_params(("parallel",)),
        name="pack_in_proj",
    )(jnp.asarray(src), jnp.asarray(scale), jnp.asarray(rows), view)


CAST_BLOCK_ELEMS = 1 << 20


def _cast_kernel(w_ref, o_ref):
    o_ref[...] = w_ref[...].astype(o_ref.dtype)


def _cast_bf16(w):
    depth, r, c = w.shape
    tr = 16
    while tr * 2 * c <= CAST_BLOCK_ELEMS and r % (tr * 2) == 0:
        tr *= 2
    assert r % tr == 0
    return pl.pallas_call(
        _cast_kernel,
        grid=(depth, r // tr),
        in_specs=[pl.BlockSpec((None, tr, c), lambda l, i: (l, i, 0))],
        out_specs=pl.BlockSpec((None, tr, c), lambda l, i: (l, i, 0)),
        out_shape=jax.ShapeDtypeStruct(w.shape, BF16),
        compiler_params=_params(("parallel", "parallel")),
        name="cast_bf16",
    )(w)


def kernel(x, c, rel_bias, w_mod, b_mod, mix_norm_pre, mix_norm_post, w_in, fox_bias, conv_w, conv_b, conv_ln_g, conv_ln_b, sinks, w_branch_a, w_branch_b, w_branch_c, w_branch_d, w_out, ffn_norm_pre, ffn_norm_post, w_ffn_gate, w_ffn_up, w_ffn_down):
    depth = w_mod.shape[0]
    bsz, seq, d = x.shape
    tm = min(1024, seq)
    tm_small = min(256, seq)
    tm_ffn = min(512, seq)

    mod = _modulation(c, w_mod, b_mod)
    tab_a = _bias_tables(rel_bias[:, :A_HEADS], _moba_buckets(), scale=LOG2E)
    tab_d = _bias_tables(rel_bias[:, A_HEADS:], _swa_buckets(), scale=1.0)[:, 0]

    wa, wb, wc, wd, wo = [_cast_bf16(w) for w in (w_branch_a, w_branch_b, w_branch_c, w_branch_d, w_out)]

    w_proj = _pack_in_proj(w_in)

    for l in range(depth):
        sh_m, sc_m, gt_m, sh_f, sc_f, gt_f = [mod[l, :, None, i * d:(i + 1) * d] for i in range(6)]

        proj, fox_raw, wgu, wdn = _norm_proj(x, mix_norm_pre[l], sc_m, sh_m, w_proj, w_ffn_gate, w_ffn_up, w_ffn_down, l,
                                             tm=tm, tn=PROJ_TN)
        fox_qa, fox_ka, fox_cum = _fox_gate(fox_raw, fox_bias[l], t=min(512, seq))

        ya = _moba(proj, tab_a)
        yb = _fox(proj, fox_qa, fox_ka, _fox_first_block(proj, fox_cum))
        yc = _conv(proj, conv_w[l], conv_b[l], conv_ln_g[l], conv_ln_b[l])
        yd = _swa(proj, sinks[l], tab_d)

        x, h_ffn = _merge(ya, yb, yc, yd, proj, x, gt_m, mix_norm_post[l], ffn_norm_pre[l], sc_f, sh_f,
                          wa, wb, wc, wd, wo, l, tm=tm_small)
        x = _ffn(x, h_ffn, gt_f, ffn_norm_post[l], wgu, wdn, 0, tm=tm_ffn, tf=FFN_TF)
    return x
```

```python
import functools
import math

import jax
import jax.numpy as jnp
import numpy as np
from jax import lax
from jax.experimental import pallas as pl
from jax.experimental.pallas import tpu as pltpu

F32 = jnp.float32
BF16 = jnp.bfloat16
HIGHEST = lax.Precision.HIGHEST

HEAD_DIM = 128
A_HEADS = 4
MOBA_BLOCK = 256
MOBA_TOPK = 3
B_HEADS = 4
CONV_CH = 512
CONV_WIDTH = 31
D_Q_HEADS = 8
D_KV_HEADS = 2
D_HEAD_DIM = 64
WINDOW = 128
N_BUCKETS = 32
MAX_DISTANCE = 1024
N_BRANCHES = 4
RMS_EPS = 1e-6
LN_EPS = 1e-5

A_W = A_HEADS * HEAD_DIM
B_W = B_HEADS * HEAD_DIM
D_QW = D_Q_HEADS * D_HEAD_DIM
D_KVW = D_KV_HEADS * D_HEAD_DIM

LANES = 128
SUBLANES = 8
VMEM_LIMIT = 56 * 1024 * 1024

NEG = -1e30
LOG2E = math.log2(math.e)
ATT_G = 4
ATT_T = MOBA_BLOCK
ATT_U = 4
FFN_TF = 512
NORM_ROWS = 32

D_MODEL = 2048
GATE_W = N_BRANCHES * D_MODEL
COL_AQ, COL_AK, COL_AV = GATE_W, GATE_W + A_W, GATE_W + 2 * A_W
COL_BQ, COL_BK, COL_BV = COL_AV + A_W, COL_AV + A_W + B_W, COL_AV + A_W + 2 * B_W
COL_CU = COL_BV + B_W
COL_DQ = COL_CU + 2 * CONV_CH
COL_DK = COL_DQ + D_QW
COL_DV = COL_DK + D_KVW
PROJ_W = COL_DV + D_KVW
PROJ_TN = 1024
PROJ_W_PAD = -(-PROJ_W // PROJ_TN) * PROJ_TN
MOBA_FAR = -(-(MAX_DISTANCE + MOBA_BLOCK - 1) // MOBA_BLOCK)
VOID_COL = LANES - 1
assert ATT_G % ATT_U == 0 and (MOBA_FAR - 1) % ATT_U == 0


def _params(sem, vmem=VMEM_LIMIT):
    return pltpu.CompilerParams(dimension_semantics=sem, vmem_limit_bytes=vmem)


def _t5_bucket(dist):
    max_exact = N_BUCKETS // 2
    d = jnp.maximum(dist, 0)
    log_ratio = jnp.log(jnp.maximum(d, 1).astype(jnp.float32) / max_exact) / math.log(MAX_DISTANCE / max_exact)
    large = max_exact + (log_ratio * (N_BUCKETS - max_exact)).astype(jnp.int32)
    large = jnp.minimum(large, N_BUCKETS - 1)
    return jnp.where(d < max_exact, d, large)


def _sigmoid(x):
    return 0.5 * jnp.tanh(0.5 * x) + 0.5


def _rms(y):
    return y * lax.rsqrt(jnp.mean(y * y, axis=-1, keepdims=True) + RMS_EPS)


def _modulated_norm(x, g, sc, sh):
    return (_rms(x) * g) * (1.0 + sc) + sh


MOD_TN = 512
MOD_KC = 256


def _mod_kernel(ct_ref, w_ref, b_ref, o_ref):
    d, nb = ct_ref.shape
    tn = w_ref.shape[1]
    ct = ct_ref[...]
    ca = ct * jax.nn.sigmoid(ct)
    rows = []
    for r in range(nb):
        acc = jnp.zeros((1, tn), F32)
        for kc in range(d // MOD_KC):
            sl = slice(kc * MOD_KC, (kc + 1) * MOD_KC)
            acc = acc + jnp.sum(w_ref[sl, :] * ca[sl, r:r + 1], axis=0, keepdims=True)
        rows.append(acc)
    o_ref[...] = jnp.concatenate(rows, axis=0) + b_ref[...]


def _modulation(c, w_mod, b_mod):
    depth, d, n = w_mod.shape
    bsz = c.shape[0]
    return pl.pallas_call(
        _mod_kernel,
        grid=(depth, n // MOD_TN),
        in_specs=[
            pl.BlockSpec((d, bsz), lambda l, j: (0, 0)),
            pl.BlockSpec((None, d, MOD_TN), lambda l, j: (l, 0, j)),
            pl.BlockSpec((None, 1, MOD_TN), lambda l, j: (l, 0, j)),
        ],
        out_specs=pl.BlockSpec((None, bsz, MOD_TN), lambda l, j: (l, 0, j)),
        out_shape=jax.ShapeDtypeStruct((depth, bsz, n), F32),
        compiler_params=_params(("parallel", "parallel")),
        name="mod",
    )(c.T, w_mod, b_mod.reshape(depth, 1, n))


def _bias_table_kernel(rb_ref, bucket_ref, o_ref, *, scale):
    h = pl.program_id(0)
    b = bucket_ref[...]
    acc = jnp.where(b < 0, NEG, 0.0).astype(F32)
    for u in range(N_BUCKETS):
        acc = jnp.where(b == u, rb_ref[h, u] * scale, acc)
    o_ref[...] = acc


def _bias_tables(rel_bias_heads, bucket, *, scale):
    nh = rel_bias_heads.shape[1]
    nt, r, c = bucket.shape
    return pl.pallas_call(
        functools.partial(_bias_table_kernel, scale=scale),
        grid=(nh, nt),
        in_specs=[
            pl.BlockSpec(memory_space=pltpu.SMEM),
            pl.BlockSpec((None, r, c), lambda h, t: (t, 0, 0)),
        ],
        out_specs=pl.BlockSpec((None, None, r, c), lambda h, t: (h, t, 0, 0)),
        out_shape=jax.ShapeDtypeStruct((nh, nt, r, c), F32),
        compiler_params=_params(("parallel", "parallel")),
        name="bias_table",
    )(rel_bias_heads.T, bucket)


def _moba_buckets():
    blk = MOBA_BLOCK
    i = jnp.arange(blk)[None, :]
    j = jnp.arange(blk)[:, None]
    tabs = []
    for delta in range(MOBA_FAR):
        dist = delta * blk + i - j
        tabs.append(jnp.where(dist >= 0, _t5_bucket(dist), -1))
    tabs.append(_t5_bucket(jnp.full((blk, blk), MAX_DISTANCE, jnp.int32)))
    return jnp.stack(tabs).astype(jnp.int32)


def _swa_buckets():
    qi = jnp.arange(WINDOW)[None, :]
    kj = jnp.arange(2 * WINDOW)[:, None]
    dist = qi + WINDOW - kj
    in_win = (dist >= 0) & (dist < WINDOW)
    return jnp.where(in_win, _t5_bucket(dist), -1).astype(jnp.int32)[None]


def _norm_proj_kernel(x_ref, g_ref, sc_ref, sh_ref, w_ref, wg_ref, wu_ref, wdn_ref,
                      o_ref, os_ref, wgu_out, wdn_out, h_scr):
    j = pl.program_id(2)
    tn = w_ref.shape[0]
    last = pl.num_programs(2) - 1

    def cast_ffn_weights():
        tf = FFN_TF
        for f in range(wg_ref.shape[1] // tf):
            wgu_out[:, 2 * f * tf:(2 * f + 1) * tf] = wg_ref[:, f * tf:(f + 1) * tf].astype(BF16)
            wgu_out[:, (2 * f + 1) * tf:(2 * f + 2) * tf] = wu_ref[:, f * tf:(f + 1) * tf].astype(BF16)
        wdn_out[...] = wdn_ref[...].astype(BF16)

    @pl.when(j == 0)
    def _():
        gain = g_ref[...] * (1.0 + sc_ref[...])

        def norm_rows(r, carry):
            rows = pl.ds(pl.multiple_of(r * NORM_ROWS, NORM_ROWS), NORM_ROWS)
            h_scr[rows, :] = (_rms(x_ref[rows, :]) * gain + sh_ref[...]).astype(BF16)
            return carry

        lax.fori_loop(0, x_ref.shape[0] // NORM_ROWS, norm_rows, 0, unroll=8)

    @pl.when(j < GATE_W // tn)
    def _():
        o_ref[...] = _sigmoid(_dot_nt(h_scr[...], w_ref[...])).astype(o_ref.dtype)
        cast_ffn_weights()

    @pl.when((j >= GATE_W // tn) & (j < last))
    def _():
        o_ref[...] = _dot_nt(h_scr[...], w_ref[...]).astype(o_ref.dtype)
        cast_ffn_weights()

    @pl.when(j == last)
    def _():
        acc = _dot_nt(h_scr[...], w_ref[...])
        o_ref[...] = acc.astype(o_ref.dtype)
        col = PACK_SIDE_ROW % tn
        os_ref[...] = acc[:, col:col + os_ref.shape[1]]
        cast_ffn_weights()


def _rows_per_step(rows, steps, tile):
    r = tile
    while rows % r or rows // r > steps:
        r += tile
    return r


def _norm_proj(x, g, sc, sh, w, w_gate, w_up, w_down, layer, *, tm, tn):
    bsz, seq, d = x.shape
    dff = w_down.shape[1]
    n = PROJ_W_PAD
    ns = LANES
    ni, nj = seq // tm, n // tn
    assert GATE_W % tn == 0 and n % tn == 0 and w.shape[1] == PACK_ROWS and dff % FFN_TF == 0
    assert PACK_SIDE_ROW // tn == nj - 1 and GATE_W // tn < nj - 1 and PACK_SIDE_ROW % ns == 0
    r_gu = _rows_per_step(d, bsz * ni * nj, 2 * SUBLANES)
    r_dn = _rows_per_step(dff, bsz * ni * nj, 2 * SUBLANES)
    step = lambda b, i, j: (b * ni + i) * nj + j
    gu_blk = lambda b, i, j: (layer, jnp.minimum(step(b, i, j), d // r_gu - 1), 0)
    dn_blk = lambda b, i, j: (layer, jnp.minimum(step(b, i, j), dff // r_dn - 1), 0)
    first = lambda idx: (lambda b, i, j: (0,) + idx(b, i, j)[1:])
    return pl.pallas_call(
        _norm_proj_kernel,
        grid=(bsz, ni, nj),
        in_specs=[
            pl.BlockSpec((None, tm, d), lambda b, i, j: (b, i, 0)),
            pl.BlockSpec((1, d), lambda b, i, j: (0, 0)),
            pl.BlockSpec((None, 1, d), lambda b, i, j: (b, 0, 0)),
            pl.BlockSpec((None, 1, d), lambda b, i, j: (b, 0, 0)),
            pl.BlockSpec((None, tn, d), lambda b, i, j: (layer, j, 0)),
            pl.BlockSpec((None, r_gu, dff), gu_blk),
            pl.BlockSpec((None, r_gu, dff), gu_blk),
            pl.BlockSpec((None, r_dn, d), dn_blk),
        ],
        out_specs=[pl.BlockSpec((None, tm, tn), lambda b, i, j: (b, i, j)),
                   pl.BlockSpec((None, tm, ns), lambda b, i, j: (b, i, 0)),
                   pl.BlockSpec((None, r_gu, 2 * dff), first(gu_blk)),
                   pl.BlockSpec((None, r_dn, d), first(dn_blk))],
        out_shape=[jax.ShapeDtypeStruct((bsz, seq, n), BF16), jax.ShapeDtypeStruct((bsz, seq, ns), F32),
                   jax.ShapeDtypeStruct((1, d, 2 * dff), BF16), jax.ShapeDtypeStruct((1, dff, d), BF16)],
        scratch_shapes=[pltpu.VMEM((tm, d), BF16)],
        compiler_params=_params(("arbitrary", "arbitrary", "arbitrary")),
        name="norm_proj",
    )(x, g.reshape(1, d), sc, sh, w, w_gate, w_up, w_down)


def _split3(c):
    hi = c.astype(BF16).astype(F32)
    mid = (c - hi).astype(BF16).astype(F32)
    lo = (c - hi - mid).astype(BF16).astype(F32)
    return hi, mid, lo


def _fox_gate_kernel(x_ref, fb_ref, qa_ref, ka_ref, cum_ref, carry_scr):
    t = x_ref.shape[0]

    @pl.when(pl.program_id(1) == 0)
    def _():
        carry_scr[...] = jnp.zeros_like(carry_scr)

    lf = jax.nn.log_sigmoid(x_ref[...] + fb_ref[...])
    tri = (lax.broadcasted_iota(jnp.int32, (t, t), 0) >= lax.broadcasted_iota(jnp.int32, (t, t), 1)).astype(F32)
    c = jnp.dot(tri, lf, precision=HIGHEST, preferred_element_type=F32) + carry_scr[...]
    carry_scr[...] = c[t - 1:t, :]
    cum_ref[...] = c * LOG2E
    lane = lax.broadcasted_iota(jnp.int32, (t, LANES), 1)
    sub = lax.broadcasted_iota(jnp.int32, (SUBLANES, t), 0)
    ct = c.T * LOG2E
    for h in range(B_HEADS):
        hi, mid, lo = _split3(jnp.broadcast_to(c[:, h:h + 1], (t, LANES)) * LOG2E)
        ka = jnp.where(lane == 0, -hi, jnp.where(lane == 1, -mid, jnp.where(lane == 2, -lo, jnp.where(lane < 6, 1.0, 0.0))))
        ka_ref[h] = ka.astype(BF16)
        hi, mid, lo = _split3(jnp.broadcast_to(ct[h:h + 1, :], (SUBLANES, t)))
        qa = jnp.where(sub < 3, 1.0, jnp.where(sub == 3, hi, jnp.where(sub == 4, mid, jnp.where(sub == 5, lo, 0.0))))
        qa_ref[h] = jnp.concatenate([qa, jnp.zeros((LANES - SUBLANES, t), F32)], axis=0).astype(BF16)


def _fox_gate(raw, fox_b, *, t):
    bsz, seq, _ = raw.shape
    fb = jnp.zeros((1, LANES), F32).at[0, :B_HEADS].set(fox_b)
    aug = pl.BlockSpec((None, B_HEADS, t, LANES), lambda b, i: (b, 0, i, 0))
    aug_t = pl.BlockSpec((None, B_HEADS, LANES, t), lambda b, i: (b, 0, 0, i))
    return pl.pallas_call(
        _fox_gate_kernel,
        grid=(bsz, seq // t),
        in_specs=[
            pl.BlockSpec((None, t, LANES), lambda b, i: (b, i, 0)),
            pl.BlockSpec((1, LANES), lambda b, i: (0, 0)),
        ],
        out_specs=[aug_t, aug, pl.BlockSpec((None, t, LANES), lambda b, i: (b, i, 0))],
        out_shape=[jax.ShapeDtypeStruct((bsz, B_HEADS, LANES, seq), BF16),
                   jax.ShapeDtypeStruct((bsz, B_HEADS, seq, LANES), BF16),
                   jax.ShapeDtypeStruct((bsz, seq, LANES), F32)],
        scratch_shapes=[pltpu.VMEM((1, LANES), F32)],
        compiler_params=_params(("parallel", "arbitrary")),
        name="fox_gate",
    )(raw, fb)


ONES_ROWS = 16


def _softmax_step(zt, vt, carry):
    m, acc = carry
    m_new = jnp.maximum(m, jnp.max(zt, axis=0, keepdims=True))
    p = jnp.exp2(zt - m_new)
    acc = jnp.exp2(m - m_new) * acc + jnp.dot(vt, p.astype(BF16), preferred_element_type=F32)
    return m_new, acc


def _softmax_init(tq, dh):
    return (jnp.full((1, tq), NEG, F32), jnp.zeros((dh + ONES_ROWS, tq), F32))


def _softmax_finish(carry, dh):
    _, acc = carry
    return (acc[:dh, :] / acc[dh:dh + 1, :]).T


def _fill_vt(vt_scr, v_ref, chunk):
    seq, dh = v_ref.shape
    for cix in range(seq // chunk):
        sl = slice(cix * chunk, (cix + 1) * chunk)
        vt_scr[:dh, sl] = _transpose_bf16(v_ref[sl, :])
    vt_scr[dh:, :] = jnp.ones((ONES_ROWS, seq), BF16)


ATT_LOOKAHEAD = 6


def _run_substeps(steps, carry):
    carry = list(carry)
    pending = {}
    for s in range(min(ATT_LOOKAHEAD, len(steps))):
        pending[s] = steps[s][1]()
    for s, (j, _, fix, vt) in enumerate(steps):
        if s + ATT_LOOKAHEAD < len(steps):
            pending[s + ATT_LOOKAHEAD] = steps[s + ATT_LOOKAHEAD][1]()
        z, cj = pending.pop(s), carry[j]
        if fix is not None:
            z, cj = fix(z, cj)
        carry[j] = _softmax_step(z, vt, cj)
    return carry


def _dense_loop(start, n_blocks, make_steps, carry):
    big = 2 * ATT_U
    carry = lax.fori_loop(0, n_blocks // big,
                          lambda grp, c: tuple(_run_substeps(make_steps(start + grp * big, big), c)), carry)
    first = start + (n_blocks // big) * big
    return lax.fori_loop(0, (start + n_blocks - first) // ATT_U,
                         lambda grp, c: tuple(_run_substeps(make_steps(first + grp * ATT_U, ATT_U), c)), carry)


def _transpose_bf16(x):
    return x.astype(F32).T.astype(BF16)


_NT = (((1,), (1,)), ((), ()))


def _dot_nt(a, b):
    return lax.dot_general(a, b, _NT, preferred_element_type=F32)


def _moba_kernel(cfar_ref, q_ref, k_ref, v_ref, tab_ref, o_ref, kmean_scr, sc_scr, qt_scr, vt_scr):
    g, t = ATT_G, ATT_T
    seq, dh = k_ref.shape
    nblk = seq // t
    rows = g * t
    i = pl.program_id(2)
    base = i * g

    @pl.when(i == 0)
    def _():
        r = lax.broadcasted_iota(jnp.int32, (nblk, seq), 0)
        c = lax.broadcasted_iota(jnp.int32, (nblk, seq), 1)
        avg = jnp.where((c >= r * t) & (c < (r + 1) * t), 1.0 / t, 0.0).astype(BF16)
        kmean_scr[...] = jnp.dot(avg, k_ref[...], preferred_element_type=F32)
        _fill_vt(vt_scr, v_ref, rows)

    q = q_ref[...]
    qt_scr[:dh, :] = _transpose_bf16(q)
    sc = lax.dot_general(kmean_scr[...], q.astype(F32), _NT, precision=HIGHEST, preferred_element_type=F32)
    row = lax.broadcasted_iota(jnp.int32, (nblk, rows), 0)
    own = base + lax.broadcasted_iota(jnp.int32, (nblk, rows), 1) // t
    past = row < own
    sc = jnp.where(past, sc, -jnp.inf)
    sc_scr[...] = sc

    def rank_body(m, rank):
        sm = sc_scr[pl.ds(m, 1), :]
        return rank + jnp.where(sm > sc, 1.0, jnp.where(sm == sc, jnp.where(row > m, 1.0, 0.0), 0.0))

    rank = lax.fori_loop(0, base + g - 1, rank_body, jnp.zeros((nblk, rows), F32))
    pen = jnp.where(past, jnp.where(rank < MOBA_TOPK, 0.0, NEG), jnp.where(row == own, 0.0, NEG))
    void = jnp.where(lax.broadcasted_iota(jnp.int32, (SUBLANES, rows), 0) == SUBLANES - 1, NEG, 0.0)
    pen = jnp.concatenate([pen, jnp.zeros((LANES - SUBLANES - nblk, rows), F32), void], axis=0)
    qt_scr[dh:, :] = pen.astype(BF16)

    lane = lax.broadcasted_iota(jnp.int32, (t, LANES), 1)

    def kv(n, col):
        start = pl.multiple_of(n * t, t)
        onehot = jnp.where(lane == col, 1.0, 0.0).astype(BF16)
        return jnp.concatenate([k_ref[pl.ds(start, t), :], onehot], axis=1), vt_scr[:, pl.ds(start, t)]

    def logits(kb, j):
        return jnp.dot(kb, qt_scr[:, j * t:(j + 1) * t], preferred_element_type=F32)

    def far_steps(first, count):
        steps = []
        for u in range(count):
            kb, vb = kv(first + u, first + u)
            steps += [(j, functools.partial(logits, kb, j), None, vb) for j in range(g)]
        return steps

    carry = _dense_loop(0, jnp.maximum(base - (MOBA_FAR - 1), 0), far_steps,
                        tuple(_softmax_init(t, dh) for _ in range(g)))
    cfar = cfar_ref[pl.program_id(1)]

    def near_fix(delta, z, cj):
        if delta == MOBA_FAR - 1:
            m, acc = cj
            cj = (m + cfar, acc)
        return tab_ref[delta] + z, cj

    steps = []
    for k in range(-(MOBA_FAR - 1), g):
        n = base + k
        if k < 0:
            kb, vb = kv(jnp.maximum(n, 0), jnp.where(n >= 0, n, VOID_COL))
        else:
            kb, vb = kv(n, n)
        for j in range(max(k, 0), g):
            fix = functools.partial(near_fix, j - k) if j - k < MOBA_FAR else None
            steps.append((j, functools.partial(logits, kb, j), fix, vb))
    carry = _run_substeps(steps, carry)
    for j in range(g):
        o_ref[j * t:(j + 1) * t, :] = _softmax_finish(carry[j], dh).astype(o_ref.dtype)


def _moba(proj, tab):
    bsz, seq, _ = proj.shape
    g, t = ATT_G, ATT_T
    nblk = seq // t
    assert seq % (g * t) == 0 and nblk <= LANES - SUBLANES and nblk % SUBLANES == 0
    qc, kc, vc = COL_AQ // HEAD_DIM, COL_AK // HEAD_DIM, COL_AV // HEAD_DIM
    cfar = tab[:, MOBA_FAR, 0, 0]
    return pl.pallas_call(
        _moba_kernel,
        grid=(bsz, A_HEADS, nblk // g),
        in_specs=[
            pl.BlockSpec(memory_space=pltpu.SMEM),
            pl.BlockSpec((None, g * t, HEAD_DIM), lambda b, h, i: (b, i, qc + h)),
            pl.BlockSpec((None, seq, HEAD_DIM), lambda b, h, i: (b, 0, kc + h)),
            pl.BlockSpec((None, seq, HEAD_DIM), lambda b, h, i: (b, 0, vc + h)),
            pl.BlockSpec((None, MOBA_FAR, t, t), lambda b, h, i: (h, 0, 0, 0)),
        ],
        out_specs=pl.BlockSpec((None, g * t, HEAD_DIM), lambda b, h, i: (b, i, h)),
        out_shape=jax.ShapeDtypeStruct((bsz, seq, A_W), BF16),
        scratch_shapes=[pltpu.VMEM((nblk, HEAD_DIM), F32), pltpu.VMEM((nblk, g * t), F32),
                        pltpu.VMEM((HEAD_DIM + LANES, g * t), BF16), pltpu.VMEM((HEAD_DIM + ONES_ROWS, seq), BF16)],
        compiler_params=_params(("parallel", "parallel", "arbitrary")),
        name="moba",
    )(cfar, proj, proj, proj, tab)


def _fox_kernel(start_ref, q_ref, qa_ref, k_ref, ka_ref, v_ref, o_ref, qt_scr, vt_scr):
    g, t = ATT_G, ATT_T
    rows = g * t
    seq, dh = k_ref.shape
    base = pl.program_id(2) * g
    first = start_ref[(pl.program_id(0) * pl.num_programs(1) + pl.program_id(1)) * pl.num_programs(2)
                      + pl.program_id(2)]

    @pl.when(base == 0)
    def _():
        _fill_vt(vt_scr, v_ref, rows)

    qt_scr[:dh, :] = _transpose_bf16(q_ref[...])
    qt_scr[dh:, :] = qa_ref[...]

    def kv(n):
        start = pl.multiple_of(n * t, t)
        return (jnp.concatenate([k_ref[pl.ds(start, t), :], ka_ref[pl.ds(start, t), :]], axis=1),
                vt_scr[:, pl.ds(start, t)])

    def logits(kb, j):
        return jnp.dot(kb, qt_scr[:, j * t:(j + 1) * t], preferred_element_type=F32)

    def dense_steps(first, count):
        steps = []
        for u in range(count):
            kb, vb = kv(first + u)
            steps += [(j, functools.partial(logits, kb, j), None, vb) for j in range(g)]
        return steps

    carry = _dense_loop(first, base - first, dense_steps, tuple(_softmax_init(t, dh) for _ in range(g)))
    causal = lax.broadcasted_iota(jnp.int32, (t, t), 0) <= lax.broadcasted_iota(jnp.int32, (t, t), 1)

    def diag_fix(z, cj):
        return jnp.where(causal, z, NEG), cj

    steps = []
    for k in range(g):
        kb, vb = kv(base + k)
        steps += [(j, functools.partial(logits, kb, j), diag_fix if j == k else None, vb) for j in range(k, g)]
    carry = _run_substeps(steps, carry)
    for j in range(g):
        o_ref[j * t:(j + 1) * t, :] = _softmax_finish(carry[j], dh).astype(o_ref.dtype)


FOX_SKIP_BITS = 200.0
FOX_NORM_SLACK = 1.0 + 2.0 ** -6


def _fox_bounds_kernel(q_ref, k_ref, cum_ref, o_ref, krun_scr, kpref_scr, clast_scr):
    i = pl.program_id(1)
    rows, width = q_ref.shape
    g = rows // ATT_T

    @pl.when(i == 0)
    def _():
        krun_scr[...] = jnp.zeros_like(krun_scr)
        kpref_scr[...] = jnp.zeros_like(kpref_scr)
        clast_scr[...] = jnp.zeros_like(clast_scr)

    head_of = lax.broadcasted_iota(jnp.int32, (width, LANES), 0) // HEAD_DIM
    head_sum = jnp.where(head_of == lax.broadcasted_iota(jnp.int32, (width, LANES), 1), 1.0, 0.0).astype(BF16)

    def max_norm(ref):
        x = ref[...].astype(F32)
        n2 = jnp.dot((x * x).astype(BF16), head_sum, preferred_element_type=F32)
        return jnp.sqrt(jnp.max(n2, axis=0, keepdims=True)) * FOX_NORM_SLACK

    qn, kn = max_norm(q_ref), max_norm(k_ref)
    c_first, c_last = cum_ref[0:1, :], cum_ref[rows - 1:rows, :]
    tile = lax.broadcasted_iota(jnp.int32, kpref_scr.shape, 0)
    ub = qn * kpref_scr[...] + (c_first - clast_scr[...]) + qn * kn
    ok = (ub < -FOX_SKIP_BITS) & (tile < i)
    o_ref[...] = jnp.max(jnp.where(ok, (tile + 1) * g, 0), axis=0, keepdims=True)
    krun = jnp.maximum(krun_scr[...], kn)
    krun_scr[...] = krun
    kpref_scr[pl.ds(i, 1), :] = krun
    clast_scr[pl.ds(i, 1), :] = c_last


def _fox_first_block(proj, cum2):
    bsz, seq, _ = proj.shape
    rows = ATT_G * ATT_T
    ntile = seq // rows
    hist = pltpu.VMEM((-(-ntile // SUBLANES) * SUBLANES, LANES), F32)
    first = pl.pallas_call(
        _fox_bounds_kernel,
        grid=(bsz, ntile),
        in_specs=[
            pl.BlockSpec((None, rows, B_W), lambda b, i: (b, i, COL_BQ // B_W)),
            pl.BlockSpec((None, rows, B_W), lambda b, i: (b, i, COL_BK // B_W)),
            pl.BlockSpec((None, rows, LANES), lambda b, i: (b, i, 0)),
        ],
        out_specs=pl.BlockSpec((None, None, 1, LANES), lambda b, i: (b, i, 0, 0)),
        out_shape=jax.ShapeDtypeStruct((bsz, ntile, 1, LANES), jnp.int32),
        scratch_shapes=[pltpu.VMEM((1, LANES), F32), hist, hist],
        compiler_params=_params(("parallel", "arbitrary")),
        name="fox_bounds",
    )(proj, proj, cum2)
    return first[:, :, 0, :B_HEADS].transpose(0, 2, 1).reshape(-1)


def _fox(proj, qa, ka, first):
    bsz, seq, _ = proj.shape
    g, t = ATT_G, ATT_T
    assert seq % (g * t) == 0
    qc, kc, vc = COL_BQ // HEAD_DIM, COL_BK // HEAD_DIM, COL_BV // HEAD_DIM
    return pl.pallas_call(
        _fox_kernel,
        grid_spec=pltpu.PrefetchScalarGridSpec(
            num_scalar_prefetch=1,
            grid=(bsz, B_HEADS, seq // (g * t)),
            in_specs=[
                pl.BlockSpec((None, g * t, HEAD_DIM), lambda b, h, i, s: (b, i, qc + h)),
                pl.BlockSpec((None, None, LANES, g * t), lambda b, h, i, s: (b, h, 0, i)),
                pl.BlockSpec((None, seq, HEAD_DIM), lambda b, h, i, s: (b, 0, kc + h)),
                pl.BlockSpec((None, None, seq, LANES), lambda b, h, i, s: (b, h, 0, 0)),
                pl.BlockSpec((None, seq, HEAD_DIM), lambda b, h, i, s: (b, 0, vc + h)),
            ],
            out_specs=pl.BlockSpec((None, g * t, HEAD_DIM), lambda b, h, i, s: (b, i, h)),
            scratch_shapes=[pltpu.VMEM((HEAD_DIM + LANES, g * t), BF16),
                            pltpu.VMEM((HEAD_DIM + ONES_ROWS, seq), BF16)],
        ),
        out_shape=jax.ShapeDtypeStruct((bsz, seq, B_W), BF16),
        compiler_params=_params(("parallel", "parallel", "arbitrary")),
        name="fox",
    )(first, proj, qa, proj, ka, proj)


SWA_BLOCKS = 4


def _swa_kernel(sink_ref, q_ref, kp_ref, kc_ref, vp_ref, vc_ref, tab_ref, o_ref):
    w = WINDOW
    dh = D_HEAD_DIM
    grp = D_Q_HEADS // D_KV_HEADS
    i = pl.program_id(1)
    nblk = q_ref.shape[0] // w
    span = (nblk + 1) * w
    kall = jnp.concatenate([kp_ref[...], kc_ref[...]], axis=0)
    vt = jnp.concatenate([_transpose_bf16(vp_ref[...]), _transpose_bf16(vc_ref[...])], axis=1)
    ones = jnp.ones((ONES_ROWS, span), BF16)
    kg = [kall[:, g * dh:(g + 1) * dh] for g in range(D_KV_HEADS)]
    vg = [jnp.concatenate([vt[g * dh:(g + 1) * dh, :], ones], axis=0) for g in range(D_KV_HEADS)]
    qt = _transpose_bf16(q_ref[...])
    key = lax.broadcasted_iota(jnp.int32, (2 * w, w), 0)
    first = key >= jnp.where(i > 0, 0, w)

    def block_logits(blk):
        return [jnp.dot(kg[h // grp][blk * w:(blk + 2) * w, :], qt[h * dh:(h + 1) * dh, blk * w:(blk + 1) * w],
                        preferred_element_type=F32) for h in range(D_Q_HEADS)]

    logits = block_logits(0)
    for blk in range(nblk):
        nxt = block_logits(blk + 1) if blk + 1 < nblk else None
        outs = []
        for h in range(D_Q_HEADS):
            s = tab_ref[h] + logits[h]
            if blk == 0:
                s = jnp.where(first, s, NEG)
            sink = sink_ref[h]
            m = jnp.maximum(jnp.max(s, axis=0, keepdims=True), sink)
            p = jnp.exp(s - m)
            acc = jnp.dot(vg[h // grp][:, blk * w:(blk + 2) * w], p.astype(BF16), preferred_element_type=F32)
            den = acc[dh:dh + 1, :] + jnp.exp(sink - m)
            outs.append(acc[:dh, :] / den)
        o_ref[blk * w:(blk + 1) * w, :] = jnp.concatenate(outs, axis=0).T.astype(o_ref.dtype)
        logits = nxt


def _swa(proj, sinks, tab):
    bsz, seq, _ = proj.shape
    w = WINDOW
    nb = SWA_BLOCKS
    assert seq % (nb * w) == 0
    qc, kc, vc = COL_DQ // D_QW, COL_DK // D_KVW, COL_DV // D_KVW
    prev = lambda i: jnp.maximum(i * nb - 1, 0)
    return pl.pallas_call(
        _swa_kernel,
        grid=(bsz, seq // (nb * w)),
        in_specs=[
            pl.BlockSpec(memory_space=pltpu.SMEM),
            pl.BlockSpec((None, nb * w, D_QW), lambda b, i: (b, i, qc)),
            pl.BlockSpec((None, w, D_KVW), lambda b, i: (b, prev(i), kc)),
            pl.BlockSpec((None, nb * w, D_KVW), lambda b, i: (b, i, kc)),
            pl.BlockSpec((None, w, D_KVW), lambda b, i: (b, prev(i), vc)),
            pl.BlockSpec((None, nb * w, D_KVW), lambda b, i: (b, i, vc)),
            pl.BlockSpec((D_Q_HEADS, 2 * w, w), lambda b, i: (0, 0, 0)),
        ],
        out_specs=pl.BlockSpec((None, nb * w, D_QW), lambda b, i: (b, i, 0)),
        out_shape=jax.ShapeDtypeStruct((bsz, seq, D_QW), BF16),
        compiler_params=_params(("parallel", "parallel")),
        name="swa",
    )(sinks, proj, proj, proj, proj, proj, tab)


CONV_T = 512
CONV_HALO = 32
CONV_ROWS = 64


def _conv_kernel(a_ref, g_ref, ha_ref, hg_ref, w_ref, b_ref, lg_ref, lb_ref, o_ref, buf):
    t = a_ref.shape[0]
    halo = CONV_HALO
    hp = ha_ref[...].astype(F32) * jax.nn.sigmoid(hg_ref[...].astype(F32))
    buf[0:halo, :] = jnp.where(pl.program_id(1) > 0, hp, 0.0)
    buf[halo:halo + t, :] = a_ref[...].astype(F32) * jax.nn.sigmoid(g_ref[...].astype(F32))
    off = halo - (CONV_WIDTH - 1)
    for r in range(t // CONV_ROWS):
        acc = jnp.broadcast_to(b_ref[...], (CONV_ROWS, CONV_CH))
        for rho in range(SUBLANES):
            rows = CONV_ROWS if rho == 0 else CONV_ROWS + SUBLANES
            part = None
            for s in range(off, off + CONV_WIDTH):
                if s % SUBLANES != rho:
                    continue
                start = r * CONV_ROWS + s - rho
                term = w_ref[s - off:s - off + 1, :] * buf[start:start + rows, :]
                part = term if part is None else part + term
            acc = acc + part[rho:rho + CONV_ROWS, :]
        mu = jnp.mean(acc, axis=-1, keepdims=True)
        d = acc - mu
        var = jnp.mean(d * d, axis=-1, keepdims=True)
        y = d * lax.rsqrt(var + LN_EPS) * lg_ref[...] + lb_ref[...]
        o_ref[r * CONV_ROWS:(r + 1) * CONV_ROWS, :] = (y * jax.nn.sigmoid(y)).astype(o_ref.dtype)


def _conv(proj, conv_w, conv_b, ln_g, ln_b):
    bsz, seq, _ = proj.shape
    t = min(CONV_T, seq)
    ca, cg = COL_CU // CONV_CH, COL_CU // CONV_CH + 1
    hb = t // CONV_HALO
    prev = lambda i: jnp.maximum(i * hb - 1, 0)
    row = lambda v: v.reshape(1, CONV_CH)
    return pl.pallas_call(
        _conv_kernel,
        grid=(bsz, seq // t),
        in_specs=[
            pl.BlockSpec((None, t, CONV_CH), lambda b, i: (b, i, ca)),
            pl.BlockSpec((None, t, CONV_CH), lambda b, i: (b, i, cg)),
            pl.BlockSpec((None, CONV_HALO, CONV_CH), lambda b, i: (b, prev(i), ca)),
            pl.BlockSpec((None, CONV_HALO, CONV_CH), lambda b, i: (b, prev(i), cg)),
            pl.BlockSpec((CONV_WIDTH, CONV_CH), lambda b, i: (0, 0)),
            pl.BlockSpec((1, CONV_CH), lambda b, i: (0, 0)),
            pl.BlockSpec((1, CONV_CH), lambda b, i: (0, 0)),
            pl.BlockSpec((1, CONV_CH), lambda b, i: (0, 0)),
        ],
        out_specs=pl.BlockSpec((None, t, CONV_CH), lambda b, i: (b, i, 0)),
        out_shape=jax.ShapeDtypeStruct((bsz, seq, CONV_CH), BF16),
        scratch_shapes=[pltpu.VMEM((CONV_HALO + t, CONV_CH), F32)],
        compiler_params=_params(("parallel", "parallel")),
        name="conv",
    )(proj, proj, proj, proj, conv_w, row(conv_b), row(ln_g), row(ln_b))


def _merge_kernel(ya_ref, yb_ref, yc_ref, yd_ref, gates_ref, x_ref, gt_ref, gp_ref, gf_ref, scf_ref, shf_ref,
                  wa_ref, wb_ref, wc_ref, wd_ref, wo_ref, o_ref, h_ref):
    d = x_ref.shape[1]
    y = None
    for i, (br, w) in enumerate(((ya_ref, wa_ref), (yb_ref, wb_ref), (yc_ref, wc_ref), (yd_ref, wd_ref))):
        t = jnp.dot(br[...], w[...], preferred_element_type=F32) * gates_ref[:, i * d:(i + 1) * d].astype(F32)
        y = t if y is None else y + t
    z = jnp.dot(y.astype(BF16), wo_ref[...], preferred_element_type=F32)
    x_new = x_ref[...] + gt_ref[...] * (_rms(z) * gp_ref[...])
    o_ref[...] = x_new
    h_ref[...] = _modulated_norm(x_new, gf_ref[...], scf_ref[...], shf_ref[...]).astype(h_ref.dtype)


def _merge(ya, yb, yc, yd, gates, x, gt, g_post, g_ffn, sc_f, sh_f, wa, wb, wc, wd, wo, layer, *, tm):
    bsz, seq, d = x.shape
    tok = lambda width: pl.BlockSpec((None, tm, width), lambda b, i: (b, i, 0))
    vec = lambda: pl.BlockSpec((None, 1, d), lambda b, i: (b, 0, 0))
    par = lambda: pl.BlockSpec((1, d), lambda b, i: (0, 0))
    full = lambda a: pl.BlockSpec((None,) + a.shape[1:], lambda b, i: (layer, 0, 0),
                                  pipeline_mode=pl.Buffered(1))
    return pl.pallas_call(
        _merge_kernel,
        grid=(bsz, seq // tm),
        in_specs=[tok(A_W), tok(B_W), tok(CONV_CH), tok(D_QW), tok(N_BRANCHES * d), tok(d),
                  vec(), par(), par(), vec(), vec(),
                  full(wa), full(wb), full(wc), full(wd), full(wo)],
        out_specs=[tok(d), tok(d)],
        out_shape=[jax.ShapeDtypeStruct((bsz, seq, d), F32), jax.ShapeDtypeStruct((bsz, seq, d), BF16)],
        compiler_params=_params(("parallel", "parallel")),
        name="merge",
    )(ya, yb, yc, yd, gates, x, gt, g_post.reshape(1, d), g_ffn.reshape(1, d), sc_f, sh_f, wa, wb, wc, wd, wo)


def _ffn_kernel(x_ref, h_ref, gt_ref, gp_ref, wgu_ref, wd_ref, o_ref, acc_scr):
    f = pl.program_id(2)
    tf = wd_ref.shape[0]

    def partial_down():
        gu = jnp.dot(h_ref[...], wgu_ref[...], preferred_element_type=F32)
        g, u = gu[:, :tf], gu[:, tf:]
        a = ((g * jax.nn.sigmoid(g)) * u).astype(BF16)
        return jnp.dot(a, wd_ref[...], preferred_element_type=F32)

    @pl.when(f == 0)
    def _():
        acc_scr[...] = partial_down()

    @pl.when(f > 0)
    def _():
        acc_scr[...] += partial_down()

    @pl.when(f == pl.num_programs(2) - 1)
    def _():
        o_ref[...] = x_ref[...] + gt_ref[...] * (_rms(acc_scr[...]) * gp_ref[...])


def _ffn(x, h, gt, g_post, wgu, wd, layer, *, tm, tf):
    bsz, seq, d = x.shape
    dff = wd.shape[1]
    vec = lambda: pl.BlockSpec((None, 1, d), lambda b, i, f: (b, 0, 0))
    par = lambda: pl.BlockSpec((1, d), lambda b, i, f: (0, 0))
    return pl.pallas_call(
        _ffn_kernel,
        grid=(bsz, seq // tm, dff // tf),
        in_specs=[
            pl.BlockSpec((None, tm, d), lambda b, i, f: (b, i, 0)),
            pl.BlockSpec((None, tm, d), lambda b, i, f: (b, i, 0)),
            vec(), par(),
            pl.BlockSpec((None, d, 2 * tf), lambda b, i, f: (layer, 0, f)),
            pl.BlockSpec((None, tf, d), lambda b, i, f: (layer, f, 0)),
        ],
        out_specs=pl.BlockSpec((None, tm, d), lambda b, i, f: (b, i, 0)),
        out_shape=jax.ShapeDtypeStruct((bsz, seq, d), F32),
        scratch_shapes=[pltpu.VMEM((tm, d), F32)],
        compiler_params=_params(("parallel", "parallel", "arbitrary")),
        name="ffn",
    )(x, h, gt, g_post.reshape(1, d), wgu, wd)


SRC_QKV = 3 * A_W + 3 * B_W
SRC_REST = SRC_QKV + B_HEADS
SRC_GATE = SRC_REST + (PROJ_W - GATE_W - SRC_QKV)
IN_W = SRC_GATE + GATE_W
PACK_R = 256
PACK_SIDE_ROW = PROJ_W
PACK_ROWS = PROJ_W_PAD
assert GATE_W % PACK_R == 0 and SRC_QKV % PACK_R == 0 and PROJ_W % PACK_R == 0 and PROJ_W_PAD % PACK_R == 0


def _pack_tables():
    s2 = HEAD_DIM ** -0.5 * LOG2E
    src, scale, rows = [], [], []
    for b in range(PACK_ROWS // PACK_R):
        row = b * PACK_R
        if row < GATE_W:
            src.append(SRC_GATE + row); scale.append(1.0); rows.append(PACK_R)
        elif row < GATE_W + SRC_QKV:
            s = row - GATE_W
            is_q = s < A_W or 3 * A_W <= s < 3 * A_W + B_W
            src.append(s); scale.append(s2 if is_q else 1.0); rows.append(PACK_R)
        elif row < PROJ_W:
            s = row - GATE_W - SRC_QKV
            is_q = 2 * CONV_CH <= s < 2 * CONV_CH + D_QW
            src.append(SRC_REST + s); scale.append(D_HEAD_DIM ** -0.5 if is_q else 1.0); rows.append(PACK_R)
        elif row == PACK_SIDE_ROW:
            src.append(SRC_QKV); scale.append(1.0); rows.append(B_HEADS)
        else:
            src.append(0); scale.append(0.0); rows.append(0)
    return np.asarray(src, np.int32), np.asarray(scale, np.float32), np.asarray(rows, np.int32)


def _pack_kernel(src_ref, scale_ref, rows_ref, a_ref, o_ref):
    blk = pl.program_id(0)
    depth, _, d = o_ref.shape
    kt = d // LANES
    keep = lax.broadcasted_iota(jnp.int32, (PACK_R, d), 0) < rows_ref[blk]
    for layer in range(depth):
        cols = [a_ref[pl.ds(t * depth + layer, PACK_R, stride=kt * depth), :] for t in range(kt)]
        o_ref[layer] = jnp.where(keep, jnp.concatenate(cols, axis=1) * scale_ref[blk], 0.0).astype(o_ref.dtype)


def _pack_in_proj(w):
    depth, d, in_w = w.shape
    assert in_w == IN_W and d % LANES == 0
    kt = d // LANES
    view = jnp.transpose(w.reshape(depth, kt, LANES, in_w), (3, 1, 0, 2)).reshape(in_w * kt * depth, LANES)
    src, scale, rows = _pack_tables()
    per_feature = kt * depth
    return pl.pallas_call(
        _pack_kernel,
        grid_spec=pltpu.PrefetchScalarGridSpec(
            num_scalar_prefetch=3,
            grid=(PACK_ROWS // PACK_R,),
            in_specs=[pl.BlockSpec((pl.Element(PACK_R * per_feature), pl.Element(LANES)),
                                   lambda b, src, scale, rows: (src[b] * per_feature, 0))],
            out_specs=pl.BlockSpec((depth, PACK_R, d), lambda b, src, scale, rows: (0, b, 0)),
        ),
        out_shape=jax.ShapeDtypeStruct((depth, PACK_ROWS, d), BF16),
        compiler_params=_params(("parallel",)),
        name="pack_in_proj",
    )(jnp.asarray(src), jnp.asarray(scale), jnp.asarray(rows), view)


CAST_BLOCK_ELEMS = 1 << 20


def _cast_kernel(w_ref, o_ref):
    o_ref[...] = w_ref[...].astype(o_ref.dtype)


def _cast_bf16(w):
    depth, r, c = w.shape
    tr = 16
    while tr * 2 * c <= CAST_BLOCK_ELEMS and r % (tr * 2) == 0:
        tr *= 2
    assert r % tr == 0
    return pl.pallas_call(
        _cast_kernel,
        grid=(depth, r // tr),
        in_specs=[pl.BlockSpec((None, tr, c), lambda l, i: (l, i, 0))],
        out_specs=pl.BlockSpec((None, tr, c), lambda l, i: (l, i, 0)),
        out_shape=jax.ShapeDtypeStruct(w.shape, BF16),
        compiler_params=_params(("parallel", "parallel")),
        name="cast_bf16",
    )(w)


def kernel(x, c, rel_bias, w_mod, b_mod, mix_norm_pre, mix_norm_post, w_in, fox_bias, conv_w, conv_b, conv_ln_g, conv_ln_b, sinks, w_branch_a, w_branch_b, w_branch_c, w_branch_d, w_out, ffn_norm_pre, ffn_norm_post, w_ffn_gate, w_ffn_up, w_ffn_down):
    depth = w_mod.shape[0]
    bsz, seq, d = x.shape
    tm = min(1024, seq)
    tm_small = min(256, seq)
    tm_ffn = min(512, seq)

    mod = _modulation(c, w_mod, b_mod)
    tab_a = _bias_tables(rel_bias[:, :A_HEADS], _moba_buckets(), scale=LOG2E)
    tab_d = _bias_tables(rel_bias[:, A_HEADS:], _swa_buckets(), scale=1.0)[:, 0]

    wa, wb, wc, wd, wo = [_cast_bf16(w) for w in (w_branch_a, w_branch_b, w_branch_c, w_branch_d, w_out)]

    w_proj = _pack_in_proj(w_in)

    for l in range(depth):
        sh_m, sc_m, gt_m, sh_f, sc_f, gt_f = [mod[l, :, None, i * d:(i + 1) * d] for i in range(6)]

        proj, fox_raw, wgu, wdn = _norm_proj(x, mix_norm_pre[l], sc_m, sh_m, w_proj, w_ffn_gate, w_ffn_up, w_ffn_down, l,
                                             tm=tm, tn=PROJ_TN)
        fox_qa, fox_ka, fox_cum = _fox_gate(fox_raw, fox_bias[l], t=min(512, seq))

        ya = _moba(proj, tab_a)
        yb = _fox(proj, fox_qa, fox_ka, _fox_first_block(proj, fox_cum))
        yc = _conv(proj, conv_w[l], conv_b[l], conv_ln_g[l], conv_ln_b[l])
        yd = _swa(proj, sinks[l], tab_d)

        x, h_ffn = _merge(ya, yb, yc, yd, proj, x, gt_m, mix_norm_post[l], ffn_norm_pre[l], sc_f, sh_f,
                          wa, wb, wc, wd, wo, l, tm=tm_small)
        x = _ffn(x, h_ffn, gt_f, ffn_norm_post[l], wgu, wdn, 0, tm=tm_ffn, tf=FFN_TF)
    return x
```

```python
import functools
import math

import jax
import jax.numpy as jnp
import numpy as np
from jax import lax
from jax.experimental import pallas as pl
from jax.experimental.pallas import tpu as pltpu

F32 = jnp.float32
BF16 = jnp.bfloat16
HIGHEST = lax.Precision.HIGHEST

HEAD_DIM = 128
A_HEADS = 4
MOBA_BLOCK = 256
MOBA_TOPK = 3
B_HEADS = 4
CONV_CH = 512
CONV_WIDTH = 31
D_Q_HEADS = 8
D_KV_HEADS = 2
D_HEAD_DIM = 64
WINDOW = 128
N_BUCKETS = 32
MAX_DISTANCE = 1024
N_BRANCHES = 4
RMS_EPS = 1e-6
LN_EPS = 1e-5

A_W = A_HEADS * HEAD_DIM
B_W = B_HEADS * HEAD_DIM
D_QW = D_Q_HEADS * D_HEAD_DIM
D_KVW = D_KV_HEADS * D_HEAD_DIM

LANES = 128
SUBLANES = 8
VMEM_LIMIT = 56 * 1024 * 1024

NEG = -1e30
LOG2E = math.log2(math.e)
ATT_G = 4
ATT_T = MOBA_BLOCK
ATT_U = 4
FFN_TF = 512
NORM_ROWS = 32

D_MODEL = 2048
GATE_W = N_BRANCHES * D_MODEL
COL_AQ, COL_AK, COL_AV = GATE_W, GATE_W + A_W, GATE_W + 2 * A_W
COL_BQ, COL_BK, COL_BV = COL_AV + A_W, COL_AV + A_W + B_W, COL_AV + A_W + 2 * B_W
COL_CU = COL_BV + B_W
COL_DQ = COL_CU + 2 * CONV_CH
COL_DK = COL_DQ + D_QW
COL_DV = COL_DK + D_KVW
PROJ_W = COL_DV + D_KVW
PROJ_TN = 1024
PROJ_W_PAD = -(-PROJ_W // PROJ_TN) * PROJ_TN
MOBA_FAR = -(-(MAX_DISTANCE + MOBA_BLOCK - 1) // MOBA_BLOCK)
VOID_COL = LANES - 1
assert ATT_G % ATT_U == 0 and (MOBA_FAR - 1) % ATT_U == 0


def _params(sem, vmem=VMEM_LIMIT):
    return pltpu.CompilerParams(dimension_semantics=sem, vmem_limit_bytes=vmem)


def _t5_bucket(dist):
    max_exact = N_BUCKETS // 2
    d = jnp.maximum(dist, 0)
    log_ratio = jnp.log(jnp.maximum(d, 1).astype(jnp.float32) / max_exact) / math.log(MAX_DISTANCE / max_exact)
    large = max_exact + (log_ratio * (N_BUCKETS - max_exact)).astype(jnp.int32)
    large = jnp.minimum(large, N_BUCKETS - 1)
    return jnp.where(d < max_exact, d, large)


def _sigmoid(x):
    return 0.5 * jnp.tanh(0.5 * x) + 0.5


def _rms(y):
    return y * lax.rsqrt(jnp.mean(y * y, axis=-1, keepdims=True) + RMS_EPS)


def _modulated_norm(x, g, sc, sh):
    return (_rms(x) * g) * (1.0 + sc) + sh


MOD_TN = 512
MOD_KC = 256


def _mod_kernel(ct_ref, w_ref, b_ref, o_ref):
    d, nb = ct_ref.shape
    tn = w_ref.shape[1]
    ct = ct_ref[...]
    ca = ct * jax.nn.sigmoid(ct)
    rows = []
    for r in range(nb):
        acc = jnp.zeros((1, tn), F32)
        for kc in range(d // MOD_KC):
            sl = slice(kc * MOD_KC, (kc + 1) * MOD_KC)
            acc = acc + jnp.sum(w_ref[sl, :] * ca[sl, r:r + 1], axis=0, keepdims=True)
        rows.append(acc)
    o_ref[...] = jnp.concatenate(rows, axis=0) + b_ref[...]


def _modulation(c, w_mod, b_mod):
    depth, d, n = w_mod.shape
    bsz = c.shape[0]
    return pl.pallas_call(
        _mod_kernel,
        grid=(depth, n // MOD_TN),
        in_specs=[
            pl.BlockSpec((d, bsz), lambda l, j: (0, 0)),
            pl.BlockSpec((None, d, MOD_TN), lambda l, j: (l, 0, j)),
            pl.BlockSpec((None, 1, MOD_TN), lambda l, j: (l, 0, j)),
        ],
        out_specs=pl.BlockSpec((None, bsz, MOD_TN), lambda l, j: (l, 0, j)),
        out_shape=jax.ShapeDtypeStruct((depth, bsz, n), F32),
        compiler_params=_params(("parallel", "parallel")),
        name="mod",
    )(c.T, w_mod, b_mod.reshape(depth, 1, n))


def _bias_table_kernel(rb_ref, bucket_ref, o_ref, *, scale):
    h = pl.program_id(0)
    b = bucket_ref[...]
    acc = jnp.where(b < 0, NEG, 0.0).astype(F32)
    for u in range(N_BUCKETS):
        acc = jnp.where(b == u, rb_ref[h, u] * scale, acc)
    o_ref[...] = acc


def _bias_tables(rel_bias_heads, bucket, *, scale):
    nh = rel_bias_heads.shape[1]
    nt, r, c = bucket.shape
    return pl.pallas_call(
        functools.partial(_bias_table_kernel, scale=scale),
        grid=(nh, nt),
        in_specs=[
            pl.BlockSpec(memory_space=pltpu.SMEM),
            pl.BlockSpec((None, r, c), lambda h, t: (t, 0, 0)),
        ],
        out_specs=pl.BlockSpec((None, None, r, c), lambda h, t: (h, t, 0, 0)),
        out_shape=jax.ShapeDtypeStruct((nh, nt, r, c), F32),
        compiler_params=_params(("parallel", "parallel")),
        name="bias_table",
    )(rel_bias_heads.T, bucket)


def _moba_buckets():
    blk = MOBA_BLOCK
    i = jnp.arange(blk)[None, :]
    j = jnp.arange(blk)[:, None]
    tabs = []
    for delta in range(MOBA_FAR):
        dist = delta * blk + i - j
        tabs.append(jnp.where(dist >= 0, _t5_bucket(dist), -1))
    tabs.append(_t5_bucket(jnp.full((blk, blk), MAX_DISTANCE, jnp.int32)))
    return jnp.stack(tabs).astype(jnp.int32)


def _swa_buckets():
    qi = jnp.arange(WINDOW)[None, :]
    kj = jnp.arange(2 * WINDOW)[:, None]
    dist = qi + WINDOW - kj
    in_win = (dist >= 0) & (dist < WINDOW)
    return jnp.where(in_win, _t5_bucket(dist), -1).astype(jnp.int32)[None]


def _norm_proj_kernel(x_ref, g_ref, sc_ref, sh_ref, w_ref, wg_ref, wu_ref, wdn_ref,
                      o_ref, os_ref, wgu_out, wdn_out, h_scr):
    j = pl.program_id(2)
    tn = w_ref.shape[0]
    last = pl.num_programs(2) - 1

    def cast_ffn_weights():
        tf = FFN_TF
        for f in range(wg_ref.shape[1] // tf):
            wgu_out[:, 2 * f * tf:(2 * f + 1) * tf] = wg_ref[:, f * tf:(f + 1) * tf].astype(BF16)
            wgu_out[:, (2 * f + 1) * tf:(2 * f + 2) * tf] = wu_ref[:, f * tf:(f + 1) * tf].astype(BF16)
        wdn_out[...] = wdn_ref[...].astype(BF16)

    @pl.when(j == 0)
    def _():
        gain = g_ref[...] * (1.0 + sc_ref[...])

        def norm_rows(r, carry):
            rows = pl.ds(pl.multiple_of(r * NORM_ROWS, NORM_ROWS), NORM_ROWS)
            h_scr[rows, :] = (_rms(x_ref[rows, :]) * gain + sh_ref[...]).astype(BF16)
            return carry

        lax.fori_loop(0, x_ref.shape[0] // NORM_ROWS, norm_rows, 0, unroll=8)

    @pl.when(j < GATE_W // tn)
    def _():
        o_ref[...] = _sigmoid(_dot_nt(h_scr[...], w_ref[...])).astype(o_ref.dtype)
        cast_ffn_weights()

    @pl.when((j >= GATE_W // tn) & (j < last))
    def _():
        o_ref[...] = _dot_nt(h_scr[...], w_ref[...]).astype(o_ref.dtype)
        cast_ffn_weights()

    @pl.when(j == last)
    def _():
        acc = _dot_nt(h_scr[...], w_ref[...])
        o_ref[...] = acc.astype(o_ref.dtype)
        col = PACK_SIDE_ROW % tn
        os_ref[...] = acc[:, col:col + os_ref.shape[1]]
        cast_ffn_weights()


def _rows_per_step(rows, steps, tile):
    r = tile
    while rows % r or rows // r > steps:
        r += tile
    return r


def _norm_proj(x, g, sc, sh, w, w_gate, w_up, w_down, layer, *, tm, tn):
    bsz, seq, d = x.shape
    dff = w_down.shape[1]
    n = PROJ_W_PAD
    ns = LANES
    ni, nj = seq // tm, n // tn
    assert GATE_W % tn == 0 and n % tn == 0 and w.shape[1] == PACK_ROWS and dff % FFN_TF == 0
    assert PACK_SIDE_ROW // tn == nj - 1 and GATE_W // tn < nj - 1 and PACK_SIDE_ROW % ns == 0
    r_gu = _rows_per_step(d, bsz * ni * nj, 2 * SUBLANES)
    r_dn = _rows_per_step(dff, bsz * ni * nj, 2 * SUBLANES)
    step = lambda b, i, j: (b * ni + i) * nj + j
    gu_blk = lambda b, i, j: (layer, jnp.minimum(step(b, i, j), d // r_gu - 1), 0)
    dn_blk = lambda b, i, j: (layer, jnp.minimum(step(b, i, j), dff // r_dn - 1), 0)
    first = lambda idx: (lambda b, i, j: (0,) + idx(b, i, j)[1:])
    return pl.pallas_call(
        _norm_proj_kernel,
        grid=(bsz, ni, nj),
        in_specs=[
            pl.BlockSpec((None, tm, d), lambda b, i, j: (b, i, 0)),
            pl.BlockSpec((1, d), lambda b, i, j: (0, 0)),
            pl.BlockSpec((None, 1, d), lambda b, i, j: (b, 0, 0)),
            pl.BlockSpec((None, 1, d), lambda b, i, j: (b, 0, 0)),
            pl.BlockSpec((None, tn, d), lambda b, i, j: (layer, j, 0)),
            pl.BlockSpec((None, r_gu, dff), gu_blk),
            pl.BlockSpec((None, r_gu, dff), gu_blk),
            pl.BlockSpec((None, r_dn, d), dn_blk),
        ],
        out_specs=[pl.BlockSpec((None, tm, tn), lambda b, i, j: (b, i, j)),
                   pl.BlockSpec((None, tm, ns), lambda b, i, j: (b, i, 0)),
                   pl.BlockSpec((None, r_gu, 2 * dff), first(gu_blk)),
                   pl.BlockSpec((None, r_dn, d), first(dn_blk))],
        out_shape=[jax.ShapeDtypeStruct((bsz, seq, n), BF16), jax.ShapeDtypeStruct((bsz, seq, ns), F32),
                   jax.ShapeDtypeStruct((1, d, 2 * dff), BF16), jax.ShapeDtypeStruct((1, dff, d), BF16)],
        scratch_shapes=[pltpu.VMEM((tm, d), BF16)],
        compiler_params=_params(("arbitrary", "arbitrary", "arbitrary")),
        name="norm_proj",
    )(x, g.reshape(1, d), sc, sh, w, w_gate, w_up, w_down)


def _split3(c):
    hi = c.astype(BF16).astype(F32)
    mid = (c - hi).astype(BF16).astype(F32)
    lo = (c - hi - mid).astype(BF16).astype(F32)
    return hi, mid, lo


def _fox_gate_kernel(x_ref, fb_ref, qa_ref, ka_ref, cum_ref, carry_scr):
    t = x_ref.shape[0]

    @pl.when(pl.program_id(1) == 0)
    def _():
        carry_scr[...] = jnp.zeros_like(carry_scr)

    lf = jax.nn.log_sigmoid(x_ref[...] + fb_ref[...])
    tri = (lax.broadcasted_iota(jnp.int32, (t, t), 0) >= lax.broadcasted_iota(jnp.int32, (t, t), 1)).astype(F32)
    c = jnp.dot(tri, lf, precision=HIGHEST, preferred_element_type=F32) + carry_scr[...]
    carry_scr[...] = c[t - 1:t, :]
    cum_ref[...] = c * LOG2E
    lane = lax.broadcasted_iota(jnp.int32, (t, LANES), 1)
    sub = lax.broadcasted_iota(jnp.int32, (SUBLANES, t), 0)
    ct = c.T * LOG2E
    for h in range(B_HEADS):
        hi, mid, lo = _split3(jnp.broadcast_to(c[:, h:h + 1], (t, LANES)) * LOG2E)
        ka = jnp.where(lane == 0, -hi, jnp.where(lane == 1, -mid, jnp.where(lane == 2, -lo, jnp.where(lane < 6, 1.0, 0.0))))
        ka_ref[h] = ka.astype(BF16)
        hi, mid, lo = _split3(jnp.broadcast_to(ct[h:h + 1, :], (SUBLANES, t)))
        qa = jnp.where(sub < 3, 1.0, jnp.where(sub == 3, hi, jnp.where(sub == 4, mid, jnp.where(sub == 5, lo, 0.0))))
        qa_ref[h] = jnp.concatenate([qa, jnp.zeros((LANES - SUBLANES, t), F32)], axis=0).astype(BF16)


def _fox_gate(raw, fox_b, *, t):
    bsz, seq, _ = raw.shape
    fb = jnp.zeros((1, LANES), F32).at[0, :B_HEADS].set(fox_b)
    aug = pl.BlockSpec((None, B_HEADS, t, LANES), lambda b, i: (b, 0, i, 0))
    aug_t = pl.BlockSpec((None, B_HEADS, LANES, t), lambda b, i: (b, 0, 0, i))
    return pl.pallas_call(
        _fox_gate_kernel,
        grid=(bsz, seq // t),
        in_specs=[
            pl.BlockSpec((None, t, LANES), lambda b, i: (b, i, 0)),
            pl.BlockSpec((1, LANES), lambda b, i: (0, 0)),
        ],
        out_specs=[aug_t, aug, pl.BlockSpec((None, t, LANES), lambda b, i: (b, i, 0))],
        out_shape=[jax.ShapeDtypeStruct((bsz, B_HEADS, LANES, seq), BF16),
                   jax.ShapeDtypeStruct((bsz, B_HEADS, seq, LANES), BF16),
                   jax.ShapeDtypeStruct((bsz, seq, LANES), F32)],
        scratch_shapes=[pltpu.VMEM((1, LANES), F32)],
        compiler_params=_params(("parallel", "arbitrary")),
        name="fox_gate",
    )(raw, fb)


ONES_ROWS = 16


def _softmax_step(zt, vt, carry):
    m, acc = carry
    m_new = jnp.maximum(m, jnp.max(zt, axis=0, keepdims=True))
    p = jnp.exp2(zt - m_new)
    acc = jnp.exp2(m - m_new) * acc + jnp.dot(vt, p.astype(BF16), preferred_element_type=F32)
    return m_new, acc


def _softmax_init(tq, dh):
    return (jnp.full((1, tq), NEG, F32), jnp.zeros((dh + ONES_ROWS, tq), F32))


def _softmax_finish(carry, dh):
    _, acc = carry
    return (acc[:dh, :] / acc[dh:dh + 1, :]).T


def _fill_vt(vt_scr, v_ref, chunk):
    seq, dh = v_ref.shape
    for cix in range(seq // chunk):
        sl = slice(cix * chunk, (cix + 1) * chunk)
        vt_scr[:dh, sl] = _transpose_bf16(v_ref[sl, :])
    vt_scr[dh:, :] = jnp.ones((ONES_ROWS, seq), BF16)


ATT_LOOKAHEAD = 6


def _run_substeps(steps, carry):
    carry = list(carry)
    pending = {}
    for s in range(min(ATT_LOOKAHEAD, len(steps))):
        pending[s] = steps[s][1]()
    for s, (j, _, fix, vt) in enumerate(steps):
        if s + ATT_LOOKAHEAD < len(steps):
            pending[s + ATT_LOOKAHEAD] = steps[s + ATT_LOOKAHEAD][1]()
        z, cj = pending.pop(s), carry[j]
        if fix is not None:
            z, cj = fix(z, cj)
        carry[j] = _softmax_step(z, vt, cj)
    return carry


def _dense_loop(start, n_blocks, make_steps, carry):
    big = 2 * ATT_U
    carry = lax.fori_loop(0, n_blocks // big,
                          lambda grp, c: tuple(_run_substeps(make_steps(start + grp * big, big), c)), carry)
    first = start + (n_blocks // big) * big
    return lax.fori_loop(0, (start + n_blocks - first) // ATT_U,
                         lambda grp, c: tuple(_run_substeps(make_steps(first + grp * ATT_U, ATT_U), c)), carry)


def _transpose_bf16(x):
    return x.astype(F32).T.astype(BF16)


_NT = (((1,), (1,)), ((), ()))


def _dot_nt(a, b):
    return lax.dot_general(a, b, _NT, preferred_element_type=F32)


def _moba_kernel(cfar_ref, q_ref, k_ref, v_ref, tab_ref, o_ref, kmean_scr, qt_scr, vt_scr):
    g, t = ATT_G, ATT_T
    seq, dh = k_ref.shape
    nblk = seq // t
    rows = g * t
    i = pl.program_id(2)
    base = i * g

    @pl.when(i == 0)
    def _():
        r = lax.broadcasted_iota(jnp.int32, (nblk, seq), 0)
        c = lax.broadcasted_iota(jnp.int32, (nblk, seq), 1)
        avg = jnp.where((c >= r * t) & (c < (r + 1) * t), 1.0 / t, 0.0).astype(BF16)
        kmean_scr[...] = jnp.dot(avg, k_ref[...], preferred_element_type=F32)
        _fill_vt(vt_scr, v_ref, rows)

    q = q_ref[...]
    qt_scr[:dh, :] = _transpose_bf16(q)
    sc = lax.dot_general(kmean_scr[...], q.astype(F32), _NT, precision=HIGHEST, preferred_element_type=F32)
    row = lax.broadcasted_iota(jnp.int32, (nblk, rows), 0)
    own = base + lax.broadcasted_iota(jnp.int32, (nblk, rows), 1) // t
    past = row < own
    sc = jnp.where(past, sc, -jnp.inf)
    rowf = row.astype(F32)
    picked = jnp.zeros((nblk, rows), F32)
    for _ in range(MOBA_TOPK):
        best = jnp.max(sc, axis=0, keepdims=True)
        first = jnp.min(jnp.where(sc == best, rowf, float(nblk)), axis=0, keepdims=True)
        pick = rowf == first
        picked = jnp.where(pick, 1.0, picked)
        sc = jnp.where(pick, -jnp.inf, sc)
    pen = jnp.where(past, jnp.where(picked > 0.0, 0.0, NEG), jnp.where(row == own, 0.0, NEG))
    void = jnp.where(lax.broadcasted_iota(jnp.int32, (SUBLANES, rows), 0) == SUBLANES - 1, NEG, 0.0)
    pen = jnp.concatenate([pen, jnp.zeros((LANES - SUBLANES - nblk, rows), F32), void], axis=0)
    qt_scr[dh:, :] = pen.astype(BF16)

    lane = lax.broadcasted_iota(jnp.int32, (t, LANES), 1)

    def kv(n, col):
        start = pl.multiple_of(n * t, t)
        onehot = jnp.where(lane == col, 1.0, 0.0).astype(BF16)
        return jnp.concatenate([k_ref[pl.ds(start, t), :], onehot], axis=1), vt_scr[:, pl.ds(start, t)]

    def logits(kb, j):
        return jnp.dot(kb, qt_scr[:, j * t:(j + 1) * t], preferred_element_type=F32)

    def far_steps(first, count):
        steps = []
        for u in range(count):
            kb, vb = kv(first + u, first + u)
            steps += [(j, functools.partial(logits, kb, j), None, vb) for j in range(g)]
        return steps

    carry = _dense_loop(0, jnp.maximum(base - (MOBA_FAR - 1), 0), far_steps,
                        tuple(_softmax_init(t, dh) for _ in range(g)))
    cfar = cfar_ref[pl.program_id(1)]

    def near_fix(delta, z, cj):
        if delta == MOBA_FAR - 1:
            m, acc = cj
            cj = (m + cfar, acc)
        return tab_ref[delta] + z, cj

    steps = []
    for k in range(-(MOBA_FAR - 1), g):
        n = base + k
        if k < 0:
            kb, vb = kv(jnp.maximum(n, 0), jnp.where(n >= 0, n, VOID_COL))
        else:
            kb, vb = kv(n, n)
        for j in range(max(k, 0), g):
            fix = functools.partial(near_fix, j - k) if j - k < MOBA_FAR else None
            steps.append((j, functools.partial(logits, kb, j), fix, vb))
    carry = _run_substeps(steps, carry)
    for j in range(g):
        o_ref[j * t:(j + 1) * t, :] = _softmax_finish(carry[j], dh).astype(o_ref.dtype)


def _moba(proj, tab):
    bsz, seq, _ = proj.shape
    g, t = ATT_G, ATT_T
    nblk = seq // t
    assert seq % (g * t) == 0 and nblk <= LANES - SUBLANES and nblk % SUBLANES == 0
    qc, kc, vc = COL_AQ // HEAD_DIM, COL_AK // HEAD_DIM, COL_AV // HEAD_DIM
    cfar = tab[:, MOBA_FAR, 0, 0]
    return pl.pallas_call(
        _moba_kernel,
        grid=(bsz, A_HEADS, nblk // g),
        in_specs=[
            pl.BlockSpec(memory_space=pltpu.SMEM),
            pl.BlockSpec((None, g * t, HEAD_DIM), lambda b, h, i: (b, i, qc + h)),
            pl.BlockSpec((None, seq, HEAD_DIM), lambda b, h, i: (b, 0, kc + h)),
            pl.BlockSpec((None, seq, HEAD_DIM), lambda b, h, i: (b, 0, vc + h)),
            pl.BlockSpec((None, MOBA_FAR, t, t), lambda b, h, i: (h, 0, 0, 0)),
        ],
        out_specs=pl.BlockSpec((None, g * t, HEAD_DIM), lambda b, h, i: (b, i, h)),
        out_shape=jax.ShapeDtypeStruct((bsz, seq, A_W), BF16),
        scratch_shapes=[pltpu.VMEM((nblk, HEAD_DIM), F32),
                        pltpu.VMEM((HEAD_DIM + LANES, g * t), BF16), pltpu.VMEM((HEAD_DIM + ONES_ROWS, seq), BF16)],
        compiler_params=_params(("parallel", "parallel", "arbitrary")),
        name="moba",
    )(cfar, proj, proj, proj, tab)


def _fox_kernel(start_ref, q_ref, qa_ref, k_ref, ka_ref, v_ref, o_ref, qt_scr, vt_scr):
    g, t = ATT_G, ATT_T
    rows = g * t
    seq, dh = k_ref.shape
    base = pl.program_id(2) * g
    first = start_ref[(pl.program_id(0) * pl.num_programs(1) + pl.program_id(1)) * pl.num_programs(2)
                      + pl.program_id(2)]

    @pl.when(base == 0)
    def _():
        _fill_vt(vt_scr, v_ref, rows)

    qt_scr[:dh, :] = _transpose_bf16(q_ref[...])
    qt_scr[dh:, :] = qa_ref[...]

    def kv(n):
        start = pl.multiple_of(n * t, t)
        return (jnp.concatenate([k_ref[pl.ds(start, t), :], ka_ref[pl.ds(start, t), :]], axis=1),
                vt_scr[:, pl.ds(start, t)])

    def logits(kb, j):
        return jnp.dot(kb, qt_scr[:, j * t:(j + 1) * t], preferred_element_type=F32)

    def dense_steps(first, count):
        steps = []
        for u in range(count):
            kb, vb = kv(first + u)
            steps += [(j, functools.partial(logits, kb, j), None, vb) for j in range(g)]
        return steps

    carry = _dense_loop(first, base - first, dense_steps, tuple(_softmax_init(t, dh) for _ in range(g)))
    causal = lax.broadcasted_iota(jnp.int32, (t, t), 0) <= lax.broadcasted_iota(jnp.int32, (t, t), 1)

    def diag_fix(z, cj):
        return jnp.where(causal, z, NEG), cj

    steps = []
    for k in range(g):
        kb, vb = kv(base + k)
        steps += [(j, functools.partial(logits, kb, j), diag_fix if j == k else None, vb) for j in range(k, g)]
    carry = _run_substeps(steps, carry)
    for j in range(g):
        o_ref[j * t:(j + 1) * t, :] = _softmax_finish(carry[j], dh).astype(o_ref.dtype)


FOX_SKIP_BITS = 200.0
FOX_NORM_SLACK = 1.0 + 2.0 ** -6


def _fox_bounds_kernel(q_ref, k_ref, cum_ref, o_ref, krun_scr, kpref_scr, clast_scr):
    i = pl.program_id(1)
    rows, width = q_ref.shape
    g = rows // ATT_T

    @pl.when(i == 0)
    def _():
        krun_scr[...] = jnp.zeros_like(krun_scr)
        kpref_scr[...] = jnp.zeros_like(kpref_scr)
        clast_scr[...] = jnp.zeros_like(clast_scr)

    head_of = lax.broadcasted_iota(jnp.int32, (width, LANES), 0) // HEAD_DIM
    head_sum = jnp.where(head_of == lax.broadcasted_iota(jnp.int32, (width, LANES), 1), 1.0, 0.0).astype(BF16)

    def max_norm(ref):
        x = ref[...].astype(F32)
        n2 = jnp.dot((x * x).astype(BF16), head_sum, preferred_element_type=F32)
        return jnp.sqrt(jnp.max(n2, axis=0, keepdims=True)) * FOX_NORM_SLACK

    qn, kn = max_norm(q_ref), max_norm(k_ref)
    c_first, c_last = cum_ref[0:1, :], cum_ref[rows - 1:rows, :]
    tile = lax.broadcasted_iota(jnp.int32, kpref_scr.shape, 0)
    ub = qn * kpref_scr[...] + (c_first - clast_scr[...]) + qn * kn
    ok = (ub < -FOX_SKIP_BITS) & (tile < i)
    o_ref[...] = jnp.max(jnp.where(ok, (tile + 1) * g, 0), axis=0, keepdims=True)
    krun = jnp.maximum(krun_scr[...], kn)
    krun_scr[...] = krun
    kpref_scr[pl.ds(i, 1), :] = krun
    clast_scr[pl.ds(i, 1), :] = c_last


def _fox_first_block(proj, cum2):
    bsz, seq, _ = proj.shape
    rows = ATT_G * ATT_T
    ntile = seq // rows
    hist = pltpu.VMEM((-(-ntile // SUBLANES) * SUBLANES, LANES), F32)
    first = pl.pallas_call(
        _fox_bounds_kernel,
        grid=(bsz, ntile),
        in_specs=[
            pl.BlockSpec((None, rows, B_W), lambda b, i: (b, i, COL_BQ // B_W)),
            pl.BlockSpec((None, rows, B_W), lambda b, i: (b, i, COL_BK // B_W)),
            pl.BlockSpec((None, rows, LANES), lambda b, i: (b, i, 0)),
        ],
        out_specs=pl.BlockSpec((None, None, 1, LANES), lambda b, i: (b, i, 0, 0)),
        out_shape=jax.ShapeDtypeStruct((bsz, ntile, 1, LANES), jnp.int32),
        scratch_shapes=[pltpu.VMEM((1, LANES), F32), hist, hist],
        compiler_params=_params(("parallel", "arbitrary")),
        name="fox_bounds",
    )(proj, proj, cum2)
    return first[:, :, 0, :B_HEADS].transpose(0, 2, 1).reshape(-1)


def _fox(proj, qa, ka, first):
    bsz, seq, _ = proj.shape
    g, t = ATT_G, ATT_T
    assert seq % (g * t) == 0
    qc, kc, vc = COL_BQ // HEAD_DIM, COL_BK // HEAD_DIM, COL_BV // HEAD_DIM
    return pl.pallas_call(
        _fox_kernel,
        grid_spec=pltpu.PrefetchScalarGridSpec(
            num_scalar_prefetch=1,
            grid=(bsz, B_HEADS, seq // (g * t)),
            in_specs=[
                pl.BlockSpec((None, g * t, HEAD_DIM), lambda b, h, i, s: (b, i, qc + h)),
                pl.BlockSpec((None, None, LANES, g * t), lambda b, h, i, s: (b, h, 0, i)),
                pl.BlockSpec((None, seq, HEAD_DIM), lambda b, h, i, s: (b, 0, kc + h)),
                pl.BlockSpec((None, None, seq, LANES), lambda b, h, i, s: (b, h, 0, 0)),
                pl.BlockSpec((None, seq, HEAD_DIM), lambda b, h, i, s: (b, 0, vc + h)),
            ],
            out_specs=pl.BlockSpec((None, g * t, HEAD_DIM), lambda b, h, i, s: (b, i, h)),
            scratch_shapes=[pltpu.VMEM((HEAD_DIM + LANES, g * t), BF16),
                            pltpu.VMEM((HEAD_DIM + ONES_ROWS, seq), BF16)],
        ),
        out_shape=jax.ShapeDtypeStruct((bsz, seq, B_W), BF16),
        compiler_params=_params(("parallel", "parallel", "arbitrary")),
        name="fox",
    )(first, proj, qa, proj, ka, proj)


SWA_BLOCKS = 4


def _swa_kernel(sink_ref, q_ref, kp_ref, kc_ref, vp_ref, vc_ref, tab_ref, o_ref):
    w = WINDOW
    dh = D_HEAD_DIM
    grp = D_Q_HEADS // D_KV_HEADS
    i = pl.program_id(1)
    nblk = q_ref.shape[0] // w
    span = (nblk + 1) * w
    kall = jnp.concatenate([kp_ref[...], kc_ref[...]], axis=0)
    vt = jnp.concatenate([_transpose_bf16(vp_ref[...]), _transpose_bf16(vc_ref[...])], axis=1)
    ones = jnp.ones((ONES_ROWS, span), BF16)
    kg = [kall[:, g * dh:(g + 1) * dh] for g in range(D_KV_HEADS)]
    vg = [jnp.concatenate([vt[g * dh:(g + 1) * dh, :], ones], axis=0) for g in range(D_KV_HEADS)]
    qt = _transpose_bf16(q_ref[...])
    key = lax.broadcasted_iota(jnp.int32, (2 * w, w), 0)
    first = key >= jnp.where(i > 0, 0, w)

    def block_logits(blk):
        return [jnp.dot(kg[h // grp][blk * w:(blk + 2) * w, :], qt[h * dh:(h + 1) * dh, blk * w:(blk + 1) * w],
                        preferred_element_type=F32) for h in range(D_Q_HEADS)]

    logits = block_logits(0)
    for blk in range(nblk):
        nxt = block_logits(blk + 1) if blk + 1 < nblk else None
        outs = []
        for h in range(D_Q_HEADS):
            s = tab_ref[h] + logits[h]
            if blk == 0:
                s = jnp.where(first, s, NEG)
            sink = sink_ref[h]
            m = jnp.maximum(jnp.max(s, axis=0, keepdims=True), sink)
            p = jnp.exp(s - m)
            acc = jnp.dot(vg[h // grp][:, blk * w:(blk + 2) * w], p.astype(BF16), preferred_element_type=F32)
            den = acc[dh:dh + 1, :] + jnp.exp(sink - m)
            outs.append(acc[:dh, :] / den)
        o_ref[blk * w:(blk + 1) * w, :] = jnp.concatenate(outs, axis=0).T.astype(o_ref.dtype)
        logits = nxt


def _swa(proj, sinks, tab):
    bsz, seq, _ = proj.shape
    w = WINDOW
    nb = SWA_BLOCKS
    assert seq % (nb * w) == 0
    qc, kc, vc = COL_DQ // D_QW, COL_DK // D_KVW, COL_DV // D_KVW
    prev = lambda i: jnp.maximum(i * nb - 1, 0)
    return pl.pallas_call(
        _swa_kernel,
        grid=(bsz, seq // (nb * w)),
        in_specs=[
            pl.BlockSpec(memory_space=pltpu.SMEM),
            pl.BlockSpec((None, nb * w, D_QW), lambda b, i: (b, i, qc)),
            pl.BlockSpec((None, w, D_KVW), lambda b, i: (b, prev(i), kc)),
            pl.BlockSpec((None, nb * w, D_KVW), lambda b, i: (b, i, kc)),
            pl.BlockSpec((None, w, D_KVW), lambda b, i: (b, prev(i), vc)),
            pl.BlockSpec((None, nb * w, D_KVW), lambda b, i: (b, i, vc)),
            pl.BlockSpec((D_Q_HEADS, 2 * w, w), lambda b, i: (0, 0, 0)),
        ],
        out_specs=pl.BlockSpec((None, nb * w, D_QW), lambda b, i: (b, i, 0)),
        out_shape=jax.ShapeDtypeStruct((bsz, seq, D_QW), BF16),
        compiler_params=_params(("parallel", "parallel")),
        name="swa",
    )(sinks, proj, proj, proj, proj, proj, tab)


CONV_T = 512
CONV_HALO = 32
CONV_ROWS = 64


def _conv_kernel(a_ref, g_ref, ha_ref, hg_ref, w_ref, b_ref, lg_ref, lb_ref, o_ref, buf):
    t = a_ref.shape[0]
    halo = CONV_HALO
    hp = ha_ref[...].astype(F32) * jax.nn.sigmoid(hg_ref[...].astype(F32))
    buf[0:halo, :] = jnp.where(pl.program_id(1) > 0, hp, 0.0)
    buf[halo:halo + t, :] = a_ref[...].astype(F32) * jax.nn.sigmoid(g_ref[...].astype(F32))
    off = halo - (CONV_WIDTH - 1)
    for r in range(t // CONV_ROWS):
        acc = jnp.broadcast_to(b_ref[...], (CONV_ROWS, CONV_CH))
        for rho in range(SUBLANES):
            rows = CONV_ROWS if rho == 0 else CONV_ROWS + SUBLANES
            part = None
            for s in range(off, off + CONV_WIDTH):
                if s % SUBLANES != rho:
                    continue
                start = r * CONV_ROWS + s - rho
                term = w_ref[s - off:s - off + 1, :] * buf[start:start + rows, :]
                part = term if part is None else part + term
            acc = acc + part[rho:rho + CONV_ROWS, :]
        mu = jnp.mean(acc, axis=-1, keepdims=True)
        d = acc - mu
        var = jnp.mean(d * d, axis=-1, keepdims=True)
        y = d * lax.rsqrt(var + LN_EPS) * lg_ref[...] + lb_ref[...]
        o_ref[r * CONV_ROWS:(r + 1) * CONV_ROWS, :] = (y * jax.nn.sigmoid(y)).astype(o_ref.dtype)


def _conv(proj, conv_w, conv_b, ln_g, ln_b):
    bsz, seq, _ = proj.shape
    t = min(CONV_T, seq)
    ca, cg = COL_CU // CONV_CH, COL_CU // CONV_CH + 1
    hb = t // CONV_HALO
    prev = lambda i: jnp.maximum(i * hb - 1, 0)
    row = lambda v: v.reshape(1, CONV_CH)
    return pl.pallas_call(
        _conv_kernel,
        grid=(bsz, seq // t),
        in_specs=[
            pl.BlockSpec((None, t, CONV_CH), lambda b, i: (b, i, ca)),
            pl.BlockSpec((None, t, CONV_CH), lambda b, i: (b, i, cg)),
            pl.BlockSpec((None, CONV_HALO, CONV_CH), lambda b, i: (b, prev(i), ca)),
            pl.BlockSpec((None, CONV_HALO, CONV_CH), lambda b, i: (b, prev(i), cg)),
            pl.BlockSpec((CONV_WIDTH, CONV_CH), lambda b, i: (0, 0)),
            pl.BlockSpec((1, CONV_CH), lambda b, i: (0, 0)),
            pl.BlockSpec((1, CONV_CH), lambda b, i: (0, 0)),
            pl.BlockSpec((1, CONV_CH), lambda b, i: (0, 0)),
        ],
        out_specs=pl.BlockSpec((None, t, CONV_CH), lambda b, i: (b, i, 0)),
        out_shape=jax.ShapeDtypeStruct((bsz, seq, CONV_CH), BF16),
        scratch_shapes=[pltpu.VMEM((CONV_HALO + t, CONV_CH), F32)],
        compiler_params=_params(("parallel", "parallel")),
        name="conv",
    )(proj, proj, proj, proj, conv_w, row(conv_b), row(ln_g), row(ln_b))


def _merge_kernel(ya_ref, yb_ref, yc_ref, yd_ref, gates_ref, x_ref, gt_ref, gp_ref, gf_ref, scf_ref, shf_ref,
                  wa_ref, wb_ref, wc_ref, wd_ref, wo_ref, o_ref, h_ref):
    d = x_ref.shape[1]
    y = None
    for i, (br, w) in enumerate(((ya_ref, wa_ref), (yb_ref, wb_ref), (yc_ref, wc_ref), (yd_ref, wd_ref))):
        t = jnp.dot(br[...], w[...], preferred_element_type=F32) * gates_ref[:, i * d:(i + 1) * d].astype(F32)
        y = t if y is None else y + t
    z = jnp.dot(y.astype(BF16), wo_ref[...], preferred_element_type=F32)
    x_new = x_ref[...] + gt_ref[...] * (_rms(z) * gp_ref[...])
    o_ref[...] = x_new
    h_ref[...] = _modulated_norm(x_new, gf_ref[...], scf_ref[...], shf_ref[...]).astype(h_ref.dtype)


def _merge(ya, yb, yc, yd, gates, x, gt, g_post, g_ffn, sc_f, sh_f, wa, wb, wc, wd, wo, layer, *, tm):
    bsz, seq, d = x.shape
    tok = lambda width: pl.BlockSpec((None, tm, width), lambda b, i: (b, i, 0))
    vec = lambda: pl.BlockSpec((None, 1, d), lambda b, i: (b, 0, 0))
    par = lambda: pl.BlockSpec((1, d), lambda b, i: (0, 0))
    full = lambda a: pl.BlockSpec((None,) + a.shape[1:], lambda b, i: (layer, 0, 0),
                                  pipeline_mode=pl.Buffered(1))
    return pl.pallas_call(
        _merge_kernel,
        grid=(bsz, seq // tm),
        in_specs=[tok(A_W), tok(B_W), tok(CONV_CH), tok(D_QW), tok(N_BRANCHES * d), tok(d),
                  vec(), par(), par(), vec(), vec(),
                  full(wa), full(wb), full(wc), full(wd), full(wo)],
        out_specs=[tok(d), tok(d)],
        out_shape=[jax.ShapeDtypeStruct((bsz, seq, d), F32), jax.ShapeDtypeStruct((bsz, seq, d), BF16)],
        compiler_params=_params(("parallel", "parallel")),
        name="merge",
    )(ya, yb, yc, yd, gates, x, gt, g_post.reshape(1, d), g_ffn.reshape(1, d), sc_f, sh_f, wa, wb, wc, wd, wo)


def _ffn_kernel(x_ref, h_ref, gt_ref, gp_ref, wgu_ref, wd_ref, o_ref, acc_scr):
    f = pl.program_id(2)
    tf = wd_ref.shape[0]

    def partial_down():
        gu = jnp.dot(h_ref[...], wgu_ref[...], preferred_element_type=F32)
        g, u = gu[:, :tf], gu[:, tf:]
        a = ((g * jax.nn.sigmoid(g)) * u).astype(BF16)
        return jnp.dot(a, wd_ref[...], preferred_element_type=F32)

    @pl.when(f == 0)
    def _():
        acc_scr[...] = partial_down()

    @pl.when(f > 0)
    def _():
        acc_scr[...] += partial_down()

    @pl.when(f == pl.num_programs(2) - 1)
    def _():
        o_ref[...] = x_ref[...] + gt_ref[...] * (_rms(acc_scr[...]) * gp_ref[...])


def _ffn(x, h, gt, g_post, wgu, wd, layer, *, tm, tf):
    bsz, seq, d = x.shape
    dff = wd.shape[1]
    vec = lambda: pl.BlockSpec((None, 1, d), lambda b, i, f: (b, 0, 0))
    par = lambda: pl.BlockSpec((1, d), lambda b, i, f: (0, 0))
    return pl.pallas_call(
        _ffn_kernel,
        grid=(bsz, seq // tm, dff // tf),
        in_specs=[
            pl.BlockSpec((None, tm, d), lambda b, i, f: (b, i, 0)),
            pl.BlockSpec((None, tm, d), lambda b, i, f: (b, i, 0)),
            vec(), par(),
            pl.BlockSpec((None, d, 2 * tf), lambda b, i, f: (layer, 0, f)),
            pl.BlockSpec((None, tf, d), lambda b, i, f: (layer, f, 0)),
        ],
        out_specs=pl.BlockSpec((None, tm, d), lambda b, i, f: (b, i, 0)),
        out_shape=jax.ShapeDtypeStruct((bsz, seq, d), F32),
        scratch_shapes=[pltpu.VMEM((tm, d), F32)],
        compiler_params=_params(("parallel", "parallel", "arbitrary")),
        name="ffn",
    )(x, h, gt, g_post.reshape(1, d), wgu, wd)


SRC_QKV = 3 * A_W + 3 * B_W
SRC_REST = SRC_QKV + B_HEADS
SRC_GATE = SRC_REST + (PROJ_W - GATE_W - SRC_QKV)
IN_W = SRC_GATE + GATE_W
PACK_R = 256
PACK_SIDE_ROW = PROJ_W
PACK_ROWS = PROJ_W_PAD
assert GATE_W % PACK_R == 0 and SRC_QKV % PACK_R == 0 and PROJ_W % PACK_R == 0 and PROJ_W_PAD % PACK_R == 0


def _pack_tables():
    s2 = HEAD_DIM ** -0.5 * LOG2E
    src, scale, rows = [], [], []
    for b in range(PACK_ROWS // PACK_R):
        row = b * PACK_R
        if row < GATE_W:
            src.append(SRC_GATE + row); scale.append(1.0); rows.append(PACK_R)
        elif row < GATE_W + SRC_QKV:
            s = row - GATE_W
            is_q = s < A_W or 3 * A_W <= s < 3 * A_W + B_W
            src.append(s); scale.append(s2 if is_q else 1.0); rows.append(PACK_R)
        elif row < PROJ_W:
            s = row - GATE_W - SRC_QKV
            is_q = 2 * CONV_CH <= s < 2 * CONV_CH + D_QW
            src.append(SRC_REST + s); scale.append(D_HEAD_DIM ** -0.5 if is_q else 1.0); rows.append(PACK_R)
        elif row == PACK_SIDE_ROW:
            src.append(SRC_QKV); scale.append(1.0); rows.append(B_HEADS)
        else:
            src.append(0); scale.append(0.0); rows.append(0)
    return np.asarray(src, np.int32), np.asarray(scale, np.float32), np.asarray(rows, np.int32)


def _pack_kernel(src_ref, scale_ref, rows_ref, a_ref, o_ref):
    blk = pl.program_id(0)
    depth, _, d = o_ref.shape
    kt = d // LANES
    keep = lax.broadcasted_iota(jnp.int32, (PACK_R, d), 0) < rows_ref[blk]
    for layer in range(depth):
        cols = [a_ref[pl.ds(t * depth + layer, PACK_R, stride=kt * depth), :] for t in range(kt)]
        o_ref[layer] = jnp.where(keep, jnp.concatenate(cols, axis=1) * scale_ref[blk], 0.0).astype(o_ref.dtype)


def _pack_in_proj(w):
    depth, d, in_w = w.shape
    assert in_w == IN_W and d % LANES == 0
    kt = d // LANES
    view = jnp.transpose(w.reshape(depth, kt, LANES, in_w), (3, 1, 0, 2)).reshape(in_w * kt * depth, LANES)
    src, scale, rows = _pack_tables()
    per_feature = kt * depth
    return pl.pallas_call(
        _pack_kernel,
        grid_spec=pltpu.PrefetchScalarGridSpec(
            num_scalar_prefetch=3,
            grid=(PACK_ROWS // PACK_R,),
            in_specs=[pl.BlockSpec((pl.Element(PACK_R * per_feature), pl.Element(LANES)),
                                   lambda b, src, scale, rows: (src[b] * per_feature, 0))],
            out_specs=pl.BlockSpec((depth, PACK_R, d), lambda b, src, scale, rows: (0, b, 0)),
        ),
        out_shape=jax.ShapeDtypeStruct((depth, PACK_ROWS, d), BF16),
        compiler_params=_params(("parallel",)),
        name="pack_in_proj",
    )(jnp.asarray(src), jnp.asarray(scale), jnp.asarray(rows), view)


CAST_BLOCK_ELEMS = 1 << 20


def _cast_kernel(w_ref, o_ref):
    o_ref[...] = w_ref[...].astype(o_ref.dtype)


def _cast_bf16(w):
    depth, r, c = w.shape
    tr = 16
    while tr * 2 * c <= CAST_BLOCK_ELEMS and r % (tr * 2) == 0:
        tr *= 2
    assert r % tr == 0
    return pl.pallas_call(
        _cast_kernel,
        grid=(depth, r // tr),
        in_specs=[pl.BlockSpec((None, tr, c), lambda l, i: (l, i, 0))],
        out_specs=pl.BlockSpec((None, tr, c), lambda l, i: (l, i, 0)),
        out_shape=jax.ShapeDtypeStruct(w.shape, BF16),
        compiler_params=_params(("parallel", "parallel")),
        name="cast_bf16",
    )(w)


def kernel(x, c, rel_bias, w_mod, b_mod, mix_norm_pre, mix_norm_post, w_in, fox_bias, conv_w, conv_b, conv_ln_g, conv_ln_b, sinks, w_branch_a, w_branch_b, w_branch_c, w_branch_d, w_out, ffn_norm_pre, ffn_norm_post, w_ffn_gate, w_ffn_up, w_ffn_down):
    depth = w_mod.shape[0]
    bsz, seq, d = x.shape
    tm = min(1024, seq)
    tm_small = min(256, seq)
    tm_ffn = min(512, seq)

    mod = _modulation(c, w_mod, b_mod)
    tab_a = _bias_tables(rel_bias[:, :A_HEADS], _moba_buckets(), scale=LOG2E)
    tab_d = _bias_tables(rel_bias[:, A_HEADS:], _swa_buckets(), scale=1.0)[:, 0]

    wa, wb, wc, wd, wo = [_cast_bf16(w) for w in (w_branch_a, w_branch_b, w_branch_c, w_branch_d, w_out)]

    w_proj = _pack_in_proj(w_in)

    for l in range(depth):
        sh_m, sc_m, gt_m, sh_f, sc_f, gt_f = [mod[l, :, None, i * d:(i + 1) * d] for i in range(6)]

        proj, fox_raw, wgu, wdn = _norm_proj(x, mix_norm_pre[l], sc_m, sh_m, w_proj, w_ffn_gate, w_ffn_up, w_ffn_down, l,
                                             tm=tm, tn=PROJ_TN)
        fox_qa, fox_ka, fox_cum = _fox_gate(fox_raw, fox_bias[l], t=min(512, seq))

        ya = _moba(proj, tab_a)
        yb = _fox(proj, fox_qa, fox_ka, _fox_first_block(proj, fox_cum))
        yc = _conv(proj, conv_w[l], conv_b[l], conv_ln_g[l], conv_ln_b[l])
        yd = _swa(proj, sinks[l], tab_d)

        x, h_ffn = _merge(ya, yb, yc, yd, proj, x, gt_m, mix_norm_post[l], ffn_norm_pre[l], sc_f, sh_f,
                          wa, wb, wc, wd, wo, l, tm=tm_small)
        x = _ffn(x, h_ffn, gt_f, ffn_norm_post[l], wgu, wdn, 0, tm=tm_ffn, tf=FFN_TF)
    return x
```

```python
import functools
import math

import jax
import jax.numpy as jnp
import numpy as np
from jax import lax
from jax.experimental import pallas as pl
from jax.experimental.pallas import tpu as pltpu

F32 = jnp.float32
BF16 = jnp.bfloat16
HIGHEST = lax.Precision.HIGHEST

HEAD_DIM = 128
A_HEADS = 4
MOBA_BLOCK = 256
MOBA_TOPK = 3
B_HEADS = 4
CONV_CH = 512
CONV_WIDTH = 31
D_Q_HEADS = 8
D_KV_HEADS = 2
D_HEAD_DIM = 64
WINDOW = 128
N_BUCKETS = 32
MAX_DISTANCE = 1024
N_BRANCHES = 4
RMS_EPS = 1e-6
LN_EPS = 1e-5

A_W = A_HEADS * HEAD_DIM
B_W = B_HEADS * HEAD_DIM
D_QW = D_Q_HEADS * D_HEAD_DIM
D_KVW = D_KV_HEADS * D_HEAD_DIM

LANES = 128
SUBLANES = 8
VMEM_LIMIT = 56 * 1024 * 1024

NEG = -1e30
LOG2E = math.log2(math.e)
ATT_G = 4
ATT_T = MOBA_BLOCK
ATT_U = 4
FFN_TF = 512
NORM_ROWS = 32

D_MODEL = 2048
GATE_W = N_BRANCHES * D_MODEL
COL_AQ, COL_AK, COL_AV = GATE_W, GATE_W + A_W, GATE_W + 2 * A_W
COL_BQ, COL_BK, COL_BV = COL_AV + A_W, COL_AV + A_W + B_W, COL_AV + A_W + 2 * B_W
COL_CU = COL_BV + B_W
COL_DQ = COL_CU + 2 * CONV_CH
COL_DK = COL_DQ + D_QW
COL_DV = COL_DK + D_KVW
PROJ_W = COL_DV + D_KVW
PROJ_TN = 1024
PROJ_W_PAD = -(-PROJ_W // PROJ_TN) * PROJ_TN
MOBA_FAR = -(-(MAX_DISTANCE + MOBA_BLOCK - 1) // MOBA_BLOCK)
VOID_COL = LANES - 1
assert ATT_G % ATT_U == 0 and (MOBA_FAR - 1) % ATT_U == 0


def _params(sem, vmem=VMEM_LIMIT):
    return pltpu.CompilerParams(dimension_semantics=sem, vmem_limit_bytes=vmem)


def _t5_bucket(dist):
    max_exact = N_BUCKETS // 2
    d = jnp.maximum(dist, 0)
    log_ratio = jnp.log(jnp.maximum(d, 1).astype(jnp.float32) / max_exact) / math.log(MAX_DISTANCE / max_exact)
    large = max_exact + (log_ratio * (N_BUCKETS - max_exact)).astype(jnp.int32)
    large = jnp.minimum(large, N_BUCKETS - 1)
    return jnp.where(d < max_exact, d, large)


def _sigmoid(x):
    return 0.5 * jnp.tanh(0.5 * x) + 0.5


def _rms(y):
    return y * lax.rsqrt(jnp.mean(y * y, axis=-1, keepdims=True) + RMS_EPS)


def _modulated_norm(x, g, sc, sh):
    return (_rms(x) * g) * (1.0 + sc) + sh


MOD_TN = 512
MOD_KC = 256


def _mod_kernel(ct_ref, w_ref, b_ref, o_ref):
    d, nb = ct_ref.shape
    tn = w_ref.shape[1]
    ct = ct_ref[...]
    ca = ct * jax.nn.sigmoid(ct)
    rows = []
    for r in range(nb):
        acc = jnp.zeros((1, tn), F32)
        for kc in range(d // MOD_KC):
            sl = slice(kc * MOD_KC, (kc + 1) * MOD_KC)
            acc = acc + jnp.sum(w_ref[sl, :] * ca[sl, r:r + 1], axis=0, keepdims=True)
        rows.append(acc)
    o_ref[...] = jnp.concatenate(rows, axis=0) + b_ref[...]


def _modulation(c, w_mod, b_mod):
    depth, d, n = w_mod.shape
    bsz = c.shape[0]
    return pl.pallas_call(
        _mod_kernel,
        grid=(depth, n // MOD_TN),
        in_specs=[
            pl.BlockSpec((d, bsz), lambda l, j: (0, 0)),
            pl.BlockSpec((None, d, MOD_TN), lambda l, j: (l, 0, j)),
            pl.BlockSpec((None, 1, MOD_TN), lambda l, j: (l, 0, j)),
        ],
        out_specs=pl.BlockSpec((None, bsz, MOD_TN), lambda l, j: (l, 0, j)),
        out_shape=jax.ShapeDtypeStruct((depth, bsz, n), F32),
        compiler_params=_params(("parallel", "parallel")),
        name="mod",
    )(c.T, w_mod, b_mod.reshape(depth, 1, n))


def _bias_table_kernel(rb_ref, bucket_ref, o_ref, *, scale):
    h = pl.program_id(0)
    b = bucket_ref[...]
    acc = jnp.where(b < 0, NEG, 0.0).astype(F32)
    for u in range(N_BUCKETS):
        acc = jnp.where(b == u, rb_ref[h, u] * scale, acc)
    o_ref[...] = acc


def _bias_tables(rel_bias_heads, bucket, *, scale):
    nh = rel_bias_heads.shape[1]
    nt, r, c = bucket.shape
    return pl.pallas_call(
        functools.partial(_bias_table_kernel, scale=scale),
        grid=(nh, nt),
        in_specs=[
            pl.BlockSpec(memory_space=pltpu.SMEM),
            pl.BlockSpec((None, r, c), lambda h, t: (t, 0, 0)),
        ],
        out_specs=pl.BlockSpec((None, None, r, c), lambda h, t: (h, t, 0, 0)),
        out_shape=jax.ShapeDtypeStruct((nh, nt, r, c), F32),
        compiler_params=_params(("parallel", "parallel")),
        name="bias_table",
    )(rel_bias_heads.T, bucket)


def _moba_buckets():
    blk = MOBA_BLOCK
    i = jnp.arange(blk)[None, :]
    j = jnp.arange(blk)[:, None]
    tabs = []
    for delta in range(MOBA_FAR):
        dist = delta * blk + i - j
        tabs.append(jnp.where(dist >= 0, _t5_bucket(dist), -1))
    tabs.append(_t5_bucket(jnp.full((blk, blk), MAX_DISTANCE, jnp.int32)))
    return jnp.stack(tabs).astype(jnp.int32)


def _swa_buckets():
    qi = jnp.arange(WINDOW)[None, :]
    kj = jnp.arange(2 * WINDOW)[:, None]
    dist = qi + WINDOW - kj
    in_win = (dist >= 0) & (dist < WINDOW)
    return jnp.where(in_win, _t5_bucket(dist), -1).astype(jnp.int32)[None]


def _norm_proj_kernel(x_ref, g_ref, sc_ref, sh_ref, w_ref, wg_ref, wu_ref, wdn_ref,
                      o_ref, os_ref, wgu_out, wdn_out, h_scr):
    j = pl.program_id(2)
    tn = w_ref.shape[0]
    last = pl.num_programs(2) - 1

    def cast_ffn_weights():
        tf = FFN_TF
        for f in range(wg_ref.shape[1] // tf):
            wgu_out[:, 2 * f * tf:(2 * f + 1) * tf] = wg_ref[:, f * tf:(f + 1) * tf].astype(BF16)
            wgu_out[:, (2 * f + 1) * tf:(2 * f + 2) * tf] = wu_ref[:, f * tf:(f + 1) * tf].astype(BF16)
        wdn_out[...] = wdn_ref[...].astype(BF16)

    @pl.when(j == 0)
    def _():
        gain = g_ref[...] * (1.0 + sc_ref[...])

        def norm_rows(r, carry):
            rows = pl.ds(pl.multiple_of(r * NORM_ROWS, NORM_ROWS), NORM_ROWS)
            h_scr[rows, :] = (_rms(x_ref[rows, :]) * gain + sh_ref[...]).astype(BF16)
            return carry

        lax.fori_loop(0, x_ref.shape[0] // NORM_ROWS, norm_rows, 0, unroll=8)

    @pl.when(j < GATE_W // tn)
    def _():
        o_ref[...] = _sigmoid(_dot_nt(h_scr[...], w_ref[...])).astype(o_ref.dtype)
        cast_ffn_weights()

    @pl.when((j >= GATE_W // tn) & (j < last))
    def _():
        o_ref[...] = _dot_nt(h_scr[...], w_ref[...]).astype(o_ref.dtype)
        cast_ffn_weights()

    @pl.when(j == last)
    def _():
        acc = _dot_nt(h_scr[...], w_ref[...])
        o_ref[...] = acc.astype(o_ref.dtype)
        col = PACK_SIDE_ROW % tn
        os_ref[...] = acc[:, col:col + os_ref.shape[1]]
        cast_ffn_weights()


def _rows_per_step(rows, steps, tile):
    r = tile
    while rows % r or rows // r > steps:
        r += tile
    return r


def _norm_proj(x, g, sc, sh, w, w_gate, w_up, w_down, layer, *, tm, tn):
    bsz, seq, d = x.shape
    dff = w_down.shape[1]
    n = PROJ_W_PAD
    ns = LANES
    ni, nj = seq // tm, n // tn
    assert GATE_W % tn == 0 and n % tn == 0 and w.shape[1] == PACK_ROWS and dff % FFN_TF == 0
    assert PACK_SIDE_ROW // tn == nj - 1 and GATE_W // tn < nj - 1 and PACK_SIDE_ROW % ns == 0
    r_gu = _rows_per_step(d, bsz * ni * nj, 2 * SUBLANES)
    r_dn = _rows_per_step(dff, bsz * ni * nj, 2 * SUBLANES)
    step = lambda b, i, j: (b * ni + i) * nj + j
    gu_blk = lambda b, i, j: (layer, jnp.minimum(step(b, i, j), d // r_gu - 1), 0)
    dn_blk = lambda b, i, j: (layer, jnp.minimum(step(b, i, j), dff // r_dn - 1), 0)
    first = lambda idx: (lambda b, i, j: (0,) + idx(b, i, j)[1:])
    return pl.pallas_call(
        _norm_proj_kernel,
        grid=(bsz, ni, nj),
        in_specs=[
            pl.BlockSpec((None, tm, d), lambda b, i, j: (b, i, 0)),
            pl.BlockSpec((1, d), lambda b, i, j: (0, 0)),
            pl.BlockSpec((None, 1, d), lambda b, i, j: (b, 0, 0)),
            pl.BlockSpec((None, 1, d), lambda b, i, j: (b, 0, 0)),
            pl.BlockSpec((None, tn, d), lambda b, i, j: (layer, j, 0)),
            pl.BlockSpec((None, r_gu, dff), gu_blk),
            pl.BlockSpec((None, r_gu, dff), gu_blk),
            pl.BlockSpec((None, r_dn, d), dn_blk),
        ],
        out_specs=[pl.BlockSpec((None, tm, tn), lambda b, i, j: (b, i, j)),
                   pl.BlockSpec((None, tm, ns), lambda b, i, j: (b, i, 0)),
                   pl.BlockSpec((None, r_gu, 2 * dff), first(gu_blk)),
                   pl.BlockSpec((None, r_dn, d), first(dn_blk))],
        out_shape=[jax.ShapeDtypeStruct((bsz, seq, n), BF16), jax.ShapeDtypeStruct((bsz, seq, ns), F32),
                   jax.ShapeDtypeStruct((1, d, 2 * dff), BF16), jax.ShapeDtypeStruct((1, dff, d), BF16)],
        scratch_shapes=[pltpu.VMEM((tm, d), BF16)],
        compiler_params=_params(("arbitrary", "arbitrary", "arbitrary")),
        name="norm_proj",
    )(x, g.reshape(1, d), sc, sh, w, w_gate, w_up, w_down)


def _split3(c):
    hi = c.astype(BF16).astype(F32)
    mid = (c - hi).astype(BF16).astype(F32)
    lo = (c - hi - mid).astype(BF16).astype(F32)
    return hi, mid, lo


def _fox_gate_kernel(x_ref, fb_ref, qa_ref, ka_ref, cum_ref, carry_scr):
    t = x_ref.shape[0]

    @pl.when(pl.program_id(1) == 0)
    def _():
        carry_scr[...] = jnp.zeros_like(carry_scr)

    lf = jax.nn.log_sigmoid(x_ref[...] + fb_ref[...])
    tri = (lax.broadcasted_iota(jnp.int32, (t, t), 0) >= lax.broadcasted_iota(jnp.int32, (t, t), 1)).astype(F32)
    c = jnp.dot(tri, lf, precision=HIGHEST, preferred_element_type=F32) + carry_scr[...]
    carry_scr[...] = c[t - 1:t, :]
    cum_ref[...] = c * LOG2E
    lane = lax.broadcasted_iota(jnp.int32, (t, LANES), 1)
    sub = lax.broadcasted_iota(jnp.int32, (SUBLANES, t), 0)
    ct = c.T * LOG2E
    for h in range(B_HEADS):
        hi, mid, lo = _split3(jnp.broadcast_to(c[:, h:h + 1], (t, LANES)) * LOG2E)
        ka = jnp.where(lane == 0, -hi, jnp.where(lane == 1, -mid, jnp.where(lane == 2, -lo, jnp.where(lane < 6, 1.0, 0.0))))
        ka_ref[h] = ka.astype(BF16)
        hi, mid, lo = _split3(jnp.broadcast_to(ct[h:h + 1, :], (SUBLANES, t)))
        qa = jnp.where(sub < 3, 1.0, jnp.where(sub == 3, hi, jnp.where(sub == 4, mid, jnp.where(sub == 5, lo, 0.0))))
        qa_ref[h] = jnp.concatenate([qa, jnp.zeros((LANES - SUBLANES, t), F32)], axis=0).astype(BF16)


def _fox_gate(raw, fox_b, *, t):
    bsz, seq, _ = raw.shape
    fb = jnp.zeros((1, LANES), F32).at[0, :B_HEADS].set(fox_b)
    aug = pl.BlockSpec((None, B_HEADS, t, LANES), lambda b, i: (b, 0, i, 0))
    aug_t = pl.BlockSpec((None, B_HEADS, LANES, t), lambda b, i: (b, 0, 0, i))
    return pl.pallas_call(
        _fox_gate_kernel,
        grid=(bsz, seq // t),
        in_specs=[
            pl.BlockSpec((None, t, LANES), lambda b, i: (b, i, 0)),
            pl.BlockSpec((1, LANES), lambda b, i: (0, 0)),
        ],
        out_specs=[aug_t, aug, pl.BlockSpec((None, t, LANES), lambda b, i: (b, i, 0))],
        out_shape=[jax.ShapeDtypeStruct((bsz, B_HEADS, LANES, seq), BF16),
                   jax.ShapeDtypeStruct((bsz, B_HEADS, seq, LANES), BF16),
                   jax.ShapeDtypeStruct((bsz, seq, LANES), F32)],
        scratch_shapes=[pltpu.VMEM((1, LANES), F32)],
        compiler_params=_params(("parallel", "arbitrary")),
        name="fox_gate",
    )(raw, fb)


ONES_ROWS = 16


def _softmax_step(zt, vt, carry):
    m, acc = carry
    m_new = jnp.maximum(m, jnp.max(zt, axis=0, keepdims=True))
    p = jnp.exp2(zt - m_new)
    acc = jnp.exp2(m - m_new) * acc + jnp.dot(vt, p.astype(BF16), preferred_element_type=F32)
    return m_new, acc


def _softmax_init(tq, dh):
    return (jnp.full((1, tq), NEG, F32), jnp.zeros((dh + ONES_ROWS, tq), F32))


def _softmax_finish(carry, dh):
    _, acc = carry
    return (acc[:dh, :] / acc[dh:dh + 1, :]).T


def _fill_vt(vt_scr, v_ref, chunk):
    seq, dh = v_ref.shape
    for cix in range(seq // chunk):
        sl = slice(cix * chunk, (cix + 1) * chunk)
        vt_scr[:dh, sl] = _transpose_bf16(v_ref[sl, :])
    vt_scr[dh:, :] = jnp.ones((ONES_ROWS, seq), BF16)


ATT_LOOKAHEAD = 6


def _run_substeps(steps, carry):
    carry = list(carry)
    pending = {}
    for s in range(min(ATT_LOOKAHEAD, len(steps))):
        pending[s] = steps[s][1]()
    for s, (j, _, fix, vt) in enumerate(steps):
        if s + ATT_LOOKAHEAD < len(steps):
            pending[s + ATT_LOOKAHEAD] = steps[s + ATT_LOOKAHEAD][1]()
        z, cj = pending.pop(s), carry[j]
        if fix is not None:
            z, cj = fix(z, cj)
        carry[j] = _softmax_step(z, vt, cj)
    return carry


def _dense_loop(start, n_blocks, make_steps, carry):
    big = 2 * ATT_U
    carry = lax.fori_loop(0, n_blocks // big,
                          lambda grp, c: tuple(_run_substeps(make_steps(start + grp * big, big), c)), carry)
    first = start + (n_blocks // big) * big
    return lax.fori_loop(0, (start + n_blocks - first) // ATT_U,
                         lambda grp, c: tuple(_run_substeps(make_steps(first + grp * ATT_U, ATT_U), c)), carry)


def _transpose_bf16(x):
    return x.astype(F32).T.astype(BF16)


_NT = (((1,), (1,)), ((), ()))


def _dot_nt(a, b):
    return lax.dot_general(a, b, _NT, preferred_element_type=F32)


def _moba_kernel(cfar_ref, q_ref, k_ref, v_ref, tab_ref, o_ref, kmean_scr, qt_scr, vt_scr):
    g, t = ATT_G, ATT_T
    seq, dh = k_ref.shape
    nblk = seq // t
    rows = g * t
    i = pl.program_id(2)
    base = i * g

    @pl.when(i == 0)
    def _():
        r = lax.broadcasted_iota(jnp.int32, (nblk, seq), 0)
        c = lax.broadcasted_iota(jnp.int32, (nblk, seq), 1)
        avg = jnp.where((c >= r * t) & (c < (r + 1) * t), 1.0 / t, 0.0).astype(BF16)
        kmean_scr[...] = jnp.dot(avg, k_ref[...], preferred_element_type=F32)
        _fill_vt(vt_scr, v_ref, rows)

    q = q_ref[...]
    qt_scr[:dh, :] = _transpose_bf16(q)
    sc = lax.dot_general(kmean_scr[...], q.astype(F32), _NT, precision=HIGHEST, preferred_element_type=F32)
    row = lax.broadcasted_iota(jnp.int32, (nblk, rows), 0)
    own = base + lax.broadcasted_iota(jnp.int32, (nblk, rows), 1) // t
    past = row < own
    sc = jnp.where(past, sc, -jnp.inf)
    rowf = row.astype(F32)
    picked = jnp.zeros((nblk, rows), F32)
    for _ in range(MOBA_TOPK):
        best = jnp.max(sc, axis=0, keepdims=True)
        first = jnp.min(jnp.where(sc == best, rowf, float(nblk)), axis=0, keepdims=True)
        pick = rowf == first
        picked = jnp.where(pick, 1.0, picked)
        sc = jnp.where(pick, -jnp.inf, sc)
    pen = jnp.where(past, jnp.where(picked > 0.0, 0.0, NEG), jnp.where(row == own, 0.0, NEG))
    void = jnp.where(lax.broadcasted_iota(jnp.int32, (SUBLANES, rows), 0) == SUBLANES - 1, NEG, 0.0)
    pen = jnp.concatenate([pen, jnp.zeros((LANES - SUBLANES - nblk, rows), F32), void], axis=0)
    qt_scr[dh:, :] = pen.astype(BF16)

    lane = lax.broadcasted_iota(jnp.int32, (t, LANES), 1)

    def kv(n, col):
        start = pl.multiple_of(n * t, t)
        onehot = jnp.where(lane == col, 1.0, 0.0).astype(BF16)
        return jnp.concatenate([k_ref[pl.ds(start, t), :], onehot], axis=1), vt_scr[:, pl.ds(start, t)]

    def logits(kb, j):
        return jnp.dot(kb, qt_scr[:, j * t:(j + 1) * t], preferred_element_type=F32)

    def far_steps(first, count):
        steps = []
        for u in range(count):
            kb, vb = kv(first + u, first + u)
            steps += [(j, functools.partial(logits, kb, j), None, vb) for j in range(g)]
        return steps

    carry = _dense_loop(0, jnp.maximum(base - (MOBA_FAR - 1), 0), far_steps,
                        tuple(_softmax_init(t, dh) for _ in range(g)))
    cfar = cfar_ref[pl.program_id(1)]

    def near_fix(delta, z, cj):
        if delta == MOBA_FAR - 1:
            m, acc = cj
            cj = (m + cfar, acc)
        return tab_ref[delta] + z, cj

    steps = []
    for k in range(-(MOBA_FAR - 1), g):
        n = base + k
        if k < 0:
            kb, vb = kv(jnp.maximum(n, 0), jnp.where(n >= 0, n, VOID_COL))
        else:
            kb, vb = kv(n, n)
        for j in range(max(k, 0), g):
            fix = functools.partial(near_fix, j - k) if j - k < MOBA_FAR else None
            steps.append((j, functools.partial(logits, kb, j), fix, vb))
    carry = _run_substeps(steps, carry)
    for j in range(g):
        o_ref[j * t:(j + 1) * t, :] = _softmax_finish(carry[j], dh).astype(o_ref.dtype)


def _moba(proj, tab):
    bsz, seq, _ = proj.shape
    g, t = ATT_G, ATT_T
    nblk = seq // t
    assert seq % (g * t) == 0 and nblk <= LANES - SUBLANES and nblk % SUBLANES == 0
    qc, kc, vc = COL_AQ // HEAD_DIM, COL_AK // HEAD_DIM, COL_AV // HEAD_DIM
    cfar = tab[:, MOBA_FAR, 0, 0]
    return pl.pallas_call(
        _moba_kernel,
        grid=(bsz, A_HEADS, nblk // g),
        in_specs=[
            pl.BlockSpec(memory_space=pltpu.SMEM),
            pl.BlockSpec((None, g * t, HEAD_DIM), lambda b, h, i: (b, i, qc + h)),
            pl.BlockSpec((None, seq, HEAD_DIM), lambda b, h, i: (b, 0, kc + h)),
            pl.BlockSpec((None, seq, HEAD_DIM), lambda b, h, i: (b, 0, vc + h)),
            pl.BlockSpec((None, MOBA_FAR, t, t), lambda b, h, i: (h, 0, 0, 0)),
        ],
        out_specs=pl.BlockSpec((None, g * t, HEAD_DIM), lambda b, h, i: (b, i, h)),
        out_shape=jax.ShapeDtypeStruct((bsz, seq, A_W), BF16),
        scratch_shapes=[pltpu.VMEM((nblk, HEAD_DIM), F32),
                        pltpu.VMEM((HEAD_DIM + LANES, g * t), BF16), pltpu.VMEM((HEAD_DIM + ONES_ROWS, seq), BF16)],
        compiler_params=_params(("parallel", "parallel", "arbitrary")),
        name="moba",
    )(cfar, proj, proj, proj, tab)


def _fox_kernel(start_ref, q_ref, qa_ref, k_ref, ka_ref, v_ref, o_ref, qt_scr, vt_scr):
    g, t = ATT_G, ATT_T
    rows = g * t
    seq, dh = k_ref.shape
    base = pl.program_id(2) * g
    first = start_ref[(pl.program_id(0) * pl.num_programs(1) + pl.program_id(1)) * pl.num_programs(2)
                      + pl.program_id(2)]

    @pl.when(base == 0)
    def _():
        _fill_vt(vt_scr, v_ref, rows)

    qt_scr[:dh, :] = _transpose_bf16(q_ref[...])
    qt_scr[dh:, :] = qa_ref[...]

    def kv(n):
        start = pl.multiple_of(n * t, t)
        return (jnp.concatenate([k_ref[pl.ds(start, t), :], ka_ref[pl.ds(start, t), :]], axis=1),
                vt_scr[:, pl.ds(start, t)])

    def logits(kb, j):
        return jnp.dot(kb, qt_scr[:, j * t:(j + 1) * t], preferred_element_type=F32)

    def dense_steps(first, count):
        steps = []
        for u in range(count):
            kb, vb = kv(first + u)
            steps += [(j, functools.partial(logits, kb, j), None, vb) for j in range(g)]
        return steps

    carry = _dense_loop(first, base - first, dense_steps, tuple(_softmax_init(t, dh) for _ in range(g)))
    causal = lax.broadcasted_iota(jnp.int32, (t, t), 0) <= lax.broadcasted_iota(jnp.int32, (t, t), 1)

    def diag_fix(z, cj):
        return jnp.where(causal, z, NEG), cj

    steps = []
    for k in range(g):
        kb, vb = kv(base + k)
        steps += [(j, functools.partial(logits, kb, j), diag_fix if j == k else None, vb) for j in range(k, g)]
    carry = _run_substeps(steps, carry)
    for j in range(g):
        o_ref[j * t:(j + 1) * t, :] = _softmax_finish(carry[j], dh).astype(o_ref.dtype)


FOX_SKIP_BITS = 200.0
FOX_NORM_SLACK = 1.0 + 2.0 ** -6


def _fox_bounds_kernel(q_ref, k_ref, cum_ref, o_ref, krun_scr, kpref_scr, clast_scr):
    i = pl.program_id(1)
    rows, width = q_ref.shape
    g = rows // ATT_T

    @pl.when(i == 0)
    def _():
        krun_scr[...] = jnp.zeros_like(krun_scr)
        kpref_scr[...] = jnp.zeros_like(kpref_scr)
        clast_scr[...] = jnp.zeros_like(clast_scr)

    head_of = lax.broadcasted_iota(jnp.int32, (width, LANES), 0) // HEAD_DIM
    head_sum = jnp.where(head_of == lax.broadcasted_iota(jnp.int32, (width, LANES), 1), 1.0, 0.0).astype(BF16)

    def max_norm(ref):
        x = ref[...].astype(F32)
        n2 = jnp.dot((x * x).astype(BF16), head_sum, preferred_element_type=F32)
        return jnp.sqrt(jnp.max(n2, axis=0, keepdims=True)) * FOX_NORM_SLACK

    qn, kn = max_norm(q_ref), max_norm(k_ref)
    c_first, c_last = cum_ref[0:1, :], cum_ref[rows - 1:rows, :]
    tile = lax.broadcasted_iota(jnp.int32, kpref_scr.shape, 0)
    ub = qn * kpref_scr[...] + (c_first - clast_scr[...]) + qn * kn
    ok = (ub < -FOX_SKIP_BITS) & (tile < i)
    o_ref[...] = jnp.max(jnp.where(ok, (tile + 1) * g, 0), axis=0, keepdims=True)
    krun = jnp.maximum(krun_scr[...], kn)
    krun_scr[...] = krun
    kpref_scr[pl.ds(i, 1), :] = krun
    clast_scr[pl.ds(i, 1), :] = c_last


def _fox_first_block(proj, cum2):
    bsz, seq, _ = proj.shape
    rows = ATT_G * ATT_T
    ntile = seq // rows
    hist = pltpu.VMEM((-(-ntile // SUBLANES) * SUBLANES, LANES), F32)
    first = pl.pallas_call(
        _fox_bounds_kernel,
        grid=(bsz, ntile),
        in_specs=[
            pl.BlockSpec((None, rows, B_W), lambda b, i: (b, i, COL_BQ // B_W)),
            pl.BlockSpec((None, rows, B_W), lambda b, i: (b, i, COL_BK // B_W)),
            pl.BlockSpec((None, rows, LANES), lambda b, i: (b, i, 0)),
        ],
        out_specs=pl.BlockSpec((None, None, 1, LANES), lambda b, i: (b, i, 0, 0)),
        out_shape=jax.ShapeDtypeStruct((bsz, ntile, 1, LANES), jnp.int32),
        scratch_shapes=[pltpu.VMEM((1, LANES), F32), hist, hist],
        compiler_params=_params(("parallel", "arbitrary")),
        name="fox_bounds",
    )(proj, proj, cum2)
    return first[:, :, 0, :B_HEADS].transpose(0, 2, 1).reshape(-1)


def _fox(proj, qa, ka, first):
    bsz, seq, _ = proj.shape
    g, t = ATT_G, ATT_T
    assert seq % (g * t) == 0
    qc, kc, vc = COL_BQ // HEAD_DIM, COL_BK // HEAD_DIM, COL_BV // HEAD_DIM
    return pl.pallas_call(
        _fox_kernel,
        grid_spec=pltpu.PrefetchScalarGridSpec(
            num_scalar_prefetch=1,
            grid=(bsz, B_HEADS, seq // (g * t)),
            in_specs=[
                pl.BlockSpec((None, g * t, HEAD_DIM), lambda b, h, i, s: (b, i, qc + h)),
                pl.BlockSpec((None, None, LANES, g * t), lambda b, h, i, s: (b, h, 0, i)),
                pl.BlockSpec((None, seq, HEAD_DIM), lambda b, h, i, s: (b, 0, kc + h)),
                pl.BlockSpec((None, None, seq, LANES), lambda b, h, i, s: (b, h, 0, 0)),
                pl.BlockSpec((None, seq, HEAD_DIM), lambda b, h, i, s: (b, 0, vc + h)),
            ],
            out_specs=pl.BlockSpec((None, g * t, HEAD_DIM), lambda b, h, i, s: (b, i, h)),
            scratch_shapes=[pltpu.VMEM((HEAD_DIM + LANES, g * t), BF16),
                            pltpu.VMEM((HEAD_DIM + ONES_ROWS, seq), BF16)],
        ),
        out_shape=jax.ShapeDtypeStruct((bsz, seq, B_W), BF16),
        compiler_params=_params(("parallel", "parallel", "arbitrary")),
        name="fox",
    )(first, proj, qa, proj, ka, proj)


SWA_BLOCKS = 4


def _swa_kernel(sink_ref, q_ref, kp_ref, kc_ref, vp_ref, vc_ref, tab_ref, o_ref):
    w = WINDOW
    dh = D_HEAD_DIM
    grp = D_Q_HEADS // D_KV_HEADS
    i = pl.program_id(1)
    nblk = q_ref.shape[0] // w
    span = (nblk + 1) * w
    kall = jnp.concatenate([kp_ref[...], kc_ref[...]], axis=0)
    vt = jnp.concatenate([_transpose_bf16(vp_ref[...]), _transpose_bf16(vc_ref[...])], axis=1)
    ones = jnp.ones((ONES_ROWS, span), BF16)
    kg = [kall[:, g * dh:(g + 1) * dh] for g in range(D_KV_HEADS)]
    vg = [jnp.concatenate([vt[g * dh:(g + 1) * dh, :], ones], axis=0) for g in range(D_KV_HEADS)]
    qt = _transpose_bf16(q_ref[...])
    key = lax.broadcasted_iota(jnp.int32, (2 * w, w), 0)
    first = key >= jnp.where(i > 0, 0, w)

    def block_logits(blk):
        return [jnp.dot(kg[h // grp][blk * w:(blk + 2) * w, :], qt[h * dh:(h + 1) * dh, blk * w:(blk + 1) * w],
                        preferred_element_type=F32) for h in range(D_Q_HEADS)]

    logits = block_logits(0)
    for blk in range(nblk):
        nxt = block_logits(blk + 1) if blk + 1 < nblk else None
        outs = []
        for h in range(D_Q_HEADS):
            s = tab_ref[h] + logits[h]
            if blk == 0:
                s = jnp.where(first, s, NEG)
            sink = sink_ref[h]
            m = jnp.maximum(jnp.max(s, axis=0, keepdims=True), sink)
            p = jnp.exp(s - m)
            acc = jnp.dot(vg[h // grp][:, blk * w:(blk + 2) * w], p.astype(BF16), preferred_element_type=F32)
            den = acc[dh:dh + 1, :] + jnp.exp(sink - m)
            outs.append(acc[:dh, :] / den)
        o_ref[blk * w:(blk + 1) * w, :] = jnp.concatenate(outs, axis=0).T.astype(o_ref.dtype)
        logits = nxt


def _swa(proj, sinks, tab):
    bsz, seq, _ = proj.shape
    w = WINDOW
    nb = SWA_BLOCKS
    assert seq % (nb * w) == 0
    qc, kc, vc = COL_DQ // D_QW, COL_DK // D_KVW, COL_DV // D_KVW
    prev = lambda i: jnp.maximum(i * nb - 1, 0)
    return pl.pallas_call(
        _swa_kernel,
        grid=(bsz, seq // (nb * w)),
        in_specs=[
            pl.BlockSpec(memory_space=pltpu.SMEM),
            pl.BlockSpec((None, nb * w, D_QW), lambda b, i: (b, i, qc)),
            pl.BlockSpec((None, w, D_KVW), lambda b, i: (b, prev(i), kc)),
            pl.BlockSpec((None, nb * w, D_KVW), lambda b, i: (b, i, kc)),
            pl.BlockSpec((None, w, D_KVW), lambda b, i: (b, prev(i), vc)),
            pl.BlockSpec((None, nb * w, D_KVW), lambda b, i: (b, i, vc)),
            pl.BlockSpec((D_Q_HEADS, 2 * w, w), lambda b, i: (0, 0, 0)),
        ],
        out_specs=pl.BlockSpec((None, nb * w, D_QW), lambda b, i: (b, i, 0)),
        out_shape=jax.ShapeDtypeStruct((bsz, seq, D_QW), BF16),
        compiler_params=_params(("parallel", "parallel")),
        name="swa",
    )(sinks, proj, proj, proj, proj, proj, tab)


CONV_T = 512
CONV_HALO = 32
CONV_ROWS = 256


def _conv_kernel(a_ref, g_ref, ha_ref, hg_ref, w_ref, b_ref, lg_ref, lb_ref, o_ref, buf):
    t = a_ref.shape[0]
    halo = CONV_HALO
    hp = ha_ref[...].astype(F32) * jax.nn.sigmoid(hg_ref[...].astype(F32))
    buf[0:halo, :] = jnp.where(pl.program_id(1) > 0, hp, 0.0)
    buf[halo:halo + t, :] = a_ref[...].astype(F32) * jax.nn.sigmoid(g_ref[...].astype(F32))
    off = halo - (CONV_WIDTH - 1)
    for r in range(t // CONV_ROWS):
        acc = jnp.broadcast_to(b_ref[...], (CONV_ROWS, CONV_CH))
        for rho in range(SUBLANES):
            rows = CONV_ROWS if rho == 0 else CONV_ROWS + SUBLANES
            part = None
            for s in range(off, off + CONV_WIDTH):
                if s % SUBLANES != rho:
                    continue
                start = r * CONV_ROWS + s - rho
                term = w_ref[s - off:s - off + 1, :] * buf[start:start + rows, :]
                part = term if part is None else part + term
            acc = acc + part[rho:rho + CONV_ROWS, :]
        mu = jnp.mean(acc, axis=-1, keepdims=True)
        d = acc - mu
        var = jnp.mean(d * d, axis=-1, keepdims=True)
        y = d * lax.rsqrt(var + LN_EPS) * lg_ref[...] + lb_ref[...]
        o_ref[r * CONV_ROWS:(r + 1) * CONV_ROWS, :] = (y * jax.nn.sigmoid(y)).astype(o_ref.dtype)


def _conv(proj, conv_w, conv_b, ln_g, ln_b):
    bsz, seq, _ = proj.shape
    t = min(CONV_T, seq)
    ca, cg = COL_CU // CONV_CH, COL_CU // CONV_CH + 1
    hb = t // CONV_HALO
    prev = lambda i: jnp.maximum(i * hb - 1, 0)
    row = lambda v: v.reshape(1, CONV_CH)
    return pl.pallas_call(
        _conv_kernel,
        grid=(bsz, seq // t),
        in_specs=[
            pl.BlockSpec((None, t, CONV_CH), lambda b, i: (b, i, ca)),
            pl.BlockSpec((None, t, CONV_CH), lambda b, i: (b, i, cg)),
            pl.BlockSpec((None, CONV_HALO, CONV_CH), lambda b, i: (b, prev(i), ca)),
            pl.BlockSpec((None, CONV_HALO, CONV_CH), lambda b, i: (b, prev(i), cg)),
            pl.BlockSpec((CONV_WIDTH, CONV_CH), lambda b, i: (0, 0)),
            pl.BlockSpec((1, CONV_CH), lambda b, i: (0, 0)),
            pl.BlockSpec((1, CONV_CH), lambda b, i: (0, 0)),
            pl.BlockSpec((1, CONV_CH), lambda b, i: (0, 0)),
        ],
        out_specs=pl.BlockSpec((None, t, CONV_CH), lambda b, i: (b, i, 0)),
        out_shape=jax.ShapeDtypeStruct((bsz, seq, CONV_CH), BF16),
        scratch_shapes=[pltpu.VMEM((CONV_HALO + t, CONV_CH), F32)],
        compiler_params=_params(("parallel", "parallel")),
        name="conv",
    )(proj, proj, proj, proj, conv_w, row(conv_b), row(ln_g), row(ln_b))


def _merge_kernel(ya_ref, yb_ref, yc_ref, yd_ref, gates_ref, x_ref, gt_ref, gp_ref, gf_ref, scf_ref, shf_ref,
                  wa_ref, wb_ref, wc_ref, wd_ref, wo_ref, o_ref, h_ref):
    d = x_ref.shape[1]
    y = None
    for i, (br, w) in enumerate(((ya_ref, wa_ref), (yb_ref, wb_ref), (yc_ref, wc_ref), (yd_ref, wd_ref))):
        t = jnp.dot(br[...], w[...], preferred_element_type=F32) * gates_ref[:, i * d:(i + 1) * d].astype(F32)
        y = t if y is None else y + t
    z = jnp.dot(y.astype(BF16), wo_ref[...], preferred_element_type=F32)
    x_new = x_ref[...] + gt_ref[...] * (_rms(z) * gp_ref[...])
    o_ref[...] = x_new
    h_ref[...] = _modulated_norm(x_new, gf_ref[...], scf_ref[...], shf_ref[...]).astype(h_ref.dtype)


def _merge(ya, yb, yc, yd, gates, x, gt, g_post, g_ffn, sc_f, sh_f, wa, wb, wc, wd, wo, layer, *, tm):
    bsz, seq, d = x.shape
    tok = lambda width: pl.BlockSpec((None, tm, width), lambda b, i: (b, i, 0))
    vec = lambda: pl.BlockSpec((None, 1, d), lambda b, i: (b, 0, 0))
    par = lambda: pl.BlockSpec((1, d), lambda b, i: (0, 0))
    full = lambda a: pl.BlockSpec((None,) + a.shape[1:], lambda b, i: (layer, 0, 0),
                                  pipeline_mode=pl.Buffered(1))
    return pl.pallas_call(
        _merge_kernel,
        grid=(bsz, seq // tm),
        in_specs=[tok(A_W), tok(B_W), tok(CONV_CH), tok(D_QW), tok(N_BRANCHES * d), tok(d),
                  vec(), par(), par(), vec(), vec(),
                  full(wa), full(wb), full(wc), full(wd), full(wo)],
        out_specs=[tok(d), tok(d)],
        out_shape=[jax.ShapeDtypeStruct((bsz, seq, d), F32), jax.ShapeDtypeStruct((bsz, seq, d), BF16)],
        compiler_params=_params(("parallel", "parallel")),
        name="merge",
    )(ya, yb, yc, yd, gates, x, gt, g_post.reshape(1, d), g_ffn.reshape(1, d), sc_f, sh_f, wa, wb, wc, wd, wo)


def _ffn_kernel(x_ref, h_ref, gt_ref, gp_ref, wgu_ref, wd_ref, o_ref, acc_scr):
    f = pl.program_id(2)
    tf = wd_ref.shape[0]

    def partial_down():
        gu = jnp.dot(h_ref[...], wgu_ref[...], preferred_element_type=F32)
        g, u = gu[:, :tf], gu[:, tf:]
        a = ((g * jax.nn.sigmoid(g)) * u).astype(BF16)
        return jnp.dot(a, wd_ref[...], preferred_element_type=F32)

    @pl.when(f == 0)
    def _():
        acc_scr[...] = partial_down()

    @pl.when(f > 0)
    def _():
        acc_scr[...] += partial_down()

    @pl.when(f == pl.num_programs(2) - 1)
    def _():
        def finish_rows(r, carry):
            rows = pl.ds(pl.multiple_of(r * NORM_ROWS, NORM_ROWS), NORM_ROWS)
            o_ref[rows, :] = x_ref[rows, :] + gt_ref[...] * (_rms(acc_scr[rows, :]) * gp_ref[...])
            return carry

        lax.fori_loop(0, o_ref.shape[0] // NORM_ROWS, finish_rows, 0, unroll=8)


def _ffn(x, h, gt, g_post, wgu, wd, layer, *, tm, tf):
    bsz, seq, d = x.shape
    dff = wd.shape[1]
    vec = lambda: pl.BlockSpec((None, 1, d), lambda b, i, f: (b, 0, 0))
    par = lambda: pl.BlockSpec((1, d), lambda b, i, f: (0, 0))
    return pl.pallas_call(
        _ffn_kernel,
        grid=(bsz, seq // tm, dff // tf),
        in_specs=[
            pl.BlockSpec((None, tm, d), lambda b, i, f: (b, i, 0)),
            pl.BlockSpec((None, tm, d), lambda b, i, f: (b, i, 0)),
            vec(), par(),
            pl.BlockSpec((None, d, 2 * tf), lambda b, i, f: (layer, 0, f)),
            pl.BlockSpec((None, tf, d), lambda b, i, f: (layer, f, 0)),
        ],
        out_specs=pl.BlockSpec((None, tm, d), lambda b, i, f: (b, i, 0)),
        out_shape=jax.ShapeDtypeStruct((bsz, seq, d), F32),
        scratch_shapes=[pltpu.VMEM((tm, d), F32)],
        compiler_params=_params(("parallel", "parallel", "arbitrary")),
        name="ffn",
    )(x, h, gt, g_post.reshape(1, d), wgu, wd)


SRC_QKV = 3 * A_W + 3 * B_W
SRC_REST = SRC_QKV + B_HEADS
SRC_GATE = SRC_REST + (PROJ_W - GATE_W - SRC_QKV)
IN_W = SRC_GATE + GATE_W
PACK_R = 256
PACK_SIDE_ROW = PROJ_W
PACK_ROWS = PROJ_W_PAD
assert GATE_W % PACK_R == 0 and SRC_QKV % PACK_R == 0 and PROJ_W % PACK_R == 0 and PROJ_W_PAD % PACK_R == 0


def _pack_tables():
    s2 = HEAD_DIM ** -0.5 * LOG2E
    src, scale, rows = [], [], []
    for b in range(PACK_ROWS // PACK_R):
        row = b * PACK_R
        if row < GATE_W:
            src.append(SRC_GATE + row); scale.append(1.0); rows.append(PACK_R)
        elif row < GATE_W + SRC_QKV:
            s = row - GATE_W
            is_q = s < A_W or 3 * A_W <= s < 3 * A_W + B_W
            src.append(s); scale.append(s2 if is_q else 1.0); rows.append(PACK_R)
        elif row < PROJ_W:
            s = row - GATE_W - SRC_QKV
            is_q = 2 * CONV_CH <= s < 2 * CONV_CH + D_QW
            src.append(SRC_REST + s); scale.append(D_HEAD_DIM ** -0.5 if is_q else 1.0); rows.append(PACK_R)
        elif row == PACK_SIDE_ROW:
            src.append(SRC_QKV); scale.append(1.0); rows.append(B_HEADS)
        else:
            src.append(0); scale.append(0.0); rows.append(0)
    return np.asarray(src, np.int32), np.asarray(scale, np.float32), np.asarray(rows, np.int32)


def _pack_kernel(src_ref, scale_ref, rows_ref, a_ref, o_ref):
    blk = pl.program_id(0)
    depth, _, d = o_ref.shape
    kt = d // LANES
    keep = lax.broadcasted_iota(jnp.int32, (PACK_R, d), 0) < rows_ref[blk]
    for layer in range(depth):
        cols = [a_ref[pl.ds(t * depth + layer, PACK_R, stride=kt * depth), :] for t in range(kt)]
        o_ref[layer] = jnp.where(keep, jnp.concatenate(cols, axis=1) * scale_ref[blk], 0.0).astype(o_ref.dtype)


def _pack_in_proj(w):
    depth, d, in_w = w.shape
    assert in_w == IN_W and d % LANES == 0
    kt = d // LANES
    view = jnp.transpose(w.reshape(depth, kt, LANES, in_w), (3, 1, 0, 2)).reshape(in_w * kt * depth, LANES)
    src, scale, rows = _pack_tables()
    per_feature = kt * depth
    return pl.pallas_call(
        _pack_kernel,
        grid_spec=pltpu.PrefetchScalarGridSpec(
            num_scalar_prefetch=3,
            grid=(PACK_ROWS // PACK_R,),
            in_specs=[pl.BlockSpec((pl.Element(PACK_R * per_feature), pl.Element(LANES)),
                                   lambda b, src, scale, rows: (src[b] * per_feature, 0))],
            out_specs=pl.BlockSpec((depth, PACK_R, d), lambda b, src, scale, rows: (0, b, 0)),
        ),
        out_shape=jax.ShapeDtypeStruct((depth, PACK_ROWS, d), BF16),
        compiler_params=_params(("parallel",)),
        name="pack_in_proj",
    )(jnp.asarray(src), jnp.asarray(scale), jnp.asarray(rows), view)


CAST_BLOCK_ELEMS = 1 << 20


def _cast_kernel(w_ref, o_ref):
    o_ref[...] = w_ref[...].astype(o_ref.dtype)


def _cast_bf16(w):
    depth, r, c = w.shape
    tr = 16
    while tr * 2 * c <= CAST_BLOCK_ELEMS and r % (tr * 2) == 0:
        tr *= 2
    assert r % tr == 0
    return pl.pallas_call(
        _cast_kernel,
        grid=(depth, r // tr),
        in_specs=[pl.BlockSpec((None, tr, c), lambda l, i: (l, i, 0))],
        out_specs=pl.BlockSpec((None, tr, c), lambda l, i: (l, i, 0)),
        out_shape=jax.ShapeDtypeStruct(w.shape, BF16),
        compiler_params=_params(("parallel", "parallel")),
        name="cast_bf16",
    )(w)


def kernel(x, c, rel_bias, w_mod, b_mod, mix_norm_pre, mix_norm_post, w_in, fox_bias, conv_w, conv_b, conv_ln_g, conv_ln_b, sinks, w_branch_a, w_branch_b, w_branch_c, w_branch_d, w_out, ffn_norm_pre, ffn_norm_post, w_ffn_gate, w_ffn_up, w_ffn_down):
    depth = w_mod.shape[0]
    bsz, seq, d = x.shape
    tm = min(1024, seq)
    tm_small = min(256, seq)
    tm_ffn = min(512, seq)

    mod = _modulation(c, w_mod, b_mod)
    tab_a = _bias_tables(rel_bias[:, :A_HEADS], _moba_buckets(), scale=LOG2E)
    tab_d = _bias_tables(rel_bias[:, A_HEADS:], _swa_buckets(), scale=1.0)[:, 0]

    wa, wb, wc, wd, wo = [_cast_bf16(w) for w in (w_branch_a, w_branch_b, w_branch_c, w_branch_d, w_out)]

    w_proj = _pack_in_proj(w_in)

    for l in range(depth):
        sh_m, sc_m, gt_m, sh_f, sc_f, gt_f = [mod[l, :, None, i * d:(i + 1) * d] for i in range(6)]

        proj, fox_raw, wgu, wdn = _norm_proj(x, mix_norm_pre[l], sc_m, sh_m, w_proj, w_ffn_gate, w_ffn_up, w_ffn_down, l,
                                             tm=tm, tn=PROJ_TN)
        fox_qa, fox_ka, fox_cum = _fox_gate(fox_raw, fox_bias[l], t=min(512, seq))

        ya = _moba(proj, tab_a)
        yb = _fox(proj, fox_qa, fox_ka, _fox_first_block(proj, fox_cum))
        yc = _conv(proj, conv_w[l], conv_b[l], conv_ln_g[l], conv_ln_b[l])
        yd = _swa(proj, sinks[l], tab_d)

        x, h_ffn = _merge(ya, yb, yc, yd, proj, x, gt_m, mix_norm_post[l], ffn_norm_pre[l], sc_f, sh_f,
                          wa, wb, wc, wd, wo, l, tm=tm_small)
        x = _ffn(x, h_ffn, gt_f, ffn_norm_post[l], wgu, wdn, 0, tm=tm_ffn, tf=FFN_TF)
    return x
```

```python
import functools
import math

import jax
import jax.numpy as jnp
import numpy as np
from jax import lax
from jax.experimental import pallas as pl
from jax.experimental.pallas import tpu as pltpu

F32 = jnp.float32
BF16 = jnp.bfloat16
HIGHEST = lax.Precision.HIGHEST

HEAD_DIM = 128
A_HEADS = 4
MOBA_BLOCK = 256
MOBA_TOPK = 3
B_HEADS = 4
CONV_CH = 512
CONV_WIDTH = 31
D_Q_HEADS = 8
D_KV_HEADS = 2
D_HEAD_DIM = 64
WINDOW = 128
N_BUCKETS = 32
MAX_DISTANCE = 1024
N_BRANCHES = 4
RMS_EPS = 1e-6
LN_EPS = 1e-5

A_W = A_HEADS * HEAD_DIM
B_W = B_HEADS * HEAD_DIM
D_QW = D_Q_HEADS * D_HEAD_DIM
D_KVW = D_KV_HEADS * D_HEAD_DIM

LANES = 128
SUBLANES = 8
VMEM_LIMIT = 56 * 1024 * 1024

NEG = -1e30
LOG2E = math.log2(math.e)
ATT_G = 4
ATT_T = MOBA_BLOCK
ATT_U = 4
FFN_TF = 512
NORM_ROWS = 32

D_MODEL = 2048
GATE_W = N_BRANCHES * D_MODEL
COL_AQ, COL_AK, COL_AV = GATE_W, GATE_W + A_W, GATE_W + 2 * A_W
COL_BQ, COL_BK, COL_BV = COL_AV + A_W, COL_AV + A_W + B_W, COL_AV + A_W + 2 * B_W
COL_CU = COL_BV + B_W
COL_DQ = COL_CU + 2 * CONV_CH
COL_DK = COL_DQ + D_QW
COL_DV = COL_DK + D_KVW
PROJ_W = COL_DV + D_KVW
PROJ_TN = 1024
PROJ_W_PAD = -(-PROJ_W // PROJ_TN) * PROJ_TN
MOBA_FAR = -(-(MAX_DISTANCE + MOBA_BLOCK - 1) // MOBA_BLOCK)
VOID_COL = LANES - 1
assert ATT_G % ATT_U == 0 and (MOBA_FAR - 1) % ATT_U == 0


def _params(sem, vmem=VMEM_LIMIT):
    return pltpu.CompilerParams(dimension_semantics=sem, vmem_limit_bytes=vmem)


def _t5_bucket(dist):
    max_exact = N_BUCKETS // 2
    d = jnp.maximum(dist, 0)
    log_ratio = jnp.log(jnp.maximum(d, 1).astype(jnp.float32) / max_exact) / math.log(MAX_DISTANCE / max_exact)
    large = max_exact + (log_ratio * (N_BUCKETS - max_exact)).astype(jnp.int32)
    large = jnp.minimum(large, N_BUCKETS - 1)
    return jnp.where(d < max_exact, d, large)


def _sigmoid(x):
    return 0.5 * jnp.tanh(0.5 * x) + 0.5


def _rms(y):
    return y * lax.rsqrt(jnp.mean(y * y, axis=-1, keepdims=True) + RMS_EPS)


def _modulated_norm(x, g, sc, sh):
    return (_rms(x) * g) * (1.0 + sc) + sh


MOD_TN = 512
MOD_KC = 256


def _mod_kernel(ct_ref, w_ref, b_ref, o_ref):
    d, nb = ct_ref.shape
    tn = w_ref.shape[1]
    ct = ct_ref[...]
    ca = ct * jax.nn.sigmoid(ct)
    rows = []
    for r in range(nb):
        acc = jnp.zeros((1, tn), F32)
        for kc in range(d // MOD_KC):
            sl = slice(kc * MOD_KC, (kc + 1) * MOD_KC)
            acc = acc + jnp.sum(w_ref[sl, :] * ca[sl, r:r + 1], axis=0, keepdims=True)
        rows.append(acc)
    o_ref[...] = jnp.concatenate(rows, axis=0) + b_ref[...]


def _modulation(c, w_mod, b_mod):
    depth, d, n = w_mod.shape
    bsz = c.shape[0]
    return pl.pallas_call(
        _mod_kernel,
        grid=(depth, n // MOD_TN),
        in_specs=[
            pl.BlockSpec((d, bsz), lambda l, j: (0, 0)),
            pl.BlockSpec((None, d, MOD_TN), lambda l, j: (l, 0, j)),
            pl.BlockSpec((None, 1, MOD_TN), lambda l, j: (l, 0, j)),
        ],
        out_specs=pl.BlockSpec((None, bsz, MOD_TN), lambda l, j: (l, 0, j)),
        out_shape=jax.ShapeDtypeStruct((depth, bsz, n), F32),
        compiler_params=_params(("parallel", "parallel")),
        name="mod",
    )(c.T, w_mod, b_mod.reshape(depth, 1, n))


def _bias_table_kernel(rb_ref, bucket_ref, o_ref, *, scale):
    h = pl.program_id(0)
    b = bucket_ref[...]
    acc = jnp.where(b < 0, NEG, 0.0).astype(F32)
    for u in range(N_BUCKETS):
        acc = jnp.where(b == u, rb_ref[h, u] * scale, acc)
    o_ref[...] = acc


def _bias_tables(rel_bias_heads, bucket, *, scale):
    nh = rel_bias_heads.shape[1]
    nt, r, c = bucket.shape
    return pl.pallas_call(
        functools.partial(_bias_table_kernel, scale=scale),
        grid=(nh, nt),
        in_specs=[
            pl.BlockSpec(memory_space=pltpu.SMEM),
            pl.BlockSpec((None, r, c), lambda h, t: (t, 0, 0)),
        ],
        out_specs=pl.BlockSpec((None, None, r, c), lambda h, t: (h, t, 0, 0)),
        out_shape=jax.ShapeDtypeStruct((nh, nt, r, c), F32),
        compiler_params=_params(("parallel", "parallel")),
        name="bias_table",
    )(rel_bias_heads.T, bucket)


def _moba_buckets():
    blk = MOBA_BLOCK
    i = jnp.arange(blk)[None, :]
    j = jnp.arange(blk)[:, None]
    tabs = []
    for delta in range(MOBA_FAR):
        dist = delta * blk + i - j
        tabs.append(jnp.where(dist >= 0, _t5_bucket(dist), -1))
    tabs.append(_t5_bucket(jnp.full((blk, blk), MAX_DISTANCE, jnp.int32)))
    return jnp.stack(tabs).astype(jnp.int32)


def _swa_buckets():
    qi = jnp.arange(WINDOW)[None, :]
    kj = jnp.arange(2 * WINDOW)[:, None]
    dist = qi + WINDOW - kj
    in_win = (dist >= 0) & (dist < WINDOW)
    return jnp.where(in_win, _t5_bucket(dist), -1).astype(jnp.int32)[None]


def _norm_proj_kernel(x_ref, g_ref, sc_ref, sh_ref, w_ref, wg_ref, wu_ref, wdn_ref,
                      o_ref, os_ref, qt_ref, wgu_out, wdn_out, h_scr):
    j = pl.program_id(2)
    tn = w_ref.shape[0]
    last = pl.num_programs(2) - 1

    def cast_ffn_weights():
        tf = FFN_TF
        for f in range(wg_ref.shape[1] // tf):
            wgu_out[:, 2 * f * tf:(2 * f + 1) * tf] = wg_ref[:, f * tf:(f + 1) * tf].astype(BF16)
            wgu_out[:, (2 * f + 1) * tf:(2 * f + 2) * tf] = wu_ref[:, f * tf:(f + 1) * tf].astype(BF16)
        wdn_out[...] = wdn_ref[...].astype(BF16)

    @pl.when(j == 0)
    def _():
        gain = g_ref[...] * (1.0 + sc_ref[...])

        def norm_rows(r, carry):
            rows = pl.ds(pl.multiple_of(r * NORM_ROWS, NORM_ROWS), NORM_ROWS)
            h_scr[rows, :] = (_rms(x_ref[rows, :]) * gain + sh_ref[...]).astype(BF16)
            return carry

        lax.fori_loop(0, x_ref.shape[0] // NORM_ROWS, norm_rows, 0, unroll=8)

    @pl.when(j < GATE_W // tn)
    def _():
        o_ref[...] = _sigmoid(_dot_nt(h_scr[...], w_ref[...])).astype(o_ref.dtype)
        cast_ffn_weights()

    j_aq, j_bq = COL_AQ // tn, COL_BQ // tn

    @pl.when((j >= GATE_W // tn) & (j < last) & (j != j_aq) & (j != j_bq))
    def _():
        o_ref[...] = _dot_nt(h_scr[...], w_ref[...]).astype(o_ref.dtype)
        cast_ffn_weights()

    def query_tile(col):
        q0 = col % tn
        h = h_scr[...]
        qt_ref[...] = _dot_nt(w_ref[q0:q0 + qt_ref.shape[0], :], h).astype(qt_ref.dtype)
        rest = slice(0, q0) if q0 else slice(qt_ref.shape[0], tn)
        o_ref[:, rest] = _dot_nt(h, w_ref[rest, :]).astype(o_ref.dtype)
        o_ref[:, q0:q0 + qt_ref.shape[0]] = jnp.zeros((o_ref.shape[0], qt_ref.shape[0]), o_ref.dtype)
        cast_ffn_weights()

    @pl.when(j == j_aq)
    def _():
        query_tile(COL_AQ)

    @pl.when(j == j_bq)
    def _():
        query_tile(COL_BQ)

    @pl.when(j == last)
    def _():
        acc = _dot_nt(h_scr[...], w_ref[...])
        o_ref[...] = acc.astype(o_ref.dtype)
        col = PACK_SIDE_ROW % tn
        os_ref[...] = acc[:, col:col + os_ref.shape[1]]
        cast_ffn_weights()


def _rows_per_step(rows, steps, tile):
    r = tile
    while rows % r or rows // r > steps:
        r += tile
    return r


def _norm_proj(x, g, sc, sh, w, w_gate, w_up, w_down, layer, *, tm, tn):
    bsz, seq, d = x.shape
    dff = w_down.shape[1]
    n = PROJ_W_PAD
    ns = LANES
    ni, nj = seq // tm, n // tn
    assert GATE_W % tn == 0 and n % tn == 0 and w.shape[1] == PACK_ROWS and dff % FFN_TF == 0
    assert PACK_SIDE_ROW // tn == nj - 1 and GATE_W // tn < nj - 1 and PACK_SIDE_ROW % ns == 0
    j_aq, j_bq = COL_AQ // tn, COL_BQ // tn
    assert A_W == B_W and GATE_W // tn <= j_aq < j_bq < nj - 1 and tn == 2 * A_W
    assert COL_AQ % tn in (0, A_W) and COL_BQ % tn in (0, A_W)
    r_gu = _rows_per_step(d, bsz * ni * nj, 2 * SUBLANES)
    r_dn = _rows_per_step(dff, bsz * ni * nj, 2 * SUBLANES)
    step = lambda b, i, j: (b * ni + i) * nj + j
    gu_blk = lambda b, i, j: (layer, jnp.minimum(step(b, i, j), d // r_gu - 1), 0)
    dn_blk = lambda b, i, j: (layer, jnp.minimum(step(b, i, j), dff // r_dn - 1), 0)
    first = lambda idx: (lambda b, i, j: (0,) + idx(b, i, j)[1:])
    return pl.pallas_call(
        _norm_proj_kernel,
        grid=(bsz, ni, nj),
        in_specs=[
            pl.BlockSpec((None, tm, d), lambda b, i, j: (b, i, 0)),
            pl.BlockSpec((1, d), lambda b, i, j: (0, 0)),
            pl.BlockSpec((None, 1, d), lambda b, i, j: (b, 0, 0)),
            pl.BlockSpec((None, 1, d), lambda b, i, j: (b, 0, 0)),
            pl.BlockSpec((None, tn, d), lambda b, i, j: (layer, j, 0)),
            pl.BlockSpec((None, r_gu, dff), gu_blk),
            pl.BlockSpec((None, r_gu, dff), gu_blk),
            pl.BlockSpec((None, r_dn, d), dn_blk),
        ],
        out_specs=[pl.BlockSpec((None, tm, tn), lambda b, i, j: (b, i, j)),
                   pl.BlockSpec((None, tm, ns), lambda b, i, j: (b, i, 0)),
                   pl.BlockSpec((None, A_W, tm), lambda b, i, j: (b, jnp.where(j <= j_aq, 0, 1), i)),
                   pl.BlockSpec((None, r_gu, 2 * dff), first(gu_blk)),
                   pl.BlockSpec((None, r_dn, d), first(dn_blk))],
        out_shape=[jax.ShapeDtypeStruct((bsz, seq, n), BF16), jax.ShapeDtypeStruct((bsz, seq, ns), F32),
                   jax.ShapeDtypeStruct((bsz, A_W + B_W, seq), BF16),
                   jax.ShapeDtypeStruct((1, d, 2 * dff), BF16), jax.ShapeDtypeStruct((1, dff, d), BF16)],
        scratch_shapes=[pltpu.VMEM((tm, d), BF16)],
        compiler_params=_params(("arbitrary", "arbitrary", "arbitrary")),
        name="norm_proj",
    )(x, g.reshape(1, d), sc, sh, w, w_gate, w_up, w_down)


def _split3(c):
    hi = c.astype(BF16).astype(F32)
    mid = (c - hi).astype(BF16).astype(F32)
    lo = (c - hi - mid).astype(BF16).astype(F32)
    return hi, mid, lo


def _fox_gate_kernel(x_ref, fb_ref, qa_ref, ka_ref, cum_ref, carry_scr):
    t = x_ref.shape[0]

    @pl.when(pl.program_id(1) == 0)
    def _():
        carry_scr[...] = jnp.zeros_like(carry_scr)

    lf = jax.nn.log_sigmoid(x_ref[...] + fb_ref[...])
    tri = (lax.broadcasted_iota(jnp.int32, (t, t), 0) >= lax.broadcasted_iota(jnp.int32, (t, t), 1)).astype(F32)
    c = jnp.dot(tri, lf, precision=HIGHEST, preferred_element_type=F32) + carry_scr[...]
    carry_scr[...] = c[t - 1:t, :]
    cum_ref[...] = c * LOG2E
    lane = lax.broadcasted_iota(jnp.int32, (t, LANES), 1)
    sub = lax.broadcasted_iota(jnp.int32, (SUBLANES, t), 0)
    ct = c.T * LOG2E
    for h in range(B_HEADS):
        hi, mid, lo = _split3(jnp.broadcast_to(c[:, h:h + 1], (t, LANES)) * LOG2E)
        ka = jnp.where(lane == 0, -hi, jnp.where(lane == 1, -mid, jnp.where(lane == 2, -lo, jnp.where(lane < 6, 1.0, 0.0))))
        ka_ref[h] = ka.astype(BF16)
        hi, mid, lo = _split3(jnp.broadcast_to(ct[h:h + 1, :], (SUBLANES, t)))
        qa = jnp.where(sub < 3, 1.0, jnp.where(sub == 3, hi, jnp.where(sub == 4, mid, jnp.where(sub == 5, lo, 0.0))))
        qa_ref[h] = jnp.concatenate([qa, jnp.zeros((LANES - SUBLANES, t), F32)], axis=0).astype(BF16)


def _fox_gate(raw, fox_b, *, t):
    bsz, seq, _ = raw.shape
    fb = jnp.zeros((1, LANES), F32).at[0, :B_HEADS].set(fox_b)
    aug = pl.BlockSpec((None, B_HEADS, t, LANES), lambda b, i: (b, 0, i, 0))
    aug_t = pl.BlockSpec((None, B_HEADS, LANES, t), lambda b, i: (b, 0, 0, i))
    return pl.pallas_call(
        _fox_gate_kernel,
        grid=(bsz, seq // t),
        in_specs=[
            pl.BlockSpec((None, t, LANES), lambda b, i: (b, i, 0)),
            pl.BlockSpec((1, LANES), lambda b, i: (0, 0)),
        ],
        out_specs=[aug_t, aug, pl.BlockSpec((None, t, LANES), lambda b, i: (b, i, 0))],
        out_shape=[jax.ShapeDtypeStruct((bsz, B_HEADS, LANES, seq), BF16),
                   jax.ShapeDtypeStruct((bsz, B_HEADS, seq, LANES), BF16),
                   jax.ShapeDtypeStruct((bsz, seq, LANES), F32)],
        scratch_shapes=[pltpu.VMEM((1, LANES), F32)],
        compiler_params=_params(("parallel", "arbitrary")),
        name="fox_gate",
    )(raw, fb)


ONES_ROWS = 16


def _softmax_step(zt, vt, carry):
    m, acc = carry
    m_new = jnp.maximum(m, jnp.max(zt, axis=0, keepdims=True))
    p = jnp.exp2(zt - m_new)
    acc = jnp.exp2(m - m_new) * acc + jnp.dot(vt, p.astype(BF16), preferred_element_type=F32)
    return m_new, acc


def _softmax_init(tq, dh):
    return (jnp.full((1, tq), NEG, F32), jnp.zeros((dh + ONES_ROWS, tq), F32))


def _softmax_finish(carry, dh):
    _, acc = carry
    return (acc[:dh, :] / acc[dh:dh + 1, :]).T


def _fill_vt(vt_scr, v_ref, chunk):
    seq, dh = v_ref.shape
    for cix in range(seq // chunk):
        sl = slice(cix * chunk, (cix + 1) * chunk)
        vt_scr[:dh, sl] = _transpose_bf16(v_ref[sl, :])
    vt_scr[dh:, :] = jnp.ones((ONES_ROWS, seq), BF16)


ATT_LOOKAHEAD = 6


def _run_substeps(steps, carry):
    carry = list(carry)
    pending = {}
    for s in range(min(ATT_LOOKAHEAD, len(steps))):
        pending[s] = steps[s][1]()
    for s, (j, _, fix, vt) in enumerate(steps):
        if s + ATT_LOOKAHEAD < len(steps):
            pending[s + ATT_LOOKAHEAD] = steps[s + ATT_LOOKAHEAD][1]()
        z, cj = pending.pop(s), carry[j]
        if fix is not None:
            z, cj = fix(z, cj)
        carry[j] = _softmax_step(z, vt, cj)
    return carry


def _dense_loop(start, n_blocks, make_steps, carry):
    big = 2 * ATT_U
    carry = lax.fori_loop(0, n_blocks // big,
                          lambda grp, c: tuple(_run_substeps(make_steps(start + grp * big, big), c)), carry)
    first = start + (n_blocks // big) * big
    return lax.fori_loop(0, (start + n_blocks - first) // ATT_U,
                         lambda grp, c: tuple(_run_substeps(make_steps(first + grp * ATT_U, ATT_U), c)), carry)


def _transpose_bf16(x):
    return x.astype(F32).T.astype(BF16)


_NT = (((1,), (1,)), ((), ()))


def _dot_nt(a, b):
    return lax.dot_general(a, b, _NT, preferred_element_type=F32)


def _moba_kernel(cfar_ref, q_ref, k_ref, v_ref, tab_ref, o_ref, kmean_scr, qt_scr, vt_scr):
    g, t = ATT_G, ATT_T
    seq, dh = k_ref.shape
    nblk = seq // t
    rows = g * t
    i = pl.program_id(2)
    base = i * g

    @pl.when(i == 0)
    def _():
        r = lax.broadcasted_iota(jnp.int32, (nblk, seq), 0)
        c = lax.broadcasted_iota(jnp.int32, (nblk, seq), 1)
        avg = jnp.where((c >= r * t) & (c < (r + 1) * t), 1.0 / t, 0.0).astype(BF16)
        kmean_scr[...] = jnp.dot(avg, k_ref[...], preferred_element_type=F32)
        _fill_vt(vt_scr, v_ref, rows)

    qt = q_ref[...]
    qt_scr[:dh, :] = qt
    sc = jnp.dot(kmean_scr[...], qt.astype(F32), precision=HIGHEST, preferred_element_type=F32)
    row = lax.broadcasted_iota(jnp.int32, (nblk, rows), 0)
    own = base + lax.broadcasted_iota(jnp.int32, (nblk, rows), 1) // t
    past = row < own
    sc = jnp.where(past, sc, -jnp.inf)
    rowf = row.astype(F32)
    picked = jnp.zeros((nblk, rows), F32)
    for _ in range(MOBA_TOPK):
        best = jnp.max(sc, axis=0, keepdims=True)
        first = jnp.min(jnp.where(sc == best, rowf, float(nblk)), axis=0, keepdims=True)
        pick = rowf == first
        picked = jnp.where(pick, 1.0, picked)
        sc = jnp.where(pick, -jnp.inf, sc)
    pen = jnp.where(past, jnp.where(picked > 0.0, 0.0, NEG), jnp.where(row == own, 0.0, NEG))
    void = jnp.where(lax.broadcasted_iota(jnp.int32, (SUBLANES, rows), 0) == SUBLANES - 1, NEG, 0.0)
    pen = jnp.concatenate([pen, jnp.zeros((LANES - SUBLANES - nblk, rows), F32), void], axis=0)
    qt_scr[dh:, :] = pen.astype(BF16)

    lane = lax.broadcasted_iota(jnp.int32, (t, LANES), 1)

    def kv(n, col):
        start = pl.multiple_of(n * t, t)
        onehot = jnp.where(lane == col, 1.0, 0.0).astype(BF16)
        return jnp.concatenate([k_ref[pl.ds(start, t), :], onehot], axis=1), vt_scr[:, pl.ds(start, t)]

    def logits(kb, j):
        return jnp.dot(kb, qt_scr[:, j * t:(j + 1) * t], preferred_element_type=F32)

    def far_steps(first, count):
        steps = []
        for u in range(count):
            kb, vb = kv(first + u, first + u)
            steps += [(j, functools.partial(logits, kb, j), None, vb) for j in range(g)]
        return steps

    carry = _dense_loop(0, jnp.maximum(base - (MOBA_FAR - 1), 0), far_steps,
                        tuple(_softmax_init(t, dh) for _ in range(g)))
    cfar = cfar_ref[pl.program_id(1)]

    def near_fix(delta, z, cj):
        if delta == MOBA_FAR - 1:
            m, acc = cj
            cj = (m + cfar, acc)
        return tab_ref[delta] + z, cj

    steps = []
    for k in range(-(MOBA_FAR - 1), g):
        n = base + k
        if k < 0:
            kb, vb = kv(jnp.maximum(n, 0), jnp.where(n >= 0, n, VOID_COL))
        else:
            kb, vb = kv(n, n)
        for j in range(max(k, 0), g):
            fix = functools.partial(near_fix, j - k) if j - k < MOBA_FAR else None
            steps.append((j, functools.partial(logits, kb, j), fix, vb))
    carry = _run_substeps(steps, carry)
    for j in range(g):
        o_ref[j * t:(j + 1) * t, :] = _softmax_finish(carry[j], dh).astype(o_ref.dtype)


def _moba(proj, qt, tab):
    bsz, seq, _ = proj.shape
    g, t = ATT_G, ATT_T
    nblk = seq // t
    assert seq % (g * t) == 0 and nblk <= LANES - SUBLANES and nblk % SUBLANES == 0
    kc, vc = COL_AK // HEAD_DIM, COL_AV // HEAD_DIM
    cfar = tab[:, MOBA_FAR, 0, 0]
    return pl.pallas_call(
        _moba_kernel,
        grid=(bsz, A_HEADS, nblk // g),
        in_specs=[
            pl.BlockSpec(memory_space=pltpu.SMEM),
            pl.BlockSpec((None, HEAD_DIM, g * t), lambda b, h, i: (b, h, i)),
            pl.BlockSpec((None, seq, HEAD_DIM), lambda b, h, i: (b, 0, kc + h)),
            pl.BlockSpec((None, seq, HEAD_DIM), lambda b, h, i: (b, 0, vc + h)),
            pl.BlockSpec((None, MOBA_FAR, t, t), lambda b, h, i: (h, 0, 0, 0)),
        ],
        out_specs=pl.BlockSpec((None, g * t, HEAD_DIM), lambda b, h, i: (b, i, h)),
        out_shape=jax.ShapeDtypeStruct((bsz, seq, A_W), BF16),
        scratch_shapes=[pltpu.VMEM((nblk, HEAD_DIM), F32),
                        pltpu.VMEM((HEAD_DIM + LANES, g * t), BF16), pltpu.VMEM((HEAD_DIM + ONES_ROWS, seq), BF16)],
        compiler_params=_params(("parallel", "parallel", "arbitrary")),
        name="moba",
    )(cfar, qt, proj, proj, tab)


def _fox_kernel(start_ref, q_ref, qa_ref, k_ref, ka_ref, v_ref, o_ref, qt_scr, vt_scr):
    g, t = ATT_G, ATT_T
    rows = g * t
    seq, dh = k_ref.shape
    base = pl.program_id(2) * g
    first = start_ref[(pl.program_id(0) * pl.num_programs(1) + pl.program_id(1)) * pl.num_programs(2)
                      + pl.program_id(2)]

    @pl.when(base == 0)
    def _():
        _fill_vt(vt_scr, v_ref, rows)

    qt_scr[:dh, :] = q_ref[...]
    qt_scr[dh:, :] = qa_ref[...]

    def kv(n):
        start = pl.multiple_of(n * t, t)
        return (jnp.concatenate([k_ref[pl.ds(start, t), :], ka_ref[pl.ds(start, t), :]], axis=1),
                vt_scr[:, pl.ds(start, t)])

    def logits(kb, j):
        return jnp.dot(kb, qt_scr[:, j * t:(j + 1) * t], preferred_element_type=F32)

    def dense_steps(first, count):
        steps = []
        for u in range(count):
            kb, vb = kv(first + u)
            steps += [(j, functools.partial(logits, kb, j), None, vb) for j in range(g)]
        return steps

    carry = _dense_loop(first, base - first, dense_steps, tuple(_softmax_init(t, dh) for _ in range(g)))
    causal = lax.broadcasted_iota(jnp.int32, (t, t), 0) <= lax.broadcasted_iota(jnp.int32, (t, t), 1)

    def diag_fix(z, cj):
        return jnp.where(causal, z, NEG), cj

    steps = []
    for k in range(g):
        kb, vb = kv(base + k)
        steps += [(j, functools.partial(logits, kb, j), diag_fix if j == k else None, vb) for j in range(k, g)]
    carry = _run_substeps(steps, carry)
    for j in range(g):
        o_ref[j * t:(j + 1) * t, :] = _softmax_finish(carry[j], dh).astype(o_ref.dtype)


FOX_SKIP_BITS = 200.0
FOX_NORM_SLACK = 1.0 + 2.0 ** -6


def _fox_bounds_kernel(q_ref, k_ref, cum_ref, o_ref, krun_scr, kpref_scr, clast_scr):
    i = pl.program_id(1)
    rows, width = k_ref.shape
    g = rows // ATT_T

    @pl.when(i == 0)
    def _():
        krun_scr[...] = jnp.zeros_like(krun_scr)
        kpref_scr[...] = jnp.zeros_like(kpref_scr)
        clast_scr[...] = jnp.zeros_like(clast_scr)

    head_of = lax.broadcasted_iota(jnp.int32, (width, LANES), 0) // HEAD_DIM
    head_sum = jnp.where(head_of == lax.broadcasted_iota(jnp.int32, (width, LANES), 1), 1.0, 0.0).astype(BF16)

    def max_norm(ref):
        x = ref[...].astype(F32)
        n2 = jnp.dot((x * x).astype(BF16), head_sum, preferred_element_type=F32)
        return jnp.sqrt(jnp.max(n2, axis=0, keepdims=True)) * FOX_NORM_SLACK

    def max_norm_t(ref):
        lane = lax.broadcasted_iota(jnp.int32, (1, LANES), 1)
        out = jnp.zeros((1, LANES), F32)
        for h in range(width // HEAD_DIM):
            x = ref[h * HEAD_DIM:(h + 1) * HEAD_DIM, :].astype(F32)
            n2 = jnp.max(jnp.sum(x * x, axis=0, keepdims=True), axis=1, keepdims=True)
            out = jnp.where(lane == h, jnp.sqrt(n2) * FOX_NORM_SLACK, out)
        return out

    qn, kn = max_norm_t(q_ref), max_norm(k_ref)
    c_first, c_last = cum_ref[0:1, :], cum_ref[rows - 1:rows, :]
    tile = lax.broadcasted_iota(jnp.int32, kpref_scr.shape, 0)
    ub = qn * kpref_scr[...] + (c_first - clast_scr[...]) + qn * kn
    ok = (ub < -FOX_SKIP_BITS) & (tile < i)
    o_ref[...] = jnp.max(jnp.where(ok, (tile + 1) * g, 0), axis=0, keepdims=True)
    krun = jnp.maximum(krun_scr[...], kn)
    krun_scr[...] = krun
    kpref_scr[pl.ds(i, 1), :] = krun
    clast_scr[pl.ds(i, 1), :] = c_last


def _fox_first_block(proj, qt, cum2):
    bsz, seq, _ = proj.shape
    rows = ATT_G * ATT_T
    ntile = seq // rows
    hist = pltpu.VMEM((-(-ntile // SUBLANES) * SUBLANES, LANES), F32)
    first = pl.pallas_call(
        _fox_bounds_kernel,
        grid=(bsz, ntile),
        in_specs=[
            pl.BlockSpec((None, B_W, rows), lambda b, i: (b, A_W // B_W, i)),
            pl.BlockSpec((None, rows, B_W), lambda b, i: (b, i, COL_BK // B_W)),
            pl.BlockSpec((None, rows, LANES), lambda b, i: (b, i, 0)),
        ],
        out_specs=pl.BlockSpec((None, None, 1, LANES), lambda b, i: (b, i, 0, 0)),
        out_shape=jax.ShapeDtypeStruct((bsz, ntile, 1, LANES), jnp.int32),
        scratch_shapes=[pltpu.VMEM((1, LANES), F32), hist, hist],
        compiler_params=_params(("parallel", "arbitrary")),
        name="fox_bounds",
    )(qt, proj, cum2)
    return first[:, :, 0, :B_HEADS].transpose(0, 2, 1).reshape(-1)


def _fox(proj, qt, qa, ka, first):
    bsz, seq, _ = proj.shape
    g, t = ATT_G, ATT_T
    assert seq % (g * t) == 0
    kc, vc = COL_BK // HEAD_DIM, COL_BV // HEAD_DIM
    return pl.pallas_call(
        _fox_kernel,
        grid_spec=pltpu.PrefetchScalarGridSpec(
            num_scalar_prefetch=1,
            grid=(bsz, B_HEADS, seq // (g * t)),
            in_specs=[
                pl.BlockSpec((None, HEAD_DIM, g * t), lambda b, h, i, s: (b, A_HEADS + h, i)),
                pl.BlockSpec((None, None, LANES, g * t), lambda b, h, i, s: (b, h, 0, i)),
                pl.BlockSpec((None, seq, HEAD_DIM), lambda b, h, i, s: (b, 0, kc + h)),
                pl.BlockSpec((None, None, seq, LANES), lambda b, h, i, s: (b, h, 0, 0)),
                pl.BlockSpec((None, seq, HEAD_DIM), lambda b, h, i, s: (b, 0, vc + h)),
            ],
            out_specs=pl.BlockSpec((None, g * t, HEAD_DIM), lambda b, h, i, s: (b, i, h)),
            scratch_shapes=[pltpu.VMEM((HEAD_DIM + LANES, g * t), BF16),
                            pltpu.VMEM((HEAD_DIM + ONES_ROWS, seq), BF16)],
        ),
        out_shape=jax.ShapeDtypeStruct((bsz, seq, B_W), BF16),
        compiler_params=_params(("parallel", "parallel", "arbitrary")),
        name="fox",
    )(first, qt, qa, proj, ka, proj)


SWA_BLOCKS = 4


def _swa_kernel(sink_ref, q_ref, kp_ref, kc_ref, vp_ref, vc_ref, tab_ref, o_ref):
    w = WINDOW
    dh = D_HEAD_DIM
    grp = D_Q_HEADS // D_KV_HEADS
    i = pl.program_id(1)
    nblk = q_ref.shape[0] // w
    span = (nblk + 1) * w
    kall = jnp.concatenate([kp_ref[...], kc_ref[...]], axis=0)
    vt = jnp.concatenate([_transpose_bf16(vp_ref[...]), _transpose_bf16(vc_ref[...])], axis=1)
    ones = jnp.ones((ONES_ROWS, span), BF16)
    kg = [kall[:, g * dh:(g + 1) * dh] for g in range(D_KV_HEADS)]
    vg = [jnp.concatenate([vt[g * dh:(g + 1) * dh, :], ones], axis=0) for g in range(D_KV_HEADS)]
    qt = _transpose_bf16(q_ref[...])
    key = lax.broadcasted_iota(jnp.int32, (2 * w, w), 0)
    first = key >= jnp.where(i > 0, 0, w)

    def block_logits(blk):
        return [jnp.dot(kg[h // grp][blk * w:(blk + 2) * w, :], qt[h * dh:(h + 1) * dh, blk * w:(blk + 1) * w],
                        preferred_element_type=F32) for h in range(D_Q_HEADS)]

    logits = block_logits(0)
    for blk in range(nblk):
        nxt = block_logits(blk + 1) if blk + 1 < nblk else None
        outs = []
        for h in range(D_Q_HEADS):
            s = tab_ref[h] + logits[h]
            if blk == 0:
                s = jnp.where(first, s, NEG)
            sink = sink_ref[h]
            m = jnp.maximum(jnp.max(s, axis=0, keepdims=True), sink)
            p = jnp.exp(s - m)
            acc = jnp.dot(vg[h // grp][:, blk * w:(blk + 2) * w], p.astype(BF16), preferred_element_type=F32)
            den = acc[dh:dh + 1, :] + jnp.exp(sink - m)
            outs.append(acc[:dh, :] / den)
        o_ref[blk * w:(blk + 1) * w, :] = jnp.concatenate(outs, axis=0).T.astype(o_ref.dtype)
        logits = nxt


def _swa(proj, sinks, tab):
    bsz, seq, _ = proj.shape
    w = WINDOW
    nb = SWA_BLOCKS
    assert seq % (nb * w) == 0
    qc, kc, vc = COL_DQ // D_QW, COL_DK // D_KVW, COL_DV // D_KVW
    prev = lambda i: jnp.maximum(i * nb - 1, 0)
    return pl.pallas_call(
        _swa_kernel,
        grid=(bsz, seq // (nb * w)),
        in_specs=[
            pl.BlockSpec(memory_space=pltpu.SMEM),
            pl.BlockSpec((None, nb * w, D_QW), lambda b, i: (b, i, qc)),
            pl.BlockSpec((None, w, D_KVW), lambda b, i: (b, prev(i), kc)),
            pl.BlockSpec((None, nb * w, D_KVW), lambda b, i: (b, i, kc)),
            pl.BlockSpec((None, w, D_KVW), lambda b, i: (b, prev(i), vc)),
            pl.BlockSpec((None, nb * w, D_KVW), lambda b, i: (b, i, vc)),
            pl.BlockSpec((D_Q_HEADS, 2 * w, w), lambda b, i: (0, 0, 0)),
        ],
        out_specs=pl.BlockSpec((None, nb * w, D_QW), lambda b, i: (b, i, 0)),
        out_shape=jax.ShapeDtypeStruct((bsz, seq, D_QW), BF16),
        compiler_params=_params(("parallel", "parallel")),
        name="swa",
    )(sinks, proj, proj, proj, proj, proj, tab)


CONV_T = 512
CONV_HALO = 32
CONV_ROWS = 256


def _conv_kernel(a_ref, g_ref, ha_ref, hg_ref, w_ref, b_ref, lg_ref, lb_ref, o_ref, buf):
    t = a_ref.shape[0]
    halo = CONV_HALO
    hp = ha_ref[...].astype(F32) * jax.nn.sigmoid(hg_ref[...].astype(F32))
    buf[0:halo, :] = jnp.where(pl.program_id(1) > 0, hp, 0.0)
    buf[halo:halo + t, :] = a_ref[...].astype(F32) * jax.nn.sigmoid(g_ref[...].astype(F32))
    off = halo - (CONV_WIDTH - 1)
    for r in range(t // CONV_ROWS):
        acc = jnp.broadcast_to(b_ref[...], (CONV_ROWS, CONV_CH))
        for rho in range(SUBLANES):
            rows = CONV_ROWS if rho == 0 else CONV_ROWS + SUBLANES
            part = None
            for s in range(off, off + CONV_WIDTH):
                if s % SUBLANES != rho:
                    continue
                start = r * CONV_ROWS + s - rho
                term = w_ref[s - off:s - off + 1, :] * buf[start:start + rows, :]
                part = term if part is None else part + term
            acc = acc + part[rho:rho + CONV_ROWS, :]
        mu = jnp.mean(acc, axis=-1, keepdims=True)
        d = acc - mu
        var = jnp.mean(d * d, axis=-1, keepdims=True)
        y = d * lax.rsqrt(var + LN_EPS) * lg_ref[...] + lb_ref[...]
        o_ref[r * CONV_ROWS:(r + 1) * CONV_ROWS, :] = (y * jax.nn.sigmoid(y)).astype(o_ref.dtype)


def _conv(proj, conv_w, conv_b, ln_g, ln_b):
    bsz, seq, _ = proj.shape
    t = min(CONV_T, seq)
    ca, cg = COL_CU // CONV_CH, COL_CU // CONV_CH + 1
    hb = t // CONV_HALO
    prev = lambda i: jnp.maximum(i * hb - 1, 0)
    row = lambda v: v.reshape(1, CONV_CH)
    return pl.pallas_call(
        _conv_kernel,
        grid=(bsz, seq // t),
        in_specs=[
            pl.BlockSpec((None, t, CONV_CH), lambda b, i: (b, i, ca)),
            pl.BlockSpec((None, t, CONV_CH), lambda b, i: (b, i, cg)),
            pl.BlockSpec((None, CONV_HALO, CONV_CH), lambda b, i: (b, prev(i), ca)),
            pl.BlockSpec((None, CONV_HALO, CONV_CH), lambda b, i: (b, prev(i), cg)),
            pl.BlockSpec((CONV_WIDTH, CONV_CH), lambda b, i: (0, 0)),
            pl.BlockSpec((1, CONV_CH), lambda b, i: (0, 0)),
            pl.BlockSpec((1, CONV_CH), lambda b, i: (0, 0)),
            pl.BlockSpec((1, CONV_CH), lambda b, i: (0, 0)),
        ],
        out_specs=pl.BlockSpec((None, t, CONV_CH), lambda b, i: (b, i, 0)),
        out_shape=jax.ShapeDtypeStruct((bsz, seq, CONV_CH), BF16),
        scratch_shapes=[pltpu.VMEM((CONV_HALO + t, CONV_CH), F32)],
        compiler_params=_params(("parallel", "parallel")),
        name="conv",
    )(proj, proj, proj, proj, conv_w, row(conv_b), row(ln_g), row(ln_b))


def _merge_kernel(ya_ref, yb_ref, yc_ref, yd_ref, gates_ref, x_ref, gt_ref, gp_ref, gf_ref, scf_ref, shf_ref,
                  wa_ref, wb_ref, wc_ref, wd_ref, wo_ref, o_ref, h_ref):
    d = x_ref.shape[1]
    y = None
    for i, (br, w) in enumerate(((ya_ref, wa_ref), (yb_ref, wb_ref), (yc_ref, wc_ref), (yd_ref, wd_ref))):
        t = jnp.dot(br[...], w[...], preferred_element_type=F32) * gates_ref[:, i * d:(i + 1) * d].astype(F32)
        y = t if y is None else y + t
    z = jnp.dot(y.astype(BF16), wo_ref[...], preferred_element_type=F32)
    x_new = x_ref[...] + gt_ref[...] * (_rms(z) * gp_ref[...])
    o_ref[...] = x_new
    h_ref[...] = _modulated_norm(x_new, gf_ref[...], scf_ref[...], shf_ref[...]).astype(h_ref.dtype)


def _merge(ya, yb, yc, yd, gates, x, gt, g_post, g_ffn, sc_f, sh_f, wa, wb, wc, wd, wo, layer, *, tm):
    bsz, seq, d = x.shape
    tok = lambda width: pl.BlockSpec((None, tm, width), lambda b, i: (b, i, 0))
    vec = lambda: pl.BlockSpec((None, 1, d), lambda b, i: (b, 0, 0))
    par = lambda: pl.BlockSpec((1, d), lambda b, i: (0, 0))
    full = lambda a: pl.BlockSpec((None,) + a.shape[1:], lambda b, i: (layer, 0, 0),
                                  pipeline_mode=pl.Buffered(1))
    return pl.pallas_call(
        _merge_kernel,
        grid=(bsz, seq // tm),
        in_specs=[tok(A_W), tok(B_W), tok(CONV_CH), tok(D_QW), tok(N_BRANCHES * d), tok(d),
                  vec(), par(), par(), vec(), vec(),
                  full(wa), full(wb), full(wc), full(wd), full(wo)],
        out_specs=[tok(d), tok(d)],
        out_shape=[jax.ShapeDtypeStruct((bsz, seq, d), F32), jax.ShapeDtypeStruct((bsz, seq, d), BF16)],
        compiler_params=_params(("parallel", "parallel")),
        name="merge",
    )(ya, yb, yc, yd, gates, x, gt, g_post.reshape(1, d), g_ffn.reshape(1, d), sc_f, sh_f, wa, wb, wc, wd, wo)


def _ffn_kernel(x_ref, h_ref, gt_ref, gp_ref, wgu_ref, wd_ref, o_ref, acc_scr):
    f = pl.program_id(2)
    tf = wd_ref.shape[0]

    def partial_down():
        gu = jnp.dot(h_ref[...], wgu_ref[...], preferred_element_type=F32)
        g, u = gu[:, :tf], gu[:, tf:]
        a = ((g * jax.nn.sigmoid(g)) * u).astype(BF16)
        return jnp.dot(a, wd_ref[...], preferred_element_type=F32)

    @pl.when(f == 0)
    def _():
        acc_scr[...] = partial_down()

    @pl.when(f > 0)
    def _():
        acc_scr[...] += partial_down()

    @pl.when(f == pl.num_programs(2) - 1)
    def _():
        def finish_rows(r, carry):
            rows = pl.ds(pl.multiple_of(r * NORM_ROWS, NORM_ROWS), NORM_ROWS)
            o_ref[rows, :] = x_ref[rows, :] + gt_ref[...] * (_rms(acc_scr[rows, :]) * gp_ref[...])
            return carry

        lax.fori_loop(0, o_ref.shape[0] // NORM_ROWS, finish_rows, 0, unroll=8)


def _ffn(x, h, gt, g_post, wgu, wd, layer, *, tm, tf):
    bsz, seq, d = x.shape
    dff = wd.shape[1]
    vec = lambda: pl.BlockSpec((None, 1, d), lambda b, i, f: (b, 0, 0))
    par = lambda: pl.BlockSpec((1, d), lambda b, i, f: (0, 0))
    return pl.pallas_call(
        _ffn_kernel,
        grid=(bsz, seq // tm, dff // tf),
        in_specs=[
            pl.BlockSpec((None, tm, d), lambda b, i, f: (b, i, 0)),
            pl.BlockSpec((None, tm, d), lambda b, i, f: (b, i, 0)),
            vec(), par(),
            pl.BlockSpec((None, d, 2 * tf), lambda b, i, f: (layer, 0, f)),
            pl.BlockSpec((None, tf, d), lambda b, i, f: (layer, f, 0)),
        ],
        out_specs=pl.BlockSpec((None, tm, d), lambda b, i, f: (b, i, 0)),
        out_shape=jax.ShapeDtypeStruct((bsz, seq, d), F32),
        scratch_shapes=[pltpu.VMEM((tm, d), F32)],
        compiler_params=_params(("parallel", "parallel", "arbitrary")),
        name="ffn",
    )(x, h, gt, g_post.reshape(1, d), wgu, wd)


SRC_QKV = 3 * A_W + 3 * B_W
SRC_REST = SRC_QKV + B_HEADS
SRC_GATE = SRC_REST + (PROJ_W - GATE_W - SRC_QKV)
IN_W = SRC_GATE + GATE_W
PACK_R = 256
PACK_SIDE_ROW = PROJ_W
PACK_ROWS = PROJ_W_PAD
assert GATE_W % PACK_R == 0 and SRC_QKV % PACK_R == 0 and PROJ_W % PACK_R == 0 and PROJ_W_PAD % PACK_R == 0


def _pack_tables():
    s2 = HEAD_DIM ** -0.5 * LOG2E
    src, scale, rows = [], [], []
    for b in range(PACK_ROWS // PACK_R):
        row = b * PACK_R
        if row < GATE_W:
            src.append(SRC_GATE + row); scale.append(1.0); rows.append(PACK_R)
        elif row < GATE_W + SRC_QKV:
            s = row - GATE_W
            is_q = s < A_W or 3 * A_W <= s < 3 * A_W + B_W
            src.append(s); scale.append(s2 if is_q else 1.0); rows.append(PACK_R)
        elif row < PROJ_W:
            s = row - GATE_W - SRC_QKV
            is_q = 2 * CONV_CH <= s < 2 * CONV_CH + D_QW
            src.append(SRC_REST + s); scale.append(D_HEAD_DIM ** -0.5 if is_q else 1.0); rows.append(PACK_R)
        elif row == PACK_SIDE_ROW:
            src.append(SRC_QKV); scale.append(1.0); rows.append(B_HEADS)
        else:
            src.append(0); scale.append(0.0); rows.append(0)
    return np.asarray(src, np.int32), np.asarray(scale, np.float32), np.asarray(rows, np.int32)


def _pack_kernel(src_ref, scale_ref, rows_ref, a_ref, o_ref):
    blk = pl.program_id(0)
    depth, _, d = o_ref.shape
    kt = d // LANES
    keep = lax.broadcasted_iota(jnp.int32, (PACK_R, d), 0) < rows_ref[blk]
    for layer in range(depth):
        cols = [a_ref[pl.ds(t * depth + layer, PACK_R, stride=kt * depth), :] for t in range(kt)]
        o_ref[layer] = jnp.where(keep, jnp.concatenate(cols, axis=1) * scale_ref[blk], 0.0).astype(o_ref.dtype)


def _pack_in_proj(w):
    depth, d, in_w = w.shape
    assert in_w == IN_W and d % LANES == 0
    kt = d // LANES
    view = jnp.transpose(w.reshape(depth, kt, LANES, in_w), (3, 1, 0, 2)).reshape(in_w * kt * depth, LANES)
    src, scale, rows = _pack_tables()
    per_feature = kt * depth
    return pl.pallas_call(
        _pack_kernel,
        grid_spec=pltpu.PrefetchScalarGridSpec(
            num_scalar_prefetch=3,
            grid=(PACK_ROWS // PACK_R,),
            in_specs=[pl.BlockSpec((pl.Element(PACK_R * per_feature), pl.Element(LANES)),
                                   lambda b, src, scale, rows: (src[b] * per_feature, 0))],
            out_specs=pl.BlockSpec((depth, PACK_R, d), lambda b, src, scale, rows: (0, b, 0)),
        ),
        out_shape=jax.ShapeDtypeStruct((depth, PACK_ROWS, d), BF16),
        compiler_params=_params(("parallel",)),
        name="pack_in_proj",
    )(jnp.asarray(src), jnp.asarray(scale), jnp.asarray(rows), view)


CAST_BLOCK_ELEMS = 1 << 20


def _cast_kernel(w_ref, o_ref):
    o_ref[...] = w_ref[...].astype(o_ref.dtype)


def _cast_bf16(w):
    depth, r, c = w.shape
    tr = 16
    while tr * 2 * c <= CAST_BLOCK_ELEMS and r % (tr * 2) == 0:
        tr *= 2
    assert r % tr == 0
    return pl.pallas_call(
        _cast_kernel,
        grid=(depth, r // tr),
        in_specs=[pl.BlockSpec((None, tr, c), lambda l, i: (l, i, 0))],
        out_specs=pl.BlockSpec((None, tr, c), lambda l, i: (l, i, 0)),
        out_shape=jax.ShapeDtypeStruct(w.shape, BF16),
        compiler_params=_params(("parallel", "parallel")),
        name="cast_bf16",
    )(w)


def kernel(x, c, rel_bias, w_mod, b_mod, mix_norm_pre, mix_norm_post, w_in, fox_bias, conv_w, conv_b, conv_ln_g, conv_ln_b, sinks, w_branch_a, w_branch_b, w_branch_c, w_branch_d, w_out, ffn_norm_pre, ffn_norm_post, w_ffn_gate, w_ffn_up, w_ffn_down):
    depth = w_mod.shape[0]
    bsz, seq, d = x.shape
    tm = min(1024, seq)
    tm_small = min(256, seq)
    tm_ffn = min(512, seq)

    mod = _modulation(c, w_mod, b_mod)
    tab_a = _bias_tables(rel_bias[:, :A_HEADS], _moba_buckets(), scale=LOG2E)
    tab_d = _bias_tables(rel_bias[:, A_HEADS:], _swa_buckets(), scale=1.0)[:, 0]

    wa, wb, wc, wd, wo = [_cast_bf16(w) for w in (w_branch_a, w_branch_b, w_branch_c, w_branch_d, w_out)]

    w_proj = _pack_in_proj(w_in)

    for l in range(depth):
        sh_m, sc_m, gt_m, sh_f, sc_f, gt_f = [mod[l, :, None, i * d:(i + 1) * d] for i in range(6)]

        proj, fox_raw, q_t, wgu, wdn = _norm_proj(x, mix_norm_pre[l], sc_m, sh_m, w_proj, w_ffn_gate, w_ffn_up, w_ffn_down, l,
                                             tm=tm, tn=PROJ_TN)
        fox_qa, fox_ka, fox_cum = _fox_gate(fox_raw, fox_bias[l], t=min(512, seq))

        ya = _moba(proj, q_t, tab_a)
        yb = _fox(proj, q_t, fox_qa, fox_ka, _fox_first_block(proj, q_t, fox_cum))
        yc = _conv(proj, conv_w[l], conv_b[l], conv_ln_g[l], conv_ln_b[l])
        yd = _swa(proj, sinks[l], tab_d)

        x, h_ffn = _merge(ya, yb, yc, yd, proj, x, gt_m, mix_norm_post[l], ffn_norm_pre[l], sc_f, sh_f,
                          wa, wb, wc, wd, wo, l, tm=tm_small)
        x = _ffn(x, h_ffn, gt_f, ffn_norm_post[l], wgu, wdn, 0, tm=tm_ffn, tf=FFN_TF)
    return x
```

```python
import functools
import math

import jax
import jax.numpy as jnp
import numpy as np
from jax import lax
from jax.experimental import pallas as pl
from jax.experimental.pallas import tpu as pltpu

F32 = jnp.float32
BF16 = jnp.bfloat16
HIGHEST = lax.Precision.HIGHEST

HEAD_DIM = 128
A_HEADS = 4
MOBA_BLOCK = 256
MOBA_TOPK = 3
B_HEADS = 4
CONV_CH = 512
CONV_WIDTH = 31
D_Q_HEADS = 8
D_KV_HEADS = 2
D_HEAD_DIM = 64
WINDOW = 128
N_BUCKETS = 32
MAX_DISTANCE = 1024
N_BRANCHES = 4
RMS_EPS = 1e-6
LN_EPS = 1e-5

A_W = A_HEADS * HEAD_DIM
B_W = B_HEADS * HEAD_DIM
D_QW = D_Q_HEADS * D_HEAD_DIM
D_KVW = D_KV_HEADS * D_HEAD_DIM

LANES = 128
SUBLANES = 8
VMEM_LIMIT = 56 * 1024 * 1024

NEG = -1e30
LOG2E = math.log2(math.e)
ATT_G = 4
ATT_T = MOBA_BLOCK
ATT_U = 4
FFN_TF = 512
NORM_ROWS = 32

D_MODEL = 2048
GATE_W = N_BRANCHES * D_MODEL
COL_AQ, COL_AK, COL_AV = GATE_W, GATE_W + A_W, GATE_W + 2 * A_W
COL_BQ, COL_BK, COL_BV = COL_AV + A_W, COL_AV + A_W + B_W, COL_AV + A_W + 2 * B_W
COL_CU = COL_BV + B_W
COL_DQ = COL_CU + 2 * CONV_CH
COL_DK = COL_DQ + D_QW
COL_DV = COL_DK + D_KVW
PROJ_W = COL_DV + D_KVW
PROJ_TN = 1024
PROJ_W_PAD = -(-PROJ_W // PROJ_TN) * PROJ_TN
MOBA_FAR = -(-(MAX_DISTANCE + MOBA_BLOCK - 1) // MOBA_BLOCK)
VOID_COL = LANES - 1
assert ATT_G % ATT_U == 0 and (MOBA_FAR - 1) % ATT_U == 0


def _params(sem, vmem=VMEM_LIMIT):
    return pltpu.CompilerParams(dimension_semantics=sem, vmem_limit_bytes=vmem)


def _t5_bucket(dist):
    max_exact = N_BUCKETS // 2
    d = jnp.maximum(dist, 0)
    log_ratio = jnp.log(jnp.maximum(d, 1).astype(jnp.float32) / max_exact) / math.log(MAX_DISTANCE / max_exact)
    large = max_exact + (log_ratio * (N_BUCKETS - max_exact)).astype(jnp.int32)
    large = jnp.minimum(large, N_BUCKETS - 1)
    return jnp.where(d < max_exact, d, large)


def _sigmoid(x):
    return 0.5 * jnp.tanh(0.5 * x) + 0.5


def _rms(y):
    return y * lax.rsqrt(jnp.mean(y * y, axis=-1, keepdims=True) + RMS_EPS)


def _modulated_norm(x, g, sc, sh):
    return (_rms(x) * g) * (1.0 + sc) + sh


MOD_TN = 512
MOD_KC = 256


def _mod_kernel(ct_ref, w_ref, b_ref, o_ref):
    d, nb = ct_ref.shape
    tn = w_ref.shape[1]
    ct = ct_ref[...]
    ca = ct * jax.nn.sigmoid(ct)
    rows = []
    for r in range(nb):
        acc = jnp.zeros((1, tn), F32)
        for kc in range(d // MOD_KC):
            sl = slice(kc * MOD_KC, (kc + 1) * MOD_KC)
            acc = acc + jnp.sum(w_ref[sl, :] * ca[sl, r:r + 1], axis=0, keepdims=True)
        rows.append(acc)
    o_ref[...] = jnp.concatenate(rows, axis=0) + b_ref[...]


def _modulation(c, w_mod, b_mod):
    depth, d, n = w_mod.shape
    bsz = c.shape[0]
    return pl.pallas_call(
        _mod_kernel,
        grid=(depth, n // MOD_TN),
        in_specs=[
            pl.BlockSpec((d, bsz), lambda l, j: (0, 0)),
            pl.BlockSpec((None, d, MOD_TN), lambda l, j: (l, 0, j)),
            pl.BlockSpec((None, 1, MOD_TN), lambda l, j: (l, 0, j)),
        ],
        out_specs=pl.BlockSpec((None, bsz, MOD_TN), lambda l, j: (l, 0, j)),
        out_shape=jax.ShapeDtypeStruct((depth, bsz, n), F32),
        compiler_params=_params(("parallel", "parallel")),
        name="mod",
    )(c.T, w_mod, b_mod.reshape(depth, 1, n))


def _bias_table_kernel(rb_ref, bucket_ref, o_ref, *, scale):
    h = pl.program_id(0)
    b = bucket_ref[...]
    acc = jnp.where(b < 0, NEG, 0.0).astype(F32)
    for u in range(N_BUCKETS):
        acc = jnp.where(b == u, rb_ref[h, u] * scale, acc)
    o_ref[...] = acc


def _bias_tables(rel_bias_heads, bucket, *, scale):
    nh = rel_bias_heads.shape[1]
    nt, r, c = bucket.shape
    return pl.pallas_call(
        functools.partial(_bias_table_kernel, scale=scale),
        grid=(nh, nt),
        in_specs=[
            pl.BlockSpec(memory_space=pltpu.SMEM),
            pl.BlockSpec((None, r, c), lambda h, t: (t, 0, 0)),
        ],
        out_specs=pl.BlockSpec((None, None, r, c), lambda h, t: (h, t, 0, 0)),
        out_shape=jax.ShapeDtypeStruct((nh, nt, r, c), F32),
        compiler_params=_params(("parallel", "parallel")),
        name="bias_table",
    )(rel_bias_heads.T, bucket)


def _moba_buckets():
    blk = MOBA_BLOCK
    i = jnp.arange(blk)[None, :]
    j = jnp.arange(blk)[:, None]
    tabs = []
    for delta in range(MOBA_FAR):
        dist = delta * blk + i - j
        tabs.append(jnp.where(dist >= 0, _t5_bucket(dist), -1))
    tabs.append(_t5_bucket(jnp.full((blk, blk), MAX_DISTANCE, jnp.int32)))
    return jnp.stack(tabs).astype(jnp.int32)


def _swa_buckets():
    qi = jnp.arange(WINDOW)[None, :]
    kj = jnp.arange(2 * WINDOW)[:, None]
    dist = qi + WINDOW - kj
    in_win = (dist >= 0) & (dist < WINDOW)
    return jnp.where(in_win, _t5_bucket(dist), -1).astype(jnp.int32)[None]


def _norm_proj_kernel(x_ref, g_ref, sc_ref, sh_ref, w_ref, wg_ref, wu_ref, wdn_ref,
                      o_ref, os_ref, qt_ref, wgu_out, wdn_out, h_scr):
    j = pl.program_id(2)
    tn = w_ref.shape[0]
    last = pl.num_programs(2) - 1

    def cast_ffn_weights():
        tf = FFN_TF
        for f in range(wg_ref.shape[1] // tf):
            wgu_out[:, 2 * f * tf:(2 * f + 1) * tf] = wg_ref[:, f * tf:(f + 1) * tf].astype(BF16)
            wgu_out[:, (2 * f + 1) * tf:(2 * f + 2) * tf] = wu_ref[:, f * tf:(f + 1) * tf].astype(BF16)
        wdn_out[...] = wdn_ref[...].astype(BF16)

    @pl.when(j == 0)
    def _():
        gain = g_ref[...] * (1.0 + sc_ref[...])

        def norm_rows(r, carry):
            rows = pl.ds(pl.multiple_of(r * NORM_ROWS, NORM_ROWS), NORM_ROWS)
            h_scr[rows, :] = (_rms(x_ref[rows, :]) * gain + sh_ref[...]).astype(BF16)
            return carry

        lax.fori_loop(0, x_ref.shape[0] // NORM_ROWS, norm_rows, 0, unroll=8)

    @pl.when(j < GATE_W // tn)
    def _():
        o_ref[...] = _sigmoid(_dot_nt(h_scr[...], w_ref[...])).astype(o_ref.dtype)
        cast_ffn_weights()

    j_aq, j_bq = COL_AQ // tn, COL_BQ // tn

    @pl.when((j >= GATE_W // tn) & (j < last) & (j != j_aq) & (j != j_bq))
    def _():
        o_ref[...] = _dot_nt(h_scr[...], w_ref[...]).astype(o_ref.dtype)
        cast_ffn_weights()

    def query_tile(col):
        q0 = col % tn
        h = h_scr[...]
        qt_ref[...] = _dot_nt(w_ref[q0:q0 + qt_ref.shape[0], :], h).astype(qt_ref.dtype)
        rest = slice(0, q0) if q0 else slice(qt_ref.shape[0], tn)
        o_ref[:, rest] = _dot_nt(h, w_ref[rest, :]).astype(o_ref.dtype)
        o_ref[:, q0:q0 + qt_ref.shape[0]] = jnp.zeros((o_ref.shape[0], qt_ref.shape[0]), o_ref.dtype)
        cast_ffn_weights()

    @pl.when(j == j_aq)
    def _():
        query_tile(COL_AQ)

    @pl.when(j == j_bq)
    def _():
        query_tile(COL_BQ)

    @pl.when(j == last)
    def _():
        acc = _dot_nt(h_scr[...], w_ref[...])
        o_ref[...] = acc.astype(o_ref.dtype)
        col = PACK_SIDE_ROW % tn
        os_ref[...] = acc[:, col:col + os_ref.shape[1]]
        cast_ffn_weights()


def _rows_per_step(rows, steps, tile):
    r = tile
    while rows % r or rows // r > steps:
        r += tile
    return r


def _norm_proj(x, g, sc, sh, w, w_gate, w_up, w_down, layer, *, tm, tn):
    bsz, seq, d = x.shape
    dff = w_down.shape[1]
    n = PROJ_W_PAD
    ns = LANES
    ni, nj = seq // tm, n // tn
    assert GATE_W % tn == 0 and n % tn == 0 and w.shape[1] == PACK_ROWS and dff % FFN_TF == 0
    assert PACK_SIDE_ROW // tn == nj - 1 and GATE_W // tn < nj - 1 and PACK_SIDE_ROW % ns == 0
    j_aq, j_bq = COL_AQ // tn, COL_BQ // tn
    assert A_W == B_W and GATE_W // tn <= j_aq < j_bq < nj - 1 and tn == 2 * A_W
    assert COL_AQ % tn in (0, A_W) and COL_BQ % tn in (0, A_W)
    r_gu = _rows_per_step(d, bsz * ni * nj, 2 * SUBLANES)
    r_dn = _rows_per_step(dff, bsz * ni * nj, 2 * SUBLANES)
    step = lambda b, i, j: (b * ni + i) * nj + j
    gu_blk = lambda b, i, j: (layer, jnp.minimum(step(b, i, j), d // r_gu - 1), 0)
    dn_blk = lambda b, i, j: (layer, jnp.minimum(step(b, i, j), dff // r_dn - 1), 0)
    first = lambda idx: (lambda b, i, j: (0,) + idx(b, i, j)[1:])
    return pl.pallas_call(
        _norm_proj_kernel,
        grid=(bsz, ni, nj),
        in_specs=[
            pl.BlockSpec((None, tm, d), lambda b, i, j: (b, i, 0)),
            pl.BlockSpec((1, d), lambda b, i, j: (0, 0)),
            pl.BlockSpec((None, 1, d), lambda b, i, j: (b, 0, 0)),
            pl.BlockSpec((None, 1, d), lambda b, i, j: (b, 0, 0)),
            pl.BlockSpec((None, tn, d), lambda b, i, j: (layer, j, 0)),
            pl.BlockSpec((None, r_gu, dff), gu_blk),
            pl.BlockSpec((None, r_gu, dff), gu_blk),
            pl.BlockSpec((None, r_dn, d), dn_blk),
        ],
        out_specs=[pl.BlockSpec((None, tm, tn), lambda b, i, j: (b, i, j)),
                   pl.BlockSpec((None, tm, ns), lambda b, i, j: (b, i, 0)),
                   pl.BlockSpec((None, A_W, tm), lambda b, i, j: (b, jnp.where(j <= j_aq, 0, 1), i)),
                   pl.BlockSpec((None, r_gu, 2 * dff), first(gu_blk)),
                   pl.BlockSpec((None, r_dn, d), first(dn_blk))],
        out_shape=[jax.ShapeDtypeStruct((bsz, seq, n), BF16), jax.ShapeDtypeStruct((bsz, seq, ns), F32),
                   jax.ShapeDtypeStruct((bsz, A_W + B_W, seq), BF16),
                   jax.ShapeDtypeStruct((1, d, 2 * dff), BF16), jax.ShapeDtypeStruct((1, dff, d), BF16)],
        scratch_shapes=[pltpu.VMEM((tm, d), BF16)],
        compiler_params=_params(("arbitrary", "arbitrary", "arbitrary")),
        name="norm_proj",
    )(x, g.reshape(1, d), sc, sh, w, w_gate, w_up, w_down)


def _split3(c):
    hi = c.astype(BF16).astype(F32)
    mid = (c - hi).astype(BF16).astype(F32)
    lo = (c - hi - mid).astype(BF16).astype(F32)
    return hi, mid, lo


def _fox_gate_kernel(x_ref, fb_ref, qa_ref, ka_ref, cum_ref, carry_scr):
    t = x_ref.shape[0]

    @pl.when(pl.program_id(1) == 0)
    def _():
        carry_scr[...] = jnp.zeros_like(carry_scr)

    lf = jax.nn.log_sigmoid(x_ref[...] + fb_ref[...])
    tri = jnp.where(lax.broadcasted_iota(jnp.int32, (t, t), 0) >= lax.broadcasted_iota(jnp.int32, (t, t), 1),
                    1.0, 0.0).astype(BF16)
    c = carry_scr[...]
    for piece in _split3(lf):
        c = c + jnp.dot(tri, piece.astype(BF16), preferred_element_type=F32)
    carry_scr[...] = c[t - 1:t, :]
    cum_ref[...] = c * LOG2E
    lane = lax.broadcasted_iota(jnp.int32, (t, LANES), 1)
    sub = lax.broadcasted_iota(jnp.int32, (SUBLANES, t), 0)
    ct = c.T * LOG2E
    for h in range(B_HEADS):
        hi, mid, lo = _split3(jnp.broadcast_to(c[:, h:h + 1], (t, LANES)) * LOG2E)
        ka = jnp.where(lane == 0, -hi, jnp.where(lane == 1, -mid, jnp.where(lane == 2, -lo, jnp.where(lane < 6, 1.0, 0.0))))
        ka_ref[h] = ka.astype(BF16)
        hi, mid, lo = _split3(jnp.broadcast_to(ct[h:h + 1, :], (SUBLANES, t)))
        qa = jnp.where(sub < 3, 1.0, jnp.where(sub == 3, hi, jnp.where(sub == 4, mid, jnp.where(sub == 5, lo, 0.0))))
        qa_ref[h] = jnp.concatenate([qa, jnp.zeros((LANES - SUBLANES, t), F32)], axis=0).astype(BF16)


def _fox_gate(raw, fox_b, *, t):
    bsz, seq, _ = raw.shape
    fb = jnp.zeros((1, LANES), F32).at[0, :B_HEADS].set(fox_b)
    aug = pl.BlockSpec((None, B_HEADS, t, LANES), lambda b, i: (b, 0, i, 0))
    aug_t = pl.BlockSpec((None, B_HEADS, LANES, t), lambda b, i: (b, 0, 0, i))
    return pl.pallas_call(
        _fox_gate_kernel,
        grid=(bsz, seq // t),
        in_specs=[
            pl.BlockSpec((None, t, LANES), lambda b, i: (b, i, 0)),
            pl.BlockSpec((1, LANES), lambda b, i: (0, 0)),
        ],
        out_specs=[aug_t, aug, pl.BlockSpec((None, t, LANES), lambda b, i: (b, i, 0))],
        out_shape=[jax.ShapeDtypeStruct((bsz, B_HEADS, LANES, seq), BF16),
                   jax.ShapeDtypeStruct((bsz, B_HEADS, seq, LANES), BF16),
                   jax.ShapeDtypeStruct((bsz, seq, LANES), F32)],
        scratch_shapes=[pltpu.VMEM((1, LANES), F32)],
        compiler_params=_params(("parallel", "arbitrary")),
        name="fox_gate",
    )(raw, fb)


ONES_ROWS = 16


def _softmax_step(zt, vt, carry):
    m, acc = carry
    m_new = jnp.maximum(m, jnp.max(zt, axis=0, keepdims=True))
    p = jnp.exp2(zt - m_new)
    acc = jnp.exp2(m - m_new) * acc + jnp.dot(vt, p.astype(BF16), preferred_element_type=F32)
    return m_new, acc


def _softmax_init(tq, dh):
    return (jnp.full((1, tq), NEG, F32), jnp.zeros((dh + ONES_ROWS, tq), F32))


def _softmax_finish(carry, dh):
    _, acc = carry
    return (acc[:dh, :] / acc[dh:dh + 1, :]).T


def _fill_vt(vt_scr, v_ref, chunk):
    seq, dh = v_ref.shape
    for cix in range(seq // chunk):
        sl = slice(cix * chunk, (cix + 1) * chunk)
        vt_scr[:dh, sl] = _transpose_bf16(v_ref[sl, :])
    vt_scr[dh:, :] = jnp.ones((ONES_ROWS, seq), BF16)


ATT_LOOKAHEAD = 6


def _run_substeps(steps, carry):
    carry = list(carry)
    pending = {}
    for s in range(min(ATT_LOOKAHEAD, len(steps))):
        pending[s] = steps[s][1]()
    for s, (j, _, fix, vt) in enumerate(steps):
        if s + ATT_LOOKAHEAD < len(steps):
            pending[s + ATT_LOOKAHEAD] = steps[s + ATT_LOOKAHEAD][1]()
        z, cj = pending.pop(s), carry[j]
        if fix is not None:
            z, cj = fix(z, cj)
        carry[j] = _softmax_step(z, vt, cj)
    return carry


def _dense_loop(start, n_blocks, make_steps, carry):
    big = 2 * ATT_U
    carry = lax.fori_loop(0, n_blocks // big,
                          lambda grp, c: tuple(_run_substeps(make_steps(start + grp * big, big), c)), carry)
    first = start + (n_blocks // big) * big
    return lax.fori_loop(0, (start + n_blocks - first) // ATT_U,
                         lambda grp, c: tuple(_run_substeps(make_steps(first + grp * ATT_U, ATT_U), c)), carry)


def _transpose_bf16(x):
    return x.astype(F32).T.astype(BF16)


_NT = (((1,), (1,)), ((), ()))


def _dot_nt(a, b):
    return lax.dot_general(a, b, _NT, preferred_element_type=F32)


def _moba_kernel(cfar_ref, q_ref, k_ref, v_ref, tab_ref, o_ref, kmean_scr, qt_scr, vt_scr):
    g, t = ATT_G, ATT_T
    seq, dh = k_ref.shape
    nblk = seq // t
    rows = g * t
    i = pl.program_id(2)
    base = i * g

    @pl.when(i == 0)
    def _():
        r = lax.broadcasted_iota(jnp.int32, (nblk, seq), 0)
        c = lax.broadcasted_iota(jnp.int32, (nblk, seq), 1)
        avg = jnp.where((c >= r * t) & (c < (r + 1) * t), 1.0 / t, 0.0).astype(BF16)
        kmean_scr[...] = jnp.dot(avg, k_ref[...], preferred_element_type=F32)
        _fill_vt(vt_scr, v_ref, rows)

    qt = q_ref[...]
    qt_scr[:dh, :] = qt
    sc = jnp.dot(kmean_scr[...], qt.astype(F32), precision=HIGHEST, preferred_element_type=F32)
    row = lax.broadcasted_iota(jnp.int32, (nblk, rows), 0)
    own = base + lax.broadcasted_iota(jnp.int32, (nblk, rows), 1) // t
    past = row < own
    sc = jnp.where(past, sc, -jnp.inf)
    rowf = row.astype(F32)
    picked = jnp.zeros((nblk, rows), F32)
    for _ in range(MOBA_TOPK):
        best = jnp.max(sc, axis=0, keepdims=True)
        first = jnp.min(jnp.where(sc == best, rowf, float(nblk)), axis=0, keepdims=True)
        pick = rowf == first
        picked = jnp.where(pick, 1.0, picked)
        sc = jnp.where(pick, -jnp.inf, sc)
    pen = jnp.where(past, jnp.where(picked > 0.0, 0.0, NEG), jnp.where(row == own, 0.0, NEG))
    void = jnp.where(lax.broadcasted_iota(jnp.int32, (SUBLANES, rows), 0) == SUBLANES - 1, NEG, 0.0)
    pen = jnp.concatenate([pen, jnp.zeros((LANES - SUBLANES - nblk, rows), F32), void], axis=0)
    qt_scr[dh:, :] = pen.astype(BF16)

    lane = lax.broadcasted_iota(jnp.int32, (t, LANES), 1)

    def kv(n, col):
        start = pl.multiple_of(n * t, t)
        onehot = jnp.where(lane == col, 1.0, 0.0).astype(BF16)
        return jnp.concatenate([k_ref[pl.ds(start, t), :], onehot], axis=1), vt_scr[:, pl.ds(start, t)]

    def logits(kb, j):
        return jnp.dot(kb, qt_scr[:, j * t:(j + 1) * t], preferred_element_type=F32)

    def far_steps(first, count):
        steps = []
        for u in range(count):
            kb, vb = kv(first + u, first + u)
            steps += [(j, functools.partial(logits, kb, j), None, vb) for j in range(g)]
        return steps

    carry = _dense_loop(0, jnp.maximum(base - (MOBA_FAR - 1), 0), far_steps,
                        tuple(_softmax_init(t, dh) for _ in range(g)))
    cfar = cfar_ref[pl.program_id(1)]

    def near_fix(delta, z, cj):
        if delta == MOBA_FAR - 1:
            m, acc = cj
            cj = (m + cfar, acc)
        return tab_ref[delta] + z, cj

    steps = []
    for k in range(-(MOBA_FAR - 1), g):
        n = base + k
        if k < 0:
            kb, vb = kv(jnp.maximum(n, 0), jnp.where(n >= 0, n, VOID_COL))
        else:
            kb, vb = kv(n, n)
        for j in range(max(k, 0), g):
            fix = functools.partial(near_fix, j - k) if j - k < MOBA_FAR else None
            steps.append((j, functools.partial(logits, kb, j), fix, vb))
    carry = _run_substeps(steps, carry)
    for j in range(g):
        o_ref[j * t:(j + 1) * t, :] = _softmax_finish(carry[j], dh).astype(o_ref.dtype)


def _moba(proj, qt, tab):
    bsz, seq, _ = proj.shape
    g, t = ATT_G, ATT_T
    nblk = seq // t
    assert seq % (g * t) == 0 and nblk <= LANES - SUBLANES and nblk % SUBLANES == 0
    kc, vc = COL_AK // HEAD_DIM, COL_AV // HEAD_DIM
    cfar = tab[:, MOBA_FAR, 0, 0]
    return pl.pallas_call(
        _moba_kernel,
        grid=(bsz, A_HEADS, nblk // g),
        in_specs=[
            pl.BlockSpec(memory_space=pltpu.SMEM),
            pl.BlockSpec((None, HEAD_DIM, g * t), lambda b, h, i: (b, h, i)),
            pl.BlockSpec((None, seq, HEAD_DIM), lambda b, h, i: (b, 0, kc + h)),
            pl.BlockSpec((None, seq, HEAD_DIM), lambda b, h, i: (b, 0, vc + h)),
            pl.BlockSpec((None, MOBA_FAR, t, t), lambda b, h, i: (h, 0, 0, 0)),
        ],
        out_specs=pl.BlockSpec((None, g * t, HEAD_DIM), lambda b, h, i: (b, i, h)),
        out_shape=jax.ShapeDtypeStruct((bsz, seq, A_W), BF16),
        scratch_shapes=[pltpu.VMEM((nblk, HEAD_DIM), F32),
                        pltpu.VMEM((HEAD_DIM + LANES, g * t), BF16), pltpu.VMEM((HEAD_DIM + ONES_ROWS, seq), BF16)],
        compiler_params=_params(("parallel", "parallel", "arbitrary")),
        name="moba",
    )(cfar, qt, proj, proj, tab)


def _fox_kernel(start_ref, q_ref, qa_ref, k_ref, ka_ref, v_ref, o_ref, qt_scr, vt_scr):
    g, t = ATT_G, ATT_T
    rows = g * t
    seq, dh = k_ref.shape
    base = pl.program_id(2) * g
    first = start_ref[(pl.program_id(0) * pl.num_programs(1) + pl.program_id(1)) * pl.num_programs(2)
                      + pl.program_id(2)]

    @pl.when(base == 0)
    def _():
        _fill_vt(vt_scr, v_ref, rows)

    qt_scr[:dh, :] = q_ref[...]
    qt_scr[dh:, :] = qa_ref[...]

    def kv(n):
        start = pl.multiple_of(n * t, t)
        return (jnp.concatenate([k_ref[pl.ds(start, t), :], ka_ref[pl.ds(start, t), :]], axis=1),
                vt_scr[:, pl.ds(start, t)])

    def logits(kb, j):
        return jnp.dot(kb, qt_scr[:, j * t:(j + 1) * t], preferred_element_type=F32)

    def dense_steps(first, count):
        steps = []
        for u in range(count):
            kb, vb = kv(first + u)
            steps += [(j, functools.partial(logits, kb, j), None, vb) for j in range(g)]
        return steps

    carry = _dense_loop(first, base - first, dense_steps, tuple(_softmax_init(t, dh) for _ in range(g)))
    causal = lax.broadcasted_iota(jnp.int32, (t, t), 0) <= lax.broadcasted_iota(jnp.int32, (t, t), 1)

    def diag_fix(z, cj):
        return jnp.where(causal, z, NEG), cj

    steps = []
    for k in range(g):
        kb, vb = kv(base + k)
        steps += [(j, functools.partial(logits, kb, j), diag_fix if j == k else None, vb) for j in range(k, g)]
    carry = _run_substeps(steps, carry)
    for j in range(g):
        o_ref[j * t:(j + 1) * t, :] = _softmax_finish(carry[j], dh).astype(o_ref.dtype)


FOX_SKIP_BITS = 200.0
FOX_NORM_SLACK = 1.0 + 2.0 ** -6


def _fox_bounds_kernel(q_ref, k_ref, cum_ref, o_ref, krun_scr, kpref_scr, clast_scr):
    i = pl.program_id(1)
    rows, width = k_ref.shape
    g = rows // ATT_T

    @pl.when(i == 0)
    def _():
        krun_scr[...] = jnp.zeros_like(krun_scr)
        kpref_scr[...] = jnp.zeros_like(kpref_scr)
        clast_scr[...] = jnp.zeros_like(clast_scr)

    head_of = lax.broadcasted_iota(jnp.int32, (width, LANES), 0) // HEAD_DIM
    head_sum = jnp.where(head_of == lax.broadcasted_iota(jnp.int32, (width, LANES), 1), 1.0, 0.0).astype(BF16)

    def max_norm(ref):
        x = ref[...].astype(F32)
        n2 = jnp.dot((x * x).astype(BF16), head_sum, preferred_element_type=F32)
        return jnp.sqrt(jnp.max(n2, axis=0, keepdims=True)) * FOX_NORM_SLACK

    def max_norm_t(ref):
        lane = lax.broadcasted_iota(jnp.int32, (1, LANES), 1)
        out = jnp.zeros((1, LANES), F32)
        for h in range(width // HEAD_DIM):
            x = ref[h * HEAD_DIM:(h + 1) * HEAD_DIM, :].astype(F32)
            n2 = jnp.max(jnp.sum(x * x, axis=0, keepdims=True), axis=1, keepdims=True)
            out = jnp.where(lane == h, jnp.sqrt(n2) * FOX_NORM_SLACK, out)
        return out

    qn, kn = max_norm_t(q_ref), max_norm(k_ref)
    c_first, c_last = cum_ref[0:1, :], cum_ref[rows - 1:rows, :]
    tile = lax.broadcasted_iota(jnp.int32, kpref_scr.shape, 0)
    ub = qn * kpref_scr[...] + (c_first - clast_scr[...]) + qn * kn
    ok = (ub < -FOX_SKIP_BITS) & (tile < i)
    o_ref[...] = jnp.max(jnp.where(ok, (tile + 1) * g, 0), axis=0, keepdims=True)
    krun = jnp.maximum(krun_scr[...], kn)
    krun_scr[...] = krun
    kpref_scr[pl.ds(i, 1), :] = krun
    clast_scr[pl.ds(i, 1), :] = c_last


def _fox_first_block(proj, qt, cum2):
    bsz, seq, _ = proj.shape
    rows = ATT_G * ATT_T
    ntile = seq // rows
    hist = pltpu.VMEM((-(-ntile // SUBLANES) * SUBLANES, LANES), F32)
    first = pl.pallas_call(
        _fox_bounds_kernel,
        grid=(bsz, ntile),
        in_specs=[
            pl.BlockSpec((None, B_W, rows), lambda b, i: (b, A_W // B_W, i)),
            pl.BlockSpec((None, rows, B_W), lambda b, i: (b, i, COL_BK // B_W)),
            pl.BlockSpec((None, rows, LANES), lambda b, i: (b, i, 0)),
        ],
        out_specs=pl.BlockSpec((None, None, 1, LANES), lambda b, i: (b, i, 0, 0)),
        out_shape=jax.ShapeDtypeStruct((bsz, ntile, 1, LANES), jnp.int32),
        scratch_shapes=[pltpu.VMEM((1, LANES), F32), hist, hist],
        compiler_params=_params(("parallel", "arbitrary")),
        name="fox_bounds",
    )(qt, proj, cum2)
    return first[:, :, 0, :B_HEADS].transpose(0, 2, 1).reshape(-1)


def _fox(proj, qt, qa, ka, first):
    bsz, seq, _ = proj.shape
    g, t = ATT_G, ATT_T
    assert seq % (g * t) == 0
    kc, vc = COL_BK // HEAD_DIM, COL_BV // HEAD_DIM
    return pl.pallas_call(
        _fox_kernel,
        grid_spec=pltpu.PrefetchScalarGridSpec(
            num_scalar_prefetch=1,
            grid=(bsz, B_HEADS, seq // (g * t)),
            in_specs=[
                pl.BlockSpec((None, HEAD_DIM, g * t), lambda b, h, i, s: (b, A_HEADS + h, i)),
                pl.BlockSpec((None, None, LANES, g * t), lambda b, h, i, s: (b, h, 0, i)),
                pl.BlockSpec((None, seq, HEAD_DIM), lambda b, h, i, s: (b, 0, kc + h)),
                pl.BlockSpec((None, None, seq, LANES), lambda b, h, i, s: (b, h, 0, 0)),
                pl.BlockSpec((None, seq, HEAD_DIM), lambda b, h, i, s: (b, 0, vc + h)),
            ],
            out_specs=pl.BlockSpec((None, g * t, HEAD_DIM), lambda b, h, i, s: (b, i, h)),
            scratch_shapes=[pltpu.VMEM((HEAD_DIM + LANES, g * t), BF16),
                            pltpu.VMEM((HEAD_DIM + ONES_ROWS, seq), BF16)],
        ),
        out_shape=jax.ShapeDtypeStruct((bsz, seq, B_W), BF16),
        compiler_params=_params(("parallel", "parallel", "arbitrary")),
        name="fox",
    )(first, qt, qa, proj, ka, proj)


SWA_BLOCKS = 4


def _swa_kernel(sink_ref, q_ref, kp_ref, kc_ref, vp_ref, vc_ref, tab_ref, o_ref):
    w = WINDOW
    dh = D_HEAD_DIM
    grp = D_Q_HEADS // D_KV_HEADS
    i = pl.program_id(1)
    nblk = q_ref.shape[0] // w
    span = (nblk + 1) * w
    kall = jnp.concatenate([kp_ref[...], kc_ref[...]], axis=0)
    vt = jnp.concatenate([_transpose_bf16(vp_ref[...]), _transpose_bf16(vc_ref[...])], axis=1)
    ones = jnp.ones((ONES_ROWS, span), BF16)
    kg = [kall[:, g * dh:(g + 1) * dh] for g in range(D_KV_HEADS)]
    vg = [jnp.concatenate([vt[g * dh:(g + 1) * dh, :], ones], axis=0) for g in range(D_KV_HEADS)]
    qt = _transpose_bf16(q_ref[...])
    key = lax.broadcasted_iota(jnp.int32, (2 * w, w), 0)
    first = key >= jnp.where(i > 0, 0, w)

    def block_logits(blk):
        return [jnp.dot(kg[h // grp][blk * w:(blk + 2) * w, :], qt[h * dh:(h + 1) * dh, blk * w:(blk + 1) * w],
                        preferred_element_type=F32) for h in range(D_Q_HEADS)]

    logits = block_logits(0)
    for blk in range(nblk):
        nxt = block_logits(blk + 1) if blk + 1 < nblk else None
        outs = []
        for h in range(D_Q_HEADS):
            s = tab_ref[h] + logits[h]
            if blk == 0:
                s = jnp.where(first, s, NEG)
            sink = sink_ref[h]
            m = jnp.maximum(jnp.max(s, axis=0, keepdims=True), sink)
            p = jnp.exp(s - m)
            acc = jnp.dot(vg[h // grp][:, blk * w:(blk + 2) * w], p.astype(BF16), preferred_element_type=F32)
            den = acc[dh:dh + 1, :] + jnp.exp(sink - m)
            outs.append(acc[:dh, :] / den)
        o_ref[blk * w:(blk + 1) * w, :] = jnp.concatenate(outs, axis=0).T.astype(o_ref.dtype)
        logits = nxt


def _swa(proj, sinks, tab):
    bsz, seq, _ = proj.shape
    w = WINDOW
    nb = SWA_BLOCKS
    assert seq % (nb * w) == 0
    qc, kc, vc = COL_DQ // D_QW, COL_DK // D_KVW, COL_DV // D_KVW
    prev = lambda i: jnp.maximum(i * nb - 1, 0)
    return pl.pallas_call(
        _swa_kernel,
        grid=(bsz, seq // (nb * w)),
        in_specs=[
            pl.BlockSpec(memory_space=pltpu.SMEM),
            pl.BlockSpec((None, nb * w, D_QW), lambda b, i: (b, i, qc)),
            pl.BlockSpec((None, w, D_KVW), lambda b, i: (b, prev(i), kc)),
            pl.BlockSpec((None, nb * w, D_KVW), lambda b, i: (b, i, kc)),
            pl.BlockSpec((None, w, D_KVW), lambda b, i: (b, prev(i), vc)),
            pl.BlockSpec((None, nb * w, D_KVW), lambda b, i: (b, i, vc)),
            pl.BlockSpec((D_Q_HEADS, 2 * w, w), lambda b, i: (0, 0, 0)),
        ],
        out_specs=pl.BlockSpec((None, nb * w, D_QW), lambda b, i: (b, i, 0)),
        out_shape=jax.ShapeDtypeStruct((bsz, seq, D_QW), BF16),
        compiler_params=_params(("parallel", "parallel")),
        name="swa",
    )(sinks, proj, proj, proj, proj, proj, tab)


CONV_T = 512
CONV_HALO = 32
CONV_ROWS = 256


def _conv_kernel(a_ref, g_ref, ha_ref, hg_ref, w_ref, b_ref, lg_ref, lb_ref, o_ref, buf):
    t = a_ref.shape[0]
    halo = CONV_HALO
    hp = ha_ref[...].astype(F32) * jax.nn.sigmoid(hg_ref[...].astype(F32))
    buf[0:halo, :] = jnp.where(pl.program_id(1) > 0, hp, 0.0)
    buf[halo:halo + t, :] = a_ref[...].astype(F32) * jax.nn.sigmoid(g_ref[...].astype(F32))
    off = halo - (CONV_WIDTH - 1)
    for r in range(t // CONV_ROWS):
        acc = jnp.broadcast_to(b_ref[...], (CONV_ROWS, CONV_CH))
        for rho in range(SUBLANES):
            rows = CONV_ROWS if rho == 0 else CONV_ROWS + SUBLANES
            part = None
            for s in range(off, off + CONV_WIDTH):
                if s % SUBLANES != rho:
                    continue
                start = r * CONV_ROWS + s - rho
                term = w_ref[s - off:s - off + 1, :] * buf[start:start + rows, :]
                part = term if part is None else part + term
            acc = acc + part[rho:rho + CONV_ROWS, :]
        mu = jnp.mean(acc, axis=-1, keepdims=True)
        d = acc - mu
        var = jnp.mean(d * d, axis=-1, keepdims=True)
        y = d * lax.rsqrt(var + LN_EPS) * lg_ref[...] + lb_ref[...]
        o_ref[r * CONV_ROWS:(r + 1) * CONV_ROWS, :] = (y * jax.nn.sigmoid(y)).astype(o_ref.dtype)


def _conv(proj, conv_w, conv_b, ln_g, ln_b):
    bsz, seq, _ = proj.shape
    t = min(CONV_T, seq)
    ca, cg = COL_CU // CONV_CH, COL_CU // CONV_CH + 1
    hb = t // CONV_HALO
    prev = lambda i: jnp.maximum(i * hb - 1, 0)
    row = lambda v: v.reshape(1, CONV_CH)
    return pl.pallas_call(
        _conv_kernel,
        grid=(bsz, seq // t),
        in_specs=[
            pl.BlockSpec((None, t, CONV_CH), lambda b, i: (b, i, ca)),
            pl.BlockSpec((None, t, CONV_CH), lambda b, i: (b, i, cg)),
            pl.BlockSpec((None, CONV_HALO, CONV_CH), lambda b, i: (b, prev(i), ca)),
            pl.BlockSpec((None, CONV_HALO, CONV_CH), lambda b, i: (b, prev(i), cg)),
            pl.BlockSpec((CONV_WIDTH, CONV_CH), lambda b, i: (0, 0)),
            pl.BlockSpec((1, CONV_CH), lambda b, i: (0, 0)),
            pl.BlockSpec((1, CONV_CH), lambda b, i: (0, 0)),
            pl.BlockSpec((1, CONV_CH), lambda b, i: (0, 0)),
        ],
        out_specs=pl.BlockSpec((None, t, CONV_CH), lambda b, i: (b, i, 0)),
        out_shape=jax.ShapeDtypeStruct((bsz, seq, CONV_CH), BF16),
        scratch_shapes=[pltpu.VMEM((CONV_HALO + t, CONV_CH), F32)],
        compiler_params=_params(("parallel", "parallel")),
        name="conv",
    )(proj, proj, proj, proj, conv_w, row(conv_b), row(ln_g), row(ln_b))


def _merge_kernel(ya_ref, yb_ref, yc_ref, yd_ref, gates_ref, x_ref, gt_ref, gp_ref, gf_ref, scf_ref, shf_ref,
                  wa_ref, wb_ref, wc_ref, wd_ref, wo_ref, o_ref, h_ref):
    d = x_ref.shape[1]
    y = None
    for i, (br, w) in enumerate(((ya_ref, wa_ref), (yb_ref, wb_ref), (yc_ref, wc_ref), (yd_ref, wd_ref))):
        t = jnp.dot(br[...], w[...], preferred_element_type=F32) * gates_ref[:, i * d:(i + 1) * d].astype(F32)
        y = t if y is None else y + t
    z = jnp.dot(y.astype(BF16), wo_ref[...], preferred_element_type=F32)
    x_new = x_ref[...] + gt_ref[...] * (_rms(z) * gp_ref[...])
    o_ref[...] = x_new
    h_ref[...] = _modulated_norm(x_new, gf_ref[...], scf_ref[...], shf_ref[...]).astype(h_ref.dtype)


def _merge(ya, yb, yc, yd, gates, x, gt, g_post, g_ffn, sc_f, sh_f, wa, wb, wc, wd, wo, layer, *, tm):
    bsz, seq, d = x.shape
    tok = lambda width: pl.BlockSpec((None, tm, width), lambda b, i: (b, i, 0))
    vec = lambda: pl.BlockSpec((None, 1, d), lambda b, i: (b, 0, 0))
    par = lambda: pl.BlockSpec((1, d), lambda b, i: (0, 0))
    full = lambda a: pl.BlockSpec((None,) + a.shape[1:], lambda b, i: (layer, 0, 0),
                                  pipeline_mode=pl.Buffered(1))
    return pl.pallas_call(
        _merge_kernel,
        grid=(bsz, seq // tm),
        in_specs=[tok(A_W), tok(B_W), tok(CONV_CH), tok(D_QW), tok(N_BRANCHES * d), tok(d),
                  vec(), par(), par(), vec(), vec(),
                  full(wa), full(wb), full(wc), full(wd), full(wo)],
        out_specs=[tok(d), tok(d)],
        out_shape=[jax.ShapeDtypeStruct((bsz, seq, d), F32), jax.ShapeDtypeStruct((bsz, seq, d), BF16)],
        compiler_params=_params(("parallel", "parallel")),
        name="merge",
    )(ya, yb, yc, yd, gates, x, gt, g_post.reshape(1, d), g_ffn.reshape(1, d), sc_f, sh_f, wa, wb, wc, wd, wo)


def _ffn_kernel(x_ref, h_ref, gt_ref, gp_ref, wgu_ref, wd_ref, o_ref, acc_scr):
    f = pl.program_id(2)
    tf = wd_ref.shape[0]

    def partial_down():
        gu = jnp.dot(h_ref[...], wgu_ref[...], preferred_element_type=F32)
        g, u = gu[:, :tf], gu[:, tf:]
        a = ((g * jax.nn.sigmoid(g)) * u).astype(BF16)
        return jnp.dot(a, wd_ref[...], preferred_element_type=F32)

    @pl.when(f == 0)
    def _():
        acc_scr[...] = partial_down()

    @pl.when(f > 0)
    def _():
        acc_scr[...] += partial_down()

    @pl.when(f == pl.num_programs(2) - 1)
    def _():
        def finish_rows(r, carry):
            rows = pl.ds(pl.multiple_of(r * NORM_ROWS, NORM_ROWS), NORM_ROWS)
            o_ref[rows, :] = x_ref[rows, :] + gt_ref[...] * (_rms(acc_scr[rows, :]) * gp_ref[...])
            return carry

        lax.fori_loop(0, o_ref.shape[0] // NORM_ROWS, finish_rows, 0, unroll=8)


def _ffn(x, h, gt, g_post, wgu, wd, layer, *, tm, tf):
    bsz, seq, d = x.shape
    dff = wd.shape[1]
    vec = lambda: pl.BlockSpec((None, 1, d), lambda b, i, f: (b, 0, 0))
    par = lambda: pl.BlockSpec((1, d), lambda b, i, f: (0, 0))
    return pl.pallas_call(
        _ffn_kernel,
        grid=(bsz, seq // tm, dff // tf),
        in_specs=[
            pl.BlockSpec((None, tm, d), lambda b, i, f: (b, i, 0)),
            pl.BlockSpec((None, tm, d), lambda b, i, f: (b, i, 0)),
            vec(), par(),
            pl.BlockSpec((None, d, 2 * tf), lambda b, i, f: (layer, 0, f)),
            pl.BlockSpec((None, tf, d), lambda b, i, f: (layer, f, 0)),
        ],
        out_specs=pl.BlockSpec((None, tm, d), lambda b, i, f: (b, i, 0)),
        out_shape=jax.ShapeDtypeStruct((bsz, seq, d), F32),
        scratch_shapes=[pltpu.VMEM((tm, d), F32)],
        compiler_params=_params(("parallel", "parallel", "arbitrary")),
        name="ffn",
    )(x, h, gt, g_post.reshape(1, d), wgu, wd)


SRC_QKV = 3 * A_W + 3 * B_W
SRC_REST = SRC_QKV + B_HEADS
SRC_GATE = SRC_REST + (PROJ_W - GATE_W - SRC_QKV)
IN_W = SRC_GATE + GATE_W
PACK_R = 256
PACK_SIDE_ROW = PROJ_W
PACK_ROWS = PROJ_W_PAD
assert GATE_W % PACK_R == 0 and SRC_QKV % PACK_R == 0 and PROJ_W % PACK_R == 0 and PROJ_W_PAD % PACK_R == 0


def _pack_tables():
    s2 = HEAD_DIM ** -0.5 * LOG2E
    src, scale, rows = [], [], []
    for b in range(PACK_ROWS // PACK_R):
        row = b * PACK_R
        if row < GATE_W:
            src.append(SRC_GATE + row); scale.append(1.0); rows.append(PACK_R)
        elif row < GATE_W + SRC_QKV:
            s = row - GATE_W
            is_q = s < A_W or 3 * A_W <= s < 3 * A_W + B_W
            src.append(s); scale.append(s2 if is_q else 1.0); rows.append(PACK_R)
        elif row < PROJ_W:
            s = row - GATE_W - SRC_QKV
            is_q = 2 * CONV_CH <= s < 2 * CONV_CH + D_QW
            src.append(SRC_REST + s); scale.append(D_HEAD_DIM ** -0.5 if is_q else 1.0); rows.append(PACK_R)
        elif row == PACK_SIDE_ROW:
            src.append(SRC_QKV); scale.append(1.0); rows.append(B_HEADS)
        else:
            src.append(0); scale.append(0.0); rows.append(0)
    return np.asarray(src, np.int32), np.asarray(scale, np.float32), np.asarray(rows, np.int32)


def _pack_kernel(src_ref, scale_ref, rows_ref, a_ref, o_ref):
    blk = pl.program_id(0)
    depth, _, d = o_ref.shape
    kt = d // LANES
    keep = lax.broadcasted_iota(jnp.int32, (PACK_R, d), 0) < rows_ref[blk]
    for layer in range(depth):
        cols = [a_ref[pl.ds(t * depth + layer, PACK_R, stride=kt * depth), :] for t in range(kt)]
        o_ref[layer] = jnp.where(keep, jnp.concatenate(cols, axis=1) * scale_ref[blk], 0.0).astype(o_ref.dtype)


def _pack_in_proj(w):
    depth, d, in_w = w.shape
    assert in_w == IN_W and d % LANES == 0
    kt = d // LANES
    view = jnp.transpose(w.reshape(depth, kt, LANES, in_w), (3, 1, 0, 2)).reshape(in_w * kt * depth, LANES)
    src, scale, rows = _pack_tables()
    per_feature = kt * depth
    return pl.pallas_call(
        _pack_kernel,
        grid_spec=pltpu.PrefetchScalarGridSpec(
            num_scalar_prefetch=3,
            grid=(PACK_ROWS // PACK_R,),
            in_specs=[pl.BlockSpec((pl.Element(PACK_R * per_feature), pl.Element(LANES)),
                                   lambda b, src, scale, rows: (src[b] * per_feature, 0))],
            out_specs=pl.BlockSpec((depth, PACK_R, d), lambda b, src, scale, rows: (0, b, 0)),
        ),
        out_shape=jax.ShapeDtypeStruct((depth, PACK_ROWS, d), BF16),
        compiler_params=_params(("parallel",)),
        name="pack_in_proj",
    )(jnp.asarray(src), jnp.asarray(scale), jnp.asarray(rows), view)


CAST_BLOCK_ELEMS = 1 << 20


def _cast_kernel(w_ref, o_ref):
    o_ref[...] = w_ref[...].astype(o_ref.dtype)


def _cast_bf16(w):
    depth, r, c = w.shape
    tr = 16
    while tr * 2 * c <= CAST_BLOCK_ELEMS and r % (tr * 2) == 0:
        tr *= 2
    assert r % tr == 0
    return pl.pallas_call(
        _cast_kernel,
        grid=(depth, r // tr),
        in_specs=[pl.BlockSpec((None, tr, c), lambda l, i: (l, i, 0))],
        out_specs=pl.BlockSpec((None, tr, c), lambda l, i: (l, i, 0)),
        out_shape=jax.ShapeDtypeStruct(w.shape, BF16),
        compiler_params=_params(("parallel", "parallel")),
        name="cast_bf16",
    )(w)


def kernel(x, c, rel_bias, w_mod, b_mod, mix_norm_pre, mix_norm_post, w_in, fox_bias, conv_w, conv_b, conv_ln_g, conv_ln_b, sinks, w_branch_a, w_branch_b, w_branch_c, w_branch_d, w_out, ffn_norm_pre, ffn_norm_post, w_ffn_gate, w_ffn_up, w_ffn_down):
    depth = w_mod.shape[0]
    bsz, seq, d = x.shape
    tm = min(1024, seq)
    tm_small = min(256, seq)
    tm_ffn = min(512, seq)

    mod = _modulation(c, w_mod, b_mod)
    tab_a = _bias_tables(rel_bias[:, :A_HEADS], _moba_buckets(), scale=LOG2E)
    tab_d = _bias_tables(rel_bias[:, A_HEADS:], _swa_buckets(), scale=1.0)[:, 0]

    wa, wb, wc, wd, wo = [_cast_bf16(w) for w in (w_branch_a, w_branch_b, w_branch_c, w_branch_d, w_out)]

    w_proj = _pack_in_proj(w_in)

    for l in range(depth):
        sh_m, sc_m, gt_m, sh_f, sc_f, gt_f = [mod[l, :, None, i * d:(i + 1) * d] for i in range(6)]

        proj, fox_raw, q_t, wgu, wdn = _norm_proj(x, mix_norm_pre[l], sc_m, sh_m, w_proj, w_ffn_gate, w_ffn_up, w_ffn_down, l,
                                             tm=tm, tn=PROJ_TN)
        fox_qa, fox_ka, fox_cum = _fox_gate(fox_raw, fox_bias[l], t=min(512, seq))

        ya = _moba(proj, q_t, tab_a)
        yb = _fox(proj, q_t, fox_qa, fox_ka, _fox_first_block(proj, q_t, fox_cum))
        yc = _conv(proj, conv_w[l], conv_b[l], conv_ln_g[l], conv_ln_b[l])
        yd = _swa(proj, sinks[l], tab_d)

        x, h_ffn = _merge(ya, yb, yc, yd, proj, x, gt_m, mix_norm_post[l], ffn_norm_pre[l], sc_f, sh_f,
                          wa, wb, wc, wd, wo, l, tm=tm_small)
        x = _ffn(x, h_ffn, gt_f, ffn_norm_post[l], wgu, wdn, 0, tm=tm_ffn, tf=FFN_TF)
    return x
```

```python
import functools
import math

import jax
import jax.numpy as jnp
import numpy as np
from jax import lax
from jax.experimental import pallas as pl
from jax.experimental.pallas import tpu as pltpu

F32 = jnp.float32
BF16 = jnp.bfloat16
HIGHEST = lax.Precision.HIGHEST

HEAD_DIM = 128
A_HEADS = 4
MOBA_BLOCK = 256
MOBA_TOPK = 3
B_HEADS = 4
CONV_CH = 512
CONV_WIDTH = 31
D_Q_HEADS = 8
D_KV_HEADS = 2
D_HEAD_DIM = 64
WINDOW = 128
N_BUCKETS = 32
MAX_DISTANCE = 1024
N_BRANCHES = 4
RMS_EPS = 1e-6
LN_EPS = 1e-5

A_W = A_HEADS * HEAD_DIM
B_W = B_HEADS * HEAD_DIM
D_QW = D_Q_HEADS * D_HEAD_DIM
D_KVW = D_KV_HEADS * D_HEAD_DIM

LANES = 128
SUBLANES = 8
VMEM_LIMIT = 56 * 1024 * 1024

NEG = -1e30
LOG2E = math.log2(math.e)
ATT_G = 4
ATT_T = MOBA_BLOCK
ATT_U = 4
FFN_TF = 512
NORM_ROWS = 32

D_MODEL = 2048
GATE_W = N_BRANCHES * D_MODEL
COL_AQ, COL_AK, COL_AV = GATE_W, GATE_W + A_W, GATE_W + 2 * A_W
COL_BQ, COL_BK, COL_BV = COL_AV + A_W, COL_AV + A_W + B_W, COL_AV + A_W + 2 * B_W
COL_CU = COL_BV + B_W
COL_DQ = COL_CU + 2 * CONV_CH
COL_DK = COL_DQ + D_QW
COL_DV = COL_DK + D_KVW
PROJ_W = COL_DV + D_KVW
PROJ_TN = 1024
PROJ_W_PAD = -(-PROJ_W // PROJ_TN) * PROJ_TN
MOBA_FAR = -(-(MAX_DISTANCE + MOBA_BLOCK - 1) // MOBA_BLOCK)
VOID_COL = LANES - 1
assert ATT_G % ATT_U == 0 and (MOBA_FAR - 1) % ATT_U == 0


def _params(sem, vmem=VMEM_LIMIT):
    return pltpu.CompilerParams(dimension_semantics=sem, vmem_limit_bytes=vmem)


def _t5_bucket(dist):
    max_exact = N_BUCKETS // 2
    d = jnp.maximum(dist, 0)
    log_ratio = jnp.log(jnp.maximum(d, 1).astype(jnp.float32) / max_exact) / math.log(MAX_DISTANCE / max_exact)
    large = max_exact + (log_ratio * (N_BUCKETS - max_exact)).astype(jnp.int32)
    large = jnp.minimum(large, N_BUCKETS - 1)
    return jnp.where(d < max_exact, d, large)


def _sigmoid(x):
    return 0.5 * jnp.tanh(0.5 * x) + 0.5


def _rms(y):
    return y * lax.rsqrt(jnp.mean(y * y, axis=-1, keepdims=True) + RMS_EPS)


def _modulated_norm(x, g, sc, sh):
    return (_rms(x) * g) * (1.0 + sc) + sh


MOD_TN = 1024
MOD_KC = 256


def _mod_kernel(ct_ref, w_ref, b_ref, o_ref):
    d, nb = ct_ref.shape
    tn = w_ref.shape[1]
    ct = ct_ref[...]
    ca = ct * jax.nn.sigmoid(ct)
    rows = []
    for r in range(nb):
        acc = jnp.zeros((1, tn), F32)
        for kc in range(d // MOD_KC):
            sl = slice(kc * MOD_KC, (kc + 1) * MOD_KC)
            acc = acc + jnp.sum(w_ref[sl, :] * ca[sl, r:r + 1], axis=0, keepdims=True)
        rows.append(acc)
    o_ref[...] = jnp.concatenate(rows, axis=0) + b_ref[...]


def _modulation(c, w_mod, b_mod):
    depth, d, n = w_mod.shape
    bsz = c.shape[0]
    return pl.pallas_call(
        _mod_kernel,
        grid=(depth, n // MOD_TN),
        in_specs=[
            pl.BlockSpec((d, bsz), lambda l, j: (0, 0)),
            pl.BlockSpec((None, d, MOD_TN), lambda l, j: (l, 0, j)),
            pl.BlockSpec((None, 1, MOD_TN), lambda l, j: (l, 0, j)),
        ],
        out_specs=pl.BlockSpec((None, bsz, MOD_TN), lambda l, j: (l, 0, j)),
        out_shape=jax.ShapeDtypeStruct((depth, bsz, n), F32),
        compiler_params=_params(("parallel", "parallel")),
        name="mod",
    )(c.T, w_mod, b_mod.reshape(depth, 1, n))


def _bias_table_kernel(rb_ref, bucket_ref, o_ref, *, scale):
    h = pl.program_id(0)
    b = bucket_ref[...]
    acc = jnp.where(b < 0, NEG, 0.0).astype(F32)
    for u in range(N_BUCKETS):
        acc = jnp.where(b == u, rb_ref[h, u] * scale, acc)
    o_ref[...] = acc


def _bias_tables(rel_bias_heads, bucket, *, scale):
    nh = rel_bias_heads.shape[1]
    nt, r, c = bucket.shape
    return pl.pallas_call(
        functools.partial(_bias_table_kernel, scale=scale),
        grid=(nh, nt),
        in_specs=[
            pl.BlockSpec(memory_space=pltpu.SMEM),
            pl.BlockSpec((None, r, c), lambda h, t: (t, 0, 0)),
        ],
        out_specs=pl.BlockSpec((None, None, r, c), lambda h, t: (h, t, 0, 0)),
        out_shape=jax.ShapeDtypeStruct((nh, nt, r, c), F32),
        compiler_params=_params(("parallel", "parallel")),
        name="bias_table",
    )(rel_bias_heads.T, bucket)


def _moba_buckets():
    blk = MOBA_BLOCK
    i = jnp.arange(blk)[None, :]
    j = jnp.arange(blk)[:, None]
    tabs = []
    for delta in range(MOBA_FAR):
        dist = delta * blk + i - j
        tabs.append(jnp.where(dist >= 0, _t5_bucket(dist), -1))
    tabs.append(_t5_bucket(jnp.full((blk, blk), MAX_DISTANCE, jnp.int32)))
    return jnp.stack(tabs).astype(jnp.int32)


def _swa_buckets():
    qi = jnp.arange(WINDOW)[None, :]
    kj = jnp.arange(2 * WINDOW)[:, None]
    dist = qi + WINDOW - kj
    in_win = (dist >= 0) & (dist < WINDOW)
    return jnp.where(in_win, _t5_bucket(dist), -1).astype(jnp.int32)[None]


def _norm_proj_kernel(x_ref, g_ref, sc_ref, sh_ref, w_ref, wg_ref, wu_ref, wdn_ref,
                      o_ref, os_ref, qt_ref, wgu_out, wdn_out, h_scr):
    j = pl.program_id(2)
    tn = w_ref.shape[0]
    last = pl.num_programs(2) - 1

    def cast_ffn_weights():
        tf = FFN_TF
        for f in range(wg_ref.shape[1] // tf):
            wgu_out[:, 2 * f * tf:(2 * f + 1) * tf] = wg_ref[:, f * tf:(f + 1) * tf].astype(BF16)
            wgu_out[:, (2 * f + 1) * tf:(2 * f + 2) * tf] = wu_ref[:, f * tf:(f + 1) * tf].astype(BF16)
        wdn_out[...] = wdn_ref[...].astype(BF16)

    @pl.when(j == 0)
    def _():
        gain = g_ref[...] * (1.0 + sc_ref[...])

        def norm_rows(r, carry):
            rows = pl.ds(pl.multiple_of(r * NORM_ROWS, NORM_ROWS), NORM_ROWS)
            h_scr[rows, :] = (_rms(x_ref[rows, :]) * gain + sh_ref[...]).astype(BF16)
            return carry

        lax.fori_loop(0, x_ref.shape[0] // NORM_ROWS, norm_rows, 0, unroll=8)

    @pl.when(j < GATE_W // tn)
    def _():
        o_ref[...] = _sigmoid(_dot_nt(h_scr[...], w_ref[...])).astype(o_ref.dtype)
        cast_ffn_weights()

    j_aq, j_bq = COL_AQ // tn, COL_BQ // tn

    @pl.when((j >= GATE_W // tn) & (j < last) & (j != j_aq) & (j != j_bq))
    def _():
        o_ref[...] = _dot_nt(h_scr[...], w_ref[...]).astype(o_ref.dtype)
        cast_ffn_weights()

    def query_tile(col):
        q0 = col % tn
        h = h_scr[...]
        qt_ref[...] = _dot_nt(w_ref[q0:q0 + qt_ref.shape[0], :], h).astype(qt_ref.dtype)
        rest = slice(0, q0) if q0 else slice(qt_ref.shape[0], tn)
        o_ref[:, rest] = _dot_nt(h, w_ref[rest, :]).astype(o_ref.dtype)
        o_ref[:, q0:q0 + qt_ref.shape[0]] = jnp.zeros((o_ref.shape[0], qt_ref.shape[0]), o_ref.dtype)
        cast_ffn_weights()

    @pl.when(j == j_aq)
    def _():
        query_tile(COL_AQ)

    @pl.when(j == j_bq)
    def _():
        query_tile(COL_BQ)

    @pl.when(j == last)
    def _():
        acc = _dot_nt(h_scr[...], w_ref[...])
        o_ref[...] = acc.astype(o_ref.dtype)
        col = PACK_SIDE_ROW % tn
        os_ref[...] = acc[:, col:col + os_ref.shape[1]]
        cast_ffn_weights()


def _rows_per_step(rows, steps, tile):
    r = tile
    while rows % r or rows // r > steps:
        r += tile
    return r


def _norm_proj(x, g, sc, sh, w, w_gate, w_up, w_down, layer, *, tm, tn):
    bsz, seq, d = x.shape
    dff = w_down.shape[1]
    n = PROJ_W_PAD
    ns = LANES
    ni, nj = seq // tm, n // tn
    assert GATE_W % tn == 0 and n % tn == 0 and w.shape[1] == PACK_ROWS and dff % FFN_TF == 0
    assert PACK_SIDE_ROW // tn == nj - 1 and GATE_W // tn < nj - 1 and PACK_SIDE_ROW % ns == 0
    j_aq, j_bq = COL_AQ // tn, COL_BQ // tn
    assert A_W == B_W and GATE_W // tn <= j_aq < j_bq < nj - 1 and tn == 2 * A_W
    assert COL_AQ % tn in (0, A_W) and COL_BQ % tn in (0, A_W)
    r_gu = _rows_per_step(d, bsz * ni * nj, 2 * SUBLANES)
    r_dn = _rows_per_step(dff, bsz * ni * nj, 2 * SUBLANES)
    step = lambda b, i, j: (b * ni + i) * nj + j
    gu_blk = lambda b, i, j: (layer, jnp.minimum(step(b, i, j), d // r_gu - 1), 0)
    dn_blk = lambda b, i, j: (layer, jnp.minimum(step(b, i, j), dff // r_dn - 1), 0)
    first = lambda idx: (lambda b, i, j: (0,) + idx(b, i, j)[1:])
    return pl.pallas_call(
        _norm_proj_kernel,
        grid=(bsz, ni, nj),
        in_specs=[
            pl.BlockSpec((None, tm, d), lambda b, i, j: (b, i, 0)),
            pl.BlockSpec((1, d), lambda b, i, j: (0, 0)),
            pl.BlockSpec((None, 1, d), lambda b, i, j: (b, 0, 0)),
            pl.BlockSpec((None, 1, d), lambda b, i, j: (b, 0, 0)),
            pl.BlockSpec((None, tn, d), lambda b, i, j: (layer, j, 0)),
            pl.BlockSpec((None, r_gu, dff), gu_blk),
            pl.BlockSpec((None, r_gu, dff), gu_blk),
            pl.BlockSpec((None, r_dn, d), dn_blk),
        ],
        out_specs=[pl.BlockSpec((None, tm, tn), lambda b, i, j: (b, i, j)),
                   pl.BlockSpec((None, tm, ns), lambda b, i, j: (b, i, 0)),
                   pl.BlockSpec((None, A_W, tm), lambda b, i, j: (b, jnp.where(j <= j_aq, 0, 1), i)),
                   pl.BlockSpec((None, r_gu, 2 * dff), first(gu_blk)),
                   pl.BlockSpec((None, r_dn, d), first(dn_blk))],
        out_shape=[jax.ShapeDtypeStruct((bsz, seq, n), BF16), jax.ShapeDtypeStruct((bsz, seq, ns), F32),
                   jax.ShapeDtypeStruct((bsz, A_W + B_W, seq), BF16),
                   jax.ShapeDtypeStruct((1, d, 2 * dff), BF16), jax.ShapeDtypeStruct((1, dff, d), BF16)],
        scratch_shapes=[pltpu.VMEM((tm, d), BF16)],
        compiler_params=_params(("arbitrary", "arbitrary", "arbitrary")),
        name="norm_proj",
    )(x, g.reshape(1, d), sc, sh, w, w_gate, w_up, w_down)


def _split3(c):
    hi = c.astype(BF16).astype(F32)
    mid = (c - hi).astype(BF16).astype(F32)
    lo = (c - hi - mid).astype(BF16).astype(F32)
    return hi, mid, lo


def _fox_gate_kernel(x_ref, fb_ref, qa_ref, ka_ref, cum_ref, carry_scr):
    t = x_ref.shape[0]

    @pl.when(pl.program_id(1) == 0)
    def _():
        carry_scr[...] = jnp.zeros_like(carry_scr)

    lf = jax.nn.log_sigmoid(x_ref[...] + fb_ref[...])
    tri = jnp.where(lax.broadcasted_iota(jnp.int32, (t, t), 0) >= lax.broadcasted_iota(jnp.int32, (t, t), 1),
                    1.0, 0.0).astype(BF16)
    c = carry_scr[...]
    for piece in _split3(lf):
        c = c + jnp.dot(tri, piece.astype(BF16), preferred_element_type=F32)
    carry_scr[...] = c[t - 1:t, :]
    cum_ref[...] = c * LOG2E
    lane = lax.broadcasted_iota(jnp.int32, (t, LANES), 1)
    sub = lax.broadcasted_iota(jnp.int32, (SUBLANES, t), 0)
    ct = c.T * LOG2E
    for h in range(B_HEADS):
        hi, mid, lo = _split3(jnp.broadcast_to(c[:, h:h + 1], (t, LANES)) * LOG2E)
        ka = jnp.where(lane == 0, -hi, jnp.where(lane == 1, -mid, jnp.where(lane == 2, -lo, jnp.where(lane < 6, 1.0, 0.0))))
        ka_ref[h] = ka.astype(BF16)
        hi, mid, lo = _split3(jnp.broadcast_to(ct[h:h + 1, :], (SUBLANES, t)))
        qa = jnp.where(sub < 3, 1.0, jnp.where(sub == 3, hi, jnp.where(sub == 4, mid, jnp.where(sub == 5, lo, 0.0))))
        qa_ref[h] = jnp.concatenate([qa, jnp.zeros((LANES - SUBLANES, t), F32)], axis=0).astype(BF16)


def _fox_gate(raw, fox_b, *, t):
    bsz, seq, _ = raw.shape
    fb = jnp.zeros((1, LANES), F32).at[0, :B_HEADS].set(fox_b)
    aug = pl.BlockSpec((None, B_HEADS, t, LANES), lambda b, i: (b, 0, i, 0))
    aug_t = pl.BlockSpec((None, B_HEADS, LANES, t), lambda b, i: (b, 0, 0, i))
    return pl.pallas_call(
        _fox_gate_kernel,
        grid=(bsz, seq // t),
        in_specs=[
            pl.BlockSpec((None, t, LANES), lambda b, i: (b, i, 0)),
            pl.BlockSpec((1, LANES), lambda b, i: (0, 0)),
        ],
        out_specs=[aug_t, aug, pl.BlockSpec((None, t, LANES), lambda b, i: (b, i, 0))],
        out_shape=[jax.ShapeDtypeStruct((bsz, B_HEADS, LANES, seq), BF16),
                   jax.ShapeDtypeStruct((bsz, B_HEADS, seq, LANES), BF16),
                   jax.ShapeDtypeStruct((bsz, seq, LANES), F32)],
        scratch_shapes=[pltpu.VMEM((1, LANES), F32)],
        compiler_params=_params(("parallel", "arbitrary")),
        name="fox_gate",
    )(raw, fb)


ONES_ROWS = 16


def _softmax_step(zt, vt, carry):
    m, acc = carry
    m_new = jnp.maximum(m, jnp.max(zt, axis=0, keepdims=True))
    p = jnp.exp2(zt - m_new)
    acc = jnp.exp2(m - m_new) * acc + jnp.dot(vt, p.astype(BF16), preferred_element_type=F32)
    return m_new, acc


def _softmax_init(tq, dh):
    return (jnp.full((1, tq), NEG, F32), jnp.zeros((dh + ONES_ROWS, tq), F32))


def _softmax_finish(carry, dh):
    _, acc = carry
    return (acc[:dh, :] / acc[dh:dh + 1, :]).T


def _fill_vt(vt_scr, v_ref, chunk):
    seq, dh = v_ref.shape
    for cix in range(seq // chunk):
        sl = slice(cix * chunk, (cix + 1) * chunk)
        vt_scr[:dh, sl] = _transpose_bf16(v_ref[sl, :])
    vt_scr[dh:, :] = jnp.ones((ONES_ROWS, seq), BF16)


ATT_LOOKAHEAD = 10


def _run_substeps(steps, carry):
    carry = list(carry)
    pending = {}
    for s in range(min(ATT_LOOKAHEAD, len(steps))):
        pending[s] = steps[s][1]()
    for s, (j, _, fix, vt) in enumerate(steps):
        if s + ATT_LOOKAHEAD < len(steps):
            pending[s + ATT_LOOKAHEAD] = steps[s + ATT_LOOKAHEAD][1]()
        z, cj = pending.pop(s), carry[j]
        if fix is not None:
            z, cj = fix(z, cj)
        carry[j] = _softmax_step(z, vt, cj)
    return carry


def _dense_loop(start, n_blocks, make_steps, carry):
    big = 2 * ATT_U
    carry = lax.fori_loop(0, n_blocks // big,
                          lambda grp, c: tuple(_run_substeps(make_steps(start + grp * big, big), c)), carry)
    first = start + (n_blocks // big) * big
    return lax.fori_loop(0, (start + n_blocks - first) // ATT_U,
                         lambda grp, c: tuple(_run_substeps(make_steps(first + grp * ATT_U, ATT_U), c)), carry)


def _transpose_bf16(x):
    return x.astype(F32).T.astype(BF16)


_NT = (((1,), (1,)), ((), ()))


def _dot_nt(a, b):
    return lax.dot_general(a, b, _NT, preferred_element_type=F32)


def _moba_kernel(cfar_ref, q_ref, k_ref, v_ref, tab_ref, o_ref, kmean_scr, qt_scr, vt_scr):
    g, t = ATT_G, ATT_T
    seq, dh = k_ref.shape
    nblk = seq // t
    rows = g * t
    i = pl.program_id(2)
    base = i * g

    @pl.when(i == 0)
    def _():
        r = lax.broadcasted_iota(jnp.int32, (nblk, seq), 0)
        c = lax.broadcasted_iota(jnp.int32, (nblk, seq), 1)
        avg = jnp.where((c >= r * t) & (c < (r + 1) * t), 1.0 / t, 0.0).astype(BF16)
        kmean_scr[...] = jnp.dot(avg, k_ref[...], preferred_element_type=F32)
        _fill_vt(vt_scr, v_ref, rows)

    qt = q_ref[...]
    qt_scr[:dh, :] = qt
    sc = jnp.dot(kmean_scr[...], qt.astype(F32), precision=HIGHEST, preferred_element_type=F32)
    row = lax.broadcasted_iota(jnp.int32, (nblk, rows), 0)
    own = base + lax.broadcasted_iota(jnp.int32, (nblk, rows), 1) // t
    past = row < own
    sc = jnp.where(past, sc, -jnp.inf)
    rowf = row.astype(F32)
    picked = jnp.zeros((nblk, rows), F32)
    for _ in range(MOBA_TOPK):
        best = jnp.max(sc, axis=0, keepdims=True)
        first = jnp.min(jnp.where(sc == best, rowf, float(nblk)), axis=0, keepdims=True)
        pick = rowf == first
        picked = jnp.where(pick, 1.0, picked)
        sc = jnp.where(pick, -jnp.inf, sc)
    pen = jnp.where(past, jnp.where(picked > 0.0, 0.0, NEG), jnp.where(row == own, 0.0, NEG))
    void = jnp.where(lax.broadcasted_iota(jnp.int32, (SUBLANES, rows), 0) == SUBLANES - 1, NEG, 0.0)
    pen = jnp.concatenate([pen, jnp.zeros((LANES - SUBLANES - nblk, rows), F32), void], axis=0)
    qt_scr[dh:, :] = pen.astype(BF16)

    lane = lax.broadcasted_iota(jnp.int32, (t, LANES), 1)

    def kv(n, col):
        start = pl.multiple_of(n * t, t)
        onehot = jnp.where(lane == col, 1.0, 0.0).astype(BF16)
        return jnp.concatenate([k_ref[pl.ds(start, t), :], onehot], axis=1), vt_scr[:, pl.ds(start, t)]

    def logits(kb, j):
        return jnp.dot(kb, qt_scr[:, j * t:(j + 1) * t], preferred_element_type=F32)

    def far_steps(first, count):
        steps = []
        for u in range(count):
            kb, vb = kv(first + u, first + u)
            steps += [(j, functools.partial(logits, kb, j), None, vb) for j in range(g)]
        return steps

    carry = _dense_loop(0, jnp.maximum(base - (MOBA_FAR - 1), 0), far_steps,
                        tuple(_softmax_init(t, dh) for _ in range(g)))
    cfar = cfar_ref[pl.program_id(1)]

    def near_fix(delta, z, cj):
        if delta == MOBA_FAR - 1:
            m, acc = cj
            cj = (m + cfar, acc)
        return tab_ref[delta] + z, cj

    steps = []
    for k in range(-(MOBA_FAR - 1), g):
        n = base + k
        if k < 0:
            kb, vb = kv(jnp.maximum(n, 0), jnp.where(n >= 0, n, VOID_COL))
        else:
            kb, vb = kv(n, n)
        for j in range(max(k, 0), g):
            fix = functools.partial(near_fix, j - k) if j - k < MOBA_FAR else None
            steps.append((j, functools.partial(logits, kb, j), fix, vb))
    carry = _run_substeps(steps, carry)
    for j in range(g):
        o_ref[j * t:(j + 1) * t, :] = _softmax_finish(carry[j], dh).astype(o_ref.dtype)


def _moba(proj, qt, tab):
    bsz, seq, _ = proj.shape
    g, t = ATT_G, ATT_T
    nblk = seq // t
    assert seq % (g * t) == 0 and nblk <= LANES - SUBLANES and nblk % SUBLANES == 0
    kc, vc = COL_AK // HEAD_DIM, COL_AV // HEAD_DIM
    cfar = tab[:, MOBA_FAR, 0, 0]
    return pl.pallas_call(
        _moba_kernel,
        grid=(bsz, A_HEADS, nblk // g),
        in_specs=[
            pl.BlockSpec(memory_space=pltpu.SMEM),
            pl.BlockSpec((None, HEAD_DIM, g * t), lambda b, h, i: (b, h, i)),
            pl.BlockSpec((None, seq, HEAD_DIM), lambda b, h, i: (b, 0, kc + h)),
            pl.BlockSpec((None, seq, HEAD_DIM), lambda b, h, i: (b, 0, vc + h)),
            pl.BlockSpec((None, MOBA_FAR, t, t), lambda b, h, i: (h, 0, 0, 0)),
        ],
        out_specs=pl.BlockSpec((None, g * t, HEAD_DIM), lambda b, h, i: (b, i, h)),
        out_shape=jax.ShapeDtypeStruct((bsz, seq, A_W), BF16),
        scratch_shapes=[pltpu.VMEM((nblk, HEAD_DIM), F32),
                        pltpu.VMEM((HEAD_DIM + LANES, g * t), BF16), pltpu.VMEM((HEAD_DIM + ONES_ROWS, seq), BF16)],
        compiler_params=_params(("parallel", "parallel", "arbitrary")),
        name="moba",
    )(cfar, qt, proj, proj, tab)


def _fox_kernel(start_ref, q_ref, qa_ref, k_ref, ka_ref, v_ref, o_ref, qt_scr, vt_scr):
    g, t = ATT_G, ATT_T
    rows = g * t
    seq, dh = k_ref.shape
    base = pl.program_id(2) * g
    first = start_ref[(pl.program_id(0) * pl.num_programs(1) + pl.program_id(1)) * pl.num_programs(2)
                      + pl.program_id(2)]

    @pl.when(base == 0)
    def _():
        _fill_vt(vt_scr, v_ref, rows)

    qt_scr[:dh, :] = q_ref[...]
    qt_scr[dh:, :] = qa_ref[...]

    def kv(n):
        start = pl.multiple_of(n * t, t)
        return (jnp.concatenate([k_ref[pl.ds(start, t), :], ka_ref[pl.ds(start, t), :]], axis=1),
                vt_scr[:, pl.ds(start, t)])

    def logits(kb, j):
        return jnp.dot(kb, qt_scr[:, j * t:(j + 1) * t], preferred_element_type=F32)

    def dense_steps(first, count):
        steps = []
        for u in range(count):
            kb, vb = kv(first + u)
            steps += [(j, functools.partial(logits, kb, j), None, vb) for j in range(g)]
        return steps

    carry = _dense_loop(first, base - first, dense_steps, tuple(_softmax_init(t, dh) for _ in range(g)))
    causal = lax.broadcasted_iota(jnp.int32, (t, t), 0) <= lax.broadcasted_iota(jnp.int32, (t, t), 1)

    def diag_fix(z, cj):
        return jnp.where(causal, z, NEG), cj

    steps = []
    for k in range(g):
        kb, vb = kv(base + k)
        steps += [(j, functools.partial(logits, kb, j), diag_fix if j == k else None, vb) for j in range(k, g)]
    carry = _run_substeps(steps, carry)
    for j in range(g):
        o_ref[j * t:(j + 1) * t, :] = _softmax_finish(carry[j], dh).astype(o_ref.dtype)


FOX_SKIP_BITS = 200.0
FOX_NORM_SLACK = 1.0 + 2.0 ** -6


def _fox_bounds_kernel(q_ref, k_ref, cum_ref, o_ref, krun_scr, kpref_scr, clast_scr):
    i = pl.program_id(1)
    rows, width = k_ref.shape
    g = rows // ATT_T

    @pl.when(i == 0)
    def _():
        krun_scr[...] = jnp.zeros_like(krun_scr)
        kpref_scr[...] = jnp.zeros_like(kpref_scr)
        clast_scr[...] = jnp.zeros_like(clast_scr)

    head_of = lax.broadcasted_iota(jnp.int32, (width, LANES), 0) // HEAD_DIM
    head_sum = jnp.where(head_of == lax.broadcasted_iota(jnp.int32, (width, LANES), 1), 1.0, 0.0).astype(BF16)

    def max_norm(ref):
        x = ref[...].astype(F32)
        n2 = jnp.dot((x * x).astype(BF16), head_sum, preferred_element_type=F32)
        return jnp.sqrt(jnp.max(n2, axis=0, keepdims=True)) * FOX_NORM_SLACK

    def max_norm_t(ref):
        lane = lax.broadcasted_iota(jnp.int32, (1, LANES), 1)
        out = jnp.zeros((1, LANES), F32)
        for h in range(width // HEAD_DIM):
            x = ref[h * HEAD_DIM:(h + 1) * HEAD_DIM, :].astype(F32)
            n2 = jnp.max(jnp.sum(x * x, axis=0, keepdims=True), axis=1, keepdims=True)
            out = jnp.where(lane == h, jnp.sqrt(n2) * FOX_NORM_SLACK, out)
        return out

    qn, kn = max_norm_t(q_ref), max_norm(k_ref)
    c_first, c_last = cum_ref[0:1, :], cum_ref[rows - 1:rows, :]
    tile = lax.broadcasted_iota(jnp.int32, kpref_scr.shape, 0)
    ub = qn * kpref_scr[...] + (c_first - clast_scr[...]) + qn * kn
    ok = (ub < -FOX_SKIP_BITS) & (tile < i)
    o_ref[...] = jnp.max(jnp.where(ok, (tile + 1) * g, 0), axis=0, keepdims=True)
    krun = jnp.maximum(krun_scr[...], kn)
    krun_scr[...] = krun
    kpref_scr[pl.ds(i, 1), :] = krun
    clast_scr[pl.ds(i, 1), :] = c_last


def _fox_first_block(proj, qt, cum2):
    bsz, seq, _ = proj.shape
    rows = ATT_G * ATT_T
    ntile = seq // rows
    hist = pltpu.VMEM((-(-ntile // SUBLANES) * SUBLANES, LANES), F32)
    first = pl.pallas_call(
        _fox_bounds_kernel,
        grid=(bsz, ntile),
        in_specs=[
            pl.BlockSpec((None, B_W, rows), lambda b, i: (b, A_W // B_W, i)),
            pl.BlockSpec((None, rows, B_W), lambda b, i: (b, i, COL_BK // B_W)),
            pl.BlockSpec((None, rows, LANES), lambda b, i: (b, i, 0)),
        ],
        out_specs=pl.BlockSpec((None, None, 1, LANES), lambda b, i: (b, i, 0, 0)),
        out_shape=jax.ShapeDtypeStruct((bsz, ntile, 1, LANES), jnp.int32),
        scratch_shapes=[pltpu.VMEM((1, LANES), F32), hist, hist],
        compiler_params=_params(("parallel", "arbitrary")),
        name="fox_bounds",
    )(qt, proj, cum2)
    return first[:, :, 0, :B_HEADS].transpose(0, 2, 1).reshape(-1)


def _fox(proj, qt, qa, ka, first):
    bsz, seq, _ = proj.shape
    g, t = ATT_G, ATT_T
    assert seq % (g * t) == 0
    kc, vc = COL_BK // HEAD_DIM, COL_BV // HEAD_DIM
    return pl.pallas_call(
        _fox_kernel,
        grid_spec=pltpu.PrefetchScalarGridSpec(
            num_scalar_prefetch=1,
            grid=(bsz, B_HEADS, seq // (g * t)),
            in_specs=[
                pl.BlockSpec((None, HEAD_DIM, g * t), lambda b, h, i, s: (b, A_HEADS + h, i)),
                pl.BlockSpec((None, None, LANES, g * t), lambda b, h, i, s: (b, h, 0, i)),
                pl.BlockSpec((None, seq, HEAD_DIM), lambda b, h, i, s: (b, 0, kc + h)),
                pl.BlockSpec((None, None, seq, LANES), lambda b, h, i, s: (b, h, 0, 0)),
                pl.BlockSpec((None, seq, HEAD_DIM), lambda b, h, i, s: (b, 0, vc + h)),
            ],
            out_specs=pl.BlockSpec((None, g * t, HEAD_DIM), lambda b, h, i, s: (b, i, h)),
            scratch_shapes=[pltpu.VMEM((HEAD_DIM + LANES, g * t), BF16),
                            pltpu.VMEM((HEAD_DIM + ONES_ROWS, seq), BF16)],
        ),
        out_shape=jax.ShapeDtypeStruct((bsz, seq, B_W), BF16),
        compiler_params=_params(("parallel", "parallel", "arbitrary")),
        name="fox",
    )(first, qt, qa, proj, ka, proj)


SWA_BLOCKS = 8


def _swa_kernel(sink_ref, q_ref, kp_ref, kc_ref, vp_ref, vc_ref, tab_ref, o_ref):
    w = WINDOW
    dh = D_HEAD_DIM
    grp = D_Q_HEADS // D_KV_HEADS
    i = pl.program_id(1)
    nblk = q_ref.shape[0] // w
    span = (nblk + 1) * w
    kall = jnp.concatenate([kp_ref[...], kc_ref[...]], axis=0)
    vt = jnp.concatenate([_transpose_bf16(vp_ref[...]), _transpose_bf16(vc_ref[...])], axis=1)
    ones = jnp.ones((ONES_ROWS, span), BF16)
    kg = [kall[:, g * dh:(g + 1) * dh] for g in range(D_KV_HEADS)]
    vg = [jnp.concatenate([vt[g * dh:(g + 1) * dh, :], ones], axis=0) for g in range(D_KV_HEADS)]
    qt = _transpose_bf16(q_ref[...])
    key = lax.broadcasted_iota(jnp.int32, (2 * w, w), 0)
    first = key >= jnp.where(i > 0, 0, w)

    def block_logits(blk):
        return [jnp.dot(kg[h // grp][blk * w:(blk + 2) * w, :], qt[h * dh:(h + 1) * dh, blk * w:(blk + 1) * w],
                        preferred_element_type=F32) for h in range(D_Q_HEADS)]

    logits = block_logits(0)
    for blk in range(nblk):
        nxt = block_logits(blk + 1) if blk + 1 < nblk else None
        outs = []
        for h in range(D_Q_HEADS):
            s = tab_ref[h] + logits[h]
            if blk == 0:
                s = jnp.where(first, s, NEG)
            sink = sink_ref[h]
            m = jnp.maximum(jnp.max(s, axis=0, keepdims=True), sink)
            p = jnp.exp(s - m)
            acc = jnp.dot(vg[h // grp][:, blk * w:(blk + 2) * w], p.astype(BF16), preferred_element_type=F32)
            den = acc[dh:dh + 1, :] + jnp.exp(sink - m)
            outs.append(acc[:dh, :] / den)
        o_ref[blk * w:(blk + 1) * w, :] = jnp.concatenate(outs, axis=0).T.astype(o_ref.dtype)
        logits = nxt


def _swa(proj, sinks, tab):
    bsz, seq, _ = proj.shape
    w = WINDOW
    nb = SWA_BLOCKS
    assert seq % (nb * w) == 0
    qc, kc, vc = COL_DQ // D_QW, COL_DK // D_KVW, COL_DV // D_KVW
    prev = lambda i: jnp.maximum(i * nb - 1, 0)
    return pl.pallas_call(
        _swa_kernel,
        grid=(bsz, seq // (nb * w)),
        in_specs=[
            pl.BlockSpec(memory_space=pltpu.SMEM),
            pl.BlockSpec((None, nb * w, D_QW), lambda b, i: (b, i, qc)),
            pl.BlockSpec((None, w, D_KVW), lambda b, i: (b, prev(i), kc)),
            pl.BlockSpec((None, nb * w, D_KVW), lambda b, i: (b, i, kc)),
            pl.BlockSpec((None, w, D_KVW), lambda b, i: (b, prev(i), vc)),
            pl.BlockSpec((None, nb * w, D_KVW), lambda b, i: (b, i, vc)),
            pl.BlockSpec((D_Q_HEADS, 2 * w, w), lambda b, i: (0, 0, 0)),
        ],
        out_specs=pl.BlockSpec((None, nb * w, D_QW), lambda b, i: (b, i, 0)),
        out_shape=jax.ShapeDtypeStruct((bsz, seq, D_QW), BF16),
        compiler_params=_params(("parallel", "parallel")),
        name="swa",
    )(sinks, proj, proj, proj, proj, proj, tab)


CONV_T = 1024
CONV_HALO = 32
CONV_ROWS = 256


def _conv_kernel(a_ref, g_ref, ha_ref, hg_ref, w_ref, b_ref, lg_ref, lb_ref, o_ref, buf):
    t = a_ref.shape[0]
    halo = CONV_HALO
    hp = ha_ref[...].astype(F32) * jax.nn.sigmoid(hg_ref[...].astype(F32))
    buf[0:halo, :] = jnp.where(pl.program_id(1) > 0, hp, 0.0)
    buf[halo:halo + t, :] = a_ref[...].astype(F32) * jax.nn.sigmoid(g_ref[...].astype(F32))
    off = halo - (CONV_WIDTH - 1)
    for r in range(t // CONV_ROWS):
        acc = jnp.broadcast_to(b_ref[...], (CONV_ROWS, CONV_CH))
        for rho in range(SUBLANES):
            rows = CONV_ROWS if rho == 0 else CONV_ROWS + SUBLANES
            part = None
            for s in range(off, off + CONV_WIDTH):
                if s % SUBLANES != rho:
                    continue
                start = r * CONV_ROWS + s - rho
                term = w_ref[s - off:s - off + 1, :] * buf[start:start + rows, :]
                part = term if part is None else part + term
            acc = acc + part[rho:rho + CONV_ROWS, :]
        mu = jnp.mean(acc, axis=-1, keepdims=True)
        d = acc - mu
        var = jnp.mean(d * d, axis=-1, keepdims=True)
        y = d * lax.rsqrt(var + LN_EPS) * lg_ref[...] + lb_ref[...]
        o_ref[r * CONV_ROWS:(r + 1) * CONV_ROWS, :] = (y * jax.nn.sigmoid(y)).astype(o_ref.dtype)


def _conv(proj, conv_w, conv_b, ln_g, ln_b):
    bsz, seq, _ = proj.shape
    t = min(CONV_T, seq)
    ca, cg = COL_CU // CONV_CH, COL_CU // CONV_CH + 1
    hb = t // CONV_HALO
    prev = lambda i: jnp.maximum(i * hb - 1, 0)
    row = lambda v: v.reshape(1, CONV_CH)
    return pl.pallas_call(
        _conv_kernel,
        grid=(bsz, seq // t),
        in_specs=[
            pl.BlockSpec((None, t, CONV_CH), lambda b, i: (b, i, ca)),
            pl.BlockSpec((None, t, CONV_CH), lambda b, i: (b, i, cg)),
            pl.BlockSpec((None, CONV_HALO, CONV_CH), lambda b, i: (b, prev(i), ca)),
            pl.BlockSpec((None, CONV_HALO, CONV_CH), lambda b, i: (b, prev(i), cg)),
            pl.BlockSpec((CONV_WIDTH, CONV_CH), lambda b, i: (0, 0)),
            pl.BlockSpec((1, CONV_CH), lambda b, i: (0, 0)),
            pl.BlockSpec((1, CONV_CH), lambda b, i: (0, 0)),
            pl.BlockSpec((1, CONV_CH), lambda b, i: (0, 0)),
        ],
        out_specs=pl.BlockSpec((None, t, CONV_CH), lambda b, i: (b, i, 0)),
        out_shape=jax.ShapeDtypeStruct((bsz, seq, CONV_CH), BF16),
        scratch_shapes=[pltpu.VMEM((CONV_HALO + t, CONV_CH), F32)],
        compiler_params=_params(("parallel", "parallel")),
        name="conv",
    )(proj, proj, proj, proj, conv_w, row(conv_b), row(ln_g), row(ln_b))


def _merge_kernel(ya_ref, yb_ref, yc_ref, yd_ref, gates_ref, x_ref, gt_ref, gp_ref, gf_ref, scf_ref, shf_ref,
                  wa_ref, wb_ref, wc_ref, wd_ref, wo_ref, o_ref, h_ref):
    d = x_ref.shape[1]
    y = None
    for i, (br, w) in enumerate(((ya_ref, wa_ref), (yb_ref, wb_ref), (yc_ref, wc_ref), (yd_ref, wd_ref))):
        t = jnp.dot(br[...], w[...], preferred_element_type=F32) * gates_ref[:, i * d:(i + 1) * d].astype(F32)
        y = t if y is None else y + t
    z = jnp.dot(y.astype(BF16), wo_ref[...], preferred_element_type=F32)
    x_new = x_ref[...] + gt_ref[...] * (_rms(z) * gp_ref[...])
    o_ref[...] = x_new
    h_ref[...] = _modulated_norm(x_new, gf_ref[...], scf_ref[...], shf_ref[...]).astype(h_ref.dtype)


def _merge(ya, yb, yc, yd, gates, x, gt, g_post, g_ffn, sc_f, sh_f, wa, wb, wc, wd, wo, layer, *, tm):
    bsz, seq, d = x.shape
    tok = lambda width: pl.BlockSpec((None, tm, width), lambda b, i: (b, i, 0))
    vec = lambda: pl.BlockSpec((None, 1, d), lambda b, i: (b, 0, 0))
    par = lambda: pl.BlockSpec((1, d), lambda b, i: (0, 0))
    full = lambda a: pl.BlockSpec((None,) + a.shape[1:], lambda b, i: (layer, 0, 0),
                                  pipeline_mode=pl.Buffered(1))
    return pl.pallas_call(
        _merge_kernel,
        grid=(bsz, seq // tm),
        in_specs=[tok(A_W), tok(B_W), tok(CONV_CH), tok(D_QW), tok(N_BRANCHES * d), tok(d),
                  vec(), par(), par(), vec(), vec(),
                  full(wa), full(wb), full(wc), full(wd), full(wo)],
        out_specs=[tok(d), tok(d)],
        out_shape=[jax.ShapeDtypeStruct((bsz, seq, d), F32), jax.ShapeDtypeStruct((bsz, seq, d), BF16)],
        compiler_params=_params(("parallel", "parallel")),
        name="merge",
    )(ya, yb, yc, yd, gates, x, gt, g_post.reshape(1, d), g_ffn.reshape(1, d), sc_f, sh_f, wa, wb, wc, wd, wo)


def _ffn_kernel(x_ref, h_ref, gt_ref, gp_ref, wgu_ref, wd_ref, o_ref, acc_scr):
    f = pl.program_id(2)
    tf = wd_ref.shape[0]

    def partial_down():
        gu = jnp.dot(h_ref[...], wgu_ref[...], preferred_element_type=F32)
        g, u = gu[:, :tf], gu[:, tf:]
        a = ((g * jax.nn.sigmoid(g)) * u).astype(BF16)
        return jnp.dot(a, wd_ref[...], preferred_element_type=F32)

    @pl.when(f == 0)
    def _():
        acc_scr[...] = partial_down()

    @pl.when(f > 0)
    def _():
        acc_scr[...] += partial_down()

    @pl.when(f == pl.num_programs(2) - 1)
    def _():
        def finish_rows(r, carry):
            rows = pl.ds(pl.multiple_of(r * NORM_ROWS, NORM_ROWS), NORM_ROWS)
            o_ref[rows, :] = x_ref[rows, :] + gt_ref[...] * (_rms(acc_scr[rows, :]) * gp_ref[...])
            return carry

        lax.fori_loop(0, o_ref.shape[0] // NORM_ROWS, finish_rows, 0, unroll=8)


def _ffn(x, h, gt, g_post, wgu, wd, layer, *, tm, tf):
    bsz, seq, d = x.shape
    dff = wd.shape[1]
    vec = lambda: pl.BlockSpec((None, 1, d), lambda b, i, f: (b, 0, 0))
    par = lambda: pl.BlockSpec((1, d), lambda b, i, f: (0, 0))
    return pl.pallas_call(
        _ffn_kernel,
        grid=(bsz, seq // tm, dff // tf),
        in_specs=[
            pl.BlockSpec((None, tm, d), lambda b, i, f: (b, i, 0)),
            pl.BlockSpec((None, tm, d), lambda b, i, f: (b, i, 0)),
            vec(), par(),
            pl.BlockSpec((None, d, 2 * tf), lambda b, i, f: (layer, 0, f)),
            pl.BlockSpec((None, tf, d), lambda b, i, f: (layer, f, 0)),
        ],
        out_specs=pl.BlockSpec((None, tm, d), lambda b, i, f: (b, i, 0)),
        out_shape=jax.ShapeDtypeStruct((bsz, seq, d), F32),
        scratch_shapes=[pltpu.VMEM((tm, d), F32)],
        compiler_params=_params(("parallel", "parallel", "arbitrary")),
        name="ffn",
    )(x, h, gt, g_post.reshape(1, d), wgu, wd)


SRC_QKV = 3 * A_W + 3 * B_W
SRC_REST = SRC_QKV + B_HEADS
SRC_GATE = SRC_REST + (PROJ_W - GATE_W - SRC_QKV)
IN_W = SRC_GATE + GATE_W
PACK_R = 256
PACK_SIDE_ROW = PROJ_W
PACK_ROWS = PROJ_W_PAD
assert GATE_W % PACK_R == 0 and SRC_QKV % PACK_R == 0 and PROJ_W % PACK_R == 0 and PROJ_W_PAD % PACK_R == 0


def _pack_tables():
    s2 = HEAD_DIM ** -0.5 * LOG2E
    src, scale, rows = [], [], []
    for b in range(PACK_ROWS // PACK_R):
        row = b * PACK_R
        if row < GATE_W:
            src.append(SRC_GATE + row); scale.append(1.0); rows.append(PACK_R)
        elif row < GATE_W + SRC_QKV:
            s = row - GATE_W
            is_q = s < A_W or 3 * A_W <= s < 3 * A_W + B_W
            src.append(s); scale.append(s2 if is_q else 1.0); rows.append(PACK_R)
        elif row < PROJ_W:
            s = row - GATE_W - SRC_QKV
            is_q = 2 * CONV_CH <= s < 2 * CONV_CH + D_QW
            src.append(SRC_REST + s); scale.append(D_HEAD_DIM ** -0.5 if is_q else 1.0); rows.append(PACK_R)
        elif row == PACK_SIDE_ROW:
            src.append(SRC_QKV); scale.append(1.0); rows.append(B_HEADS)
        else:
            src.append(0); scale.append(0.0); rows.append(0)
    return np.asarray(src, np.int32), np.asarray(scale, np.float32), np.asarray(rows, np.int32)


def _pack_kernel(src_ref, scale_ref, rows_ref, a_ref, o_ref):
    blk = pl.program_id(0)
    depth, _, d = o_ref.shape
    kt = d // LANES
    keep = lax.broadcasted_iota(jnp.int32, (PACK_R, d), 0) < rows_ref[blk]
    for layer in range(depth):
        cols = [a_ref[pl.ds(t * depth + layer, PACK_R, stride=kt * depth), :] for t in range(kt)]
        o_ref[layer] = jnp.where(keep, jnp.concatenate(cols, axis=1) * scale_ref[blk], 0.0).astype(o_ref.dtype)


def _pack_in_proj(w):
    depth, d, in_w = w.shape
    assert in_w == IN_W and d % LANES == 0
    kt = d // LANES
    view = jnp.transpose(w.reshape(depth, kt, LANES, in_w), (3, 1, 0, 2)).reshape(in_w * kt * depth, LANES)
    src, scale, rows = _pack_tables()
    per_feature = kt * depth
    return pl.pallas_call(
        _pack_kernel,
        grid_spec=pltpu.PrefetchScalarGridSpec(
            num_scalar_prefetch=3,
            grid=(PACK_ROWS // PACK_R,),
            in_specs=[pl.BlockSpec((pl.Element(PACK_R * per_feature), pl.Element(LANES)),
                                   lambda b, src, scale, rows: (src[b] * per_feature, 0))],
            out_specs=pl.BlockSpec((depth, PACK_R, d), lambda b, src, scale, rows: (0, b, 0)),
        ),
        out_shape=jax.ShapeDtypeStruct((depth, PACK_ROWS, d), BF16),
        compiler_params=_params(("parallel",)),
        name="pack_in_proj",
    )(jnp.asarray(src), jnp.asarray(scale), jnp.asarray(rows), view)


CAST_BLOCK_ELEMS = 1 << 20


def _cast_kernel(w_ref, o_ref):
    o_ref[...] = w_ref[...].astype(o_ref.dtype)


def _cast_bf16(w):
    depth, r, c = w.shape
    tr = 16
    while tr * 2 * c <= CAST_BLOCK_ELEMS and r % (tr * 2) == 0:
        tr *= 2
    assert r % tr == 0
    return pl.pallas_call(
        _cast_kernel,
        grid=(depth, r // tr),
        in_specs=[pl.BlockSpec((None, tr, c), lambda l, i: (l, i, 0))],
        out_specs=pl.BlockSpec((None, tr, c), lambda l, i: (l, i, 0)),
        out_shape=jax.ShapeDtypeStruct(w.shape, BF16),
        compiler_params=_params(("parallel", "parallel")),
        name="cast_bf16",
    )(w)


def kernel(x, c, rel_bias, w_mod, b_mod, mix_norm_pre, mix_norm_post, w_in, fox_bias, conv_w, conv_b, conv_ln_g, conv_ln_b, sinks, w_branch_a, w_branch_b, w_branch_c, w_branch_d, w_out, ffn_norm_pre, ffn_norm_post, w_ffn_gate, w_ffn_up, w_ffn_down):
    depth = w_mod.shape[0]
    bsz, seq, d = x.shape
    tm = min(1024, seq)
    tm_small = min(256, seq)
    tm_ffn = min(512, seq)

    mod = _modulation(c, w_mod, b_mod)
    tab_a = _bias_tables(rel_bias[:, :A_HEADS], _moba_buckets(), scale=LOG2E)
    tab_d = _bias_tables(rel_bias[:, A_HEADS:], _swa_buckets(), scale=1.0)[:, 0]

    wa, wb, wc, wd, wo = [_cast_bf16(w) for w in (w_branch_a, w_branch_b, w_branch_c, w_branch_d, w_out)]

    w_proj = _pack_in_proj(w_in)

    for l in range(depth):
        sh_m, sc_m, gt_m, sh_f, sc_f, gt_f = [mod[l, :, None, i * d:(i + 1) * d] for i in range(6)]

        proj, fox_raw, q_t, wgu, wdn = _norm_proj(x, mix_norm_pre[l], sc_m, sh_m, w_proj, w_ffn_gate, w_ffn_up, w_ffn_down, l,
                                             tm=tm, tn=PROJ_TN)
        fox_qa, fox_ka, fox_cum = _fox_gate(fox_raw, fox_bias[l], t=min(512, seq))

        ya = _moba(proj, q_t, tab_a)
        yb = _fox(proj, q_t, fox_qa, fox_ka, _fox_first_block(proj, q_t, fox_cum))
        yc = _conv(proj, conv_w[l], conv_b[l], conv_ln_g[l], conv_ln_b[l])
        yd = _swa(proj, sinks[l], tab_d)

        x, h_ffn = _merge(ya, yb, yc, yd, proj, x, gt_m, mix_norm_post[l], ffn_norm_pre[l], sc_f, sh_f,
                          wa, wb, wc, wd, wo, l, tm=tm_small)
        x = _ffn(x, h_ffn, gt_f, ffn_norm_post[l], wgu, wdn, 0, tm=tm_ffn, tf=FFN_TF)
    return x
```
